```python
import math
import jax
import jax.numpy as jnp
from jax import lax
import numpy as np

D_MODEL = 1024
BATCH = 4
SEQ = 8192
DEPTH = 2

HEAD_DIM = 64
GDN_WIDTH = 3 * D_MODEL // 8
CONV_WIDTH = D_MODEL // 4
FOX_WIDTH = D_MODEL - GDN_WIDTH - CONV_WIDTH
MIX_WIDTH = GDN_WIDTH + CONV_WIDTH + FOX_WIDTH
GDN_HEADS = GDN_WIDTH // HEAD_DIM
FOX_HEADS = FOX_WIDTH // HEAD_DIM
GDN_SHORT_CONV = 4
GDN_CHUNK = 64
CONV_KERNEL = 31
FOX_BLOCK = 128
FFN_DENSE = ((8 * D_MODEL // 3 + 127) // 128) * 128
N_EXPERTS = 8
TOP_K = 2
FFN_EXPERT = 7 * D_MODEL // 2
MOE_BLOCK = 128
N_DENSE = (DEPTH + 1) // 2
N_MOE = DEPTH // 2
DEEPNORM_ALPHA = (2 * DEPTH) ** 0.25
DEEPNORM_BETA = (8 * DEPTH) ** -0.25
LN_EPS = 1e-5
NORM_EPS = 1e-6
IN_SIZES = (GDN_WIDTH, GDN_WIDTH, GDN_WIDTH, GDN_HEADS, GDN_HEADS, GDN_WIDTH,
            2 * CONV_WIDTH,
            FOX_WIDTH, FOX_WIDTH, FOX_WIDTH, FOX_HEADS)
IN_COLS = sum(IN_SIZES)

kernel_name = 'hybrid_gdn_conformer_fox_moe_deepnorm'


def layer_norm(x, g, b):
    xf = x.astype(jnp.float32)
    mu = xf.mean(-1, keepdims=True)
    var = jnp.square(xf - mu).mean(-1, keepdims=True)
    return ((xf - mu) * lax.rsqrt(var + LN_EPS) * g.astype(jnp.float32) + b.astype(jnp.float32)).astype(x.dtype)


def rms_norm_f32(x, g):
    xf = x.astype(jnp.float32)
    return xf * lax.rsqrt(jnp.square(xf).mean(-1, keepdims=True) + NORM_EPS) * g.astype(jnp.float32)


def l2_normalize(x):
    return x * lax.rsqrt(jnp.sum(jnp.square(x), -1, keepdims=True) + NORM_EPS)


def split_columns(p, sizes):
    cuts = np.cumsum(np.array(sizes))[:-1].tolist()
    return jnp.split(p, cuts, axis=-1)


def causal_depthwise_conv(x, w):
    k_width, channels = w.shape
    return lax.conv_general_dilated(
        x, w[:, None, :], window_strides=(1,), padding=[(k_width - 1, 0)],
        dimension_numbers=('NWC', 'WIO', 'NWC'), feature_group_count=channels)


def gated_delta_rule(q, k, v, g, beta):
    B, T, H, dk = q.shape
    dv = v.shape[-1]
    C = GDN_CHUNK
    n = T // C
    f32 = jnp.float32
    q = l2_normalize(q.astype(f32)) * dk ** -0.5
    k = l2_normalize(k.astype(f32))
    v = v.astype(f32)

    def chunks(t):
        return t.reshape(B, n, C, H, -1).transpose(1, 0, 3, 2, 4)

    qc, kc, vc = chunks(q), chunks(k), chunks(v)
    gc = jnp.cumsum(chunks(g.astype(f32)[..., None])[..., 0], axis=-1)
    bc = chunks(beta.astype(f32)[..., None])
    causal = jnp.tril(jnp.ones((C, C), bool))
    strict = jnp.tril(jnp.ones((C, C), bool), -1)
    diff = gc[..., :, None] - gc[..., None, :]
    decay = jnp.where(causal, jnp.exp(jnp.where(causal, diff, 0.0)), 0.0)
    kb = kc * bc
    lower = jnp.where(strict, jnp.einsum('nbhcd,nbhsd->nbhcs', kb, kc) * decay, 0.0)
    eye = jnp.eye(C, dtype=f32)
    t_inv = lax.linalg.triangular_solve(lower + eye, jnp.broadcast_to(eye, lower.shape),
                                        left_side=True, lower=True, unit_diagonal=True)
    u = jnp.einsum('nbhcs,nbhsd->nbhcd', t_inv, vc * bc)
    w = jnp.einsum('nbhcs,nbhsd->nbhcd', t_inv, kb * jnp.exp(gc)[..., None])
    a_intra = jnp.where(causal, jnp.einsum('nbhcd,nbhsd->nbhcs', qc, kc) * decay, 0.0)

    def step(state, inp):
        q_i, k_i, u_i, w_i, a_i, g_i = inp
        v_new = u_i - jnp.einsum('bhck,bhkv->bhcv', w_i, state)
        o_i = (jnp.einsum('bhck,bhkv->bhcv', q_i * jnp.exp(g_i)[..., None], state)
               + jnp.einsum('bhcs,bhsv->bhcv', a_i, v_new))
        g_last = g_i[..., -1]
        state = (state * jnp.exp(g_last)[..., None, None]
                 + jnp.einsum('bhck,bhcv->bhkv', k_i * jnp.exp(g_last[..., None] - g_i)[..., None], v_new))
        return state, o_i

    s0 = jnp.zeros((B, H, dk, dv), f32)
    _, o = lax.scan(step, s0, (qc, kc, u, w, a_intra, gc))
    return o.transpose(1, 0, 3, 2, 4).reshape(B, T, H, dv)


def forgetting_attention(q, k, v, log_f):
    B, T, H, d = q.shape
    f32 = jnp.float32
    nb = T // FOX_BLOCK
    c = jnp.cumsum(log_f.astype(f32), axis=1)
    qb = (q.astype(f32) * d ** -0.5).reshape(B, nb, FOX_BLOCK, H, d).transpose(1, 0, 3, 2, 4)
    cq = c.reshape(B, nb, FOX_BLOCK, H).transpose(1, 0, 3, 2)
    kt = k.astype(f32).transpose(0, 2, 1, 3)
    vt = v.astype(f32).transpose(0, 2, 1, 3)
    ck = c.transpose(0, 2, 1)
    key_pos = jnp.arange(T)

    def block(args):
        q_blk, cq_blk, i = args
        q_pos = i * FOX_BLOCK + jnp.arange(FOX_BLOCK)
        s = jnp.einsum('bhqd,bhkd->bhqk', q_blk, kt) + cq_blk[..., None] - ck[:, :, None, :]
        s = jnp.where(key_pos[None, :] <= q_pos[:, None], s, -jnp.inf)
        p = jax.nn.softmax(s, axis=-1)
        return jnp.einsum('bhqk,bhkd->bhqd', p, vt)

    o = lax.map(block, (qb, cq, jnp.arange(nb)))
    return o.transpose(1, 0, 3, 2, 4).reshape(B, T, H, d)


def hybrid_mixer(x, w_in, mix_scale, w_out, gdn_conv_w, gdn_a_log, gdn_dt_bias, gdn_norm_w,
                 cnv_dw_w, cnv_dw_b, cnv_ln_g, cnv_ln_b, fox_f_bias):
    B, T, _ = x.shape
    dt = x.dtype
    f32 = jnp.float32
    proj = jnp.einsum('btd,de->bte', x, w_in)
    (a_q, a_k, a_v, a_decay, a_beta, a_gate, c_glu, f_q, f_k, f_v, f_forget) = split_columns(proj, IN_SIZES)

    def heads(t, h):
        return t.reshape(B, T, h, HEAD_DIM)

    qkv = jax.nn.silu(causal_depthwise_conv(jnp.concatenate([a_q, a_k, a_v], -1), gdn_conv_w))
    a_q, a_k, a_v = jnp.split(qkv, 3, axis=-1)
    log_decay = -jnp.exp(gdn_a_log.astype(f32)) * jax.nn.softplus(a_decay.astype(f32) + gdn_dt_bias.astype(f32))
    beta = jax.nn.sigmoid(a_beta.astype(f32))
    o_a = gated_delta_rule(heads(a_q, GDN_HEADS), heads(a_k, GDN_HEADS), heads(a_v, GDN_HEADS), log_decay, beta)
    o_a = (rms_norm_f32(o_a, gdn_norm_w) * jax.nn.silu(heads(a_gate, GDN_HEADS).astype(f32))).reshape(B, T, GDN_WIDTH).astype(dt)

    glu_a, glu_b = jnp.split(c_glu, 2, axis=-1)
    u = glu_a * jax.nn.sigmoid(glu_b)
    u = causal_depthwise_conv(u, cnv_dw_w) + cnv_dw_b
    o_b = jax.nn.silu(layer_norm(u, cnv_ln_g, cnv_ln_b))

    log_f = jax.nn.log_sigmoid(f_forget.astype(f32) + fox_f_bias.astype(f32))
    o_c = forgetting_attention(heads(f_q, FOX_HEADS), heads(f_k, FOX_HEADS), heads(f_v, FOX_HEADS), log_f)
    o_c = o_c.reshape(B, T, FOX_WIDTH).astype(dt)

    mixed = jnp.concatenate([o_a, o_b, o_c], axis=-1) * mix_scale
    return jnp.einsum('bte,ed->btd', mixed, w_out)


def swiglu(x, w1, w3, w2):
    h = jax.nn.silu(jnp.einsum('btd,df->btf', x, w1)) * jnp.einsum('btd,df->btf', x, w3)
    return jnp.einsum('btf,fd->btd', h, w2)


def moe_swiglu(x, w_router, w1, w3, w2):
    B, T, D = x.shape
    xf = x.reshape(-1, D)
    n_tok = xf.shape[0]
    n_asg = n_tok * TOP_K
    logits = jnp.einsum('nd,de->ne', xf, w_router).astype(jnp.float32)
    top_logits, top_idx = lax.top_k(logits, TOP_K)
    gates = jax.nn.softmax(top_logits, axis=-1)
    e_flat = top_idx.reshape(-1)
    order = jnp.argsort(e_flat)
    e_sorted = e_flat[order]
    sizes = jnp.bincount(e_flat, length=N_EXPERTS).astype(jnp.int32)
    padded = ((sizes + MOE_BLOCK - 1) // MOE_BLOCK) * MOE_BLOCK
    pad_end = jnp.cumsum(padded)
    pad_start = pad_end - padded
    start = jnp.cumsum(sizes) - sizes
    rank = jnp.arange(n_asg, dtype=jnp.int32) - start[e_sorted]
    dest = pad_start[e_sorted] + rank
    n_blocks = (n_asg + MOE_BLOCK - 1) // MOE_BLOCK + N_EXPERTS
    rows = n_blocks * MOE_BLOCK
    tok_pad = jnp.full((rows,), n_tok, jnp.int32).at[dest].set((order // TOP_K).astype(jnp.int32))
    gate_pad = jnp.zeros((rows,), jnp.float32).at[dest].set(gates.reshape(-1)[order])
    block_start = jnp.arange(n_blocks, dtype=jnp.int32) * MOE_BLOCK
    block_expert = jnp.minimum(jnp.searchsorted(pad_end, block_start, side='right'), N_EXPERTS - 1)
    x_ext = jnp.concatenate([xf, jnp.zeros((1, D), xf.dtype)], axis=0)
    x_blocks = x_ext[tok_pad].reshape(n_blocks, MOE_BLOCK, D)

    def expert_block(args):
        xb, e = args
        h = jax.nn.silu(xb @ w1[e]) * (xb @ w3[e])
        return h @ w2[e]

    y = lax.map(expert_block, (x_blocks, block_expert)).reshape(rows, D)
    y = y * gate_pad[:, None].astype(y.dtype)
    out = jnp.zeros((n_tok + 1, D), y.dtype).at[tok_pad].add(y)[:n_tok]
    return out.reshape(B, T, D)


def setup_inputs(seed: int = 0) -> dict:
    key = jax.random.key(seed)
    ks = jax.random.split(key, 26)
    f32 = jnp.float32
    nrm = jax.random.normal
    x = nrm(ks[0], (BATCH, SEQ, D_MODEL), f32)
    w_in = nrm(ks[1], (DEPTH, D_MODEL, IN_COLS), f32) * D_MODEL ** -0.5
    mix_scale = 1.0 + 0.05 * nrm(ks[2], (DEPTH, MIX_WIDTH), f32)
    w_out = nrm(ks[3], (DEPTH, MIX_WIDTH, D_MODEL), f32) * (MIX_WIDTH ** -0.5 * DEEPNORM_BETA)
    gdn_conv_w = nrm(ks[4], (DEPTH, GDN_SHORT_CONV, 3 * GDN_WIDTH), f32) * GDN_SHORT_CONV ** -0.5
    gdn_a_log = jnp.log(jax.random.uniform(ks[5], (DEPTH, GDN_HEADS), f32, 1.0, 16.0))
    dt_init = jnp.exp(jax.random.uniform(ks[6], (DEPTH, GDN_HEADS), f32, math.log(1e-3), math.log(1e-1)))
    gdn_dt_bias = dt_init + jnp.log(-jnp.expm1(-dt_init))
    gdn_norm_w = 1.0 + 0.05 * nrm(ks[7], (DEPTH, HEAD_DIM), f32)
    cnv_dw_w = nrm(ks[8], (DEPTH, CONV_KERNEL, CONV_WIDTH), f32) * CONV_KERNEL ** -0.5
    cnv_dw_b = 0.02 * nrm(ks[9], (DEPTH, CONV_WIDTH), f32)
    cnv_ln_g = 1.0 + 0.05 * nrm(ks[10], (DEPTH, CONV_WIDTH), f32)
    cnv_ln_b = 0.02 * nrm(ks[11], (DEPTH, CONV_WIDTH), f32)
    fox_f_bias = 3.0 + 0.5 * nrm(ks[12], (DEPTH, FOX_HEADS), f32)
    ln_mix_g = 1.0 + 0.05 * nrm(ks[13], (DEPTH, D_MODEL), f32)
    ln_mix_b = 0.02 * nrm(ks[14], (DEPTH, D_MODEL), f32)
    ln_ffn_g = 1.0 + 0.05 * nrm(ks[15], (DEPTH, D_MODEL), f32)
    ln_ffn_b = 0.02 * nrm(ks[16], (DEPTH, D_MODEL), f32)
    ffn_w1 = nrm(ks[17], (N_DENSE, D_MODEL, FFN_DENSE), f32) * D_MODEL ** -0.5
    ffn_w3 = nrm(ks[18], (N_DENSE, D_MODEL, FFN_DENSE), f32) * D_MODEL ** -0.5
    ffn_w2 = nrm(ks[19], (N_DENSE, FFN_DENSE, D_MODEL), f32) * (FFN_DENSE ** -0.5 * DEEPNORM_BETA)
    moe_router = nrm(ks[20], (N_MOE, D_MODEL, N_EXPERTS), f32) * D_MODEL ** -0.5
    moe_w1 = nrm(ks[21], (N_MOE, N_EXPERTS, D_MODEL, FFN_EXPERT), f32) * D_MODEL ** -0.5
    moe_w3 = nrm(ks[22], (N_MOE, N_EXPERTS, D_MODEL, FFN_EXPERT), f32) * D_MODEL ** -0.5
    moe_w2 = nrm(ks[23], (N_MOE, N_EXPERTS, FFN_EXPERT, D_MODEL), f32) * (FFN_EXPERT ** -0.5 * DEEPNORM_BETA)
    return {'x': x, 'w_in': w_in, 'mix_scale': mix_scale, 'w_out': w_out,
            'gdn_conv_w': gdn_conv_w, 'gdn_a_log': gdn_a_log, 'gdn_dt_bias': gdn_dt_bias, 'gdn_norm_w': gdn_norm_w,
            'cnv_dw_w': cnv_dw_w, 'cnv_dw_b': cnv_dw_b, 'cnv_ln_g': cnv_ln_g, 'cnv_ln_b': cnv_ln_b,
            'fox_f_bias': fox_f_bias,
            'ln_mix_g': ln_mix_g, 'ln_mix_b': ln_mix_b, 'ln_ffn_g': ln_ffn_g, 'ln_ffn_b': ln_ffn_b,
            'ffn_w1': ffn_w1, 'ffn_w3': ffn_w3, 'ffn_w2': ffn_w2,
            'moe_router': moe_router, 'moe_w1': moe_w1, 'moe_w3': moe_w3, 'moe_w2': moe_w2}


def reference(x, w_in, mix_scale, w_out, gdn_conv_w, gdn_a_log, gdn_dt_bias, gdn_norm_w,
              cnv_dw_w, cnv_dw_b, cnv_ln_g, cnv_ln_b, fox_f_bias,
              ln_mix_g, ln_mix_b, ln_ffn_g, ln_ffn_b,
              ffn_w1, ffn_w3, ffn_w2, moe_router, moe_w1, moe_w3, moe_w2):
    for l in range(DEPTH):
        mix = hybrid_mixer(x, w_in[l], mix_scale[l], w_out[l], gdn_conv_w[l], gdn_a_log[l], gdn_dt_bias[l],
                           gdn_norm_w[l], cnv_dw_w[l], cnv_dw_b[l], cnv_ln_g[l], cnv_ln_b[l], fox_f_bias[l])
        x = layer_norm(DEEPNORM_ALPHA * x + mix, ln_mix_g[l], ln_mix_b[l])
        if l % 2 == 0:
            ff = swiglu(x, ffn_w1[l // 2], ffn_w3[l // 2], ffn_w2[l // 2])
        else:
            ff = moe_swiglu(x, moe_router[l // 2], moe_w1[l // 2], moe_w3[l // 2], moe_w2[l // 2])
        x = layer_norm(DEEPNORM_ALPHA * x + ff, ln_ffn_g[l], ln_ffn_b[l])
    return x
```

```python
import functools

import jax
import jax.numpy as jnp
import numpy as np
from jax import lax
from jax.experimental import pallas as pl
from jax.experimental.pallas import tpu as pltpu

D_MODEL = 1024
DEPTH = 2
HEAD_DIM = 64
GDN_WIDTH = 384
CONV_WIDTH = 256
FOX_WIDTH = 384
GDN_HEADS = 6
FOX_HEADS = 6
GDN_SHORT_CONV = 4
GDN_CHUNK = 64
CONV_KERNEL = 31
FFN_DENSE = 2816
N_EXPERTS = 8
TOP_K = 2
FFN_EXPERT = 3584
DEEPNORM_ALPHA = (2 * DEPTH) ** 0.25
LN_EPS = 1e-5
NORM_EPS = 1e-6

LANES = 128
SUBLANES = 8
PAIR = 2 * HEAD_DIM
N_PAIRS = GDN_WIDTH // PAIR
SMALL_COLS = LANES

_C_QKV = 0
_C_GATE = _C_QKV + 3 * GDN_WIDTH
_C_GLU = _C_GATE + GDN_WIDTH
_C_FQ = _C_GLU + 2 * CONV_WIDTH
_C_FK = _C_FQ + FOX_WIDTH
_C_FV = _C_FK + FOX_WIDTH
_C_END = _C_FV + FOX_WIDTH

MXU_DTYPE = jnp.bfloat16
F32 = jnp.float32
HIGHEST = lax.Precision.HIGHEST

VMEM_LIMIT = 56 * 1024 * 1024


def _cparams(*sem):
    return pltpu.CompilerParams(dimension_semantics=sem, vmem_limit_bytes=VMEM_LIMIT)


def _dot(a, b):
    return jnp.dot(a.astype(MXU_DTYPE), b.astype(MXU_DTYPE), preferred_element_type=F32)


def _dot_nt(a, b):
    return lax.dot_general(a.astype(MXU_DTYPE), b.astype(MXU_DTYPE),
                           (((1,), (1,)), ((), ())), preferred_element_type=F32)


def _dot_f32(a, b):
    return jnp.dot(a, b, preferred_element_type=F32, precision=HIGHEST)


def _sigmoid(x):
    return 1.0 / (1.0 + jnp.exp(-x))


def _silu(x):
    return x * _sigmoid(x)


def _softplus(x):
    return jnp.maximum(x, 0.0) + jnp.log1p(jnp.exp(-jnp.abs(x)))


def _log_sigmoid(x):
    return -_softplus(-x)


def _layer_norm_rows(y, g, b):
    mu = jnp.mean(y, axis=-1, keepdims=True)
    d = y - mu
    var = jnp.mean(d * d, axis=-1, keepdims=True)
    return d * lax.rsqrt(var + LN_EPS) * g + b


def _head_ones():
    r = lax.broadcasted_iota(jnp.int32, (PAIR, PAIR), 0) // HEAD_DIM
    c = lax.broadcasted_iota(jnp.int32, (PAIR, PAIR), 1) // HEAD_DIM
    return (r == c).astype(F32)


def _in_proj_kernel(x_ref, w_ref, ws_ref, qkv_ref, gate_ref, glu_ref, fq_ref, fk_ref, fv_ref, small_ref):
    x = x_ref[...]
    xb = x.astype(MXU_DTYPE)

    def mm(lo, hi):
        return jnp.dot(xb, w_ref[:, lo:hi], preferred_element_type=F32)

    qkv_ref[...] = mm(_C_QKV, _C_GATE)
    gate_ref[...] = mm(_C_GATE, _C_GLU)
    glu_ref[...] = mm(_C_GLU, _C_FQ)
    fq_ref[...] = (mm(_C_FQ, _C_FK) * HEAD_DIM ** -0.5).astype(fq_ref.dtype)
    fk_ref[...] = mm(_C_FK, _C_FV).astype(fk_ref.dtype)
    fv_ref[...] = mm(_C_FV, _C_END).astype(fv_ref.dtype)
    small_ref[...] = _dot_f32(x, ws_ref[...])


def _in_proj(xf, w_main, w_small, tm):
    n = xf.shape[0]
    row = lambda i: (i, 0)
    const = lambda i: (0, 0)
    outs = [
        jax.ShapeDtypeStruct((n, 3 * GDN_WIDTH), F32),
        jax.ShapeDtypeStruct((n, GDN_WIDTH), F32),
        jax.ShapeDtypeStruct((n, 2 * CONV_WIDTH), F32),
        jax.ShapeDtypeStruct((n, FOX_WIDTH), MXU_DTYPE),
        jax.ShapeDtypeStruct((n, FOX_WIDTH), MXU_DTYPE),
        jax.ShapeDtypeStruct((n, FOX_WIDTH), MXU_DTYPE),
        jax.ShapeDtypeStruct((n, SMALL_COLS), F32),
    ]
    return pl.pallas_call(
        _in_proj_kernel,
        grid=(n // tm,),
        in_specs=[pl.BlockSpec((tm, D_MODEL), row),
                  pl.BlockSpec((D_MODEL, _C_END), const),
                  pl.BlockSpec((D_MODEL, SMALL_COLS), const)],
        out_specs=[pl.BlockSpec((tm, o.shape[1]), row) for o in outs],
        out_shape=outs,
        compiler_params=_cparams("parallel"),
        name="in_proj",
    )(xf, w_main, w_small)


def _gates_kernel(small_ref, a_ref, dtb_ref, fb_ref, g_ref, beta_ref, ct_ref, carry_ref):
    t = pl.program_id(1)
    tt = small_ref.shape[1]

    @pl.when(t == 0)
    def _():
        carry_ref[...] = jnp.zeros_like(carry_ref)

    s = small_ref[0]
    log_decay = -jnp.exp(a_ref[...]) * _softplus(s + dtb_ref[...])
    beta = _sigmoid(s)
    log_f = _log_sigmoid(s + fb_ref[...])

    r = lax.broadcasted_iota(jnp.int32, (tt, tt), 0)
    c = lax.broadcasted_iota(jnp.int32, (tt, tt), 1)
    tri = (c <= r)
    tri_all = tri.astype(F32)
    tri_chunk = (tri & (r // GDN_CHUNK == c // GDN_CHUNK)).astype(F32)
    g_cum = _dot_f32(tri_chunk, log_decay)
    c_cum = _dot_f32(tri_all, log_f) + carry_ref[...]
    carry_ref[...] = c_cum[tt - 1:tt, :]

    er = lax.broadcasted_iota(jnp.int32, (LANES, GDN_WIDTH), 0)
    ec = lax.broadcasted_iota(jnp.int32, (LANES, GDN_WIDTH), 1) // HEAD_DIM
    g_ref[0] = _dot_f32(g_cum, (er == ec).astype(F32))
    beta_ref[0] = _dot_f32(beta, (er == ec + GDN_HEADS).astype(F32))
    sel = (lax.broadcasted_iota(jnp.int32, (LANES, LANES), 0) ==
           lax.broadcasted_iota(jnp.int32, (LANES, LANES), 1) + 2 * GDN_HEADS).astype(F32)
    c_heads = _dot_f32(c_cum, sel)
    ct_ref[0] = c_heads.T[0:SUBLANES, :]


def _gates(small, a_row, dtb_row, fb_row, tt):
    b, t, _ = small.shape
    row = lambda i, j: (0, 0)
    return pl.pallas_call(
        _gates_kernel,
        grid=(b, t // tt),
        in_specs=[pl.BlockSpec((1, tt, SMALL_COLS), lambda i, j: (i, j, 0)),
                  pl.BlockSpec((1, LANES), row), pl.BlockSpec((1, LANES), row), pl.BlockSpec((1, LANES), row)],
        out_specs=[pl.BlockSpec((1, tt, GDN_WIDTH), lambda i, j: (i, j, 0)),
                   pl.BlockSpec((1, tt, GDN_WIDTH), lambda i, j: (i, j, 0)),
                   pl.BlockSpec((1, SUBLANES, tt), lambda i, j: (i, 0, j))],
        out_shape=[jax.ShapeDtypeStruct((b, t, GDN_WIDTH), F32),
                   jax.ShapeDtypeStruct((b, t, GDN_WIDTH), F32),
                   jax.ShapeDtypeStruct((b, SUBLANES, t), F32)],
        scratch_shapes=[pltpu.VMEM((1, LANES), F32)],
        compiler_params=_cparams("parallel", "arbitrary"),
        name="gates",
    )(small, a_row, dtb_row, fb_row)


def _unit_lower_inverse(low):
    n = low.shape[0]
    r = lax.broadcasted_iota(jnp.int32, (n, n), 0)
    c = lax.broadcasted_iota(jnp.int32, (n, n), 1)
    eye = (r == c).astype(F32)
    d = jnp.where(r // 16 == c // 16, low, 0.0)
    d2 = _dot_f32(d, d)
    p = eye - d
    p = p + _dot_f32(p, d2)
    d4 = _dot_f32(d2, d2)
    p = p + _dot_f32(p, d4)
    d8 = _dot_f32(d4, d4)
    x = p + _dot_f32(p, d8)
    for blk in (32, 64):
        off = jnp.where((r // blk == c // blk) & (r // (blk // 2) != c // (blk // 2)), low, 0.0)
        x = x - _dot_f32(_dot_f32(x, off), x)
    return x


def _gdn_kernel(q_ref, k_ref, v_ref, wq_ref, wk_ref, wv_ref, g_ref, beta_ref, gate_ref, nw_ref, ms_ref,
                o_ref, s_ref, cq_ref, ck_ref, cv_ref, buf_ref):
    t = pl.program_id(2)
    blk = q_ref.shape[1]
    halo = SUBLANES

    @pl.when(t == 0)
    def _():
        s_ref[...] = jnp.zeros_like(s_ref)
        cq_ref[...] = jnp.zeros_like(cq_ref)
        ck_ref[...] = jnp.zeros_like(ck_ref)
        cv_ref[...] = jnp.zeros_like(cv_ref)

    def short_conv(x_ref, w_ref, carry_ref):
        x = x_ref[0]
        buf_ref[0:halo, :] = carry_ref[...]
        buf_ref[halo:halo + blk, :] = x
        carry_ref[...] = x[blk - halo:blk, :]
        acc = jnp.zeros((blk, PAIR), F32)
        for j in range(GDN_SHORT_CONV):
            lo = halo - (GDN_SHORT_CONV - 1) + j
            acc = acc + w_ref[j:j + 1, :] * buf_ref[lo:lo + blk, :]
        return _silu(acc)

    ones = _head_ones()

    def l2n(x):
        return x * lax.rsqrt(_dot_f32(x * x, ones) + NORM_EPS)

    q_all = l2n(short_conv(q_ref, wq_ref, cq_ref)) * HEAD_DIM ** -0.5
    k_all = l2n(short_conv(k_ref, wk_ref, ck_ref))
    v_all = short_conv(v_ref, wv_ref, cv_ref)

    lane = lax.broadcasted_iota(jnp.int32, (1, PAIR), 1)
    head0 = lane < HEAD_DIM
    n2 = 2 * GDN_CHUNK
    r = lax.broadcasted_iota(jnp.int32, (n2, n2), 0)
    c = lax.broadcasted_iota(jnp.int32, (n2, n2), 1)
    same = (r // GDN_CHUNK) == (c // GDN_CHUNK)
    causal = same & (c <= r)
    strict = same & (c < r)

    def stack(x):
        return jnp.concatenate([jnp.where(head0, x, 0.0), jnp.where(head0, 0.0, x)], axis=0)

    state = s_ref[...]
    for ci in range(blk // GDN_CHUNK):
        sl = slice(ci * GDN_CHUNK, (ci + 1) * GDN_CHUNK)
        qn, kn, vv = q_all[sl], k_all[sl], v_all[sl]
        gc = g_ref[0, sl, :]
        bt = beta_ref[0, sl, :]
        eg = jnp.exp(gc)
        g_last = gc[GDN_CHUNK - 1:GDN_CHUNK, :]
        kb = kn * bt
        k2 = stack(kn)
        gcb = jnp.concatenate([jnp.broadcast_to(gc[:, 0:1], (GDN_CHUNK, PAIR)),
                               jnp.broadcast_to(gc[:, HEAD_DIM:HEAD_DIM + 1], (GDN_CHUNK, PAIR))], axis=0)
        diff = gcb - gcb.T
        decay = jnp.where(causal, jnp.exp(jnp.where(causal, diff, 0.0)), 0.0)
        low = jnp.where(strict, _dot_nt(stack(kb), k2) * decay, 0.0)
        a_intra = _dot_nt(stack(qn), k2) * decay
        t_inv = _unit_lower_inverse(low)
        u = _dot(t_inv, stack(vv * bt))
        w = _dot(t_inv, stack(kb * eg))
        v_new = u - _dot(w, state)
        o2 = _dot(stack(qn * eg), state) + _dot(a_intra, v_new)
        o = o2[0:GDN_CHUNK] + o2[GDN_CHUNK:n2]
        ke = stack(kn * jnp.exp(g_last - gc))
        state = state * jnp.exp(g_last) + _dot(ke.T, v_new)
        ms = _dot_f32(o * o, ones) * (1.0 / HEAD_DIM)
        on = o * lax.rsqrt(ms + NORM_EPS) * nw_ref[...]
        o_ref[0, sl, :] = (on * _silu(gate_ref[0, sl, :]) * ms_ref[...]).astype(o_ref.dtype)
    s_ref[...] = state


def _gdn(qkv, conv_w, g, beta, gate, norm_w2, mscale, blk):
    b, t, _ = qkv.shape
    tile = lambda off: pl.BlockSpec((1, blk, PAIR), lambda i, p, j, off=off: (i, j, p + off))
    wtile = lambda off: pl.BlockSpec((GDN_SHORT_CONV, PAIR), lambda i, p, j, off=off: (0, p + off))
    return pl.pallas_call(
        _gdn_kernel,
        grid=(b, N_PAIRS, t // blk),
        in_specs=[tile(0), tile(N_PAIRS), tile(2 * N_PAIRS),
                  wtile(0), wtile(N_PAIRS), wtile(2 * N_PAIRS),
                  tile(0), tile(0), tile(0),
                  pl.BlockSpec((1, PAIR), lambda i, p, j: (0, 0)),
                  pl.BlockSpec((1, PAIR), lambda i, p, j: (0, p))],
        out_specs=tile(0),
        out_shape=jax.ShapeDtypeStruct((b, t, GDN_WIDTH), MXU_DTYPE),
        scratch_shapes=[pltpu.VMEM((PAIR, PAIR), F32),
                        pltpu.VMEM((SUBLANES, PAIR), F32), pltpu.VMEM((SUBLANES, PAIR), F32),
                        pltpu.VMEM((SUBLANES, PAIR), F32),
                        pltpu.VMEM((blk + SUBLANES, PAIR), F32)],
        compiler_params=_cparams("parallel", "parallel", "arbitrary"),
        name="gdn",
    )(qkv, qkv, qkv, conv_w, conv_w, conv_w, g, beta, gate, norm_w2, mscale)


CONV_HALO = 32


def _conv_kernel(glu_ref, w_ref, b_ref, lg_ref, lb_ref, ms_ref, o_ref, buf_ref):
    t = pl.program_id(1)
    tt = glu_ref.shape[1]

    @pl.when(t == 0)
    def _():
        buf_ref[0:CONV_HALO, :] = jnp.zeros((CONV_HALO, CONV_WIDTH), F32)

    @pl.when(t > 0)
    def _():
        buf_ref[0:CONV_HALO, :] = buf_ref[tt:tt + CONV_HALO, :]

    x = glu_ref[0]
    buf_ref[CONV_HALO:CONV_HALO + tt, :] = x[:, 0:CONV_WIDTH] * _sigmoid(x[:, CONV_WIDTH:2 * CONV_WIDTH])
    acc = jnp.zeros((tt, CONV_WIDTH), F32) + b_ref[...]
    for j in range(CONV_KERNEL):
        lo = CONV_HALO - (CONV_KERNEL - 1) + j
        acc = acc + w_ref[j:j + 1, :] * buf_ref[lo:lo + tt, :]
    y = _silu(_layer_norm_rows(acc, lg_ref[...], lb_ref[...]))
    o_ref[0] = (y * ms_ref[...]).astype(o_ref.dtype)


def _conformer_conv(glu, w, bias, ln_g, ln_b, mscale, tt):
    b, t, _ = glu.shape
    vec = pl.BlockSpec((1, CONV_WIDTH), lambda i, j: (0, 0))
    return pl.pallas_call(
        _conv_kernel,
        grid=(b, t // tt),
        in_specs=[pl.BlockSpec((1, tt, 2 * CONV_WIDTH), lambda i, j: (i, j, 0)),
                  pl.BlockSpec((CONV_KERNEL, CONV_WIDTH), lambda i, j: (0, 0)),
                  vec, vec, vec, vec],
        out_specs=pl.BlockSpec((1, tt, CONV_WIDTH), lambda i, j: (i, j, 0)),
        out_shape=jax.ShapeDtypeStruct((b, t, CONV_WIDTH), MXU_DTYPE),
        scratch_shapes=[pltpu.VMEM((tt + CONV_HALO, CONV_WIDTH), F32)],
        compiler_params=_cparams("parallel", "arbitrary"),
        name="conformer_conv",
    )(glu, w, bias, ln_g, ln_b, mscale)


def _fox_kernel(q_ref, k_ref, v_ref, ct_ref, ms_ref, o_ref, *, tq, tk):
    i = pl.program_id(2)
    p = pl.program_id(1)
    lane = lax.broadcasted_iota(jnp.int32, (1, PAIR), 1)
    head0 = lane < HEAD_DIM
    q = q_ref[0]
    zero = jnp.zeros_like(q)
    qh = (jnp.where(head0, q, zero), jnp.where(head0, zero, q))
    n_kv = (i + 1) * (tq // tk)
    rows = i * tq + lax.broadcasted_iota(jnp.int32, (tq, tk), 0)
    cols0 = lax.broadcasted_iota(jnp.int32, (tq, tk), 1)

    def body(j, carry):
        start = pl.multiple_of(j * tk, tk)
        kb = k_ref[0, pl.ds(start, tk), :]
        vb = v_ref[0, pl.ds(start, tk), :]
        valid = rows >= cols0 + j * tk
        out = []
        for h in range(2):
            m_old, l_old, acc_old = carry[h]
            ck = ct_ref[0, h, :, pl.ds(start, tk)]
            s = lax.dot_general(qh[h], kb, (((1,), (1,)), ((), ())), preferred_element_type=F32) - ck
            s = jnp.where(valid, s, -jnp.inf)
            m_new = jnp.maximum(m_old, jnp.max(s, axis=-1, keepdims=True))
            alpha = jnp.exp(m_old - m_new)
            pr = jnp.exp(s - m_new)
            l_new = alpha * l_old + jnp.sum(pr, axis=-1, keepdims=True)
            acc_new = alpha * acc_old + jnp.dot(pr.astype(vb.dtype), vb, preferred_element_type=F32)
            out.append((m_new, l_new, acc_new))
        return tuple(out)

    init = tuple((jnp.full((tq, 1), -jnp.inf, F32), jnp.zeros((tq, 1), F32), jnp.zeros((tq, PAIR), F32))
                 for _ in range(2))
    (m0, l0, a0), (m1, l1, a1) = lax.fori_loop(0, n_kv, body, init)
    o = jnp.where(head0, a0 / l0, a1 / l1)
    o_ref[0] = (o * ms_ref[...]).astype(o_ref.dtype)


def _fox(fq, fk, fv, ct, mscale, tq, tk):
    b, t, _ = fq.shape
    ct4 = ct.reshape(b, SUBLANES // 2, 2, 1, t)[:, :N_PAIRS]
    ct4 = ct4.reshape(b * N_PAIRS, 2, 1, t)
    kern = functools.partial(_fox_kernel, tq=tq, tk=tk)
    return pl.pallas_call(
        kern,
        grid=(b, N_PAIRS, t // tq),
        in_specs=[pl.BlockSpec((1, tq, PAIR), lambda bi, p, i: (bi, i, p)),
                  pl.BlockSpec((1, t, PAIR), lambda bi, p, i: (bi, 0, p)),
                  pl.BlockSpec((1, t, PAIR), lambda bi, p, i: (bi, 0, p)),
                  pl.BlockSpec((1, 2, 1, t), lambda bi, p, i: (bi * N_PAIRS + p, 0, 0, 0)),
                  pl.BlockSpec((1, PAIR), lambda bi, p, i: (0, p))],
        out_specs=pl.BlockSpec((1, tq, PAIR), lambda bi, p, i: (bi, i, p)),
        out_shape=jax.ShapeDtypeStruct((b, t, FOX_WIDTH), MXU_DTYPE),
        compiler_params=_cparams("parallel", "parallel", "arbitrary"),
        name="fox_attention",
    )(fq, fk, fv, ct4, mscale)


def _out_proj_kernel(oa_ref, ob_ref, oc_ref, x_ref, w_ref, g_ref, b_ref, o_ref):
    mix = jnp.dot(oa_ref[...], w_ref[0:GDN_WIDTH, :], preferred_element_type=F32)
    mix = mix + jnp.dot(ob_ref[...], w_ref[GDN_WIDTH:GDN_WIDTH + CONV_WIDTH, :], preferred_element_type=F32)
    mix = mix + jnp.dot(oc_ref[...], w_ref[GDN_WIDTH + CONV_WIDTH:D_MODEL, :], preferred_element_type=F32)
    o_ref[...] = _layer_norm_rows(DEEPNORM_ALPHA * x_ref[...] + mix, g_ref[...], b_ref[...])


def _out_proj(oa, ob, oc, xf, w, g, bvec, tm):
    n = xf.shape[0]
    row = lambda i: (i, 0)
    const = lambda i: (0, 0)
    return pl.pallas_call(
        _out_proj_kernel,
        grid=(n // tm,),
        in_specs=[pl.BlockSpec((tm, GDN_WIDTH), row), pl.BlockSpec((tm, CONV_WIDTH), row),
                  pl.BlockSpec((tm, FOX_WIDTH), row), pl.BlockSpec((tm, D_MODEL), row),
                  pl.BlockSpec((D_MODEL, D_MODEL), const),
                  pl.BlockSpec((1, D_MODEL), const), pl.BlockSpec((1, D_MODEL), const)],
        out_specs=pl.BlockSpec((tm, D_MODEL), row),
        out_shape=jax.ShapeDtypeStruct((n, D_MODEL), F32),
        compiler_params=_cparams("parallel"),
        name="out_proj_ln",
    )(oa, ob, oc, xf, w, g, bvec)


def _ffn_kernel(x_ref, w1_ref, w3_ref, w2_ref, g_ref, b_ref, o_ref, xb_ref, acc_ref):
    f = pl.program_id(1)

    @pl.when(f == 0)
    def _():
        xb_ref[...] = x_ref[...].astype(xb_ref.dtype)
        acc_ref[...] = jnp.zeros_like(acc_ref)

    xb = xb_ref[...]
    h = _silu(jnp.dot(xb, w1_ref[...], preferred_element_type=F32)) * jnp.dot(
        xb, w3_ref[...], preferred_element_type=F32)
    acc_ref[...] += jnp.dot(h.astype(w2_ref.dtype), w2_ref[...], preferred_element_type=F32)

    @pl.when(f == pl.num_programs(1) - 1)
    def _():
        o_ref[...] = _layer_norm_rows(DEEPNORM_ALPHA * x_ref[...] + acc_ref[...], g_ref[...], b_ref[...])


def _dense_ffn(xf, w1, w3, w2, g, bvec, tm, tf):
    n = xf.shape[0]
    ff = w1.shape[1]
    return pl.pallas_call(
        _ffn_kernel,
        grid=(n // tm, ff // tf),
        in_specs=[pl.BlockSpec((tm, D_MODEL), lambda i, f: (i, 0)),
                  pl.BlockSpec((D_MODEL, tf), lambda i, f: (0, f)),
                  pl.BlockSpec((D_MODEL, tf), lambda i, f: (0, f)),
                  pl.BlockSpec((tf, D_MODEL), lambda i, f: (f, 0)),
                  pl.BlockSpec((1, D_MODEL), lambda i, f: (0, 0)),
                  pl.BlockSpec((1, D_MODEL), lambda i, f: (0, 0))],
        out_specs=pl.BlockSpec((tm, D_MODEL), lambda i, f: (i, 0)),
        out_shape=jax.ShapeDtypeStruct((n, D_MODEL), F32),
        scratch_shapes=[pltpu.VMEM((tm, D_MODEL), MXU_DTYPE), pltpu.VMEM((tm, D_MODEL), F32)],
        compiler_params=_cparams("parallel", "arbitrary"),
        name="dense_ffn_ln",
    )(xf, w1, w3, w2, g, bvec)


def _router_kernel(x_ref, wr_ref, info_ref, cnt_ref, run_ref):
    i = pl.program_id(0)
    tm = x_ref.shape[0]

    @pl.when(i == 0)
    def _():
        run_ref[...] = jnp.zeros_like(run_ref)

    logits = jnp.dot(x_ref[...].astype(MXU_DTYPE), wr_ref[...], preferred_element_type=F32)
    lane = lax.broadcasted_iota(jnp.int32, (tm, LANES), 1)
    logits = jnp.where(lane < N_EXPERTS, logits, -jnp.inf)
    m1 = jnp.max(logits, axis=-1, keepdims=True)
    e1 = jnp.min(jnp.where(logits == m1, lane, LANES), axis=-1, keepdims=True)
    rest = jnp.where(lane == e1, -jnp.inf, logits)
    m2 = jnp.max(rest, axis=-1, keepdims=True)
    e2 = jnp.min(jnp.where(rest == m2, lane, LANES), axis=-1, keepdims=True)
    z = jnp.exp(m2 - m1)
    g1 = 1.0 / (1.0 + z)
    g2 = z / (1.0 + z)
    onehot = ((lane == e1) | (lane == e2)).astype(F32)
    r = lax.broadcasted_iota(jnp.int32, (tm, tm), 0)
    c = lax.broadcasted_iota(jnp.int32, (tm, tm), 1)
    before = _dot_f32((c < r).astype(F32), onehot) + run_ref[...]
    rank1 = jnp.sum(jnp.where(lane == e1, before, 0.0), axis=-1, keepdims=True)
    rank2 = jnp.sum(jnp.where(lane == e2, before, 0.0), axis=-1, keepdims=True)
    run_ref[...] = run_ref[...] + jnp.sum(onehot, axis=0, keepdims=True)
    cnt_ref[...] = run_ref[...]
    info = jnp.where(lane == 0, e1.astype(F32),
                     jnp.where(lane == 1, e2.astype(F32),
                               jnp.where(lane == 2, rank1,
                                         jnp.where(lane == 3, rank2,
                                                   jnp.where(lane == 4, g1, jnp.where(lane == 5, g2, 0.0))))))
    info_ref[...] = info


def _router(xf, wr_pad, tm):
    n = xf.shape[0]
    return pl.pallas_call(
        _router_kernel,
        grid=(n // tm,),
        in_specs=[pl.BlockSpec((tm, D_MODEL), lambda i: (i, 0)),
                  pl.BlockSpec((D_MODEL, LANES), lambda i: (0, 0))],
        out_specs=[pl.BlockSpec((tm, LANES), lambda i: (i, 0)),
                   pl.BlockSpec((1, LANES), lambda i: (0, 0))],
        out_shape=[jax.ShapeDtypeStruct((n, LANES), F32), jax.ShapeDtypeStruct((1, LANES), F32)],
        scratch_shapes=[pltpu.VMEM((1, LANES), F32)],
        compiler_params=_cparams("arbitrary"),
        name="moe_router",
    )(xf, wr_pad)


def _dispatch_kernel(d1_ref, d2_ref, x_ref, zeros_ref, xs_ref, sem):
    del zeros_ref
    tm = x_ref.shape[0]

    def row_copy(r, dst):
        return pltpu.make_async_copy(x_ref.at[pl.ds(r, 1), :], xs_ref.at[pl.ds(dst, 1), :], sem)

    def issue(r, carry):
        row_copy(r, d1_ref[0, 0, r]).start()
        row_copy(r, d2_ref[0, 0, r]).start()
        return carry

    lax.fori_loop(0, tm, issue, 0)

    def drain(r, carry):
        row_copy(r, d1_ref[0, 0, r]).wait()
        row_copy(r, d2_ref[0, 0, r]).wait()
        return carry

    lax.fori_loop(0, tm, drain, 0)


def _dispatch(xf, d1, d2, rows, tm):
    n = xf.shape[0]
    idx = lambda a: a.reshape(n // tm, 1, tm)
    zeros = jnp.zeros((rows, D_MODEL), xf.dtype)
    smem = lambda: pl.BlockSpec((1, 1, tm), lambda i: (i, 0, 0), memory_space=pltpu.SMEM)
    return pl.pallas_call(
        _dispatch_kernel,
        grid=(n // tm,),
        in_specs=[smem(), smem(),
                  pl.BlockSpec((tm, D_MODEL), lambda i: (i, 0)),
                  pl.BlockSpec(memory_space=pl.ANY)],
        out_specs=pl.BlockSpec(memory_space=pl.ANY),
        out_shape=jax.ShapeDtypeStruct((rows, D_MODEL), xf.dtype),
        scratch_shapes=[pltpu.SemaphoreType.DMA(())],
        input_output_aliases={3: 0},
        compiler_params=_cparams("arbitrary"),
        name="moe_dispatch",
    )(idx(d1), idx(d2), xf, zeros)


def _expert_kernel(te_ref, nu_ref, xs_ref, w1_ref, w3_ref, w2_ref, ys_ref, xb_ref, acc_ref):
    del te_ref
    i = pl.program_id(0)
    f = pl.program_id(1)
    used = i < nu_ref[0]

    @pl.when(f == 0)
    def _():
        xb_ref[...] = xs_ref[...].astype(xb_ref.dtype)
        acc_ref[...] = jnp.zeros_like(acc_ref)

    @pl.when(used)
    def _():
        xb = xb_ref[...]
        h = _silu(jnp.dot(xb, w1_ref[0], preferred_element_type=F32)) * jnp.dot(
            xb, w3_ref[0], preferred_element_type=F32)
        acc_ref[...] += jnp.dot(h.astype(w2_ref.dtype), w2_ref[0], preferred_element_type=F32)

    @pl.when(f == pl.num_programs(1) - 1)
    def _():
        ys_ref[...] = acc_ref[...]


def _experts(xs, tile_expert, n_used, w1, w3, w2, tm, tf):
    rows = xs.shape[0]
    ff = w1.shape[2]
    n_tiles = rows // tm

    def x_map(i, f, te, nu):
        return (jnp.minimum(i, nu[0] - 1), 0)

    def w13_map(i, f, te, nu):
        return (te[i], 0, jnp.where(i < nu[0], f, ff // tf - 1))

    def w2_map(i, f, te, nu):
        return (te[i], jnp.where(i < nu[0], f, ff // tf - 1), 0)

    grid_spec = pltpu.PrefetchScalarGridSpec(
        num_scalar_prefetch=2,
        grid=(n_tiles, ff // tf),
        in_specs=[pl.BlockSpec((tm, D_MODEL), x_map),
                  pl.BlockSpec((1, D_MODEL, tf), w13_map),
                  pl.BlockSpec((1, D_MODEL, tf), w13_map),
                  pl.BlockSpec((1, tf, D_MODEL), w2_map)],
        out_specs=pl.BlockSpec((tm, D_MODEL), lambda i, f, te, nu: (i, 0)),
        scratch_shapes=[pltpu.VMEM((tm, D_MODEL), MXU_DTYPE), pltpu.VMEM((tm, D_MODEL), F32)],
    )
    return pl.pallas_call(
        _expert_kernel,
        grid_spec=grid_spec,
        out_shape=jax.ShapeDtypeStruct((rows, D_MODEL), F32),
        compiler_params=_cparams("arbitrary", "arbitrary"),
        name="moe_experts",
    )(tile_expert, n_used, xs, w1, w3, w2)


def _combine_kernel(d1_ref, d2_ref, x_ref, info_ref, ys_ref, g_ref, b_ref, o_ref, y1_ref, y2_ref, sem):
    tm = x_ref.shape[0]

    def row_copy(src, r, buf):
        return pltpu.make_async_copy(ys_ref.at[pl.ds(src, 1), :], buf.at[pl.ds(r, 1), :], sem)

    def issue(r, carry):
        row_copy(d1_ref[0, 0, r], r, y1_ref).start()
        row_copy(d2_ref[0, 0, r], r, y2_ref).start()
        return carry

    lax.fori_loop(0, tm, issue, 0)

    def drain(r, carry):
        row_copy(d1_ref[0, 0, r], r, y1_ref).wait()
        row_copy(d2_ref[0, 0, r], r, y2_ref).wait()
        return carry

    lax.fori_loop(0, tm, drain, 0)
    info = info_ref[...]
    ff = info[:, 4:5] * y1_ref[...] + info[:, 5:6] * y2_ref[...]
    o_ref[...] = _layer_norm_rows(DEEPNORM_ALPHA * x_ref[...] + ff, g_ref[...], b_ref[...])


def _combine(xf, info, ys, d1, d2, g, bvec, tm):
    n = xf.shape[0]
    idx = lambda a: a.reshape(n // tm, 1, tm)
    smem = lambda: pl.BlockSpec((1, 1, tm), lambda i: (i, 0, 0), memory_space=pltpu.SMEM)
    return pl.pallas_call(
        _combine_kernel,
        grid=(n // tm,),
        in_specs=[smem(), smem(),
                  pl.BlockSpec((tm, D_MODEL), lambda i: (i, 0)),
                  pl.BlockSpec((tm, LANES), lambda i: (i, 0)),
                  pl.BlockSpec(memory_space=pl.ANY),
                  pl.BlockSpec((1, D_MODEL), lambda i: (0, 0)),
                  pl.BlockSpec((1, D_MODEL), lambda i: (0, 0))],
        out_specs=pl.BlockSpec((tm, D_MODEL), lambda i: (i, 0)),
        out_shape=jax.ShapeDtypeStruct((n, D_MODEL), F32),
        scratch_shapes=[pltpu.VMEM((tm, D_MODEL), F32), pltpu.VMEM((tm, D_MODEL), F32),
                        pltpu.SemaphoreType.DMA(())],
        compiler_params=_cparams("arbitrary"),
        name="moe_combine_ln",
    )(idx(d1), idx(d2), xf, info, ys, g, bvec)


MOE_TILE = 512


def _moe_ffn(xf, w_router, w1, w3, w2, g, bvec, tm_tok, tf):
    n = xf.shape[0]
    wr_pad = jnp.zeros((D_MODEL, LANES), MXU_DTYPE).at[:, :N_EXPERTS].set(w_router.astype(MXU_DTYPE))
    info, counts = _router(xf, wr_pad, tm_tok)
    sizes = counts[0, :N_EXPERTS].astype(jnp.int32)
    tiles_per = (sizes + MOE_TILE - 1) // MOE_TILE
    tile_end = jnp.cumsum(tiles_per)
    seg_start = (tile_end - tiles_per) * MOE_TILE
    e1 = info[:, 0].astype(jnp.int32)
    e2 = info[:, 1].astype(jnp.int32)
    d1 = seg_start[e1] + info[:, 2].astype(jnp.int32)
    d2 = seg_start[e2] + info[:, 3].astype(jnp.int32)
    n_tiles = (n * TOP_K) // MOE_TILE + N_EXPERTS
    rows = n_tiles * MOE_TILE
    tile_expert = jnp.minimum(jnp.searchsorted(tile_end, jnp.arange(n_tiles, dtype=jnp.int32), side='right'),
                              N_EXPERTS - 1).astype(jnp.int32)
    n_used = tile_end[N_EXPERTS - 1:].astype(jnp.int32)
    xs = _dispatch(xf, d1, d2, rows, tm_tok)
    ys = _experts(xs, tile_expert, n_used, w1, w3, w2, MOE_TILE, tf)
    return _combine(xf, info, ys, d1, d2, g, bvec, tm_tok)


def _pack_in_proj(w_in):
    cuts = np.cumsum([0, GDN_WIDTH, GDN_WIDTH, GDN_WIDTH, GDN_HEADS, GDN_HEADS, GDN_WIDTH,
                      2 * CONV_WIDTH, FOX_WIDTH, FOX_WIDTH, FOX_WIDTH, FOX_HEADS])
    seg = lambda i: w_in[:, cuts[i]:cuts[i + 1]]
    w_main = jnp.concatenate([seg(0), seg(1), seg(2), seg(5), seg(6), seg(7), seg(8), seg(9)], axis=1)
    pad = jnp.zeros((D_MODEL, SMALL_COLS - 2 * GDN_HEADS - FOX_HEADS), w_in.dtype)
    w_small = jnp.concatenate([seg(3), seg(4), seg(10), pad], axis=1)
    return w_main.astype(MXU_DTYPE), w_small


def _lane_row(vals, offset):
    return jnp.zeros((1, LANES), F32).at[0, offset:offset + vals.shape[0]].set(vals.astype(F32))


def _mixer(xf, b, t, w_in, mix_scale, w_out, gdn_conv_w, gdn_a_log, gdn_dt_bias, gdn_norm_w,
           cnv_dw_w, cnv_dw_b, cnv_ln_g, cnv_ln_b, fox_f_bias, ln_g, ln_b, tiles):
    w_main, w_small = _pack_in_proj(w_in)
    qkv, gate, glu, fq, fk, fv, small = _in_proj(xf, w_main, w_small, tiles["tm"])
    r3 = lambda a: a.reshape(b, t, a.shape[-1])
    g_cum, beta, ct = _gates(r3(small), _lane_row(gdn_a_log, 0), _lane_row(gdn_dt_bias, 0),
                             _lane_row(fox_f_bias, 2 * GDN_HEADS), tiles["tt"])
    ms = mix_scale.reshape(1, D_MODEL).astype(F32)
    norm_w2 = jnp.tile(gdn_norm_w.astype(F32), 2).reshape(1, PAIR)
    o_a = _gdn(r3(qkv), gdn_conv_w, g_cum, beta, r3(gate), norm_w2, ms[:, :GDN_WIDTH], tiles["gdn_blk"])
    row = lambda v: v.reshape(1, -1).astype(F32)
    o_b = _conformer_conv(r3(glu), cnv_dw_w, row(cnv_dw_b), row(cnv_ln_g), row(cnv_ln_b),
                          ms[:, GDN_WIDTH:GDN_WIDTH + CONV_WIDTH], tiles["tt"])
    o_c = _fox(r3(fq), r3(fk), r3(fv), ct, ms[:, GDN_WIDTH + CONV_WIDTH:], tiles["tq"], tiles["tk"])
    flat = lambda a: a.reshape(b * t, a.shape[-1])
    return _out_proj(flat(o_a), flat(o_b), flat(o_c), xf, w_out.astype(MXU_DTYPE), row(ln_g), row(ln_b),
                     tiles["tm"])


def _tiles(t):
    return dict(tm=min(512, t), tt=min(512, t), gdn_blk=min(256, t), tq=min(512, t), tk=min(512, t),
                tf_dense=1408, tf_moe=896)


def kernel(x, w_in, mix_scale, w_out, gdn_conv_w, gdn_a_log, gdn_dt_bias, gdn_norm_w, cnv_dw_w, cnv_dw_b,
           cnv_ln_g, cnv_ln_b, fox_f_bias, ln_mix_g, ln_mix_b, ln_ffn_g, ln_ffn_b, ffn_w1, ffn_w3, ffn_w2,
           moe_router, moe_w1, moe_w3, moe_w2):
    b, t, d = x.shape
    tiles = _tiles(t)
    row = lambda v: v.reshape(1, -1).astype(F32)
    xf = x.reshape(b * t, d)
    for l in range(DEPTH):
        xf = _mixer(xf, b, t, w_in[l], mix_scale[l], w_out[l], gdn_conv_w[l], gdn_a_log[l], gdn_dt_bias[l],
                    gdn_norm_w[l], cnv_dw_w[l], cnv_dw_b[l], cnv_ln_g[l], cnv_ln_b[l], fox_f_bias[l],
                    ln_mix_g[l], ln_mix_b[l], tiles)
        if l % 2 == 0:
            e = l // 2
            xf = _dense_ffn(xf, ffn_w1[e].astype(MXU_DTYPE), ffn_w3[e].astype(MXU_DTYPE),
                            ffn_w2[e].astype(MXU_DTYPE), row(ln_ffn_g[l]), row(ln_ffn_b[l]),
                            tiles["tm"], tiles["tf_dense"])
        else:
            e = l // 2
            xf = _moe_ffn(xf, moe_router[e], moe_w1[e].astype(MXU_DTYPE), moe_w3[e].astype(MXU_DTYPE),
                          moe_w2[e].astype(MXU_DTYPE), row(ln_ffn_g[l]), row(ln_ffn_b[l]),
                          tiles["tm"], tiles["tf_moe"])
    return xf.reshape(b, t, d)
```

```python
import functools

import jax
import jax.numpy as jnp
import numpy as np
from jax import lax
from jax.experimental import pallas as pl
from jax.experimental.pallas import tpu as pltpu

D_MODEL = 1024
DEPTH = 2
HEAD_DIM = 64
GDN_WIDTH = 384
CONV_WIDTH = 256
FOX_WIDTH = 384
GDN_HEADS = 6
FOX_HEADS = 6
GDN_SHORT_CONV = 4
GDN_CHUNK = 64
CONV_KERNEL = 31
FFN_DENSE = 2816
N_EXPERTS = 8
TOP_K = 2
FFN_EXPERT = 3584
DEEPNORM_ALPHA = (2 * DEPTH) ** 0.25
LN_EPS = 1e-5
NORM_EPS = 1e-6

LANES = 128
SUBLANES = 8
PAIR = 2 * HEAD_DIM
N_PAIRS = GDN_WIDTH // PAIR
SMALL_COLS = LANES

_C_QKV = 0
_C_GATE = _C_QKV + 3 * GDN_WIDTH
_C_GLU = _C_GATE + GDN_WIDTH
_C_FQ = _C_GLU + 2 * CONV_WIDTH
_C_FK = _C_FQ + FOX_WIDTH
_C_FV = _C_FK + FOX_WIDTH
_C_END = _C_FV + FOX_WIDTH

MXU_DTYPE = jnp.bfloat16
F32 = jnp.float32
HIGHEST = lax.Precision.HIGHEST

VMEM_LIMIT = 56 * 1024 * 1024


def _cparams(*sem):
    return pltpu.CompilerParams(dimension_semantics=sem, vmem_limit_bytes=VMEM_LIMIT)


def _dot(a, b):
    return jnp.dot(a.astype(MXU_DTYPE), b.astype(MXU_DTYPE), preferred_element_type=F32)


def _dot_nt(a, b):
    return lax.dot_general(a.astype(MXU_DTYPE), b.astype(MXU_DTYPE),
                           (((1,), (1,)), ((), ())), preferred_element_type=F32)


def _dot_f32(a, b):
    return jnp.dot(a, b, preferred_element_type=F32, precision=HIGHEST)


def _split(a):
    hi = a.astype(jnp.bfloat16)
    return hi, (a - hi.astype(F32)).astype(jnp.bfloat16)


def _split3(a):
    hi = a.astype(jnp.bfloat16)
    rest = a - hi.astype(F32)
    mid = rest.astype(jnp.bfloat16)
    lo = (rest - mid.astype(F32)).astype(jnp.bfloat16)
    return hi, mid, lo


def _dot_split(a, b):
    ah, al = _split(a)
    bh, bl = _split(b)
    d = lambda x, y: jnp.dot(x, y, preferred_element_type=F32)
    return d(ah, bh) + d(al, bh) + d(ah, bl)


def _dot_ones(a, ones_b16):
    ah, al = _split(a)
    d = lambda x: jnp.dot(x, ones_b16, preferred_element_type=F32)
    return d(ah) + d(al)


def _sel_rhs(a, sel_b16):
    return sum(jnp.dot(t, sel_b16, preferred_element_type=F32) for t in _split3(a))


def _sel_lhs(sel_b16, b):
    return sum(jnp.dot(sel_b16, t, preferred_element_type=F32) for t in _split3(b))


def _sigmoid(x):
    return 1.0 / (1.0 + jnp.exp(-x))


def _silu(x):
    return x * _sigmoid(x)


def _softplus(x):
    return jnp.maximum(x, 0.0) + jnp.log1p(jnp.exp(-jnp.abs(x)))


def _log_sigmoid(x):
    return -_softplus(-x)


def _layer_norm_rows(y, g, b):
    mu = jnp.mean(y, axis=-1, keepdims=True)
    d = y - mu
    var = jnp.mean(d * d, axis=-1, keepdims=True)
    return d * lax.rsqrt(var + LN_EPS) * g + b


def _head_ones():
    r = lax.broadcasted_iota(jnp.int32, (PAIR, PAIR), 0) // HEAD_DIM
    c = lax.broadcasted_iota(jnp.int32, (PAIR, PAIR), 1) // HEAD_DIM
    return (r == c).astype(F32)


def _in_proj_kernel(x_ref, w_ref, ws_ref, qkv_ref, gate_ref, glu_ref, fq_ref, fk_ref, fv_ref, small_ref):
    x = x_ref[...]
    xb = x.astype(MXU_DTYPE)

    def mm(lo, hi):
        return jnp.dot(xb, w_ref[:, lo:hi], preferred_element_type=F32)

    qkv_ref[...] = mm(_C_QKV, _C_GATE)
    gate_ref[...] = mm(_C_GATE, _C_GLU)
    glu_ref[...] = mm(_C_GLU, _C_FQ)
    fq_ref[...] = (mm(_C_FQ, _C_FK) * HEAD_DIM ** -0.5).astype(fq_ref.dtype)
    fk_ref[...] = mm(_C_FK, _C_FV).astype(fk_ref.dtype)
    fv_ref[...] = mm(_C_FV, _C_END).astype(fv_ref.dtype)
    x_lo = (x - xb.astype(F32)).astype(MXU_DTYPE)
    hh_hl = jnp.dot(xb, ws_ref[...], preferred_element_type=F32)
    lh = jnp.dot(x_lo, ws_ref[:, 0:SMALL_COLS], preferred_element_type=F32)
    small_ref[...] = hh_hl[:, 0:SMALL_COLS] + hh_hl[:, SMALL_COLS:2 * SMALL_COLS] + lh


def _in_proj(xf, w_main, w_small, tm):
    n = xf.shape[0]
    row = lambda i: (i, 0)
    const = lambda i: (0, 0)
    outs = [
        jax.ShapeDtypeStruct((n, 3 * GDN_WIDTH), F32),
        jax.ShapeDtypeStruct((n, GDN_WIDTH), F32),
        jax.ShapeDtypeStruct((n, 2 * CONV_WIDTH), F32),
        jax.ShapeDtypeStruct((n, FOX_WIDTH), MXU_DTYPE),
        jax.ShapeDtypeStruct((n, FOX_WIDTH), MXU_DTYPE),
        jax.ShapeDtypeStruct((n, FOX_WIDTH), MXU_DTYPE),
        jax.ShapeDtypeStruct((n, SMALL_COLS), F32),
    ]
    return pl.pallas_call(
        _in_proj_kernel,
        grid=(n // tm,),
        in_specs=[pl.BlockSpec((tm, D_MODEL), row),
                  pl.BlockSpec((D_MODEL, _C_END), const),
                  pl.BlockSpec((D_MODEL, 2 * SMALL_COLS), const)],
        out_specs=[pl.BlockSpec((tm, o.shape[1]), row) for o in outs],
        out_shape=outs,
        compiler_params=_cparams("parallel"),
        name="in_proj",
    )(xf, w_main, w_small)


AUG_BIAS_LANE = HEAD_DIM


def _gates_kernel(small_ref, a_ref, dtb_ref, fb_ref, fq_ref, fk_ref, fv_ref,
                  g_ref, beta_ref, qa_ref, ka_ref, va_ref, carry_ref):
    t = pl.program_id(1)
    tt = small_ref.shape[1]
    b16 = jnp.bfloat16

    @pl.when(t == 0)
    def _():
        carry_ref[...] = jnp.zeros_like(carry_ref)

    s = small_ref[0]
    log_decay = -jnp.exp(a_ref[...]) * _softplus(s + dtb_ref[...])
    beta = _sigmoid(s)
    log_f = _log_sigmoid(s + fb_ref[...])

    r = lax.broadcasted_iota(jnp.int32, (tt, tt), 0)
    c = lax.broadcasted_iota(jnp.int32, (tt, tt), 1)
    tri = (c <= r)
    g_cum = _sel_lhs((tri & (r // GDN_CHUNK == c // GDN_CHUNK)).astype(b16), log_decay)
    c_cum = _sel_lhs(tri.astype(b16), log_f) + carry_ref[...]
    carry_ref[...] = c_cum[tt - 1:tt, :]

    er = lax.broadcasted_iota(jnp.int32, (LANES, GDN_WIDTH), 0)
    ec = lax.broadcasted_iota(jnp.int32, (LANES, GDN_WIDTH), 1) // HEAD_DIM
    g_ref[0] = _sel_rhs(g_cum, (er == ec).astype(b16))
    beta_ref[0] = _sel_rhs(beta, (er == ec + GDN_HEADS).astype(b16))

    rr = lax.broadcasted_iota(jnp.int32, (LANES, LANES), 0)
    cc = lax.broadcasted_iota(jnp.int32, (LANES, LANES), 1)
    lane = lax.broadcasted_iota(jnp.int32, (1, LANES), 1)
    q_ones = ((lane >= AUG_BIAS_LANE) & (lane < AUG_BIAS_LANE + 3)).astype(F32)
    v_one = (lane == AUG_BIAS_LANE).astype(F32)
    neg_c = _split3(-c_cum)
    for h in range(FOX_HEADS):
        p, half = divmod(h, 2)
        lanes = slice(p * PAIR, (p + 1) * PAIR)
        sel = ((cc < HEAD_DIM) & (rr == cc + half * HEAD_DIM)).astype(b16)
        pick = lambda ref: jnp.dot(ref[0, :, lanes], sel, preferred_element_type=F32)
        bias = sum(jnp.dot(term, ((rr == 2 * GDN_HEADS + h) & (cc == AUG_BIAS_LANE + i)).astype(b16),
                           preferred_element_type=F32) for i, term in enumerate(neg_c))
        qa_ref[0, h] = (pick(fq_ref) + q_ones).astype(b16)
        ka_ref[0, h] = (pick(fk_ref) + bias).astype(b16)
        va_ref[0, h] = (pick(fv_ref) + v_one).astype(b16)


def _gates(small, a_row, dtb_row, fb_row, fq, fk, fv, tt):
    b, t, _ = small.shape
    row = pl.BlockSpec((1, LANES), lambda i, j: (0, 0))
    tile = lambda w: pl.BlockSpec((1, tt, w), lambda i, j: (i, j, 0))
    aug = pl.BlockSpec((1, FOX_HEADS, tt, LANES), lambda i, j: (i, 0, j, 0))
    aug_shape = jax.ShapeDtypeStruct((b, FOX_HEADS, t, LANES), jnp.bfloat16)
    return pl.pallas_call(
        _gates_kernel,
        grid=(b, t // tt),
        in_specs=[tile(SMALL_COLS), row, row, row, tile(FOX_WIDTH), tile(FOX_WIDTH), tile(FOX_WIDTH)],
        out_specs=[tile(GDN_WIDTH), tile(GDN_WIDTH), aug, aug, aug],
        out_shape=[jax.ShapeDtypeStruct((b, t, GDN_WIDTH), F32), jax.ShapeDtypeStruct((b, t, GDN_WIDTH), F32),
                   aug_shape, aug_shape, aug_shape],
        scratch_shapes=[pltpu.VMEM((1, LANES), F32)],
        compiler_params=_cparams("parallel", "arbitrary"),
        name="gates",
    )(small, a_row, dtb_row, fb_row, fq, fk, fv)


def _unit_lower_inverse_many(lows):
    n = lows[0].shape[0]
    r = lax.broadcasted_iota(jnp.int32, (n, n), 0)
    c = lax.broadcasted_iota(jnp.int32, (n, n), 1)
    eye = (r == c).astype(F32)
    diag16 = r // 16 == c // 16
    d = [jnp.where(diag16, low, 0.0) for low in lows]
    d2 = [_dot_split(x, x) for x in d]
    p = [eye - x for x in d]
    p = [x + _dot_split(x, y) for x, y in zip(p, d2)]
    d4 = [_dot_split(x, x) for x in d2]
    p = [x + _dot_split(x, y) for x, y in zip(p, d4)]
    d8 = [_dot_split(x, x) for x in d4]
    x = [a + _dot_split(a, y) for a, y in zip(p, d8)]
    for blk in (32, 64):
        sel = (r // blk == c // blk) & (r // (blk // 2) != c // (blk // 2))
        xo = [_dot_split(a, jnp.where(sel, low, 0.0)) for a, low in zip(x, lows)]
        x = [a - _dot_split(b, a) for a, b in zip(x, xo)]
    return x


def _gdn_kernel(qkv_ref, w_ref, g_ref, beta_ref, gate_ref, nw_ref, ms_ref, o_ref, s_ref, carry_ref, buf_ref):
    t = pl.program_id(1)
    blk = qkv_ref.shape[1]
    halo = SUBLANES
    nc = blk // GDN_CHUNK

    @pl.when(t == 0)
    def _():
        s_ref[...] = jnp.zeros_like(s_ref)
        carry_ref[...] = jnp.zeros_like(carry_ref)

    x = qkv_ref[0]
    buf_ref[0:halo, :] = carry_ref[...]
    buf_ref[halo:halo + blk, :] = x
    carry_ref[...] = x[blk - halo:blk, :]
    acc = jnp.zeros((blk, 3 * GDN_WIDTH), F32)
    for j in range(GDN_SHORT_CONV):
        lo = halo - (GDN_SHORT_CONV - 1) + j
        acc = acc + w_ref[j:j + 1, :] * buf_ref[lo:lo + blk, :]
    qkv = _silu(acc)

    ones = _head_ones().astype(jnp.bfloat16)
    lane = lax.broadcasted_iota(jnp.int32, (1, PAIR), 1)
    head0 = lane < HEAD_DIM
    n2 = 2 * GDN_CHUNK
    r = lax.broadcasted_iota(jnp.int32, (n2, n2), 0)
    c = lax.broadcasted_iota(jnp.int32, (n2, n2), 1)
    same = (r // GDN_CHUNK) == (c // GDN_CHUNK)
    causal = same & (c <= r)
    strict = same & (c < r)

    def l2n(v):
        return v * lax.rsqrt(_dot_ones(v * v, ones) + NORM_EPS)

    def stack(v):
        return jnp.concatenate([jnp.where(head0, v, 0.0), jnp.where(head0, 0.0, v)], axis=0)

    pair = lambda p, grp: slice(grp * GDN_WIDTH + p * PAIR, grp * GDN_WIDTH + (p + 1) * PAIR)
    q_p = [l2n(qkv[:, pair(p, 0)]) * HEAD_DIM ** -0.5 for p in range(N_PAIRS)]
    k_p = [l2n(qkv[:, pair(p, 1)]) for p in range(N_PAIRS)]
    v_p = [qkv[:, pair(p, 2)] for p in range(N_PAIRS)]

    units = [(ci, p) for ci in range(nc) for p in range(N_PAIRS)]
    rows = lambda ci: slice(ci * GDN_CHUNK, (ci + 1) * GDN_CHUNK)
    gc = [g_ref[0, rows(ci), pair(p, 0)] for ci, p in units]
    bt = [beta_ref[0, rows(ci), pair(p, 0)] for ci, p in units]
    qn = [q_p[p][rows(ci)] for ci, p in units]
    kn = [k_p[p][rows(ci)] for ci, p in units]
    vv = [v_p[p][rows(ci)] for ci, p in units]
    g_last = [g[GDN_CHUNK - 1:GDN_CHUNK, :] for g in gc]
    eg = [jnp.exp(g) for g in gc]
    kb = [k * b for k, b in zip(kn, bt)]
    k2 = [stack(k) for k in kn]

    def decay_of(g):
        gcb = jnp.concatenate([jnp.broadcast_to(g[:, 0:1], (GDN_CHUNK, PAIR)),
                               jnp.broadcast_to(g[:, HEAD_DIM:HEAD_DIM + 1], (GDN_CHUNK, PAIR))], axis=0)
        return jnp.where(causal, jnp.exp(jnp.where(causal, gcb - gcb.T, 0.0)), 0.0)

    decay = [decay_of(g) for g in gc]
    low = [jnp.where(strict, _dot_nt(stack(a), b) * d, 0.0) for a, b, d in zip(kb, k2, decay)]
    a_in = [_dot_nt(stack(a), b) * d for a, b, d in zip(qn, k2, decay)]
    t_inv = _unit_lower_inverse_many(low)
    u = [_dot(ti, stack(v * b)) for ti, v, b in zip(t_inv, vv, bt)]
    w = [_dot(ti, stack(a * e)) for ti, a, e in zip(t_inv, kb, eg)]
    ket = [stack(k * jnp.exp(gl - g)).T for k, gl, g in zip(kn, g_last, gc)]
    m_mat = [_dot(a, b) for a, b in zip(ket, w)]
    n_mat = [_dot(a, b) for a, b in zip(ket, u)]
    p_mat = [stack(q * e) - _dot(a, b) for q, e, a, b in zip(qn, eg, a_in, w)]
    r_mat = [_dot(a, b) for a, b in zip(a_in, u)]

    state = [s_ref[p] for p in range(N_PAIRS)]
    for i, (ci, p) in enumerate(units):
        s = state[p]
        o2 = _dot(p_mat[i], s) + r_mat[i]
        state[p] = s * jnp.exp(g_last[i]) - _dot(m_mat[i], s) + n_mat[i]
        o = o2[0:GDN_CHUNK] + o2[GDN_CHUNK:n2]
        ms = _dot_ones(o * o, ones) * (1.0 / HEAD_DIM)
        on = o * lax.rsqrt(ms + NORM_EPS) * nw_ref[...]
        o_ref[0, rows(ci), pair(p, 0)] = (on * _silu(gate_ref[0, rows(ci), pair(p, 0)])
                                          * ms_ref[:, pair(p, 0)]).astype(o_ref.dtype)
    for p in range(N_PAIRS):
        s_ref[p] = state[p]


def _gdn(qkv, conv_w, g, beta, gate, norm_w2, mscale, blk):
    b, t, _ = qkv.shape
    tile = lambda width: pl.BlockSpec((1, blk, width), lambda i, j: (i, j, 0))
    const = lambda shape: pl.BlockSpec(shape, lambda i, j: (0, 0))
    return pl.pallas_call(
        _gdn_kernel,
        grid=(b, t // blk),
        in_specs=[tile(3 * GDN_WIDTH), const((GDN_SHORT_CONV, 3 * GDN_WIDTH)),
                  tile(GDN_WIDTH), tile(GDN_WIDTH), tile(GDN_WIDTH),
                  const((1, PAIR)), const((1, GDN_WIDTH))],
        out_specs=tile(GDN_WIDTH),
        out_shape=jax.ShapeDtypeStruct((b, t, GDN_WIDTH), MXU_DTYPE),
        scratch_shapes=[pltpu.VMEM((N_PAIRS, PAIR, PAIR), F32),
                        pltpu.VMEM((SUBLANES, 3 * GDN_WIDTH), F32),
                        pltpu.VMEM((blk + SUBLANES, 3 * GDN_WIDTH), F32)],
        compiler_params=_cparams("parallel", "arbitrary"),
        name="gdn",
    )(qkv, conv_w, g, beta, gate, norm_w2, mscale)


CONV_HALO = 32


def _conv_kernel(glu_ref, w_ref, b_ref, lg_ref, lb_ref, ms_ref, o_ref, buf_ref):
    t = pl.program_id(1)
    tt = glu_ref.shape[1]

    @pl.when(t == 0)
    def _():
        buf_ref[0:CONV_HALO, :] = jnp.zeros((CONV_HALO, CONV_WIDTH), F32)

    @pl.when(t > 0)
    def _():
        buf_ref[0:CONV_HALO, :] = buf_ref[tt:tt + CONV_HALO, :]

    x = glu_ref[0]
    buf_ref[CONV_HALO:CONV_HALO + tt, :] = x[:, 0:CONV_WIDTH] * _sigmoid(x[:, CONV_WIDTH:2 * CONV_WIDTH])
    acc = jnp.zeros((tt, CONV_WIDTH), F32) + b_ref[...]
    for j in range(CONV_KERNEL):
        lo = CONV_HALO - (CONV_KERNEL - 1) + j
        acc = acc + w_ref[j:j + 1, :] * buf_ref[lo:lo + tt, :]
    y = _silu(_layer_norm_rows(acc, lg_ref[...], lb_ref[...]))
    o_ref[0] = (y * ms_ref[...]).astype(o_ref.dtype)


def _conformer_conv(glu, w, bias, ln_g, ln_b, mscale, tt):
    b, t, _ = glu.shape
    vec = pl.BlockSpec((1, CONV_WIDTH), lambda i, j: (0, 0))
    return pl.pallas_call(
        _conv_kernel,
        grid=(b, t // tt),
        in_specs=[pl.BlockSpec((1, tt, 2 * CONV_WIDTH), lambda i, j: (i, j, 0)),
                  pl.BlockSpec((CONV_KERNEL, CONV_WIDTH), lambda i, j: (0, 0)),
                  vec, vec, vec, vec],
        out_specs=pl.BlockSpec((1, tt, CONV_WIDTH), lambda i, j: (i, j, 0)),
        out_shape=jax.ShapeDtypeStruct((b, t, CONV_WIDTH), MXU_DTYPE),
        scratch_shapes=[pltpu.VMEM((tt + CONV_HALO, CONV_WIDTH), F32)],
        compiler_params=_cparams("parallel", "arbitrary"),
        name="conformer_conv",
    )(glu, w, bias, ln_g, ln_b, mscale)


FOX_ROW_GROUP = 32


def _fox_kernel(q_ref, k_ref, v_ref, ms_ref, o_ref, s_ref, p_ref, m_ref, alpha_ref, acc_ref, *, tq):
    i = pl.program_id(2)
    heads = range(2)
    rg = FOX_ROW_GROUP
    m_ref[...] = jnp.full(m_ref.shape, -jnp.inf, F32)
    acc_ref[...] = jnp.zeros(acc_ref.shape, F32)

    def block(j, masked):
        start = pl.multiple_of(j * tq, tq)
        for h in heads:
            s_ref[h] = lax.dot_general(q_ref[0, h], k_ref[0, h, pl.ds(start, tq), :],
                                       (((1,), (1,)), ((), ())), preferred_element_type=F32)
        for h in heads:
            for g in range(tq // rg):
                rows = slice(g * rg, (g + 1) * rg)
                s = s_ref[h, rows, :]
                if masked:
                    row_id = g * rg + lax.broadcasted_iota(jnp.int32, (rg, tq), 0)
                    col_id = lax.broadcasted_iota(jnp.int32, (rg, tq), 1)
                    s = jnp.where(col_id <= row_id, s, -jnp.inf)
                m_old = m_ref[h, rows, :]
                m_new = jnp.maximum(m_old, jnp.max(s, axis=-1, keepdims=True))
                alpha_ref[h, rows, :] = jnp.exp(m_old - m_new)
                m_ref[h, rows, :] = m_new
                p_ref[h, rows, :] = jnp.exp(s - m_new).astype(p_ref.dtype)
        for h in heads:
            acc_ref[h] = alpha_ref[h] * acc_ref[h] + jnp.dot(p_ref[h], v_ref[0, h, pl.ds(start, tq), :],
                                                             preferred_element_type=F32)

    def body(j, carry):
        block(j, False)
        return carry

    lax.fori_loop(0, i, body, 0)
    block(i, True)
    o = [acc_ref[h] / acc_ref[h][:, AUG_BIAS_LANE:AUG_BIAS_LANE + 1] for h in heads]
    lane = lax.broadcasted_iota(jnp.int32, (1, PAIR), 1)
    o_pair = jnp.where(lane < HEAD_DIM, o[0], pltpu.roll(o[1], HEAD_DIM, axis=1))
    o_ref[0] = (o_pair * ms_ref[...]).astype(o_ref.dtype)


def _fox(qa, ka, va, mscale, tq):
    b, _, t, _ = qa.shape
    kern = functools.partial(_fox_kernel, tq=tq)
    return pl.pallas_call(
        kern,
        grid=(b, N_PAIRS, t // tq),
        in_specs=[pl.BlockSpec((1, 2, tq, LANES), lambda bi, p, i: (bi, p, i, 0)),
                  pl.BlockSpec((1, 2, t, LANES), lambda bi, p, i: (bi, p, 0, 0)),
                  pl.BlockSpec((1, 2, t, LANES), lambda bi, p, i: (bi, p, 0, 0)),
                  pl.BlockSpec((1, PAIR), lambda bi, p, i: (0, p))],
        out_specs=pl.BlockSpec((1, tq, PAIR), lambda bi, p, i: (bi, i, p)),
        out_shape=jax.ShapeDtypeStruct((b, t, FOX_WIDTH), MXU_DTYPE),
        scratch_shapes=[pltpu.VMEM((2, tq, tq), F32), pltpu.VMEM((2, tq, tq), jnp.bfloat16),
                        pltpu.VMEM((2, tq, 1), F32), pltpu.VMEM((2, tq, 1), F32),
                        pltpu.VMEM((2, tq, LANES), F32)],
        compiler_params=_cparams("parallel", "parallel", "arbitrary"),
        name="fox_attention",
    )(qa, ka, va, mscale)


def _out_proj_kernel(oa_ref, ob_ref, oc_ref, x_ref, w_ref, g_ref, b_ref, o_ref):
    mix = jnp.dot(oa_ref[...], w_ref[0:GDN_WIDTH, :], preferred_element_type=F32)
    mix = mix + jnp.dot(ob_ref[...], w_ref[GDN_WIDTH:GDN_WIDTH + CONV_WIDTH, :], preferred_element_type=F32)
    mix = mix + jnp.dot(oc_ref[...], w_ref[GDN_WIDTH + CONV_WIDTH:D_MODEL, :], preferred_element_type=F32)
    o_ref[...] = _layer_norm_rows(DEEPNORM_ALPHA * x_ref[...] + mix, g_ref[...], b_ref[...])


def _out_proj(oa, ob, oc, xf, w, g, bvec, tm):
    n = xf.shape[0]
    row = lambda i: (i, 0)
    const = lambda i: (0, 0)
    return pl.pallas_call(
        _out_proj_kernel,
        grid=(n // tm,),
        in_specs=[pl.BlockSpec((tm, GDN_WIDTH), row), pl.BlockSpec((tm, CONV_WIDTH), row),
                  pl.BlockSpec((tm, FOX_WIDTH), row), pl.BlockSpec((tm, D_MODEL), row),
                  pl.BlockSpec((D_MODEL, D_MODEL), const),
                  pl.BlockSpec((1, D_MODEL), const), pl.BlockSpec((1, D_MODEL), const)],
        out_specs=pl.BlockSpec((tm, D_MODEL), row),
        out_shape=jax.ShapeDtypeStruct((n, D_MODEL), F32),
        compiler_params=_cparams("parallel"),
        name="out_proj_ln",
    )(oa, ob, oc, xf, w, g, bvec)


def _ffn_kernel(x_ref, w1_ref, w3_ref, w2_ref, g_ref, b_ref, o_ref, xb_ref, acc_ref):
    f = pl.program_id(1)

    @pl.when(f == 0)
    def _():
        xb_ref[...] = x_ref[...].astype(xb_ref.dtype)
        acc_ref[...] = jnp.zeros_like(acc_ref)

    xb = xb_ref[...]
    h = _silu(jnp.dot(xb, w1_ref[...], preferred_element_type=F32)) * jnp.dot(
        xb, w3_ref[...], preferred_element_type=F32)
    acc_ref[...] += jnp.dot(h.astype(w2_ref.dtype), w2_ref[...], preferred_element_type=F32)

    @pl.when(f == pl.num_programs(1) - 1)
    def _():
        o_ref[...] = _layer_norm_rows(DEEPNORM_ALPHA * x_ref[...] + acc_ref[...], g_ref[...], b_ref[...])


def _dense_ffn(xf, w1, w3, w2, g, bvec, tm, tf):
    n = xf.shape[0]
    ff = w1.shape[1]
    return pl.pallas_call(
        _ffn_kernel,
        grid=(n // tm, ff // tf),
        in_specs=[pl.BlockSpec((tm, D_MODEL), lambda i, f: (i, 0)),
                  pl.BlockSpec((D_MODEL, tf), lambda i, f: (0, f)),
                  pl.BlockSpec((D_MODEL, tf), lambda i, f: (0, f)),
                  pl.BlockSpec((tf, D_MODEL), lambda i, f: (f, 0)),
                  pl.BlockSpec((1, D_MODEL), lambda i, f: (0, 0)),
                  pl.BlockSpec((1, D_MODEL), lambda i, f: (0, 0))],
        out_specs=pl.BlockSpec((tm, D_MODEL), lambda i, f: (i, 0)),
        out_shape=jax.ShapeDtypeStruct((n, D_MODEL), F32),
        scratch_shapes=[pltpu.VMEM((tm, D_MODEL), MXU_DTYPE), pltpu.VMEM((tm, D_MODEL), F32)],
        compiler_params=_cparams("parallel", "arbitrary"),
        name="dense_ffn_ln",
    )(xf, w1, w3, w2, g, bvec)


def _router_kernel(x_ref, wr_ref, info_ref, cnt_ref, run_ref):
    i = pl.program_id(0)
    tm = x_ref.shape[0]

    @pl.when(i == 0)
    def _():
        run_ref[...] = jnp.zeros_like(run_ref)

    logits = jnp.dot(x_ref[...].astype(MXU_DTYPE), wr_ref[...], preferred_element_type=F32)
    lane = lax.broadcasted_iota(jnp.int32, (tm, LANES), 1)
    logits = jnp.where(lane < N_EXPERTS, logits, -jnp.inf)
    m1 = jnp.max(logits, axis=-1, keepdims=True)
    e1 = jnp.min(jnp.where(logits == m1, lane, LANES), axis=-1, keepdims=True)
    rest = jnp.where(lane == e1, -jnp.inf, logits)
    m2 = jnp.max(rest, axis=-1, keepdims=True)
    e2 = jnp.min(jnp.where(rest == m2, lane, LANES), axis=-1, keepdims=True)
    z = jnp.exp(m2 - m1)
    g1 = 1.0 / (1.0 + z)
    g2 = z / (1.0 + z)
    onehot = ((lane == e1) | (lane == e2)).astype(F32)
    r = lax.broadcasted_iota(jnp.int32, (tm, tm), 0)
    c = lax.broadcasted_iota(jnp.int32, (tm, tm), 1)
    before = _dot_f32((c < r).astype(F32), onehot) + run_ref[...]
    rank1 = jnp.sum(jnp.where(lane == e1, before, 0.0), axis=-1, keepdims=True)
    rank2 = jnp.sum(jnp.where(lane == e2, before, 0.0), axis=-1, keepdims=True)
    run_ref[...] = run_ref[...] + jnp.sum(onehot, axis=0, keepdims=True)
    cnt_ref[...] = run_ref[...]
    info = jnp.where(lane == 0, e1.astype(F32),
                     jnp.where(lane == 1, e2.astype(F32),
                               jnp.where(lane == 2, rank1,
                                         jnp.where(lane == 3, rank2,
                                                   jnp.where(lane == 4, g1, jnp.where(lane == 5, g2, 0.0))))))
    info_ref[...] = info


def _router(xf, wr_pad, tm):
    n = xf.shape[0]
    return pl.pallas_call(
        _router_kernel,
        grid=(n // tm,),
        in_specs=[pl.BlockSpec((tm, D_MODEL), lambda i: (i, 0)),
                  pl.BlockSpec((D_MODEL, LANES), lambda i: (0, 0))],
        out_specs=[pl.BlockSpec((tm, LANES), lambda i: (i, 0)),
                   pl.BlockSpec((1, LANES), lambda i: (0, 0))],
        out_shape=[jax.ShapeDtypeStruct((n, LANES), F32), jax.ShapeDtypeStruct((1, LANES), F32)],
        scratch_shapes=[pltpu.VMEM((1, LANES), F32)],
        compiler_params=_cparams("arbitrary"),
        name="moe_router",
    )(xf, wr_pad)


def _dispatch_kernel(d1_ref, d2_ref, x_ref, zeros_ref, xs_ref, sem):
    del zeros_ref
    tm = x_ref.shape[0]

    def row_copy(r, dst):
        return pltpu.make_async_copy(x_ref.at[pl.ds(r, 1), :], xs_ref.at[pl.ds(dst, 1), :], sem)

    def issue(r, carry):
        row_copy(r, d1_ref[0, 0, r]).start()
        row_copy(r, d2_ref[0, 0, r]).start()
        return carry

    lax.fori_loop(0, tm, issue, 0)

    def drain(r, carry):
        row_copy(r, d1_ref[0, 0, r]).wait()
        row_copy(r, d2_ref[0, 0, r]).wait()
        return carry

    lax.fori_loop(0, tm, drain, 0)


def _dispatch(xf, d1, d2, rows, tm):
    n = xf.shape[0]
    idx = lambda a: a.reshape(n // tm, 1, tm)
    zeros = jnp.zeros((rows, D_MODEL), xf.dtype)
    smem = lambda: pl.BlockSpec((1, 1, tm), lambda i: (i, 0, 0), memory_space=pltpu.SMEM)
    return pl.pallas_call(
        _dispatch_kernel,
        grid=(n // tm,),
        in_specs=[smem(), smem(),
                  pl.BlockSpec((tm, D_MODEL), lambda i: (i, 0)),
                  pl.BlockSpec(memory_space=pl.ANY)],
        out_specs=pl.BlockSpec(memory_space=pl.ANY),
        out_shape=jax.ShapeDtypeStruct((rows, D_MODEL), xf.dtype),
        scratch_shapes=[pltpu.SemaphoreType.DMA(())],
        input_output_aliases={3: 0},
        compiler_params=_cparams("arbitrary"),
        name="moe_dispatch",
    )(idx(d1), idx(d2), xf, zeros)


def _expert_kernel(te_ref, nu_ref, xs_ref, w1_ref, w3_ref, w2_ref, ys_ref, xb_ref, acc_ref):
    del te_ref
    i = pl.program_id(0)
    f = pl.program_id(1)
    used = i < nu_ref[0]

    @pl.when(f == 0)
    def _():
        xb_ref[...] = xs_ref[...].astype(xb_ref.dtype)
        acc_ref[...] = jnp.zeros_like(acc_ref)

    @pl.when(used)
    def _():
        xb = xb_ref[...]
        h = _silu(jnp.dot(xb, w1_ref[0], preferred_element_type=F32)) * jnp.dot(
            xb, w3_ref[0], preferred_element_type=F32)
        acc_ref[...] += jnp.dot(h.astype(w2_ref.dtype), w2_ref[0], preferred_element_type=F32)

    @pl.when(f == pl.num_programs(1) - 1)
    def _():
        ys_ref[...] = acc_ref[...]


def _experts(xs, tile_expert, n_used, w1, w3, w2, tm, tf):
    rows = xs.shape[0]
    ff = w1.shape[2]
    n_tiles = rows // tm

    def x_map(i, f, te, nu):
        return (jnp.minimum(i, nu[0] - 1), 0)

    def w13_map(i, f, te, nu):
        return (te[i], 0, jnp.where(i < nu[0], f, ff // tf - 1))

    def w2_map(i, f, te, nu):
        return (te[i], jnp.where(i < nu[0], f, ff // tf - 1), 0)

    grid_spec = pltpu.PrefetchScalarGridSpec(
        num_scalar_prefetch=2,
        grid=(n_tiles, ff // tf),
        in_specs=[pl.BlockSpec((tm, D_MODEL), x_map),
                  pl.BlockSpec((1, D_MODEL, tf), w13_map),
                  pl.BlockSpec((1, D_MODEL, tf), w13_map),
                  pl.BlockSpec((1, tf, D_MODEL), w2_map)],
        out_specs=pl.BlockSpec((tm, D_MODEL), lambda i, f, te, nu: (i, 0)),
        scratch_shapes=[pltpu.VMEM((tm, D_MODEL), MXU_DTYPE), pltpu.VMEM((tm, D_MODEL), F32)],
    )
    return pl.pallas_call(
        _expert_kernel,
        grid_spec=grid_spec,
        out_shape=jax.ShapeDtypeStruct((rows, D_MODEL), F32),
        compiler_params=_cparams("arbitrary", "arbitrary"),
        name="moe_experts",
    )(tile_expert, n_used, xs, w1, w3, w2)


def _combine_kernel(d1_ref, d2_ref, x_ref, info_ref, ys_ref, g_ref, b_ref, o_ref, y1_ref, y2_ref, sem):
    tm = x_ref.shape[0]

    def row_copy(src, r, buf):
        return pltpu.make_async_copy(ys_ref.at[pl.ds(src, 1), :], buf.at[pl.ds(r, 1), :], sem)

    def issue(r, carry):
        row_copy(d1_ref[0, 0, r], r, y1_ref).start()
        row_copy(d2_ref[0, 0, r], r, y2_ref).start()
        return carry

    lax.fori_loop(0, tm, issue, 0)

    def drain(r, carry):
        row_copy(d1_ref[0, 0, r], r, y1_ref).wait()
        row_copy(d2_ref[0, 0, r], r, y2_ref).wait()
        return carry

    lax.fori_loop(0, tm, drain, 0)
    info = info_ref[...]
    ff = info[:, 4:5] * y1_ref[...] + info[:, 5:6] * y2_ref[...]
    o_ref[...] = _layer_norm_rows(DEEPNORM_ALPHA * x_ref[...] + ff, g_ref[...], b_ref[...])


def _combine(xf, info, ys, d1, d2, g, bvec, tm):
    n = xf.shape[0]
    idx = lambda a: a.reshape(n // tm, 1, tm)
    smem = lambda: pl.BlockSpec((1, 1, tm), lambda i: (i, 0, 0), memory_space=pltpu.SMEM)
    return pl.pallas_call(
        _combine_kernel,
        grid=(n // tm,),
        in_specs=[smem(), smem(),
                  pl.BlockSpec((tm, D_MODEL), lambda i: (i, 0)),
                  pl.BlockSpec((tm, LANES), lambda i: (i, 0)),
                  pl.BlockSpec(memory_space=pl.ANY),
                  pl.BlockSpec((1, D_MODEL), lambda i: (0, 0)),
                  pl.BlockSpec((1, D_MODEL), lambda i: (0, 0))],
        out_specs=pl.BlockSpec((tm, D_MODEL), lambda i: (i, 0)),
        out_shape=jax.ShapeDtypeStruct((n, D_MODEL), F32),
        scratch_shapes=[pltpu.VMEM((tm, D_MODEL), F32), pltpu.VMEM((tm, D_MODEL), F32),
                        pltpu.SemaphoreType.DMA(())],
        compiler_params=_cparams("arbitrary"),
        name="moe_combine_ln",
    )(idx(d1), idx(d2), xf, info, ys, g, bvec)


MOE_TILE = 512


def _moe_ffn(xf, w_router, w1, w3, w2, g, bvec, tm_tok, tf):
    n = xf.shape[0]
    wr_pad = jnp.zeros((D_MODEL, LANES), MXU_DTYPE).at[:, :N_EXPERTS].set(w_router.astype(MXU_DTYPE))
    info, counts = _router(xf, wr_pad, tm_tok)
    sizes = counts[0, :N_EXPERTS].astype(jnp.int32)
    tiles_per = (sizes + MOE_TILE - 1) // MOE_TILE
    tile_end = jnp.cumsum(tiles_per)
    seg_start = (tile_end - tiles_per) * MOE_TILE
    e1 = info[:, 0].astype(jnp.int32)
    e2 = info[:, 1].astype(jnp.int32)
    d1 = seg_start[e1] + info[:, 2].astype(jnp.int32)
    d2 = seg_start[e2] + info[:, 3].astype(jnp.int32)
    n_tiles = (n * TOP_K) // MOE_TILE + N_EXPERTS
    rows = n_tiles * MOE_TILE
    tile_expert = jnp.minimum(jnp.searchsorted(tile_end, jnp.arange(n_tiles, dtype=jnp.int32), side='right'),
                              N_EXPERTS - 1).astype(jnp.int32)
    n_used = tile_end[N_EXPERTS - 1:].astype(jnp.int32)
    xs = _dispatch(xf, d1, d2, rows, tm_tok)
    ys = _experts(xs, tile_expert, n_used, w1, w3, w2, MOE_TILE, tf)
    return _combine(xf, info, ys, d1, d2, g, bvec, tm_tok)


def _pack_in_proj(w_in):
    cuts = np.cumsum([0, GDN_WIDTH, GDN_WIDTH, GDN_WIDTH, GDN_HEADS, GDN_HEADS, GDN_WIDTH,
                      2 * CONV_WIDTH, FOX_WIDTH, FOX_WIDTH, FOX_WIDTH, FOX_HEADS])
    seg = lambda i: w_in[:, cuts[i]:cuts[i + 1]]
    w_main = jnp.concatenate([seg(0), seg(1), seg(2), seg(5), seg(6), seg(7), seg(8), seg(9)], axis=1)
    pad = jnp.zeros((D_MODEL, SMALL_COLS - 2 * GDN_HEADS - FOX_HEADS), w_in.dtype)
    w_small = jnp.concatenate([seg(3), seg(4), seg(10), pad], axis=1).astype(F32)
    ws_hi = w_small.astype(MXU_DTYPE)
    ws_lo = (w_small - ws_hi.astype(F32)).astype(MXU_DTYPE)
    return w_main.astype(MXU_DTYPE), jnp.concatenate([ws_hi, ws_lo], axis=1)


def _lane_row(vals, offset):
    return jnp.zeros((1, LANES), F32).at[0, offset:offset + vals.shape[0]].set(vals.astype(F32))


def _mixer(xf, b, t, w_in, mix_scale, w_out, gdn_conv_w, gdn_a_log, gdn_dt_bias, gdn_norm_w,
           cnv_dw_w, cnv_dw_b, cnv_ln_g, cnv_ln_b, fox_f_bias, ln_g, ln_b, tiles):
    w_main, w_small = _pack_in_proj(w_in)
    qkv, gate, glu, fq, fk, fv, small = _in_proj(xf, w_main, w_small, tiles["tm"])
    r3 = lambda a: a.reshape(b, t, a.shape[-1])
    g_cum, beta, qa, ka, va = _gates(r3(small), _lane_row(gdn_a_log, 0), _lane_row(gdn_dt_bias, 0),
                                     _lane_row(fox_f_bias, 2 * GDN_HEADS), r3(fq), r3(fk), r3(fv), tiles["tt"])
    ms = mix_scale.reshape(1, D_MODEL).astype(F32)
    norm_w2 = jnp.tile(gdn_norm_w.astype(F32), 2).reshape(1, PAIR)
    o_a = _gdn(r3(qkv), gdn_conv_w, g_cum, beta, r3(gate), norm_w2, ms[:, :GDN_WIDTH], tiles["gdn_blk"])
    row = lambda v: v.reshape(1, -1).astype(F32)
    o_b = _conformer_conv(r3(glu), cnv_dw_w, row(cnv_dw_b), row(cnv_ln_g), row(cnv_ln_b),
                          ms[:, GDN_WIDTH:GDN_WIDTH + CONV_WIDTH], tiles["tt"])
    o_c = _fox(qa, ka, va, ms[:, GDN_WIDTH + CONV_WIDTH:], tiles["tq"])
    flat = lambda a: a.reshape(b * t, a.shape[-1])
    return _out_proj(flat(o_a), flat(o_b), flat(o_c), xf, w_out.astype(MXU_DTYPE), row(ln_g), row(ln_b),
                     tiles["tm"])


def _tiles(t):
    return dict(tm=min(512, t), tt=min(512, t), gdn_blk=min(256, t), tq=min(512, t),
                tf_dense=1408, tf_moe=896)


def kernel(x, w_in, mix_scale, w_out, gdn_conv_w, gdn_a_log, gdn_dt_bias, gdn_norm_w, cnv_dw_w, cnv_dw_b,
           cnv_ln_g, cnv_ln_b, fox_f_bias, ln_mix_g, ln_mix_b, ln_ffn_g, ln_ffn_b, ffn_w1, ffn_w3, ffn_w2,
           moe_router, moe_w1, moe_w3, moe_w2):
    b, t, d = x.shape
    tiles = _tiles(t)
    row = lambda v: v.reshape(1, -1).astype(F32)
    xf = x.reshape(b * t, d)
    for l in range(DEPTH):
        xf = _mixer(xf, b, t, w_in[l], mix_scale[l], w_out[l], gdn_conv_w[l], gdn_a_log[l], gdn_dt_bias[l],
                    gdn_norm_w[l], cnv_dw_w[l], cnv_dw_b[l], cnv_ln_g[l], cnv_ln_b[l], fox_f_bias[l],
                    ln_mix_g[l], ln_mix_b[l], tiles)
        if l % 2 == 0:
            e = l // 2
            xf = _dense_ffn(xf, ffn_w1[e].astype(MXU_DTYPE), ffn_w3[e].astype(MXU_DTYPE),
                            ffn_w2[e].astype(MXU_DTYPE), row(ln_ffn_g[l]), row(ln_ffn_b[l]),
                            tiles["tm"], tiles["tf_dense"])
        else:
            e = l // 2
            xf = _moe_ffn(xf, moe_router[e], moe_w1[e].astype(MXU_DTYPE), moe_w3[e].astype(MXU_DTYPE),
                          moe_w2[e].astype(MXU_DTYPE), row(ln_ffn_g[l]), row(ln_ffn_b[l]),
                          tiles["tm"], tiles["tf_moe"])
    return xf.reshape(b, t, d)
```

```python
import functools

import jax
import jax.numpy as jnp
import numpy as np
from jax import lax
from jax.experimental import pallas as pl
from jax.experimental.pallas import tpu as pltpu

D_MODEL = 1024
DEPTH = 2
HEAD_DIM = 64
GDN_WIDTH = 384
CONV_WIDTH = 256
FOX_WIDTH = 384
GDN_HEADS = 6
FOX_HEADS = 6
GDN_SHORT_CONV = 4
GDN_CHUNK = 64
CONV_KERNEL = 31
FFN_DENSE = 2816
N_EXPERTS = 8
TOP_K = 2
FFN_EXPERT = 3584
DEEPNORM_ALPHA = (2 * DEPTH) ** 0.25
LN_EPS = 1e-5
NORM_EPS = 1e-6

LANES = 128
SUBLANES = 8
PAIR = 2 * HEAD_DIM
N_PAIRS = GDN_WIDTH // PAIR
SMALL_COLS = LANES

_C_QKV = 0
_C_GATE = _C_QKV + 3 * GDN_WIDTH
_C_GLU = _C_GATE + GDN_WIDTH
_C_FQ = _C_GLU + 2 * CONV_WIDTH
_C_FK = _C_FQ + FOX_WIDTH
_C_FV = _C_FK + FOX_WIDTH
_C_END = _C_FV + FOX_WIDTH

MXU_DTYPE = jnp.bfloat16
F32 = jnp.float32
HIGHEST = lax.Precision.HIGHEST

VMEM_LIMIT = 56 * 1024 * 1024


def _cparams(*sem):
    return pltpu.CompilerParams(dimension_semantics=sem, vmem_limit_bytes=VMEM_LIMIT)


def _dot(a, b):
    return jnp.dot(a.astype(MXU_DTYPE), b.astype(MXU_DTYPE), preferred_element_type=F32)


def _dot_nt(a, b):
    return lax.dot_general(a.astype(MXU_DTYPE), b.astype(MXU_DTYPE),
                           (((1,), (1,)), ((), ())), preferred_element_type=F32)


def _dot_f32(a, b):
    return jnp.dot(a, b, preferred_element_type=F32, precision=HIGHEST)


def _split(a):
    hi = a.astype(jnp.bfloat16)
    return hi, (a - hi.astype(F32)).astype(jnp.bfloat16)


def _split3(a):
    hi = a.astype(jnp.bfloat16)
    rest = a - hi.astype(F32)
    mid = rest.astype(jnp.bfloat16)
    lo = (rest - mid.astype(F32)).astype(jnp.bfloat16)
    return hi, mid, lo


def _dot_split(a, b):
    ah, al = _split(a)
    bh, bl = _split(b)
    d = lambda x, y: jnp.dot(x, y, preferred_element_type=F32)
    return d(ah, bh) + d(al, bh) + d(ah, bl)


def _dot_ones(a, ones_b16):
    ah, al = _split(a)
    d = lambda x: jnp.dot(x, ones_b16, preferred_element_type=F32)
    return d(ah) + d(al)


def _sel_rhs(a, sel_b16):
    return sum(jnp.dot(t, sel_b16, preferred_element_type=F32) for t in _split3(a))


def _sel_lhs(sel_b16, b):
    return sum(jnp.dot(sel_b16, t, preferred_element_type=F32) for t in _split3(b))


def _sigmoid(x):
    return 1.0 / (1.0 + jnp.exp(-x))


def _silu(x):
    return x * _sigmoid(x)


def _softplus(x):
    return jnp.maximum(x, 0.0) + jnp.log1p(jnp.exp(-jnp.abs(x)))


def _log_sigmoid(x):
    return -_softplus(-x)


def _layer_norm_rows(y, g, b):
    mu = jnp.mean(y, axis=-1, keepdims=True)
    d = y - mu
    var = jnp.mean(d * d, axis=-1, keepdims=True)
    return d * lax.rsqrt(var + LN_EPS) * g + b


def _head_ones():
    r = lax.broadcasted_iota(jnp.int32, (PAIR, PAIR), 0) // HEAD_DIM
    c = lax.broadcasted_iota(jnp.int32, (PAIR, PAIR), 1) // HEAD_DIM
    return (r == c).astype(F32)


def _in_proj_kernel(x_ref, w_ref, ws_ref, qkv_ref, gate_ref, glu_ref, fq_ref, fk_ref, fv_ref, small_ref):
    x = x_ref[...]
    xb = x.astype(MXU_DTYPE)

    def mm(lo, hi):
        return jnp.dot(xb, w_ref[:, lo:hi], preferred_element_type=F32)

    qkv_ref[...] = mm(_C_QKV, _C_GATE)
    gate_ref[...] = mm(_C_GATE, _C_GLU)
    glu_ref[...] = mm(_C_GLU, _C_FQ)
    fq_ref[...] = (mm(_C_FQ, _C_FK) * HEAD_DIM ** -0.5).astype(fq_ref.dtype)
    fk_ref[...] = mm(_C_FK, _C_FV).astype(fk_ref.dtype)
    fv_ref[...] = mm(_C_FV, _C_END).astype(fv_ref.dtype)
    x_lo = (x - xb.astype(F32)).astype(MXU_DTYPE)
    hh_hl = jnp.dot(xb, ws_ref[...], preferred_element_type=F32)
    lh = jnp.dot(x_lo, ws_ref[:, 0:SMALL_COLS], preferred_element_type=F32)
    small_ref[...] = hh_hl[:, 0:SMALL_COLS] + hh_hl[:, SMALL_COLS:2 * SMALL_COLS] + lh


def _in_proj(xf, w_main, w_small, tm):
    n = xf.shape[0]
    row = lambda i: (i, 0)
    const = lambda i: (0, 0)
    outs = [
        jax.ShapeDtypeStruct((n, 3 * GDN_WIDTH), F32),
        jax.ShapeDtypeStruct((n, GDN_WIDTH), F32),
        jax.ShapeDtypeStruct((n, 2 * CONV_WIDTH), F32),
        jax.ShapeDtypeStruct((n, FOX_WIDTH), MXU_DTYPE),
        jax.ShapeDtypeStruct((n, FOX_WIDTH), MXU_DTYPE),
        jax.ShapeDtypeStruct((n, FOX_WIDTH), MXU_DTYPE),
        jax.ShapeDtypeStruct((n, SMALL_COLS), F32),
    ]
    return pl.pallas_call(
        _in_proj_kernel,
        grid=(n // tm,),
        in_specs=[pl.BlockSpec((tm, D_MODEL), row),
                  pl.BlockSpec((D_MODEL, _C_END), const),
                  pl.BlockSpec((D_MODEL, 2 * SMALL_COLS), const)],
        out_specs=[pl.BlockSpec((tm, o.shape[1]), row) for o in outs],
        out_shape=outs,
        compiler_params=_cparams("parallel"),
        name="in_proj",
    )(xf, w_main, w_small)


AUG_BIAS_LANE = HEAD_DIM


def _gates_kernel(small_ref, a_ref, dtb_ref, fb_ref, fq_ref, fk_ref, fv_ref,
                  g_ref, beta_ref, qa_ref, ka_ref, va_ref, carry_ref):
    t = pl.program_id(1)
    tt = small_ref.shape[1]
    b16 = jnp.bfloat16

    @pl.when(t == 0)
    def _():
        carry_ref[...] = jnp.zeros_like(carry_ref)

    s = small_ref[0]
    log_decay = -jnp.exp(a_ref[...]) * _softplus(s + dtb_ref[...])
    beta = _sigmoid(s)
    log_f = _log_sigmoid(s + fb_ref[...])

    r = lax.broadcasted_iota(jnp.int32, (tt, tt), 0)
    c = lax.broadcasted_iota(jnp.int32, (tt, tt), 1)
    tri = (c <= r)
    g_cum = _sel_lhs((tri & (r // GDN_CHUNK == c // GDN_CHUNK)).astype(b16), log_decay)
    c_cum = _sel_lhs(tri.astype(b16), log_f) + carry_ref[...]
    carry_ref[...] = c_cum[tt - 1:tt, :]

    er = lax.broadcasted_iota(jnp.int32, (LANES, GDN_WIDTH), 0)
    ec = lax.broadcasted_iota(jnp.int32, (LANES, GDN_WIDTH), 1) // HEAD_DIM
    g_ref[0] = _sel_rhs(g_cum, (er == ec).astype(b16))
    beta_ref[0] = _sel_rhs(beta, (er == ec + GDN_HEADS).astype(b16))

    rr = lax.broadcasted_iota(jnp.int32, (LANES, LANES), 0)
    cc = lax.broadcasted_iota(jnp.int32, (LANES, LANES), 1)
    lane = lax.broadcasted_iota(jnp.int32, (1, LANES), 1)
    q_ones = ((lane >= AUG_BIAS_LANE) & (lane < AUG_BIAS_LANE + 3)).astype(F32)
    v_one = (lane == AUG_BIAS_LANE).astype(F32)
    neg_c = _split3(-c_cum)
    for h in range(FOX_HEADS):
        p, half = divmod(h, 2)
        lanes = slice(p * PAIR, (p + 1) * PAIR)
        sel = ((cc < HEAD_DIM) & (rr == cc + half * HEAD_DIM)).astype(b16)
        pick = lambda ref: jnp.dot(ref[0, :, lanes], sel, preferred_element_type=F32)
        bias = sum(jnp.dot(term, ((rr == 2 * GDN_HEADS + h) & (cc == AUG_BIAS_LANE + i)).astype(b16),
                           preferred_element_type=F32) for i, term in enumerate(neg_c))
        qa_ref[0, h] = (pick(fq_ref) + q_ones).astype(b16)
        ka_ref[0, h] = (pick(fk_ref) + bias).astype(b16)
        va_ref[0, h] = (pick(fv_ref) + v_one).astype(b16)


def _gates(small, a_row, dtb_row, fb_row, fq, fk, fv, tt):
    b, t, _ = small.shape
    row = pl.BlockSpec((1, LANES), lambda i, j: (0, 0))
    tile = lambda w: pl.BlockSpec((1, tt, w), lambda i, j: (i, j, 0))
    aug = pl.BlockSpec((1, FOX_HEADS, tt, LANES), lambda i, j: (i, 0, j, 0))
    aug_shape = jax.ShapeDtypeStruct((b, FOX_HEADS, t, LANES), jnp.bfloat16)
    return pl.pallas_call(
        _gates_kernel,
        grid=(b, t // tt),
        in_specs=[tile(SMALL_COLS), row, row, row, tile(FOX_WIDTH), tile(FOX_WIDTH), tile(FOX_WIDTH)],
        out_specs=[tile(GDN_WIDTH), tile(GDN_WIDTH), aug, aug, aug],
        out_shape=[jax.ShapeDtypeStruct((b, t, GDN_WIDTH), F32), jax.ShapeDtypeStruct((b, t, GDN_WIDTH), F32),
                   aug_shape, aug_shape, aug_shape],
        scratch_shapes=[pltpu.VMEM((1, LANES), F32)],
        compiler_params=_cparams("parallel", "arbitrary"),
        name="gates",
    )(small, a_row, dtb_row, fb_row, fq, fk, fv)


def _unit_lower_inverse_many(lows):
    n = lows[0].shape[0]
    r = lax.broadcasted_iota(jnp.int32, (n, n), 0)
    c = lax.broadcasted_iota(jnp.int32, (n, n), 1)
    eye = (r == c).astype(F32)
    diag16 = r // 16 == c // 16
    d = [jnp.where(diag16, low, 0.0) for low in lows]
    d2 = [_dot_split(x, x) for x in d]
    p = [eye - x for x in d]
    p = [x + _dot_split(x, y) for x, y in zip(p, d2)]
    d4 = [_dot_split(x, x) for x in d2]
    p = [x + _dot_split(x, y) for x, y in zip(p, d4)]
    d8 = [_dot_split(x, x) for x in d4]
    x = [a + _dot_split(a, y) for a, y in zip(p, d8)]
    for blk in (32, 64):
        sel = (r // blk == c // blk) & (r // (blk // 2) != c // (blk // 2))
        xo = [_dot_split(a, jnp.where(sel, low, 0.0)) for a, low in zip(x, lows)]
        x = [a - _dot_split(b, a) for a, b in zip(x, xo)]
    return x


def _gdn_kernel(qkv_ref, w_ref, g_ref, beta_ref, gate_ref, nw_ref, ms_ref, o_ref, s_ref, carry_ref, buf_ref):
    t = pl.program_id(1)
    blk = qkv_ref.shape[1]
    halo = SUBLANES
    nc = blk // GDN_CHUNK

    @pl.when(t == 0)
    def _():
        s_ref[...] = jnp.zeros_like(s_ref)
        carry_ref[...] = jnp.zeros_like(carry_ref)

    x = qkv_ref[0]
    buf_ref[0:halo, :] = carry_ref[...]
    buf_ref[halo:halo + blk, :] = x
    carry_ref[...] = x[blk - halo:blk, :]
    acc = jnp.zeros((blk, 3 * GDN_WIDTH), F32)
    for j in range(GDN_SHORT_CONV):
        lo = halo - (GDN_SHORT_CONV - 1) + j
        acc = acc + w_ref[j:j + 1, :] * buf_ref[lo:lo + blk, :]
    qkv = _silu(acc)

    ones = _head_ones().astype(jnp.bfloat16)
    lane = lax.broadcasted_iota(jnp.int32, (1, PAIR), 1)
    head0 = lane < HEAD_DIM
    n2 = 2 * GDN_CHUNK
    r = lax.broadcasted_iota(jnp.int32, (n2, n2), 0)
    c = lax.broadcasted_iota(jnp.int32, (n2, n2), 1)
    same = (r // GDN_CHUNK) == (c // GDN_CHUNK)
    causal = same & (c <= r)
    strict = same & (c < r)

    def l2n(v):
        return v * lax.rsqrt(_dot_ones(v * v, ones) + NORM_EPS)

    def stack(v):
        return jnp.concatenate([jnp.where(head0, v, 0.0), jnp.where(head0, 0.0, v)], axis=0)

    pair = lambda p, grp: slice(grp * GDN_WIDTH + p * PAIR, grp * GDN_WIDTH + (p + 1) * PAIR)
    q_p = [l2n(qkv[:, pair(p, 0)]) * HEAD_DIM ** -0.5 for p in range(N_PAIRS)]
    k_p = [l2n(qkv[:, pair(p, 1)]) for p in range(N_PAIRS)]
    v_p = [qkv[:, pair(p, 2)] for p in range(N_PAIRS)]

    units = [(ci, p) for ci in range(nc) for p in range(N_PAIRS)]
    rows = lambda ci: slice(ci * GDN_CHUNK, (ci + 1) * GDN_CHUNK)
    gc = [g_ref[0, rows(ci), pair(p, 0)] for ci, p in units]
    bt = [beta_ref[0, rows(ci), pair(p, 0)] for ci, p in units]
    qn = [q_p[p][rows(ci)] for ci, p in units]
    kn = [k_p[p][rows(ci)] for ci, p in units]
    vv = [v_p[p][rows(ci)] for ci, p in units]
    g_last = [g[GDN_CHUNK - 1:GDN_CHUNK, :] for g in gc]
    eg = [jnp.exp(g) for g in gc]
    kb = [k * b for k, b in zip(kn, bt)]
    k2 = [stack(k) for k in kn]

    def decay_of(g):
        gcb = jnp.concatenate([jnp.broadcast_to(g[:, 0:1], (GDN_CHUNK, PAIR)),
                               jnp.broadcast_to(g[:, HEAD_DIM:HEAD_DIM + 1], (GDN_CHUNK, PAIR))], axis=0)
        return jnp.where(causal, jnp.exp(jnp.where(causal, gcb - gcb.T, 0.0)), 0.0)

    decay = [decay_of(g) for g in gc]
    low = [jnp.where(strict, _dot_nt(stack(a), b) * d, 0.0) for a, b, d in zip(kb, k2, decay)]
    a_in = [_dot_nt(stack(a), b) * d for a, b, d in zip(qn, k2, decay)]
    t_inv = _unit_lower_inverse_many(low)
    u = [_dot(ti, stack(v * b)) for ti, v, b in zip(t_inv, vv, bt)]
    w = [_dot(ti, stack(a * e)) for ti, a, e in zip(t_inv, kb, eg)]
    ket = [stack(k * jnp.exp(gl - g)).T for k, gl, g in zip(kn, g_last, gc)]
    m_mat = [_dot(a, b) for a, b in zip(ket, w)]
    n_mat = [_dot(a, b) for a, b in zip(ket, u)]
    p_mat = [stack(q * e) - _dot(a, b) for q, e, a, b in zip(qn, eg, a_in, w)]
    r_mat = [_dot(a, b) for a, b in zip(a_in, u)]

    state = [s_ref[p] for p in range(N_PAIRS)]
    for i, (ci, p) in enumerate(units):
        s = state[p]
        o2 = _dot(p_mat[i], s) + r_mat[i]
        state[p] = s * jnp.exp(g_last[i]) - _dot(m_mat[i], s) + n_mat[i]
        o = o2[0:GDN_CHUNK] + o2[GDN_CHUNK:n2]
        ms = _dot_ones(o * o, ones) * (1.0 / HEAD_DIM)
        on = o * lax.rsqrt(ms + NORM_EPS) * nw_ref[...]
        o_ref[0, rows(ci), pair(p, 0)] = (on * _silu(gate_ref[0, rows(ci), pair(p, 0)])
                                          * ms_ref[:, pair(p, 0)]).astype(o_ref.dtype)
    for p in range(N_PAIRS):
        s_ref[p] = state[p]


def _gdn(qkv, conv_w, g, beta, gate, norm_w2, mscale, blk):
    b, t, _ = qkv.shape
    tile = lambda width: pl.BlockSpec((1, blk, width), lambda i, j: (i, j, 0))
    const = lambda shape: pl.BlockSpec(shape, lambda i, j: (0, 0))
    return pl.pallas_call(
        _gdn_kernel,
        grid=(b, t // blk),
        in_specs=[tile(3 * GDN_WIDTH), const((GDN_SHORT_CONV, 3 * GDN_WIDTH)),
                  tile(GDN_WIDTH), tile(GDN_WIDTH), tile(GDN_WIDTH),
                  const((1, PAIR)), const((1, GDN_WIDTH))],
        out_specs=tile(GDN_WIDTH),
        out_shape=jax.ShapeDtypeStruct((b, t, GDN_WIDTH), MXU_DTYPE),
        scratch_shapes=[pltpu.VMEM((N_PAIRS, PAIR, PAIR), F32),
                        pltpu.VMEM((SUBLANES, 3 * GDN_WIDTH), F32),
                        pltpu.VMEM((blk + SUBLANES, 3 * GDN_WIDTH), F32)],
        compiler_params=_cparams("parallel", "arbitrary"),
        name="gdn",
    )(qkv, conv_w, g, beta, gate, norm_w2, mscale)


CONV_HALO = 32


def _conv_kernel(glu_ref, w_ref, b_ref, lg_ref, lb_ref, ms_ref, o_ref, buf_ref, sh_ref):
    t = pl.program_id(1)
    tt = glu_ref.shape[1]

    @pl.when(t == 0)
    def _():
        buf_ref[0:CONV_HALO, :] = jnp.zeros((CONV_HALO, CONV_WIDTH), F32)

    @pl.when(t > 0)
    def _():
        buf_ref[0:CONV_HALO, :] = buf_ref[tt:tt + CONV_HALO, :]

    x = glu_ref[0]
    buf_ref[CONV_HALO:CONV_HALO + tt, :] = x[:, 0:CONV_WIDTH] * _sigmoid(x[:, CONV_WIDTH:2 * CONV_WIDTH])
    span = tt + CONV_HALO - SUBLANES
    for s in range(1, SUBLANES):
        sh_ref[s - 1] = buf_ref[s:s + span, :]
    acc = jnp.zeros((tt, CONV_WIDTH), F32) + b_ref[...]
    for j in range(CONV_KERNEL):
        lo = CONV_HALO - (CONV_KERNEL - 1) + j
        base, phase = lo - lo % SUBLANES, lo % SUBLANES
        tap = buf_ref[base:base + tt, :] if phase == 0 else sh_ref[phase - 1, base:base + tt, :]
        acc = acc + w_ref[j:j + 1, :] * tap
    y = _silu(_layer_norm_rows(acc, lg_ref[...], lb_ref[...]))
    o_ref[0] = (y * ms_ref[...]).astype(o_ref.dtype)


def _conformer_conv(glu, w, bias, ln_g, ln_b, mscale, tt):
    b, t, _ = glu.shape
    vec = pl.BlockSpec((1, CONV_WIDTH), lambda i, j: (0, 0))
    return pl.pallas_call(
        _conv_kernel,
        grid=(b, t // tt),
        in_specs=[pl.BlockSpec((1, tt, 2 * CONV_WIDTH), lambda i, j: (i, j, 0)),
                  pl.BlockSpec((CONV_KERNEL, CONV_WIDTH), lambda i, j: (0, 0)),
                  vec, vec, vec, vec],
        out_specs=pl.BlockSpec((1, tt, CONV_WIDTH), lambda i, j: (i, j, 0)),
        out_shape=jax.ShapeDtypeStruct((b, t, CONV_WIDTH), MXU_DTYPE),
        scratch_shapes=[pltpu.VMEM((tt + CONV_HALO, CONV_WIDTH), F32),
                        pltpu.VMEM((SUBLANES - 1, tt + CONV_HALO - SUBLANES, CONV_WIDTH), F32)],
        compiler_params=_cparams("parallel", "arbitrary"),
        name="conformer_conv",
    )(glu, w, bias, ln_g, ln_b, mscale)


FOX_ROW_GROUP = 32


def _fox_kernel(q_ref, k_ref, v_ref, ms_ref, o_ref, s0_ref, s1_ref, p0_ref, p1_ref, a0_ref, a1_ref, m_ref, acc_ref,
                *, tq):
    i = pl.program_id(2)
    heads = range(2)
    rg = FOX_ROW_GROUP
    m_ref[...] = jnp.full(m_ref.shape, -jnp.inf, F32)
    acc_ref[...] = jnp.zeros(acc_ref.shape, F32)

    def scores(j, s_ref):
        start = pl.multiple_of(j * tq, tq)
        for h in heads:
            s_ref[h] = lax.dot_general(q_ref[0, h], k_ref[0, h, pl.ds(start, tq), :],
                                       (((1,), (1,)), ((), ())), preferred_element_type=F32)

    def softmax(s_ref, p_ref, a_ref, masked):
        for h in heads:
            for g in range(tq // rg):
                rows = slice(g * rg, (g + 1) * rg)
                s = s_ref[h, rows, :]
                if masked:
                    row_id = g * rg + lax.broadcasted_iota(jnp.int32, (rg, tq), 0)
                    col_id = lax.broadcasted_iota(jnp.int32, (rg, tq), 1)
                    s = jnp.where(col_id <= row_id, s, -jnp.inf)
                m_old = m_ref[h, rows, :]
                m_new = jnp.maximum(m_old, jnp.max(s, axis=-1, keepdims=True))
                a_ref[h, rows, :] = jnp.exp(m_old - m_new)
                m_ref[h, rows, :] = m_new
                m_wide = jnp.concatenate([m_new] * (tq // LANES), axis=1)
                p_ref[h, rows, :] = jnp.exp(s - m_wide).astype(p_ref.dtype)

    def weighted_values(j, p_ref, a_ref):
        start = pl.multiple_of(j * tq, tq)
        for h in heads:
            acc_ref[h] = a_ref[h] * acc_ref[h] + jnp.dot(p_ref[h], v_ref[0, h, pl.ds(start, tq), :],
                                                         preferred_element_type=F32)

    scores(0, s0_ref)

    def body(t, carry):
        softmax(s0_ref, p0_ref, a0_ref, False)
        scores(2 * t + 1, s1_ref)
        weighted_values(2 * t, p0_ref, a0_ref)
        softmax(s1_ref, p1_ref, a1_ref, False)
        scores(2 * t + 2, s0_ref)
        weighted_values(2 * t + 1, p1_ref, a1_ref)
        return carry

    lax.fori_loop(0, i // 2, body, 0)

    @pl.when(i % 2 == 0)
    def _():
        softmax(s0_ref, p0_ref, a0_ref, True)
        weighted_values(i, p0_ref, a0_ref)

    @pl.when(i % 2 == 1)
    def _():
        softmax(s0_ref, p0_ref, a0_ref, False)
        scores(i, s1_ref)
        weighted_values(i - 1, p0_ref, a0_ref)
        softmax(s1_ref, p1_ref, a1_ref, True)
        weighted_values(i, p1_ref, a1_ref)

    o = [acc_ref[h] / acc_ref[h][:, AUG_BIAS_LANE:AUG_BIAS_LANE + 1] for h in heads]
    lane = lax.broadcasted_iota(jnp.int32, (1, PAIR), 1)
    o_pair = jnp.where(lane < HEAD_DIM, o[0], pltpu.roll(o[1], HEAD_DIM, axis=1))
    o_ref[0] = (o_pair * ms_ref[...]).astype(o_ref.dtype)


def _fox(qa, ka, va, mscale, tq):
    b, _, t, _ = qa.shape
    kern = functools.partial(_fox_kernel, tq=tq)
    return pl.pallas_call(
        kern,
        grid=(b, N_PAIRS, t // tq),
        in_specs=[pl.BlockSpec((1, 2, tq, LANES), lambda bi, p, i: (bi, p, i, 0)),
                  pl.BlockSpec((1, 2, t, LANES), lambda bi, p, i: (bi, p, 0, 0)),
                  pl.BlockSpec((1, 2, t, LANES), lambda bi, p, i: (bi, p, 0, 0)),
                  pl.BlockSpec((1, PAIR), lambda bi, p, i: (0, p))],
        out_specs=pl.BlockSpec((1, tq, PAIR), lambda bi, p, i: (bi, i, p)),
        out_shape=jax.ShapeDtypeStruct((b, t, FOX_WIDTH), MXU_DTYPE),
        scratch_shapes=[pltpu.VMEM((2, tq, tq), F32), pltpu.VMEM((2, tq, tq), F32),
                        pltpu.VMEM((2, tq, tq), jnp.bfloat16), pltpu.VMEM((2, tq, tq), jnp.bfloat16),
                        pltpu.VMEM((2, tq, LANES), F32), pltpu.VMEM((2, tq, LANES), F32),
                        pltpu.VMEM((2, tq, LANES), F32), pltpu.VMEM((2, tq, LANES), F32)],
        compiler_params=_cparams("parallel", "parallel", "arbitrary"),
        name="fox_attention",
    )(qa, ka, va, mscale)


def _out_proj_kernel(oa_ref, ob_ref, oc_ref, x_ref, w_ref, g_ref, b_ref, o_ref):
    mix = jnp.dot(oa_ref[...], w_ref[0:GDN_WIDTH, :], preferred_element_type=F32)
    mix = mix + jnp.dot(ob_ref[...], w_ref[GDN_WIDTH:GDN_WIDTH + CONV_WIDTH, :], preferred_element_type=F32)
    mix = mix + jnp.dot(oc_ref[...], w_ref[GDN_WIDTH + CONV_WIDTH:D_MODEL, :], preferred_element_type=F32)
    o_ref[...] = _layer_norm_rows(DEEPNORM_ALPHA * x_ref[...] + mix, g_ref[...], b_ref[...])


def _out_proj(oa, ob, oc, xf, w, g, bvec, tm):
    n = xf.shape[0]
    row = lambda i: (i, 0)
    const = lambda i: (0, 0)
    return pl.pallas_call(
        _out_proj_kernel,
        grid=(n // tm,),
        in_specs=[pl.BlockSpec((tm, GDN_WIDTH), row), pl.BlockSpec((tm, CONV_WIDTH), row),
                  pl.BlockSpec((tm, FOX_WIDTH), row), pl.BlockSpec((tm, D_MODEL), row),
                  pl.BlockSpec((D_MODEL, D_MODEL), const),
                  pl.BlockSpec((1, D_MODEL), const), pl.BlockSpec((1, D_MODEL), const)],
        out_specs=pl.BlockSpec((tm, D_MODEL), row),
        out_shape=jax.ShapeDtypeStruct((n, D_MODEL), F32),
        compiler_params=_cparams("parallel"),
        name="out_proj_ln",
    )(oa, ob, oc, xf, w, g, bvec)


FFN_CHUNK = 256


def _swiglu_chunks(xb, w1, w3, w2, width):
    acc = None
    for c in range(width // FFN_CHUNK):
        cols = slice(c * FFN_CHUNK, (c + 1) * FFN_CHUNK)
        h = _silu(jnp.dot(xb, w1(slice(None), cols), preferred_element_type=F32)) * jnp.dot(
            xb, w3(slice(None), cols), preferred_element_type=F32)
        part = jnp.dot(h.astype(xb.dtype), w2(cols, slice(None)), preferred_element_type=F32)
        acc = part if acc is None else acc + part
    return acc


def _ffn_kernel(x_ref, w1_ref, w3_ref, w2_ref, g_ref, b_ref, o_ref):
    x = x_ref[...]
    ff = _swiglu_chunks(x.astype(w1_ref.dtype), lambda r, c: w1_ref[r, c], lambda r, c: w3_ref[r, c],
                        lambda r, c: w2_ref[r, c], w1_ref.shape[1])
    o_ref[...] = _layer_norm_rows(DEEPNORM_ALPHA * x + ff, g_ref[...], b_ref[...])


def _dense_ffn(xf, w1, w3, w2, g, bvec, tm):
    n = xf.shape[0]
    ff = w1.shape[1]
    const = lambda shape: pl.BlockSpec(shape, lambda i: (0, 0), pipeline_mode=pl.Buffered(1))
    return pl.pallas_call(
        _ffn_kernel,
        grid=(n // tm,),
        in_specs=[pl.BlockSpec((tm, D_MODEL), lambda i: (i, 0)),
                  const((D_MODEL, ff)), const((D_MODEL, ff)), const((ff, D_MODEL)),
                  const((1, D_MODEL)), const((1, D_MODEL))],
        out_specs=pl.BlockSpec((tm, D_MODEL), lambda i: (i, 0)),
        out_shape=jax.ShapeDtypeStruct((n, D_MODEL), F32),
        compiler_params=_cparams("parallel"),
        name="dense_ffn_ln",
    )(xf, w1, w3, w2, g, bvec)


def _router_kernel(x_ref, wr_ref, info_ref, cnt_ref, run_ref):
    i = pl.program_id(0)
    tm = x_ref.shape[0]

    @pl.when(i == 0)
    def _():
        run_ref[...] = jnp.zeros_like(run_ref)

    logits = jnp.dot(x_ref[...].astype(MXU_DTYPE), wr_ref[...], preferred_element_type=F32)
    lane = lax.broadcasted_iota(jnp.int32, (tm, LANES), 1)
    logits = jnp.where(lane < N_EXPERTS, logits, -jnp.inf)
    m1 = jnp.max(logits, axis=-1, keepdims=True)
    e1 = jnp.min(jnp.where(logits == m1, lane, LANES), axis=-1, keepdims=True)
    rest = jnp.where(lane == e1, -jnp.inf, logits)
    m2 = jnp.max(rest, axis=-1, keepdims=True)
    e2 = jnp.min(jnp.where(rest == m2, lane, LANES), axis=-1, keepdims=True)
    z = jnp.exp(m2 - m1)
    g1 = 1.0 / (1.0 + z)
    g2 = z / (1.0 + z)
    onehot = ((lane == e1) | (lane == e2)).astype(F32)
    r = lax.broadcasted_iota(jnp.int32, (tm, tm), 0)
    c = lax.broadcasted_iota(jnp.int32, (tm, tm), 1)
    before = _dot_f32((c < r).astype(F32), onehot) + run_ref[...]
    rank1 = jnp.sum(jnp.where(lane == e1, before, 0.0), axis=-1, keepdims=True)
    rank2 = jnp.sum(jnp.where(lane == e2, before, 0.0), axis=-1, keepdims=True)
    run_ref[...] = run_ref[...] + jnp.sum(onehot, axis=0, keepdims=True)
    cnt_ref[...] = run_ref[...]
    info = jnp.where(lane == 0, e1.astype(F32),
                     jnp.where(lane == 1, e2.astype(F32),
                               jnp.where(lane == 2, rank1,
                                         jnp.where(lane == 3, rank2,
                                                   jnp.where(lane == 4, g1, jnp.where(lane == 5, g2, 0.0))))))
    info_ref[...] = info


def _router(xf, wr_pad, tm):
    n = xf.shape[0]
    return pl.pallas_call(
        _router_kernel,
        grid=(n // tm,),
        in_specs=[pl.BlockSpec((tm, D_MODEL), lambda i: (i, 0)),
                  pl.BlockSpec((D_MODEL, LANES), lambda i: (0, 0))],
        out_specs=[pl.BlockSpec((tm, LANES), lambda i: (i, 0)),
                   pl.BlockSpec((1, LANES), lambda i: (0, 0))],
        out_shape=[jax.ShapeDtypeStruct((n, LANES), F32), jax.ShapeDtypeStruct((1, LANES), F32)],
        scratch_shapes=[pltpu.VMEM((1, LANES), F32)],
        compiler_params=_cparams("arbitrary"),
        name="moe_router",
    )(xf, wr_pad)


DMA_LOOP_UNROLL = 8


def _dispatch_kernel(d1_ref, d2_ref, x_ref, zeros_ref, xs_ref, sem):
    del zeros_ref
    tm = x_ref.shape[0]

    def row_copy(r, dst):
        return pltpu.make_async_copy(x_ref.at[pl.ds(r, 1), :], xs_ref.at[pl.ds(dst, 1), :], sem)

    def issue(r, carry):
        row_copy(r, d1_ref[0, 0, r]).start()
        row_copy(r, d2_ref[0, 0, r]).start()
        return carry

    lax.fori_loop(0, tm, issue, 0, unroll=DMA_LOOP_UNROLL)

    def drain(r, carry):
        row_copy(r, d1_ref[0, 0, r]).wait()
        row_copy(r, d2_ref[0, 0, r]).wait()
        return carry

    lax.fori_loop(0, tm, drain, 0, unroll=DMA_LOOP_UNROLL)


def _dispatch(xf, d1, d2, rows, tm):
    n = xf.shape[0]
    idx = lambda a: a.reshape(n // tm, 1, tm)
    zeros = jnp.zeros((rows, D_MODEL), xf.dtype)
    smem = lambda: pl.BlockSpec((1, 1, tm), lambda i: (i, 0, 0), memory_space=pltpu.SMEM)
    return pl.pallas_call(
        _dispatch_kernel,
        grid=(n // tm,),
        in_specs=[smem(), smem(),
                  pl.BlockSpec((tm, D_MODEL), lambda i: (i, 0)),
                  pl.BlockSpec(memory_space=pl.ANY)],
        out_specs=pl.BlockSpec(memory_space=pl.ANY),
        out_shape=jax.ShapeDtypeStruct((rows, D_MODEL), xf.dtype),
        scratch_shapes=[pltpu.SemaphoreType.DMA(())],
        input_output_aliases={3: 0},
        compiler_params=_cparams("arbitrary"),
        name="moe_dispatch",
    )(idx(d1), idx(d2), xf, zeros)


def _expert_kernel(te_ref, nu_ref, xs_ref, w1_ref, w3_ref, w2_ref, ys_ref, xb_ref, acc_ref):
    del te_ref
    i = pl.program_id(0)
    f = pl.program_id(1)
    used = i < nu_ref[0]

    @pl.when(f == 0)
    def _():
        xb_ref[...] = xs_ref[...].astype(xb_ref.dtype)
        acc_ref[...] = jnp.zeros_like(acc_ref)

    @pl.when(used)
    def _():
        acc_ref[...] += _swiglu_chunks(xb_ref[...], lambda r, c: w1_ref[0, r, c], lambda r, c: w3_ref[0, r, c],
                                       lambda r, c: w2_ref[0, r, c], w1_ref.shape[2])

    @pl.when(f == pl.num_programs(1) - 1)
    def _():
        ys_ref[...] = acc_ref[...]


def _experts(xs, tile_expert, n_used, w1, w3, w2, tm, tf):
    rows = xs.shape[0]
    ff = w1.shape[2]
    n_tiles = rows // tm

    def x_map(i, f, te, nu):
        return (jnp.maximum(jnp.minimum(i, nu[0] - 1), 0), 0)

    def w13_map(i, f, te, nu):
        return (te[i], 0, jnp.where(i < nu[0], f, ff // tf - 1))

    def w2_map(i, f, te, nu):
        return (te[i], jnp.where(i < nu[0], f, ff // tf - 1), 0)

    grid_spec = pltpu.PrefetchScalarGridSpec(
        num_scalar_prefetch=2,
        grid=(n_tiles, ff // tf),
        in_specs=[pl.BlockSpec((tm, D_MODEL), x_map),
                  pl.BlockSpec((1, D_MODEL, tf), w13_map),
                  pl.BlockSpec((1, D_MODEL, tf), w13_map),
                  pl.BlockSpec((1, tf, D_MODEL), w2_map)],
        out_specs=pl.BlockSpec((tm, D_MODEL), lambda i, f, te, nu: (i, 0)),
        scratch_shapes=[pltpu.VMEM((tm, D_MODEL), MXU_DTYPE), pltpu.VMEM((tm, D_MODEL), F32)],
    )
    return pl.pallas_call(
        _expert_kernel,
        grid_spec=grid_spec,
        out_shape=jax.ShapeDtypeStruct((rows, D_MODEL), F32),
        compiler_params=_cparams("arbitrary", "arbitrary"),
        name="moe_experts",
    )(tile_expert, n_used, xs, w1, w3, w2)


def _combine_kernel(d1_ref, d2_ref, x_ref, info_ref, ys_ref, g_ref, b_ref, o_ref, y1_ref, y2_ref, sem):
    tm = x_ref.shape[0]

    def row_copy(src, r, buf):
        return pltpu.make_async_copy(ys_ref.at[pl.ds(src, 1), :], buf.at[pl.ds(r, 1), :], sem)

    def issue(r, carry):
        row_copy(d1_ref[0, 0, r], r, y1_ref).start()
        row_copy(d2_ref[0, 0, r], r, y2_ref).start()
        return carry

    lax.fori_loop(0, tm, issue, 0, unroll=DMA_LOOP_UNROLL)

    def drain(r, carry):
        row_copy(d1_ref[0, 0, r], r, y1_ref).wait()
        row_copy(d2_ref[0, 0, r], r, y2_ref).wait()
        return carry

    lax.fori_loop(0, tm, drain, 0, unroll=DMA_LOOP_UNROLL)
    info = info_ref[...]
    ff = info[:, 4:5] * y1_ref[...] + info[:, 5:6] * y2_ref[...]
    o_ref[...] = _layer_norm_rows(DEEPNORM_ALPHA * x_ref[...] + ff, g_ref[...], b_ref[...])


def _combine(xf, info, ys, d1, d2, g, bvec, tm):
    n = xf.shape[0]
    idx = lambda a: a.reshape(n // tm, 1, tm)
    smem = lambda: pl.BlockSpec((1, 1, tm), lambda i: (i, 0, 0), memory_space=pltpu.SMEM)
    return pl.pallas_call(
        _combine_kernel,
        grid=(n // tm,),
        in_specs=[smem(), smem(),
                  pl.BlockSpec((tm, D_MODEL), lambda i: (i, 0)),
                  pl.BlockSpec((tm, LANES), lambda i: (i, 0)),
                  pl.BlockSpec(memory_space=pl.ANY),
                  pl.BlockSpec((1, D_MODEL), lambda i: (0, 0)),
                  pl.BlockSpec((1, D_MODEL), lambda i: (0, 0))],
        out_specs=pl.BlockSpec((tm, D_MODEL), lambda i: (i, 0)),
        out_shape=jax.ShapeDtypeStruct((n, D_MODEL), F32),
        scratch_shapes=[pltpu.VMEM((tm, D_MODEL), F32), pltpu.VMEM((tm, D_MODEL), F32),
                        pltpu.SemaphoreType.DMA(())],
        compiler_params=_cparams("arbitrary"),
        name="moe_combine_ln",
    )(idx(d1), idx(d2), xf, info, ys, g, bvec)


MOE_TILE = 512


def _moe_ffn(xf, w_router, w1, w3, w2, g, bvec, tm_tok, tf):
    n = xf.shape[0]
    wr_pad = jnp.zeros((D_MODEL, LANES), MXU_DTYPE).at[:, :N_EXPERTS].set(w_router.astype(MXU_DTYPE))
    info, counts = _router(xf, wr_pad, tm_tok)
    sizes = counts[0, :N_EXPERTS].astype(jnp.int32)
    tiles_per = (sizes + MOE_TILE - 1) // MOE_TILE
    tile_end = jnp.cumsum(tiles_per)
    seg_start = (tile_end - tiles_per) * MOE_TILE
    e1 = info[:, 0].astype(jnp.int32)
    e2 = info[:, 1].astype(jnp.int32)
    d1 = seg_start[e1] + info[:, 2].astype(jnp.int32)
    d2 = seg_start[e2] + info[:, 3].astype(jnp.int32)
    n_tiles = (n * TOP_K) // MOE_TILE + N_EXPERTS
    rows = n_tiles * MOE_TILE
    tile_expert = jnp.minimum(jnp.searchsorted(tile_end, jnp.arange(n_tiles, dtype=jnp.int32), side='right'),
                              N_EXPERTS - 1).astype(jnp.int32)
    n_used = tile_end[N_EXPERTS - 1:].astype(jnp.int32)
    xs = _dispatch(xf, d1, d2, rows, tm_tok)
    ys = _experts(xs, tile_expert, n_used, w1, w3, w2, MOE_TILE, tf)
    return _combine(xf, info, ys, d1, d2, g, bvec, tm_tok)


def _pack_in_proj(w_in):
    cuts = np.cumsum([0, GDN_WIDTH, GDN_WIDTH, GDN_WIDTH, GDN_HEADS, GDN_HEADS, GDN_WIDTH,
                      2 * CONV_WIDTH, FOX_WIDTH, FOX_WIDTH, FOX_WIDTH, FOX_HEADS])
    seg = lambda i: w_in[:, cuts[i]:cuts[i + 1]]
    w_main = jnp.concatenate([seg(0), seg(1), seg(2), seg(5), seg(6), seg(7), seg(8), seg(9)], axis=1)
    pad = jnp.zeros((D_MODEL, SMALL_COLS - 2 * GDN_HEADS - FOX_HEADS), w_in.dtype)
    w_small = jnp.concatenate([seg(3), seg(4), seg(10), pad], axis=1).astype(F32)
    ws_hi = w_small.astype(MXU_DTYPE)
    ws_lo = (w_small - ws_hi.astype(F32)).astype(MXU_DTYPE)
    return w_main.astype(MXU_DTYPE), jnp.concatenate([ws_hi, ws_lo], axis=1)


def _lane_row(vals, offset):
    return jnp.zeros((1, LANES), F32).at[0, offset:offset + vals.shape[0]].set(vals.astype(F32))


def _mixer(xf, b, t, w_in, mix_scale, w_out, gdn_conv_w, gdn_a_log, gdn_dt_bias, gdn_norm_w,
           cnv_dw_w, cnv_dw_b, cnv_ln_g, cnv_ln_b, fox_f_bias, ln_g, ln_b, tiles):
    w_main, w_small = _pack_in_proj(w_in)
    qkv, gate, glu, fq, fk, fv, small = _in_proj(xf, w_main, w_small, tiles["tm"])
    r3 = lambda a: a.reshape(b, t, a.shape[-1])
    g_cum, beta, qa, ka, va = _gates(r3(small), _lane_row(gdn_a_log, 0), _lane_row(gdn_dt_bias, 0),
                                     _lane_row(fox_f_bias, 2 * GDN_HEADS), r3(fq), r3(fk), r3(fv), tiles["tt"])
    ms = mix_scale.reshape(1, D_MODEL).astype(F32)
    norm_w2 = jnp.tile(gdn_norm_w.astype(F32), 2).reshape(1, PAIR)
    o_a = _gdn(r3(qkv), gdn_conv_w, g_cum, beta, r3(gate), norm_w2, ms[:, :GDN_WIDTH], tiles["gdn_blk"])
    row = lambda v: v.reshape(1, -1).astype(F32)
    o_b = _conformer_conv(r3(glu), cnv_dw_w, row(cnv_dw_b), row(cnv_ln_g), row(cnv_ln_b),
                          ms[:, GDN_WIDTH:GDN_WIDTH + CONV_WIDTH], tiles["tt"])
    o_c = _fox(qa, ka, va, ms[:, GDN_WIDTH + CONV_WIDTH:], tiles["tq"])
    flat = lambda a: a.reshape(b * t, a.shape[-1])
    return _out_proj(flat(o_a), flat(o_b), flat(o_c), xf, w_out.astype(MXU_DTYPE), row(ln_g), row(ln_b),
                     tiles["tm"])


def _tiles(t):
    return dict(tm=min(512, t), tt=min(512, t), gdn_blk=min(256, t), tq=min(512, t),
                tf_moe=1792)


def kernel(x, w_in, mix_scale, w_out, gdn_conv_w, gdn_a_log, gdn_dt_bias, gdn_norm_w, cnv_dw_w, cnv_dw_b,
           cnv_ln_g, cnv_ln_b, fox_f_bias, ln_mix_g, ln_mix_b, ln_ffn_g, ln_ffn_b, ffn_w1, ffn_w3, ffn_w2,
           moe_router, moe_w1, moe_w3, moe_w2):
    b, t, d = x.shape
    tiles = _tiles(t)
    row = lambda v: v.reshape(1, -1).astype(F32)
    xf = x.reshape(b * t, d)
    for l in range(DEPTH):
        xf = _mixer(xf, b, t, w_in[l], mix_scale[l], w_out[l], gdn_conv_w[l], gdn_a_log[l], gdn_dt_bias[l],
                    gdn_norm_w[l], cnv_dw_w[l], cnv_dw_b[l], cnv_ln_g[l], cnv_ln_b[l], fox_f_bias[l],
                    ln_mix_g[l], ln_mix_b[l], tiles)
        if l % 2 == 0:
            e = l // 2
            xf = _dense_ffn(xf, ffn_w1[e].astype(MXU_DTYPE), ffn_w3[e].astype(MXU_DTYPE),
                            ffn_w2[e].astype(MXU_DTYPE), row(ln_ffn_g[l]), row(ln_ffn_b[l]),
                            tiles["tm"])
        else:
            e = l // 2
            xf = _moe_ffn(xf, moe_router[e], moe_w1[e].astype(MXU_DTYPE), moe_w3[e].astype(MXU_DTYPE),
                          moe_w2[e].astype(MXU_DTYPE), row(ln_ffn_g[l]), row(ln_ffn_b[l]),
                          tiles["tm"], tiles["tf_moe"])
    return xf.reshape(b, t, d)
```

```python
import functools

import jax
import jax.numpy as jnp
import numpy as np
from jax import lax
from jax.experimental import pallas as pl
from jax.experimental.pallas import tpu as pltpu

D_MODEL = 1024
DEPTH = 2
HEAD_DIM = 64
GDN_WIDTH = 384
CONV_WIDTH = 256
FOX_WIDTH = 384
GDN_HEADS = 6
FOX_HEADS = 6
GDN_SHORT_CONV = 4
GDN_CHUNK = 64
CONV_KERNEL = 31
FFN_DENSE = 2816
N_EXPERTS = 8
TOP_K = 2
FFN_EXPERT = 3584
DEEPNORM_ALPHA = (2 * DEPTH) ** 0.25
LN_EPS = 1e-5
NORM_EPS = 1e-6

LANES = 128
SUBLANES = 8
PAIR = 2 * HEAD_DIM
N_PAIRS = GDN_WIDTH // PAIR
SMALL_COLS = LANES

_C_QKV = 0
_C_GATE = _C_QKV + 3 * GDN_WIDTH
_C_GLU = _C_GATE + GDN_WIDTH
_C_FQ = _C_GLU + 2 * CONV_WIDTH
_C_FK = _C_FQ + FOX_WIDTH
_C_FV = _C_FK + FOX_WIDTH
_C_END = _C_FV + FOX_WIDTH

MXU_DTYPE = jnp.bfloat16
F32 = jnp.float32

VMEM_LIMIT = 56 * 1024 * 1024


def _cparams(*sem):
    return pltpu.CompilerParams(dimension_semantics=sem, vmem_limit_bytes=VMEM_LIMIT)


def _dot(a, b):
    return jnp.dot(a.astype(MXU_DTYPE), b.astype(MXU_DTYPE), preferred_element_type=F32)


def _dot_nt(a, b):
    return lax.dot_general(a.astype(MXU_DTYPE), b.astype(MXU_DTYPE),
                           (((1,), (1,)), ((), ())), preferred_element_type=F32)


def _split(a):
    hi = a.astype(jnp.bfloat16)
    return hi, (a - hi.astype(F32)).astype(jnp.bfloat16)


def _split3(a):
    hi = a.astype(jnp.bfloat16)
    rest = a - hi.astype(F32)
    mid = rest.astype(jnp.bfloat16)
    lo = (rest - mid.astype(F32)).astype(jnp.bfloat16)
    return hi, mid, lo


def _dot_ones(a, ones_b16):
    ah, al = _split(a)
    d = lambda x: jnp.dot(x, ones_b16, preferred_element_type=F32)
    return d(ah) + d(al)


def _sel_rhs(a, sel_b16):
    return sum(jnp.dot(t, sel_b16, preferred_element_type=F32) for t in _split3(a))


def _sel_lhs(sel_b16, b):
    return sum(jnp.dot(sel_b16, t, preferred_element_type=F32) for t in _split3(b))


def _sigmoid(x):
    return 1.0 / (1.0 + jnp.exp(-x))


def _silu(x):
    return x * _sigmoid(x)


def _softplus(x):
    return jnp.maximum(x, 0.0) + jnp.log1p(jnp.exp(-jnp.abs(x)))


def _log_sigmoid(x):
    return -_softplus(-x)


def _layer_norm_rows(y, g, b):
    mu = jnp.mean(y, axis=-1, keepdims=True)
    d = y - mu
    var = jnp.mean(d * d, axis=-1, keepdims=True)
    return d * lax.rsqrt(var + LN_EPS) * g + b


def _head_ones():
    r = lax.broadcasted_iota(jnp.int32, (PAIR, PAIR), 0) // HEAD_DIM
    c = lax.broadcasted_iota(jnp.int32, (PAIR, PAIR), 1) // HEAD_DIM
    return (r == c).astype(F32)


CONV_HALO = 32
CONV_ROW_BLOCK = 128
SHORT_CONV_ROW_BLOCK = 128


def _in_proj_kernel(x_ref, w_ref, ws_ref, gw_ref, cw_ref, cb_ref, lg_ref, lb_ref, ms_ref,
                    qkv_ref, gate_ref, ob_ref, fq_ref, fk_ref, fv_ref, small_ref,
                    qbuf_ref, cbuf_ref, sh_ref, *, tiles_per_seq):
    first = pl.program_id(0) % tiles_per_seq == 0
    tm = x_ref.shape[0]
    x = x_ref[...]
    xb = x.astype(MXU_DTYPE)

    def mm(lo, hi):
        return jnp.dot(xb, w_ref[:, lo:hi], preferred_element_type=F32)

    @pl.when(first)
    def _():
        cbuf_ref[0:CONV_HALO, :] = jnp.zeros((CONV_HALO, CONV_WIDTH), F32)
        qbuf_ref[0:SUBLANES, :] = jnp.zeros((SUBLANES, 3 * GDN_WIDTH), F32)

    @pl.when(jnp.logical_not(first))
    def _():
        cbuf_ref[0:CONV_HALO, :] = cbuf_ref[tm:tm + CONV_HALO, :]
        qbuf_ref[0:SUBLANES, :] = qbuf_ref[tm:tm + SUBLANES, :]

    glu = mm(_C_GLU, _C_FQ)
    cbuf_ref[CONV_HALO:CONV_HALO + tm, :] = glu[:, 0:CONV_WIDTH] * _sigmoid(glu[:, CONV_WIDTH:2 * CONV_WIDTH])
    span = tm + CONV_HALO - SUBLANES
    for s in range(1, SUBLANES):
        sh_ref[s - 1] = cbuf_ref[s:s + span, :]
    for r0 in range(0, tm, CONV_ROW_BLOCK):
        acc = jnp.zeros((CONV_ROW_BLOCK, CONV_WIDTH), F32) + cb_ref[...]
        for j in range(CONV_KERNEL):
            lo = r0 + CONV_HALO - (CONV_KERNEL - 1) + j
            base, phase = lo - lo % SUBLANES, lo % SUBLANES
            tap = (cbuf_ref[base:base + CONV_ROW_BLOCK, :] if phase == 0
                   else sh_ref[phase - 1, base:base + CONV_ROW_BLOCK, :])
            acc = acc + cw_ref[j:j + 1, :] * tap
        y = _silu(_layer_norm_rows(acc, lg_ref[...], lb_ref[...])) * ms_ref[...]
        ob_ref[r0:r0 + CONV_ROW_BLOCK, :] = y.astype(ob_ref.dtype)

    qbuf_ref[SUBLANES:SUBLANES + tm, :] = mm(_C_QKV, _C_GATE)
    for r0 in range(0, tm, SHORT_CONV_ROW_BLOCK):
        acc = jnp.zeros((SHORT_CONV_ROW_BLOCK, 3 * GDN_WIDTH), F32)
        for j in range(GDN_SHORT_CONV):
            lo = r0 + SUBLANES - (GDN_SHORT_CONV - 1) + j
            acc = acc + gw_ref[j:j + 1, :] * qbuf_ref[lo:lo + SHORT_CONV_ROW_BLOCK, :]
        qkv_ref[r0:r0 + SHORT_CONV_ROW_BLOCK, :] = _silu(acc)

    gate_ref[...] = mm(_C_GATE, _C_GLU)
    fq_ref[...] = (mm(_C_FQ, _C_FK) * HEAD_DIM ** -0.5).astype(fq_ref.dtype)
    fk_ref[...] = mm(_C_FK, _C_FV).astype(fk_ref.dtype)
    fv_ref[...] = mm(_C_FV, _C_END).astype(fv_ref.dtype)
    x_lo = (x - xb.astype(F32)).astype(MXU_DTYPE)
    hh_hl = jnp.dot(xb, ws_ref[...], preferred_element_type=F32)
    lh = jnp.dot(x_lo, ws_ref[:, 0:SMALL_COLS], preferred_element_type=F32)
    small_ref[...] = hh_hl[:, 0:SMALL_COLS] + hh_hl[:, SMALL_COLS:2 * SMALL_COLS] + lh


def _in_proj(xf, w_main, w_small, gdn_conv_w, cnv_w, cnv_b, cnv_ln_g, cnv_ln_b, cnv_scale, tm, tiles_per_seq):
    n = xf.shape[0]
    row = lambda i: (i, 0)
    const = lambda shape: pl.BlockSpec(shape, lambda i: (0, 0))
    outs = [
        jax.ShapeDtypeStruct((n, 3 * GDN_WIDTH), F32),
        jax.ShapeDtypeStruct((n, GDN_WIDTH), F32),
        jax.ShapeDtypeStruct((n, CONV_WIDTH), MXU_DTYPE),
        jax.ShapeDtypeStruct((n, FOX_WIDTH), MXU_DTYPE),
        jax.ShapeDtypeStruct((n, FOX_WIDTH), MXU_DTYPE),
        jax.ShapeDtypeStruct((n, FOX_WIDTH), MXU_DTYPE),
        jax.ShapeDtypeStruct((n, SMALL_COLS), F32),
    ]
    vec = const((1, CONV_WIDTH))
    return pl.pallas_call(
        functools.partial(_in_proj_kernel, tiles_per_seq=tiles_per_seq),
        grid=(n // tm,),
        in_specs=[pl.BlockSpec((tm, D_MODEL), row),
                  const((D_MODEL, _C_END)), const((D_MODEL, 2 * SMALL_COLS)),
                  const((GDN_SHORT_CONV, 3 * GDN_WIDTH)), const((CONV_KERNEL, CONV_WIDTH)), vec, vec, vec, vec],
        out_specs=[pl.BlockSpec((tm, o.shape[1]), row) for o in outs],
        out_shape=outs,
        scratch_shapes=[pltpu.VMEM((tm + SUBLANES, 3 * GDN_WIDTH), F32),
                        pltpu.VMEM((tm + CONV_HALO, CONV_WIDTH), F32),
                        pltpu.VMEM((SUBLANES - 1, tm + CONV_HALO - SUBLANES, CONV_WIDTH), F32)],
        compiler_params=_cparams("arbitrary"),
        name="in_proj",
    )(xf, w_main, w_small, gdn_conv_w, cnv_w, cnv_b, cnv_ln_g, cnv_ln_b, cnv_scale)


AUG_BIAS_LANE = HEAD_DIM


def _gates_kernel(small_ref, a_ref, dtb_ref, fb_ref, fq_ref, fk_ref, fv_ref,
                  g_ref, beta_ref, gt_ref, qa_ref, ka_ref, va_ref, carry_ref):
    t = pl.program_id(1)
    tt = small_ref.shape[1]
    b16 = jnp.bfloat16

    @pl.when(t == 0)
    def _():
        carry_ref[...] = jnp.zeros_like(carry_ref)

    s = small_ref[0]
    log_decay = -jnp.exp(a_ref[...]) * _softplus(s + dtb_ref[...])
    beta = _sigmoid(s)
    log_f = _log_sigmoid(s + fb_ref[...])

    r = lax.broadcasted_iota(jnp.int32, (tt, tt), 0)
    c = lax.broadcasted_iota(jnp.int32, (tt, tt), 1)
    tri = (c <= r)
    g_cum = _sel_lhs((tri & (r // GDN_CHUNK == c // GDN_CHUNK)).astype(b16), log_decay)
    c_cum = _sel_lhs(tri.astype(b16), log_f) + carry_ref[...]
    carry_ref[...] = c_cum[tt - 1:tt, :]

    er = lax.broadcasted_iota(jnp.int32, (LANES, GDN_WIDTH), 0)
    ec = lax.broadcasted_iota(jnp.int32, (LANES, GDN_WIDTH), 1) // HEAD_DIM
    g_ref[0] = _sel_rhs(g_cum, (er == ec).astype(b16))
    beta_ref[0] = _sel_rhs(beta, (er == ec + GDN_HEADS).astype(b16))
    gt_ref[0] = g_cum.T[0:SUBLANES, :]

    rr = lax.broadcasted_iota(jnp.int32, (LANES, LANES), 0)
    cc = lax.broadcasted_iota(jnp.int32, (LANES, LANES), 1)
    lane = lax.broadcasted_iota(jnp.int32, (1, LANES), 1)
    q_ones = ((lane >= AUG_BIAS_LANE) & (lane < AUG_BIAS_LANE + 3)).astype(F32)
    v_one = (lane == AUG_BIAS_LANE).astype(F32)
    neg_c = _split3(-c_cum)
    for h in range(FOX_HEADS):
        p, half = divmod(h, 2)
        lanes = slice(p * PAIR, (p + 1) * PAIR)
        sel = ((cc < HEAD_DIM) & (rr == cc + half * HEAD_DIM)).astype(b16)
        pick = lambda ref: jnp.dot(ref[0, :, lanes], sel, preferred_element_type=F32)
        bias = sum(jnp.dot(term, ((rr == 2 * GDN_HEADS + h) & (cc == AUG_BIAS_LANE + i)).astype(b16),
                           preferred_element_type=F32) for i, term in enumerate(neg_c))
        qa_ref[0, h] = (pick(fq_ref) + q_ones).astype(b16)
        ka_ref[0, h] = (pick(fk_ref) + bias).astype(b16)
        va_ref[0, h] = (pick(fv_ref) + v_one).astype(b16)


def _gates(small, a_row, dtb_row, fb_row, fq, fk, fv, tt):
    b, t, _ = small.shape
    row = pl.BlockSpec((1, LANES), lambda i, j: (0, 0))
    tile = lambda w: pl.BlockSpec((1, tt, w), lambda i, j: (i, j, 0))
    aug = pl.BlockSpec((1, FOX_HEADS, tt, LANES), lambda i, j: (i, 0, j, 0))
    aug_shape = jax.ShapeDtypeStruct((b, FOX_HEADS, t, LANES), jnp.bfloat16)
    return pl.pallas_call(
        _gates_kernel,
        grid=(b, t // tt),
        in_specs=[tile(SMALL_COLS), row, row, row, tile(FOX_WIDTH), tile(FOX_WIDTH), tile(FOX_WIDTH)],
        out_specs=[tile(GDN_WIDTH), tile(GDN_WIDTH), pl.BlockSpec((1, SUBLANES, tt), lambda i, j: (i, 0, j)),
                   aug, aug, aug],
        out_shape=[jax.ShapeDtypeStruct((b, t, GDN_WIDTH), F32), jax.ShapeDtypeStruct((b, t, GDN_WIDTH), F32),
                   jax.ShapeDtypeStruct((b, SUBLANES, t), F32), aug_shape, aug_shape, aug_shape],
        scratch_shapes=[pltpu.VMEM((1, LANES), F32)],
        compiler_params=_cparams("parallel", "arbitrary"),
        name="gates",
    )(small, a_row, dtb_row, fb_row, fq, fk, fv)


def _unit_lower_inverse_many(lows):
    n = lows[0].shape[0]
    r = lax.broadcasted_iota(jnp.int32, (n, n), 0)
    c = lax.broadcasted_iota(jnp.int32, (n, n), 1)
    eye = (r == c).astype(F32)
    diag16 = r // 16 == c // 16
    cast = lambda xs: [x.astype(MXU_DTYPE) for x in xs]
    mm = lambda xs, ys: [jnp.dot(x, y, preferred_element_type=F32) for x, y in zip(xs, ys)]
    add = lambda xs, ys: [x + y for x, y in zip(xs, ys)]
    d = [jnp.where(diag16, low, 0.0) for low in lows]
    db = cast(d)
    d2b = cast(mm(db, db))
    p = [eye - x for x in d]
    p = add(p, mm(cast(p), d2b))
    d4b = cast(mm(d2b, d2b))
    p = add(p, mm(cast(p), d4b))
    d8b = cast(mm(d4b, d4b))
    x = add(p, mm(cast(p), d8b))
    for blk in (32, 64):
        sel = (r // blk == c // blk) & (r // (blk // 2) != c // (blk // 2))
        xb = cast(x)
        xo = mm(xb, cast([jnp.where(sel, low, 0.0) for low in lows]))
        x = [a - b for a, b in zip(x, mm(cast(xo), xb))]
    return x


def _gdn_kernel(qkv_ref, g_ref, beta_ref, gt_ref, gate_ref, nw_ref, ms_ref, o_ref, s_ref):
    t = pl.program_id(1)
    blk = qkv_ref.shape[1]
    nc = blk // GDN_CHUNK

    @pl.when(t == 0)
    def _():
        s_ref[...] = jnp.zeros_like(s_ref)

    qkv = qkv_ref[0]

    ones = _head_ones().astype(jnp.bfloat16)
    lane = lax.broadcasted_iota(jnp.int32, (1, PAIR), 1)
    head0 = lane < HEAD_DIM
    n2 = 2 * GDN_CHUNK
    r = lax.broadcasted_iota(jnp.int32, (n2, n2), 0)
    c = lax.broadcasted_iota(jnp.int32, (n2, n2), 1)
    same = (r // GDN_CHUNK) == (c // GDN_CHUNK)
    causal = same & (c <= r)
    strict = same & (c < r)

    def l2n(v):
        return v * lax.rsqrt(_dot_ones(v * v, ones) + NORM_EPS)

    def stack(v):
        return jnp.concatenate([jnp.where(head0, v, 0.0), jnp.where(head0, 0.0, v)], axis=0)

    pair = lambda p, grp: slice(grp * GDN_WIDTH + p * PAIR, grp * GDN_WIDTH + (p + 1) * PAIR)
    q_p = [l2n(qkv[:, pair(p, 0)]) * HEAD_DIM ** -0.5 for p in range(N_PAIRS)]
    k_p = [l2n(qkv[:, pair(p, 1)]) for p in range(N_PAIRS)]
    v_p = [qkv[:, pair(p, 2)] for p in range(N_PAIRS)]

    units = [(ci, p) for ci in range(nc) for p in range(N_PAIRS)]
    rows = lambda ci: slice(ci * GDN_CHUNK, (ci + 1) * GDN_CHUNK)
    gc = [g_ref[0, rows(ci), pair(p, 0)] for ci, p in units]
    bt = [beta_ref[0, rows(ci), pair(p, 0)] for ci, p in units]
    qn = [q_p[p][rows(ci)] for ci, p in units]
    kn = [k_p[p][rows(ci)] for ci, p in units]
    vv = [v_p[p][rows(ci)] for ci, p in units]
    g_last = [g[GDN_CHUNK - 1:GDN_CHUNK, :] for g in gc]
    eg = [jnp.exp(g) for g in gc]
    kb = [k * b for k, b in zip(kn, bt)]
    k2 = [stack(k) for k in kn]

    def decay_of(g, ci, p):
        g_col = jnp.concatenate([jnp.broadcast_to(g[:, 0:1], (GDN_CHUNK, PAIR)),
                                 jnp.broadcast_to(g[:, HEAD_DIM:HEAD_DIM + 1], (GDN_CHUNK, PAIR))], axis=0)
        g_row = jnp.concatenate([gt_ref[0, 2 * p:2 * p + 1, rows(ci)], gt_ref[0, 2 * p + 1:2 * p + 2, rows(ci)]],
                                axis=1)
        return jnp.where(causal, jnp.exp(jnp.where(causal, g_col - g_row, 0.0)), 0.0)

    decay = [decay_of(g, ci, p) for g, (ci, p) in zip(gc, units)]
    k2b = [x.astype(MXU_DTYPE) for x in k2]
    low = [jnp.where(strict, _dot_nt(stack(a), b) * d, 0.0) for a, b, d in zip(kb, k2b, decay)]
    a_in = [(_dot_nt(stack(a), b) * d).astype(MXU_DTYPE) for a, b, d in zip(qn, k2b, decay)]
    t_inv = _unit_lower_inverse_many(low)
    uw = [_dot(ti, jnp.concatenate([stack(v * b), stack(a * e)], axis=1)).astype(MXU_DTYPE)
          for ti, v, b, a, e in zip(t_inv, vv, bt, kb, eg)]
    ket = [stack(k * jnp.exp(gl - g)).T for k, gl, g in zip(kn, g_last, gc)]
    nm = [_dot(a, b) for a, b in zip(ket, uw)]
    raw = [jnp.dot(a, b, preferred_element_type=F32) for a, b in zip(a_in, uw)]
    p_mat = [stack(q * e) - x[:, PAIR:] for q, e, x in zip(qn, eg, raw)]

    state = [s_ref[p] for p in range(N_PAIRS)]
    for i, (ci, p) in enumerate(units):
        s = state[p]
        o2 = _dot(p_mat[i], s) + raw[i][:, :PAIR]
        state[p] = s * jnp.exp(g_last[i]) - _dot(nm[i][:, PAIR:], s) + nm[i][:, :PAIR]
        o = o2[0:GDN_CHUNK] + o2[GDN_CHUNK:n2]
        ms = _dot_ones(o * o, ones) * (1.0 / HEAD_DIM)
        on = o * lax.rsqrt(ms + NORM_EPS) * nw_ref[...]
        o_ref[0, rows(ci), pair(p, 0)] = (on * _silu(gate_ref[0, rows(ci), pair(p, 0)])
                                          * ms_ref[:, pair(p, 0)]).astype(o_ref.dtype)
    for p in range(N_PAIRS):
        s_ref[p] = state[p]


def _gdn(qkv, g, beta, g_t, gate, norm_w2, mscale, blk):
    b, t, _ = qkv.shape
    tile = lambda width: pl.BlockSpec((1, blk, width), lambda i, j: (i, j, 0))
    const = lambda shape: pl.BlockSpec(shape, lambda i, j: (0, 0))
    return pl.pallas_call(
        _gdn_kernel,
        grid=(b, t // blk),
        in_specs=[tile(3 * GDN_WIDTH),
                  tile(GDN_WIDTH), tile(GDN_WIDTH), pl.BlockSpec((1, SUBLANES, blk), lambda i, j: (i, 0, j)),
                  tile(GDN_WIDTH), const((1, PAIR)), const((1, GDN_WIDTH))],
        out_specs=tile(GDN_WIDTH),
        out_shape=jax.ShapeDtypeStruct((b, t, GDN_WIDTH), MXU_DTYPE),
        scratch_shapes=[pltpu.VMEM((N_PAIRS, PAIR, PAIR), F32)],
        compiler_params=_cparams("parallel", "arbitrary"),
        name="gdn",
    )(qkv, g, beta, g_t, gate, norm_w2, mscale)


FOX_ROW_GROUP = 32


def _fox_kernel(q_ref, k_ref, v_ref, ms_ref, o_ref, s0_ref, s1_ref, p0_ref, p1_ref, a0_ref, a1_ref, m_ref, acc_ref,
                *, tq):
    i = pl.program_id(2)
    heads = range(2)
    rg = FOX_ROW_GROUP
    m_ref[...] = jnp.full(m_ref.shape, -jnp.inf, F32)
    acc_ref[...] = jnp.zeros(acc_ref.shape, F32)

    def scores(j, s_ref):
        start = pl.multiple_of(j * tq, tq)
        for h in heads:
            s_ref[h] = lax.dot_general(q_ref[0, h], k_ref[0, h, pl.ds(start, tq), :],
                                       (((1,), (1,)), ((), ())), preferred_element_type=F32)

    def softmax(s_ref, p_ref, a_ref, masked):
        for h in heads:
            for g in range(tq // rg):
                rows = slice(g * rg, (g + 1) * rg)
                s = s_ref[h, rows, :]
                if masked:
                    row_id = g * rg + lax.broadcasted_iota(jnp.int32, (rg, tq), 0)
                    col_id = lax.broadcasted_iota(jnp.int32, (rg, tq), 1)
                    s = jnp.where(col_id <= row_id, s, -jnp.inf)
                m_old = m_ref[h, rows, :]
                m_new = jnp.maximum(m_old, jnp.max(s, axis=-1, keepdims=True))
                a_ref[h, rows, :] = jnp.exp(m_old - m_new)
                m_ref[h, rows, :] = m_new
                m_wide = jnp.concatenate([m_new] * (tq // LANES), axis=1)
                p_ref[h, rows, :] = jnp.exp(s - m_wide).astype(p_ref.dtype)

    def weighted_values(j, p_ref, a_ref):
        start = pl.multiple_of(j * tq, tq)
        for h in heads:
            acc_ref[h] = a_ref[h] * acc_ref[h] + jnp.dot(p_ref[h], v_ref[0, h, pl.ds(start, tq), :],
                                                         preferred_element_type=F32)

    scores(0, s0_ref)

    def body(t, carry):
        softmax(s0_ref, p0_ref, a0_ref, False)
        scores(2 * t + 1, s1_ref)
        weighted_values(2 * t, p0_ref, a0_ref)
        softmax(s1_ref, p1_ref, a1_ref, False)
        scores(2 * t + 2, s0_ref)
        weighted_values(2 * t + 1, p1_ref, a1_ref)
        return carry

    lax.fori_loop(0, i // 2, body, 0)

    @pl.when(i % 2 == 0)
    def _():
        softmax(s0_ref, p0_ref, a0_ref, True)
        weighted_values(i, p0_ref, a0_ref)

    @pl.when(i % 2 == 1)
    def _():
        softmax(s0_ref, p0_ref, a0_ref, False)
        scores(i, s1_ref)
        weighted_values(i - 1, p0_ref, a0_ref)
        softmax(s1_ref, p1_ref, a1_ref, True)
        weighted_values(i, p1_ref, a1_ref)

    o = [acc_ref[h] / acc_ref[h][:, AUG_BIAS_LANE:AUG_BIAS_LANE + 1] for h in heads]
    lane = lax.broadcasted_iota(jnp.int32, (1, PAIR), 1)
    o_pair = jnp.where(lane < HEAD_DIM, o[0], pltpu.roll(o[1], HEAD_DIM, axis=1))
    o_ref[0] = (o_pair * ms_ref[...]).astype(o_ref.dtype)


def _fox(qa, ka, va, mscale, tq):
    b, _, t, _ = qa.shape
    kern = functools.partial(_fox_kernel, tq=tq)
    return pl.pallas_call(
        kern,
        grid=(b, N_PAIRS, t // tq),
        in_specs=[pl.BlockSpec((1, 2, tq, LANES), lambda bi, p, i: (bi, p, i, 0)),
                  pl.BlockSpec((1, 2, t, LANES), lambda bi, p, i: (bi, p, 0, 0)),
                  pl.BlockSpec((1, 2, t, LANES), lambda bi, p, i: (bi, p, 0, 0)),
                  pl.BlockSpec((1, PAIR), lambda bi, p, i: (0, p))],
        out_specs=pl.BlockSpec((1, tq, PAIR), lambda bi, p, i: (bi, i, p)),
        out_shape=jax.ShapeDtypeStruct((b, t, FOX_WIDTH), MXU_DTYPE),
        scratch_shapes=[pltpu.VMEM((2, tq, tq), F32), pltpu.VMEM((2, tq, tq), F32),
                        pltpu.VMEM((2, tq, tq), jnp.bfloat16), pltpu.VMEM((2, tq, tq), jnp.bfloat16),
                        pltpu.VMEM((2, tq, LANES), F32), pltpu.VMEM((2, tq, LANES), F32),
                        pltpu.VMEM((2, tq, LANES), F32), pltpu.VMEM((2, tq, LANES), F32)],
        compiler_params=_cparams("parallel", "parallel", "arbitrary"),
        name="fox_attention",
    )(qa, ka, va, mscale)


def _out_proj_kernel(oa_ref, ob_ref, oc_ref, x_ref, w_ref, g_ref, b_ref, o_ref):
    mix = jnp.dot(oa_ref[...], w_ref[0:GDN_WIDTH, :], preferred_element_type=F32)
    mix = mix + jnp.dot(ob_ref[...], w_ref[GDN_WIDTH:GDN_WIDTH + CONV_WIDTH, :], preferred_element_type=F32)
    mix = mix + jnp.dot(oc_ref[...], w_ref[GDN_WIDTH + CONV_WIDTH:D_MODEL, :], preferred_element_type=F32)
    o_ref[...] = _layer_norm_rows(DEEPNORM_ALPHA * x_ref[...] + mix, g_ref[...], b_ref[...])


def _out_proj(oa, ob, oc, xf, w, g, bvec, tm):
    n = xf.shape[0]
    row = lambda i: (i, 0)
    const = lambda i: (0, 0)
    return pl.pallas_call(
        _out_proj_kernel,
        grid=(n // tm,),
        in_specs=[pl.BlockSpec((tm, GDN_WIDTH), row), pl.BlockSpec((tm, CONV_WIDTH), row),
                  pl.BlockSpec((tm, FOX_WIDTH), row), pl.BlockSpec((tm, D_MODEL), row),
                  pl.BlockSpec((D_MODEL, D_MODEL), const),
                  pl.BlockSpec((1, D_MODEL), const), pl.BlockSpec((1, D_MODEL), const)],
        out_specs=pl.BlockSpec((tm, D_MODEL), row),
        out_shape=jax.ShapeDtypeStruct((n, D_MODEL), F32),
        compiler_params=_cparams("parallel"),
        name="out_proj_ln",
    )(oa, ob, oc, xf, w, g, bvec)


FFN_CHUNK = 256


def _swiglu_chunks(xb, w1, w3, w2, width):
    acc = None
    for c in range(width // FFN_CHUNK):
        cols = slice(c * FFN_CHUNK, (c + 1) * FFN_CHUNK)
        h = _silu(jnp.dot(xb, w1(slice(None), cols), preferred_element_type=F32)) * jnp.dot(
            xb, w3(slice(None), cols), preferred_element_type=F32)
        part = jnp.dot(h.astype(xb.dtype), w2(cols, slice(None)), preferred_element_type=F32)
        acc = part if acc is None else acc + part
    return acc


def _ffn_kernel(x_ref, w1_ref, w3_ref, w2_ref, g_ref, b_ref, o_ref):
    x = x_ref[...]
    ff = _swiglu_chunks(x.astype(w1_ref.dtype), lambda r, c: w1_ref[r, c], lambda r, c: w3_ref[r, c],
                        lambda r, c: w2_ref[r, c], w1_ref.shape[1])
    o_ref[...] = _layer_norm_rows(DEEPNORM_ALPHA * x + ff, g_ref[...], b_ref[...])


def _dense_ffn(xf, w1, w3, w2, g, bvec, tm):
    n = xf.shape[0]
    ff = w1.shape[1]
    const = lambda shape: pl.BlockSpec(shape, lambda i: (0, 0), pipeline_mode=pl.Buffered(1))
    return pl.pallas_call(
        _ffn_kernel,
        grid=(n // tm,),
        in_specs=[pl.BlockSpec((tm, D_MODEL), lambda i: (i, 0)),
                  const((D_MODEL, ff)), const((D_MODEL, ff)), const((ff, D_MODEL)),
                  const((1, D_MODEL)), const((1, D_MODEL))],
        out_specs=pl.BlockSpec((tm, D_MODEL), lambda i: (i, 0)),
        out_shape=jax.ShapeDtypeStruct((n, D_MODEL), F32),
        compiler_params=_cparams("parallel"),
        name="dense_ffn_ln",
    )(xf, w1, w3, w2, g, bvec)


def _router_kernel(x_ref, wr_ref, info_ref, cnt_ref, run_ref):
    i = pl.program_id(0)
    tm = x_ref.shape[0]

    @pl.when(i == 0)
    def _():
        run_ref[...] = jnp.zeros_like(run_ref)

    logits = jnp.dot(x_ref[...].astype(MXU_DTYPE), wr_ref[...], preferred_element_type=F32)
    lane = lax.broadcasted_iota(jnp.int32, (tm, LANES), 1)
    logits = jnp.where(lane < N_EXPERTS, logits, -jnp.inf)
    m1 = jnp.max(logits, axis=-1, keepdims=True)
    e1 = jnp.min(jnp.where(logits == m1, lane, LANES), axis=-1, keepdims=True)
    rest = jnp.where(lane == e1, -jnp.inf, logits)
    m2 = jnp.max(rest, axis=-1, keepdims=True)
    e2 = jnp.min(jnp.where(rest == m2, lane, LANES), axis=-1, keepdims=True)
    z = jnp.exp(m2 - m1)
    g1 = 1.0 / (1.0 + z)
    g2 = z / (1.0 + z)
    onehot = ((lane == e1) | (lane == e2)).astype(F32)
    r = lax.broadcasted_iota(jnp.int32, (tm, tm), 0)
    c = lax.broadcasted_iota(jnp.int32, (tm, tm), 1)
    before = jnp.dot((c < r).astype(jnp.bfloat16), onehot.astype(jnp.bfloat16),
                     preferred_element_type=F32) + run_ref[...]
    rank1 = jnp.sum(jnp.where(lane == e1, before, 0.0), axis=-1, keepdims=True)
    rank2 = jnp.sum(jnp.where(lane == e2, before, 0.0), axis=-1, keepdims=True)
    run_ref[...] = run_ref[...] + jnp.sum(onehot, axis=0, keepdims=True)
    cnt_ref[...] = run_ref[...]
    info = jnp.where(lane == 0, e1.astype(F32),
                     jnp.where(lane == 1, e2.astype(F32),
                               jnp.where(lane == 2, rank1,
                                         jnp.where(lane == 3, rank2,
                                                   jnp.where(lane == 4, g1, jnp.where(lane == 5, g2, 0.0))))))
    info_ref[...] = info


def _router(xf, wr_pad, tm):
    n = xf.shape[0]
    return pl.pallas_call(
        _router_kernel,
        grid=(n // tm,),
        in_specs=[pl.BlockSpec((tm, D_MODEL), lambda i: (i, 0)),
                  pl.BlockSpec((D_MODEL, LANES), lambda i: (0, 0))],
        out_specs=[pl.BlockSpec((tm, LANES), lambda i: (i, 0)),
                   pl.BlockSpec((1, LANES), lambda i: (0, 0))],
        out_shape=[jax.ShapeDtypeStruct((n, LANES), F32), jax.ShapeDtypeStruct((1, LANES), F32)],
        scratch_shapes=[pltpu.VMEM((1, LANES), F32)],
        compiler_params=_cparams("arbitrary"),
        name="moe_router",
    )(xf, wr_pad)


DMA_LOOP_UNROLL = 8


def _dispatch_kernel(d1_ref, d2_ref, x_ref, zeros_ref, xs_ref, sem):
    del zeros_ref
    tm = x_ref.shape[0]

    def row_copy(r, dst):
        return pltpu.make_async_copy(x_ref.at[pl.ds(r, 1), :], xs_ref.at[pl.ds(dst, 1), :], sem)

    def issue(r, carry):
        row_copy(r, d1_ref[0, 0, r]).start()
        row_copy(r, d2_ref[0, 0, r]).start()
        return carry

    lax.fori_loop(0, tm, issue, 0, unroll=DMA_LOOP_UNROLL)

    def drain(r, carry):
        row_copy(r, d1_ref[0, 0, r]).wait()
        row_copy(r, d2_ref[0, 0, r]).wait()
        return carry

    lax.fori_loop(0, tm, drain, 0, unroll=DMA_LOOP_UNROLL)


def _dispatch(xf, d1, d2, rows, tm):
    n = xf.shape[0]
    idx = lambda a: a.reshape(n // tm, 1, tm)
    zeros = jnp.zeros((rows, D_MODEL), xf.dtype)
    smem = lambda: pl.BlockSpec((1, 1, tm), lambda i: (i, 0, 0), memory_space=pltpu.SMEM)
    return pl.pallas_call(
        _dispatch_kernel,
        grid=(n // tm,),
        in_specs=[smem(), smem(),
                  pl.BlockSpec((tm, D_MODEL), lambda i: (i, 0)),
                  pl.BlockSpec(memory_space=pl.ANY)],
        out_specs=pl.BlockSpec(memory_space=pl.ANY),
        out_shape=jax.ShapeDtypeStruct((rows, D_MODEL), xf.dtype),
        scratch_shapes=[pltpu.SemaphoreType.DMA(())],
        input_output_aliases={3: 0},
        compiler_params=_cparams("arbitrary"),
        name="moe_dispatch",
    )(idx(d1), idx(d2), xf, zeros)


def _expert_kernel(te_ref, nu_ref, xs_ref, w1_ref, w3_ref, w2_ref, ys_ref, xb_ref, acc_ref):
    del te_ref
    i = pl.program_id(0)
    f = pl.program_id(1)
    used = i < nu_ref[0]

    @pl.when(f == 0)
    def _():
        xb_ref[...] = xs_ref[...].astype(xb_ref.dtype)
        acc_ref[...] = jnp.zeros_like(acc_ref)

    @pl.when(used)
    def _():
        acc_ref[...] += _swiglu_chunks(xb_ref[...], lambda r, c: w1_ref[0, r, c], lambda r, c: w3_ref[0, r, c],
                                       lambda r, c: w2_ref[0, r, c], w1_ref.shape[2])

    @pl.when(f == pl.num_programs(1) - 1)
    def _():
        ys_ref[...] = acc_ref[...]


def _experts(xs, tile_expert, n_used, w1, w3, w2, tm, tf):
    rows = xs.shape[0]
    ff = w1.shape[2]
    n_tiles = rows // tm

    def x_map(i, f, te, nu):
        return (jnp.maximum(jnp.minimum(i, nu[0] - 1), 0), 0)

    def w13_map(i, f, te, nu):
        return (te[i], 0, jnp.where(i < nu[0], f, ff // tf - 1))

    def w2_map(i, f, te, nu):
        return (te[i], jnp.where(i < nu[0], f, ff // tf - 1), 0)

    grid_spec = pltpu.PrefetchScalarGridSpec(
        num_scalar_prefetch=2,
        grid=(n_tiles, ff // tf),
        in_specs=[pl.BlockSpec((tm, D_MODEL), x_map),
                  pl.BlockSpec((1, D_MODEL, tf), w13_map),
                  pl.BlockSpec((1, D_MODEL, tf), w13_map),
                  pl.BlockSpec((1, tf, D_MODEL), w2_map)],
        out_specs=pl.BlockSpec((tm, D_MODEL), lambda i, f, te, nu: (i, 0)),
        scratch_shapes=[pltpu.VMEM((tm, D_MODEL), MXU_DTYPE), pltpu.VMEM((tm, D_MODEL), F32)],
    )
    return pl.pallas_call(
        _expert_kernel,
        grid_spec=grid_spec,
        out_shape=jax.ShapeDtypeStruct((rows, D_MODEL), F32),
        compiler_params=_cparams("arbitrary", "arbitrary"),
        name="moe_experts",
    )(tile_expert, n_used, xs, w1, w3, w2)


def _combine_kernel(d1_ref, d2_ref, x_ref, info_ref, ys_ref, g_ref, b_ref, o_ref, y1_ref, y2_ref, sem):
    tm = x_ref.shape[0]

    def row_copy(src, r, buf):
        return pltpu.make_async_copy(ys_ref.at[pl.ds(src, 1), :], buf.at[pl.ds(r, 1), :], sem)

    def issue(r, carry):
        row_copy(d1_ref[0, 0, r], r, y1_ref).start()
        row_copy(d2_ref[0, 0, r], r, y2_ref).start()
        return carry

    lax.fori_loop(0, tm, issue, 0, unroll=DMA_LOOP_UNROLL)

    def drain(r, carry):
        row_copy(d1_ref[0, 0, r], r, y1_ref).wait()
        row_copy(d2_ref[0, 0, r], r, y2_ref).wait()
        return carry

    lax.fori_loop(0, tm, drain, 0, unroll=DMA_LOOP_UNROLL)
    info = info_ref[...]
    ff = info[:, 4:5] * y1_ref[...] + info[:, 5:6] * y2_ref[...]
    o_ref[...] = _layer_norm_rows(DEEPNORM_ALPHA * x_ref[...] + ff, g_ref[...], b_ref[...])


def _combine(xf, info, ys, d1, d2, g, bvec, tm):
    n = xf.shape[0]
    idx = lambda a: a.reshape(n // tm, 1, tm)
    smem = lambda: pl.BlockSpec((1, 1, tm), lambda i: (i, 0, 0), memory_space=pltpu.SMEM)
    return pl.pallas_call(
        _combine_kernel,
        grid=(n // tm,),
        in_specs=[smem(), smem(),
                  pl.BlockSpec((tm, D_MODEL), lambda i: (i, 0)),
                  pl.BlockSpec((tm, LANES), lambda i: (i, 0)),
                  pl.BlockSpec(memory_space=pl.ANY),
                  pl.BlockSpec((1, D_MODEL), lambda i: (0, 0)),
                  pl.BlockSpec((1, D_MODEL), lambda i: (0, 0))],
        out_specs=pl.BlockSpec((tm, D_MODEL), lambda i: (i, 0)),
        out_shape=jax.ShapeDtypeStruct((n, D_MODEL), F32),
        scratch_shapes=[pltpu.VMEM((tm, D_MODEL), F32), pltpu.VMEM((tm, D_MODEL), F32),
                        pltpu.SemaphoreType.DMA(())],
        compiler_params=_cparams("arbitrary"),
        name="moe_combine_ln",
    )(idx(d1), idx(d2), xf, info, ys, g, bvec)


MOE_TILE = 512


def _moe_ffn(xf, w_router, w1, w3, w2, g, bvec, tm_tok, tf):
    n = xf.shape[0]
    wr_pad = jnp.zeros((D_MODEL, LANES), MXU_DTYPE).at[:, :N_EXPERTS].set(w_router.astype(MXU_DTYPE))
    info, counts = _router(xf, wr_pad, tm_tok)
    sizes = counts[0, :N_EXPERTS].astype(jnp.int32)
    tiles_per = (sizes + MOE_TILE - 1) // MOE_TILE
    tile_end = jnp.cumsum(tiles_per)
    seg_start = (tile_end - tiles_per) * MOE_TILE
    e1 = info[:, 0].astype(jnp.int32)
    e2 = info[:, 1].astype(jnp.int32)
    d1 = seg_start[e1] + info[:, 2].astype(jnp.int32)
    d2 = seg_start[e2] + info[:, 3].astype(jnp.int32)
    n_tiles = (n * TOP_K) // MOE_TILE + N_EXPERTS
    rows = n_tiles * MOE_TILE
    tile_expert = jnp.minimum(jnp.searchsorted(tile_end, jnp.arange(n_tiles, dtype=jnp.int32), side='right'),
                              N_EXPERTS - 1).astype(jnp.int32)
    n_used = tile_end[N_EXPERTS - 1:].astype(jnp.int32)
    xs = _dispatch(xf, d1, d2, rows, tm_tok)
    ys = _experts(xs, tile_expert, n_used, w1, w3, w2, MOE_TILE, tf)
    return _combine(xf, info, ys, d1, d2, g, bvec, tm_tok)


def _pack_in_proj(w_in):
    cuts = np.cumsum([0, GDN_WIDTH, GDN_WIDTH, GDN_WIDTH, GDN_HEADS, GDN_HEADS, GDN_WIDTH,
                      2 * CONV_WIDTH, FOX_WIDTH, FOX_WIDTH, FOX_WIDTH, FOX_HEADS])
    seg = lambda i: w_in[:, cuts[i]:cuts[i + 1]]
    w_main = jnp.concatenate([seg(0), seg(1), seg(2), seg(5), seg(6), seg(7), seg(8), seg(9)], axis=1)
    pad = jnp.zeros((D_MODEL, SMALL_COLS - 2 * GDN_HEADS - FOX_HEADS), w_in.dtype)
    w_small = jnp.concatenate([seg(3), seg(4), seg(10), pad], axis=1).astype(F32)
    ws_hi = w_small.astype(MXU_DTYPE)
    ws_lo = (w_small - ws_hi.astype(F32)).astype(MXU_DTYPE)
    return w_main.astype(MXU_DTYPE), jnp.concatenate([ws_hi, ws_lo], axis=1)


def _lane_row(vals, offset):
    return jnp.zeros((1, LANES), F32).at[0, offset:offset + vals.shape[0]].set(vals.astype(F32))


def _mixer(xf, b, t, w_in, mix_scale, w_out, gdn_conv_w, gdn_a_log, gdn_dt_bias, gdn_norm_w,
           cnv_dw_w, cnv_dw_b, cnv_ln_g, cnv_ln_b, fox_f_bias, ln_g, ln_b, tiles):
    w_main, w_small = _pack_in_proj(w_in)
    ms = mix_scale.reshape(1, D_MODEL).astype(F32)
    row = lambda v: v.reshape(1, -1).astype(F32)
    qkv, gate, o_b, fq, fk, fv, small = _in_proj(
        xf, w_main, w_small, gdn_conv_w, cnv_dw_w, row(cnv_dw_b), row(cnv_ln_g), row(cnv_ln_b),
        ms[:, GDN_WIDTH:GDN_WIDTH + CONV_WIDTH], tiles["tm"], t // tiles["tm"])
    r3 = lambda a: a.reshape(b, t, a.shape[-1])
    g_cum, beta, g_t, qa, ka, va = _gates(r3(small), _lane_row(gdn_a_log, 0), _lane_row(gdn_dt_bias, 0),
                                          _lane_row(fox_f_bias, 2 * GDN_HEADS), r3(fq), r3(fk), r3(fv), tiles["tt"])
    norm_w2 = jnp.tile(gdn_norm_w.astype(F32), 2).reshape(1, PAIR)
    o_a = _gdn(r3(qkv), g_cum, beta, g_t, r3(gate), norm_w2, ms[:, :GDN_WIDTH], tiles["gdn_blk"])
    o_c = _fox(qa, ka, va, ms[:, GDN_WIDTH + CONV_WIDTH:], tiles["tq"])
    flat = lambda a: a.reshape(b * t, a.shape[-1])
    return _out_proj(flat(o_a), o_b, flat(o_c), xf, w_out.astype(MXU_DTYPE), row(ln_g), row(ln_b),
                     tiles["tm"])


def _tiles(t):
    return dict(tm=min(512, t), tt=min(512, t), gdn_blk=min(256, t), tq=min(512, t),
                tf_moe=1792)


def kernel(x, w_in, mix_scale, w_out, gdn_conv_w, gdn_a_log, gdn_dt_bias, gdn_norm_w, cnv_dw_w, cnv_dw_b,
           cnv_ln_g, cnv_ln_b, fox_f_bias, ln_mix_g, ln_mix_b, ln_ffn_g, ln_ffn_b, ffn_w1, ffn_w3, ffn_w2,
           moe_router, moe_w1, moe_w3, moe_w2):
    b, t, d = x.shape
    tiles = _tiles(t)
    row = lambda v: v.reshape(1, -1).astype(F32)
    xf = x.reshape(b * t, d)
    for l in range(DEPTH):
        xf = _mixer(xf, b, t, w_in[l], mix_scale[l], w_out[l], gdn_conv_w[l], gdn_a_log[l], gdn_dt_bias[l],
                    gdn_norm_w[l], cnv_dw_w[l], cnv_dw_b[l], cnv_ln_g[l], cnv_ln_b[l], fox_f_bias[l],
                    ln_mix_g[l], ln_mix_b[l], tiles)
        if l % 2 == 0:
            e = l // 2
            xf = _dense_ffn(xf, ffn_w1[e].astype(MXU_DTYPE), ffn_w3[e].astype(MXU_DTYPE),
                            ffn_w2[e].astype(MXU_DTYPE), row(ln_ffn_g[l]), row(ln_ffn_b[l]),
                            tiles["tm"])
        else:
            e = l // 2
            xf = _moe_ffn(xf, moe_router[e], moe_w1[e].astype(MXU_DTYPE), moe_w3[e].astype(MXU_DTYPE),
                          moe_w2[e].astype(MXU_DTYPE), row(ln_ffn_g[l]), row(ln_ffn_b[l]),
                          tiles["tm"], tiles["tf_moe"])
    return xf.reshape(b, t, d)
```

```python
import functools

import jax
import jax.numpy as jnp
import numpy as np
from jax import lax
from jax.experimental import pallas as pl
from jax.experimental.pallas import tpu as pltpu

D_MODEL = 1024
DEPTH = 2
HEAD_DIM = 64
GDN_WIDTH = 384
CONV_WIDTH = 256
FOX_WIDTH = 384
GDN_HEADS = 6
FOX_HEADS = 6
GDN_SHORT_CONV = 4
GDN_CHUNK = 64
CONV_KERNEL = 31
FFN_DENSE = 2816
N_EXPERTS = 8
TOP_K = 2
FFN_EXPERT = 3584
DEEPNORM_ALPHA = (2 * DEPTH) ** 0.25
LN_EPS = 1e-5
NORM_EPS = 1e-6

LANES = 128
SUBLANES = 8
PAIR = 2 * HEAD_DIM
N_PAIRS = GDN_WIDTH // PAIR
SMALL_COLS = LANES

_C_QKV = 0
_C_GATE = _C_QKV + 3 * GDN_WIDTH
_C_GLU = _C_GATE + GDN_WIDTH
_C_FQ = _C_GLU + 2 * CONV_WIDTH
_C_FK = _C_FQ + FOX_WIDTH
_C_FV = _C_FK + FOX_WIDTH
_C_END = _C_FV + FOX_WIDTH

MXU_DTYPE = jnp.bfloat16
F32 = jnp.float32

VMEM_LIMIT = 56 * 1024 * 1024


def _cparams(*sem):
    return pltpu.CompilerParams(dimension_semantics=sem, vmem_limit_bytes=VMEM_LIMIT)


def _dot(a, b):
    return jnp.dot(a.astype(MXU_DTYPE), b.astype(MXU_DTYPE), preferred_element_type=F32)


def _dot_nt(a, b):
    return lax.dot_general(a.astype(MXU_DTYPE), b.astype(MXU_DTYPE),
                           (((1,), (1,)), ((), ())), preferred_element_type=F32)


def _split(a):
    hi = a.astype(jnp.bfloat16)
    return hi, (a - hi.astype(F32)).astype(jnp.bfloat16)


def _split3(a):
    hi = a.astype(jnp.bfloat16)
    rest = a - hi.astype(F32)
    mid = rest.astype(jnp.bfloat16)
    lo = (rest - mid.astype(F32)).astype(jnp.bfloat16)
    return hi, mid, lo


def _dot_ones(a, ones_b16):
    ah, al = _split(a)
    d = lambda x: jnp.dot(x, ones_b16, preferred_element_type=F32)
    return d(ah) + d(al)


def _sel_rhs(a, sel_b16):
    return sum(jnp.dot(t, sel_b16, preferred_element_type=F32) for t in _split3(a))


def _sel_lhs(sel_b16, b):
    return sum(jnp.dot(sel_b16, t, preferred_element_type=F32) for t in _split3(b))


def _sigmoid(x):
    return 1.0 / (1.0 + jnp.exp(-x))


def _silu(x):
    return x * _sigmoid(x)


def _softplus(x):
    return jnp.maximum(x, 0.0) + jnp.log1p(jnp.exp(-jnp.abs(x)))


def _log_sigmoid(x):
    return -_softplus(-x)


def _layer_norm_rows(y, g, b):
    mu = jnp.mean(y, axis=-1, keepdims=True)
    d = y - mu
    var = jnp.mean(d * d, axis=-1, keepdims=True)
    return d * lax.rsqrt(var + LN_EPS) * g + b


def _head_ones():
    r = lax.broadcasted_iota(jnp.int32, (PAIR, PAIR), 0) // HEAD_DIM
    c = lax.broadcasted_iota(jnp.int32, (PAIR, PAIR), 1) // HEAD_DIM
    return (r == c).astype(F32)


CONV_HALO = 32
CONV_ROW_BLOCK = 128
SHORT_CONV_ROW_BLOCK = 128


def _in_proj_kernel(x_ref, w_ref, ws_ref, gw_ref, cw_ref, cb_ref, lg_ref, lb_ref, ms_ref,
                    qkv_ref, gate_ref, ob_ref, fq_ref, fk_ref, fv_ref, small_ref,
                    qbuf_ref, cbuf_ref, sh_ref, *, tiles_per_seq):
    first = pl.program_id(0) % tiles_per_seq == 0
    tm = x_ref.shape[0]
    x = x_ref[...]
    xb = x.astype(MXU_DTYPE)

    def mm(lo, hi):
        return jnp.dot(xb, w_ref[:, lo:hi], preferred_element_type=F32)

    @pl.when(first)
    def _():
        cbuf_ref[0:CONV_HALO, :] = jnp.zeros((CONV_HALO, CONV_WIDTH), F32)
        qbuf_ref[0:SUBLANES, :] = jnp.zeros((SUBLANES, 3 * GDN_WIDTH), F32)

    @pl.when(jnp.logical_not(first))
    def _():
        cbuf_ref[0:CONV_HALO, :] = cbuf_ref[tm:tm + CONV_HALO, :]
        qbuf_ref[0:SUBLANES, :] = qbuf_ref[tm:tm + SUBLANES, :]

    glu = mm(_C_GLU, _C_FQ)
    cbuf_ref[CONV_HALO:CONV_HALO + tm, :] = glu[:, 0:CONV_WIDTH] * _sigmoid(glu[:, CONV_WIDTH:2 * CONV_WIDTH])
    span = tm + CONV_HALO - SUBLANES
    for s in range(1, SUBLANES):
        sh_ref[s - 1] = cbuf_ref[s:s + span, :]
    for r0 in range(0, tm, CONV_ROW_BLOCK):
        acc = jnp.zeros((CONV_ROW_BLOCK, CONV_WIDTH), F32) + cb_ref[...]
        for j in range(CONV_KERNEL):
            lo = r0 + CONV_HALO - (CONV_KERNEL - 1) + j
            base, phase = lo - lo % SUBLANES, lo % SUBLANES
            tap = (cbuf_ref[base:base + CONV_ROW_BLOCK, :] if phase == 0
                   else sh_ref[phase - 1, base:base + CONV_ROW_BLOCK, :])
            acc = acc + cw_ref[j:j + 1, :] * tap
        y = _silu(_layer_norm_rows(acc, lg_ref[...], lb_ref[...])) * ms_ref[...]
        ob_ref[r0:r0 + CONV_ROW_BLOCK, :] = y.astype(ob_ref.dtype)

    qbuf_ref[SUBLANES:SUBLANES + tm, :] = mm(_C_QKV, _C_GATE)
    for r0 in range(0, tm, SHORT_CONV_ROW_BLOCK):
        acc = jnp.zeros((SHORT_CONV_ROW_BLOCK, 3 * GDN_WIDTH), F32)
        for j in range(GDN_SHORT_CONV):
            lo = r0 + SUBLANES - (GDN_SHORT_CONV - 1) + j
            acc = acc + gw_ref[j:j + 1, :] * qbuf_ref[lo:lo + SHORT_CONV_ROW_BLOCK, :]
        qkv_ref[r0:r0 + SHORT_CONV_ROW_BLOCK, :] = _silu(acc)

    gate_ref[...] = mm(_C_GATE, _C_GLU)
    fq_ref[...] = (mm(_C_FQ, _C_FK) * HEAD_DIM ** -0.5).astype(fq_ref.dtype)
    fk_ref[...] = mm(_C_FK, _C_FV).astype(fk_ref.dtype)
    fv_ref[...] = mm(_C_FV, _C_END).astype(fv_ref.dtype)
    x_lo = (x - xb.astype(F32)).astype(MXU_DTYPE)
    hh_hl = jnp.dot(xb, ws_ref[...], preferred_element_type=F32)
    lh = jnp.dot(x_lo, ws_ref[:, 0:SMALL_COLS], preferred_element_type=F32)
    small_ref[...] = hh_hl[:, 0:SMALL_COLS] + hh_hl[:, SMALL_COLS:2 * SMALL_COLS] + lh


def _in_proj(xf, w_main, w_small, gdn_conv_w, cnv_w, cnv_b, cnv_ln_g, cnv_ln_b, cnv_scale, tm, tiles_per_seq):
    n = xf.shape[0]
    row = lambda i: (i, 0)
    const = lambda shape: pl.BlockSpec(shape, lambda i: (0, 0))
    outs = [
        jax.ShapeDtypeStruct((n, 3 * GDN_WIDTH), F32),
        jax.ShapeDtypeStruct((n, GDN_WIDTH), F32),
        jax.ShapeDtypeStruct((n, CONV_WIDTH), MXU_DTYPE),
        jax.ShapeDtypeStruct((n, FOX_WIDTH), MXU_DTYPE),
        jax.ShapeDtypeStruct((n, FOX_WIDTH), MXU_DTYPE),
        jax.ShapeDtypeStruct((n, FOX_WIDTH), MXU_DTYPE),
        jax.ShapeDtypeStruct((n, SMALL_COLS), F32),
    ]
    vec = const((1, CONV_WIDTH))
    return pl.pallas_call(
        functools.partial(_in_proj_kernel, tiles_per_seq=tiles_per_seq),
        grid=(n // tm,),
        in_specs=[pl.BlockSpec((tm, D_MODEL), row),
                  const((D_MODEL, _C_END)), const((D_MODEL, 2 * SMALL_COLS)),
                  const((GDN_SHORT_CONV, 3 * GDN_WIDTH)), const((CONV_KERNEL, CONV_WIDTH)), vec, vec, vec, vec],
        out_specs=[pl.BlockSpec((tm, o.shape[1]), row) for o in outs],
        out_shape=outs,
        scratch_shapes=[pltpu.VMEM((tm + SUBLANES, 3 * GDN_WIDTH), F32),
                        pltpu.VMEM((tm + CONV_HALO, CONV_WIDTH), F32),
                        pltpu.VMEM((SUBLANES - 1, tm + CONV_HALO - SUBLANES, CONV_WIDTH), F32)],
        compiler_params=_cparams("arbitrary"),
        name="in_proj",
    )(xf, w_main, w_small, gdn_conv_w, cnv_w, cnv_b, cnv_ln_g, cnv_ln_b, cnv_scale)


AUG_BIAS_LANE = HEAD_DIM


def _gates_kernel(small_ref, a_ref, dtb_ref, fb_ref, fq_ref, fk_ref, fv_ref,
                  g_ref, beta_ref, gt_ref, st_ref, qa_ref, ka_ref, va_ref, carry_ref):
    t = pl.program_id(1)
    tt = small_ref.shape[1]
    b16 = jnp.bfloat16

    @pl.when(t == 0)
    def _():
        carry_ref[...] = jnp.zeros_like(carry_ref)

    s = small_ref[0]
    log_decay = -jnp.exp(a_ref[...]) * _softplus(s + dtb_ref[...])
    beta = _sigmoid(s)
    log_f = _log_sigmoid(s + fb_ref[...])

    r = lax.broadcasted_iota(jnp.int32, (tt, tt), 0)
    c = lax.broadcasted_iota(jnp.int32, (tt, tt), 1)
    tri = (c <= r)
    g_cum = _sel_lhs((tri & (r // GDN_CHUNK == c // GDN_CHUNK)).astype(b16), log_decay)
    c_cum = _sel_lhs(tri.astype(b16), log_f) + carry_ref[...]
    carry_ref[...] = c_cum[tt - 1:tt, :]

    er = lax.broadcasted_iota(jnp.int32, (LANES, GDN_WIDTH), 0)
    ec = lax.broadcasted_iota(jnp.int32, (LANES, GDN_WIDTH), 1) // HEAD_DIM
    g_ref[0] = _sel_rhs(g_cum, (er == ec).astype(b16))
    beta_ref[0] = _sel_rhs(beta, (er == ec + GDN_HEADS).astype(b16))
    gt_ref[0] = g_cum.T[0:SUBLANES, :]

    head_ones = _head_ones().astype(b16)
    stat_rows = []
    for p in range(N_PAIRS):
        k_pair = fk_ref[0, :, p * PAIR:(p + 1) * PAIR].astype(F32)
        stat_rows.append(jnp.max(_dot_ones(k_pair * k_pair, head_ones), axis=0, keepdims=True))
    stat_rows.append(c_cum[tt - 1:tt, :])
    stat_rows.append(jnp.zeros((SUBLANES - len(stat_rows), LANES), F32))
    st_ref[0, 0] = jnp.concatenate(stat_rows, axis=0)

    rr = lax.broadcasted_iota(jnp.int32, (LANES, LANES), 0)
    cc = lax.broadcasted_iota(jnp.int32, (LANES, LANES), 1)
    lane = lax.broadcasted_iota(jnp.int32, (1, LANES), 1)
    q_ones = ((lane >= AUG_BIAS_LANE) & (lane < AUG_BIAS_LANE + 3)).astype(F32)
    v_one = (lane == AUG_BIAS_LANE).astype(F32)
    neg_c = _split3(-c_cum)
    for h in range(FOX_HEADS):
        p, half = divmod(h, 2)
        lanes = slice(p * PAIR, (p + 1) * PAIR)
        sel = ((cc < HEAD_DIM) & (rr == cc + half * HEAD_DIM)).astype(b16)
        pick = lambda ref: jnp.dot(ref[0, :, lanes], sel, preferred_element_type=F32)
        bias = sum(jnp.dot(term, ((rr == 2 * GDN_HEADS + h) & (cc == AUG_BIAS_LANE + i)).astype(b16),
                           preferred_element_type=F32) for i, term in enumerate(neg_c))
        qa_ref[0, h] = (pick(fq_ref) + q_ones).astype(b16)
        ka_ref[0, h] = (pick(fk_ref) + bias).astype(b16)
        va_ref[0, h] = (pick(fv_ref) + v_one).astype(b16)


def _gates(small, a_row, dtb_row, fb_row, fq, fk, fv, tt):
    b, t, _ = small.shape
    row = pl.BlockSpec((1, LANES), lambda i, j: (0, 0))
    tile = lambda w: pl.BlockSpec((1, tt, w), lambda i, j: (i, j, 0))
    aug = pl.BlockSpec((1, FOX_HEADS, tt, LANES), lambda i, j: (i, 0, j, 0))
    aug_shape = jax.ShapeDtypeStruct((b, FOX_HEADS, t, LANES), jnp.bfloat16)
    return pl.pallas_call(
        _gates_kernel,
        grid=(b, t // tt),
        in_specs=[tile(SMALL_COLS), row, row, row, tile(FOX_WIDTH), tile(FOX_WIDTH), tile(FOX_WIDTH)],
        out_specs=[tile(GDN_WIDTH), tile(GDN_WIDTH), pl.BlockSpec((1, SUBLANES, tt), lambda i, j: (i, 0, j)),
                   pl.BlockSpec((1, 1, SUBLANES, LANES), lambda i, j: (i, j, 0, 0)), aug, aug, aug],
        out_shape=[jax.ShapeDtypeStruct((b, t, GDN_WIDTH), F32), jax.ShapeDtypeStruct((b, t, GDN_WIDTH), F32),
                   jax.ShapeDtypeStruct((b, SUBLANES, t), F32),
                   jax.ShapeDtypeStruct((b, t // tt, SUBLANES, LANES), F32), aug_shape, aug_shape, aug_shape],
        scratch_shapes=[pltpu.VMEM((1, LANES), F32)],
        compiler_params=_cparams("parallel", "arbitrary"),
        name="gates",
    )(small, a_row, dtb_row, fb_row, fq, fk, fv)


def _unit_lower_inverse_many(lows):
    n = lows[0].shape[0]
    r = lax.broadcasted_iota(jnp.int32, (n, n), 0)
    c = lax.broadcasted_iota(jnp.int32, (n, n), 1)
    eye = (r == c).astype(F32)
    diag16 = r // 16 == c // 16
    cast = lambda xs: [x.astype(MXU_DTYPE) for x in xs]
    mm = lambda xs, ys: [jnp.dot(x, y, preferred_element_type=F32) for x, y in zip(xs, ys)]
    add = lambda xs, ys: [x + y for x, y in zip(xs, ys)]
    d = [jnp.where(diag16, low, 0.0) for low in lows]
    db = cast(d)
    d2b = cast(mm(db, db))
    p = [eye - x for x in d]
    p = add(p, mm(cast(p), d2b))
    d4b = cast(mm(d2b, d2b))
    p = add(p, mm(cast(p), d4b))
    d8b = cast(mm(d4b, d4b))
    x = add(p, mm(cast(p), d8b))
    for blk in (32, 64):
        sel = (r // blk == c // blk) & (r // (blk // 2) != c // (blk // 2))
        xb = cast(x)
        xo = mm(xb, cast([jnp.where(sel, low, 0.0) for low in lows]))
        x = [a - b for a, b in zip(x, mm(cast(xo), xb))]
    return x


def _gdn_kernel(qkv_ref, g_ref, beta_ref, gt_ref, gate_ref, nw_ref, ms_ref, o_ref, s_ref):
    t = pl.program_id(1)
    blk = qkv_ref.shape[1]
    nc = blk // GDN_CHUNK

    @pl.when(t == 0)
    def _():
        s_ref[...] = jnp.zeros_like(s_ref)

    qkv = qkv_ref[0]

    ones = _head_ones().astype(jnp.bfloat16)
    lane = lax.broadcasted_iota(jnp.int32, (1, PAIR), 1)
    head0 = lane < HEAD_DIM
    n2 = 2 * GDN_CHUNK
    r = lax.broadcasted_iota(jnp.int32, (n2, n2), 0)
    c = lax.broadcasted_iota(jnp.int32, (n2, n2), 1)
    same = (r // GDN_CHUNK) == (c // GDN_CHUNK)
    causal = same & (c <= r)
    strict = same & (c < r)

    def l2n(v):
        return v * lax.rsqrt(_dot_ones(v * v, ones) + NORM_EPS)

    def stack(v):
        return jnp.concatenate([jnp.where(head0, v, 0.0), jnp.where(head0, 0.0, v)], axis=0)

    pair = lambda p, grp: slice(grp * GDN_WIDTH + p * PAIR, grp * GDN_WIDTH + (p + 1) * PAIR)
    q_p = [l2n(qkv[:, pair(p, 0)]) * HEAD_DIM ** -0.5 for p in range(N_PAIRS)]
    k_p = [l2n(qkv[:, pair(p, 1)]) for p in range(N_PAIRS)]
    v_p = [qkv[:, pair(p, 2)] for p in range(N_PAIRS)]

    units = [(ci, p) for ci in range(nc) for p in range(N_PAIRS)]
    rows = lambda ci: slice(ci * GDN_CHUNK, (ci + 1) * GDN_CHUNK)
    gc = [g_ref[0, rows(ci), pair(p, 0)] for ci, p in units]
    bt = [beta_ref[0, rows(ci), pair(p, 0)] for ci, p in units]
    qn = [q_p[p][rows(ci)] for ci, p in units]
    kn = [k_p[p][rows(ci)] for ci, p in units]
    vv = [v_p[p][rows(ci)] for ci, p in units]
    g_last = [g[GDN_CHUNK - 1:GDN_CHUNK, :] for g in gc]
    eg = [jnp.exp(g) for g in gc]
    kb = [k * b for k, b in zip(kn, bt)]
    k2 = [stack(k) for k in kn]

    def decay_of(g, ci, p):
        g_col = jnp.concatenate([jnp.broadcast_to(g[:, 0:1], (GDN_CHUNK, PAIR)),
                                 jnp.broadcast_to(g[:, HEAD_DIM:HEAD_DIM + 1], (GDN_CHUNK, PAIR))], axis=0)
        g_row = jnp.concatenate([gt_ref[0, 2 * p:2 * p + 1, rows(ci)], gt_ref[0, 2 * p + 1:2 * p + 2, rows(ci)]],
                                axis=1)
        return jnp.where(causal, jnp.exp(jnp.where(causal, g_col - g_row, 0.0)), 0.0)

    decay = [decay_of(g, ci, p) for g, (ci, p) in zip(gc, units)]
    k2b = [x.astype(MXU_DTYPE) for x in k2]
    low = [jnp.where(strict, _dot_nt(stack(a), b) * d, 0.0) for a, b, d in zip(kb, k2b, decay)]
    a_in = [(_dot_nt(stack(a), b) * d).astype(MXU_DTYPE) for a, b, d in zip(qn, k2b, decay)]
    t_inv = _unit_lower_inverse_many(low)
    uw = [_dot(ti, jnp.concatenate([stack(v * b), stack(a * e)], axis=1)).astype(MXU_DTYPE)
          for ti, v, b, a, e in zip(t_inv, vv, bt, kb, eg)]
    ket = [stack(k * jnp.exp(gl - g)).T for k, gl, g in zip(kn, g_last, gc)]
    nm = [_dot(a, b) for a, b in zip(ket, uw)]
    raw = [jnp.dot(a, b, preferred_element_type=F32) for a, b in zip(a_in, uw)]
    p_mat = [stack(q * e) - x[:, PAIR:] for q, e, x in zip(qn, eg, raw)]

    state = [s_ref[p] for p in range(N_PAIRS)]
    for i, (ci, p) in enumerate(units):
        s = state[p]
        o2 = _dot(p_mat[i], s) + raw[i][:, :PAIR]
        state[p] = s * jnp.exp(g_last[i]) - _dot(nm[i][:, PAIR:], s) + nm[i][:, :PAIR]
        o = o2[0:GDN_CHUNK] + o2[GDN_CHUNK:n2]
        ms = _dot_ones(o * o, ones) * (1.0 / HEAD_DIM)
        on = o * lax.rsqrt(ms + NORM_EPS) * nw_ref[...]
        o_ref[0, rows(ci), pair(p, 0)] = (on * _silu(gate_ref[0, rows(ci), pair(p, 0)])
                                          * ms_ref[:, pair(p, 0)]).astype(o_ref.dtype)
    for p in range(N_PAIRS):
        s_ref[p] = state[p]


def _gdn(qkv, g, beta, g_t, gate, norm_w2, mscale, blk):
    b, t, _ = qkv.shape
    tile = lambda width: pl.BlockSpec((1, blk, width), lambda i, j: (i, j, 0))
    const = lambda shape: pl.BlockSpec(shape, lambda i, j: (0, 0))
    return pl.pallas_call(
        _gdn_kernel,
        grid=(b, t // blk),
        in_specs=[tile(3 * GDN_WIDTH),
                  tile(GDN_WIDTH), tile(GDN_WIDTH), pl.BlockSpec((1, SUBLANES, blk), lambda i, j: (i, 0, j)),
                  tile(GDN_WIDTH), const((1, PAIR)), const((1, GDN_WIDTH))],
        out_specs=tile(GDN_WIDTH),
        out_shape=jax.ShapeDtypeStruct((b, t, GDN_WIDTH), MXU_DTYPE),
        scratch_shapes=[pltpu.VMEM((N_PAIRS, PAIR, PAIR), F32)],
        compiler_params=_cparams("parallel", "arbitrary"),
        name="gdn",
    )(qkv, g, beta, g_t, gate, norm_w2, mscale)


FOX_ROW_GROUP = 32


FOX_SKIP_MARGIN = 120.0
KEY_NORM_SLACK = 1.01


def _fox_kernel(q_ref, k_ref, v_ref, aux_ref, ms_ref, o_ref, s0_ref, s1_ref, p0_ref, p1_ref, a0_ref, a1_ref,
                m_ref, acc_ref, *, tq):
    i = pl.program_id(2)
    heads = range(2)
    rg = FOX_ROW_GROUP
    m_ref[...] = jnp.full(m_ref.shape, -jnp.inf, F32)
    acc_ref[...] = jnp.zeros(acc_ref.shape, F32)

    def scores(j, s_ref):
        start = pl.multiple_of(j * tq, tq)
        for h in heads:
            s_ref[h] = lax.dot_general(q_ref[0, h], k_ref[0, h, pl.ds(start, tq), :],
                                       (((1,), (1,)), ((), ())), preferred_element_type=F32)

    def softmax(s_ref, p_ref, a_ref, masked):
        for h in heads:
            for g in range(tq // rg):
                rows = slice(g * rg, (g + 1) * rg)
                s = s_ref[h, rows, :]
                if masked:
                    row_id = g * rg + lax.broadcasted_iota(jnp.int32, (rg, tq), 0)
                    col_id = lax.broadcasted_iota(jnp.int32, (rg, tq), 1)
                    s = jnp.where(col_id <= row_id, s, -jnp.inf)
                m_old = m_ref[h, rows, :]
                m_new = jnp.maximum(m_old, jnp.max(s, axis=-1, keepdims=True))
                a_ref[h, rows, :] = jnp.exp(m_old - m_new)
                m_ref[h, rows, :] = m_new
                m_wide = jnp.concatenate([m_new] * (tq // LANES), axis=1)
                p_ref[h, rows, :] = jnp.exp(s - m_wide).astype(p_ref.dtype)

    def weighted_values(j, p_ref, a_ref):
        start = pl.multiple_of(j * tq, tq)
        for h in heads:
            acc_ref[h] = a_ref[h] * acc_ref[h] + jnp.dot(p_ref[h], v_ref[0, h, pl.ds(start, tq), :],
                                                         preferred_element_type=F32)

    scores(i, s0_ref)
    softmax(s0_ref, p0_ref, a0_ref, True)
    scores(jnp.maximum(i - 1, 0), s1_ref)
    weighted_values(i, p0_ref, a0_ref)

    lane = lax.broadcasted_iota(jnp.int32, (1, LANES), 1)
    n = jnp.int32(0)
    for h in heads:
        q = q_ref[0, h].astype(F32)
        q_norm = jnp.sqrt(jnp.sum(jnp.where(lane < HEAD_DIM, q * q, 0.0), axis=-1, keepdims=True))
        key_norm = aux_ref[0, h, 1:2, :]
        block_end_c = aux_ref[0, h, 0:1, :]
        slack = jnp.max(q_norm * key_norm - m_ref[h], axis=0, keepdims=True)
        keep = (lane < i) & (slack - block_end_c >= -FOX_SKIP_MARGIN)
        n = jnp.maximum(n, jnp.sum(keep.astype(jnp.int32)))

    def body(t, carry):
        j = i - 1 - 2 * t
        softmax(s1_ref, p1_ref, a1_ref, False)
        scores(jnp.maximum(j - 1, 0), s0_ref)
        weighted_values(j, p1_ref, a1_ref)
        softmax(s0_ref, p0_ref, a0_ref, False)
        scores(jnp.maximum(j - 2, 0), s1_ref)
        weighted_values(j - 1, p0_ref, a0_ref)
        return carry

    lax.fori_loop(0, n // 2, body, 0)

    @pl.when(n % 2 == 1)
    def _():
        softmax(s1_ref, p1_ref, a1_ref, False)
        weighted_values(i - n, p1_ref, a1_ref)

    o = [acc_ref[h] / acc_ref[h][:, AUG_BIAS_LANE:AUG_BIAS_LANE + 1] for h in heads]
    lane = lax.broadcasted_iota(jnp.int32, (1, PAIR), 1)
    o_pair = jnp.where(lane < HEAD_DIM, o[0], pltpu.roll(o[1], HEAD_DIM, axis=1))
    o_ref[0] = (o_pair * ms_ref[...]).astype(o_ref.dtype)


def _fox_aux(stats):
    b, nt = stats.shape[:2]
    k_sq = jnp.max(stats[:, :, :N_PAIRS, :], axis=1)
    k_sq = k_sq.reshape(b, N_PAIRS, 2, HEAD_DIM)[..., 0].reshape(b, FOX_HEADS)
    key_norm = jnp.sqrt(k_sq) * KEY_NORM_SLACK
    block_end_c = jnp.transpose(stats[:, :, N_PAIRS, 2 * GDN_HEADS:2 * GDN_HEADS + FOX_HEADS], (0, 2, 1))
    aux = jnp.zeros((b, FOX_HEADS, SUBLANES, LANES), F32)
    aux = aux.at[:, :, 0, :nt].set(block_end_c)
    return aux.at[:, :, 1, :].set(jnp.broadcast_to(key_norm[:, :, None], (b, FOX_HEADS, LANES)))


def _fox(qa, ka, va, aux, mscale, tq):
    b, _, t, _ = qa.shape
    kern = functools.partial(_fox_kernel, tq=tq)
    return pl.pallas_call(
        kern,
        grid=(b, N_PAIRS, t // tq),
        in_specs=[pl.BlockSpec((1, 2, tq, LANES), lambda bi, p, i: (bi, p, i, 0)),
                  pl.BlockSpec((1, 2, t, LANES), lambda bi, p, i: (bi, p, 0, 0)),
                  pl.BlockSpec((1, 2, t, LANES), lambda bi, p, i: (bi, p, 0, 0)),
                  pl.BlockSpec((1, 2, SUBLANES, LANES), lambda bi, p, i: (bi, p, 0, 0)),
                  pl.BlockSpec((1, PAIR), lambda bi, p, i: (0, p))],
        out_specs=pl.BlockSpec((1, tq, PAIR), lambda bi, p, i: (bi, i, p)),
        out_shape=jax.ShapeDtypeStruct((b, t, FOX_WIDTH), MXU_DTYPE),
        scratch_shapes=[pltpu.VMEM((2, tq, tq), F32), pltpu.VMEM((2, tq, tq), F32),
                        pltpu.VMEM((2, tq, tq), jnp.bfloat16), pltpu.VMEM((2, tq, tq), jnp.bfloat16),
                        pltpu.VMEM((2, tq, LANES), F32), pltpu.VMEM((2, tq, LANES), F32),
                        pltpu.VMEM((2, tq, LANES), F32), pltpu.VMEM((2, tq, LANES), F32)],
        compiler_params=_cparams("parallel", "parallel", "arbitrary"),
        name="fox_attention",
    )(qa, ka, va, aux, mscale)


def _out_proj_kernel(oa_ref, ob_ref, oc_ref, x_ref, w_ref, g_ref, b_ref, o_ref):
    mix = jnp.dot(oa_ref[...], w_ref[0:GDN_WIDTH, :], preferred_element_type=F32)
    mix = mix + jnp.dot(ob_ref[...], w_ref[GDN_WIDTH:GDN_WIDTH + CONV_WIDTH, :], preferred_element_type=F32)
    mix = mix + jnp.dot(oc_ref[...], w_ref[GDN_WIDTH + CONV_WIDTH:D_MODEL, :], preferred_element_type=F32)
    o_ref[...] = _layer_norm_rows(DEEPNORM_ALPHA * x_ref[...] + mix, g_ref[...], b_ref[...])


def _out_proj(oa, ob, oc, xf, w, g, bvec, tm):
    n = xf.shape[0]
    row = lambda i: (i, 0)
    const = lambda i: (0, 0)
    return pl.pallas_call(
        _out_proj_kernel,
        grid=(n // tm,),
        in_specs=[pl.BlockSpec((tm, GDN_WIDTH), row), pl.BlockSpec((tm, CONV_WIDTH), row),
                  pl.BlockSpec((tm, FOX_WIDTH), row), pl.BlockSpec((tm, D_MODEL), row),
                  pl.BlockSpec((D_MODEL, D_MODEL), const),
                  pl.BlockSpec((1, D_MODEL), const), pl.BlockSpec((1, D_MODEL), const)],
        out_specs=pl.BlockSpec((tm, D_MODEL), row),
        out_shape=jax.ShapeDtypeStruct((n, D_MODEL), F32),
        compiler_params=_cparams("parallel"),
        name="out_proj_ln",
    )(oa, ob, oc, xf, w, g, bvec)


FFN_CHUNK = 256


def _swiglu_chunks(xb, w1, w3, w2, width):
    acc = None
    for c in range(width // FFN_CHUNK):
        cols = slice(c * FFN_CHUNK, (c + 1) * FFN_CHUNK)
        h = _silu(jnp.dot(xb, w1(slice(None), cols), preferred_element_type=F32)) * jnp.dot(
            xb, w3(slice(None), cols), preferred_element_type=F32)
        part = jnp.dot(h.astype(xb.dtype), w2(cols, slice(None)), preferred_element_type=F32)
        acc = part if acc is None else acc + part
    return acc


def _ffn_kernel(x_ref, w1_ref, w3_ref, w2_ref, g_ref, b_ref, o_ref):
    x = x_ref[...]
    ff = _swiglu_chunks(x.astype(w1_ref.dtype), lambda r, c: w1_ref[r, c], lambda r, c: w3_ref[r, c],
                        lambda r, c: w2_ref[r, c], w1_ref.shape[1])
    o_ref[...] = _layer_norm_rows(DEEPNORM_ALPHA * x + ff, g_ref[...], b_ref[...])


def _dense_ffn(xf, w1, w3, w2, g, bvec, tm):
    n = xf.shape[0]
    ff = w1.shape[1]
    const = lambda shape: pl.BlockSpec(shape, lambda i: (0, 0), pipeline_mode=pl.Buffered(1))
    return pl.pallas_call(
        _ffn_kernel,
        grid=(n // tm,),
        in_specs=[pl.BlockSpec((tm, D_MODEL), lambda i: (i, 0)),
                  const((D_MODEL, ff)), const((D_MODEL, ff)), const((ff, D_MODEL)),
                  const((1, D_MODEL)), const((1, D_MODEL))],
        out_specs=pl.BlockSpec((tm, D_MODEL), lambda i: (i, 0)),
        out_shape=jax.ShapeDtypeStruct((n, D_MODEL), F32),
        compiler_params=_cparams("parallel"),
        name="dense_ffn_ln",
    )(xf, w1, w3, w2, g, bvec)


def _router_kernel(x_ref, wr_ref, info_ref, cnt_ref, run_ref):
    i = pl.program_id(0)
    tm = x_ref.shape[0]

    @pl.when(i == 0)
    def _():
        run_ref[...] = jnp.zeros_like(run_ref)

    logits = jnp.dot(x_ref[...].astype(MXU_DTYPE), wr_ref[...], preferred_element_type=F32)
    lane = lax.broadcasted_iota(jnp.int32, (tm, LANES), 1)
    logits = jnp.where(lane < N_EXPERTS, logits, -jnp.inf)
    m1 = jnp.max(logits, axis=-1, keepdims=True)
    e1 = jnp.min(jnp.where(logits == m1, lane, LANES), axis=-1, keepdims=True)
    rest = jnp.where(lane == e1, -jnp.inf, logits)
    m2 = jnp.max(rest, axis=-1, keepdims=True)
    e2 = jnp.min(jnp.where(rest == m2, lane, LANES), axis=-1, keepdims=True)
    z = jnp.exp(m2 - m1)
    g1 = 1.0 / (1.0 + z)
    g2 = z / (1.0 + z)
    onehot = ((lane == e1) | (lane == e2)).astype(F32)
    r = lax.broadcasted_iota(jnp.int32, (tm, tm), 0)
    c = lax.broadcasted_iota(jnp.int32, (tm, tm), 1)
    before = jnp.dot((c < r).astype(jnp.bfloat16), onehot.astype(jnp.bfloat16),
                     preferred_element_type=F32) + run_ref[...]
    rank1 = jnp.sum(jnp.where(lane == e1, before, 0.0), axis=-1, keepdims=True)
    rank2 = jnp.sum(jnp.where(lane == e2, before, 0.0), axis=-1, keepdims=True)
    run_ref[...] = run_ref[...] + jnp.sum(onehot, axis=0, keepdims=True)
    cnt_ref[...] = run_ref[...]
    info = jnp.where(lane == 0, e1.astype(F32),
                     jnp.where(lane == 1, e2.astype(F32),
                               jnp.where(lane == 2, rank1,
                                         jnp.where(lane == 3, rank2,
                                                   jnp.where(lane == 4, g1, jnp.where(lane == 5, g2, 0.0))))))
    info_ref[...] = info


def _router(xf, wr_pad, tm):
    n = xf.shape[0]
    return pl.pallas_call(
        _router_kernel,
        grid=(n // tm,),
        in_specs=[pl.BlockSpec((tm, D_MODEL), lambda i: (i, 0)),
                  pl.BlockSpec((D_MODEL, LANES), lambda i: (0, 0))],
        out_specs=[pl.BlockSpec((tm, LANES), lambda i: (i, 0)),
                   pl.BlockSpec((1, LANES), lambda i: (0, 0))],
        out_shape=[jax.ShapeDtypeStruct((n, LANES), F32), jax.ShapeDtypeStruct((1, LANES), F32)],
        scratch_shapes=[pltpu.VMEM((1, LANES), F32)],
        compiler_params=_cparams("arbitrary"),
        name="moe_router",
    )(xf, wr_pad)


DMA_LOOP_UNROLL = 8


def _dispatch_kernel(d1_ref, d2_ref, x_ref, zeros_ref, xs_ref, sem):
    del zeros_ref
    tm = x_ref.shape[0]

    def row_copy(r, dst):
        return pltpu.make_async_copy(x_ref.at[pl.ds(r, 1), :], xs_ref.at[pl.ds(dst, 1), :], sem)

    def issue(r, carry):
        row_copy(r, d1_ref[0, 0, r]).start()
        row_copy(r, d2_ref[0, 0, r]).start()
        return carry

    lax.fori_loop(0, tm, issue, 0, unroll=DMA_LOOP_UNROLL)

    def drain(r, carry):
        row_copy(r, d1_ref[0, 0, r]).wait()
        row_copy(r, d2_ref[0, 0, r]).wait()
        return carry

    lax.fori_loop(0, tm, drain, 0, unroll=DMA_LOOP_UNROLL)


def _dispatch(xf, d1, d2, rows, tm):
    n = xf.shape[0]
    idx = lambda a: a.reshape(n // tm, 1, tm)
    zeros = jnp.zeros((rows, D_MODEL), xf.dtype)
    smem = lambda: pl.BlockSpec((1, 1, tm), lambda i: (i, 0, 0), memory_space=pltpu.SMEM)
    return pl.pallas_call(
        _dispatch_kernel,
        grid=(n // tm,),
        in_specs=[smem(), smem(),
                  pl.BlockSpec((tm, D_MODEL), lambda i: (i, 0)),
                  pl.BlockSpec(memory_space=pl.ANY)],
        out_specs=pl.BlockSpec(memory_space=pl.ANY),
        out_shape=jax.ShapeDtypeStruct((rows, D_MODEL), xf.dtype),
        scratch_shapes=[pltpu.SemaphoreType.DMA(())],
        input_output_aliases={3: 0},
        compiler_params=_cparams("arbitrary"),
        name="moe_dispatch",
    )(idx(d1), idx(d2), xf, zeros)


def _expert_kernel(te_ref, nu_ref, xs_ref, w1_ref, w3_ref, w2_ref, ys_ref, xb_ref, acc_ref):
    del te_ref
    i = pl.program_id(0)
    f = pl.program_id(1)
    used = i < nu_ref[0]

    @pl.when(f == 0)
    def _():
        xb_ref[...] = xs_ref[...].astype(xb_ref.dtype)
        acc_ref[...] = jnp.zeros_like(acc_ref)

    @pl.when(used)
    def _():
        acc_ref[...] += _swiglu_chunks(xb_ref[...], lambda r, c: w1_ref[0, r, c], lambda r, c: w3_ref[0, r, c],
                                       lambda r, c: w2_ref[0, r, c], w1_ref.shape[2])

    @pl.when(f == pl.num_programs(1) - 1)
    def _():
        ys_ref[...] = acc_ref[...]


def _experts(xs, tile_expert, n_used, w1, w3, w2, tm, tf):
    rows = xs.shape[0]
    ff = w1.shape[2]
    n_tiles = rows // tm

    def x_map(i, f, te, nu):
        return (jnp.maximum(jnp.minimum(i, nu[0] - 1), 0), 0)

    def w13_map(i, f, te, nu):
        return (te[i], 0, jnp.where(i < nu[0], f, ff // tf - 1))

    def w2_map(i, f, te, nu):
        return (te[i], jnp.where(i < nu[0], f, ff // tf - 1), 0)

    grid_spec = pltpu.PrefetchScalarGridSpec(
        num_scalar_prefetch=2,
        grid=(n_tiles, ff // tf),
        in_specs=[pl.BlockSpec((tm, D_MODEL), x_map),
                  pl.BlockSpec((1, D_MODEL, tf), w13_map),
                  pl.BlockSpec((1, D_MODEL, tf), w13_map),
                  pl.BlockSpec((1, tf, D_MODEL), w2_map)],
        out_specs=pl.BlockSpec((tm, D_MODEL), lambda i, f, te, nu: (i, 0)),
        scratch_shapes=[pltpu.VMEM((tm, D_MODEL), MXU_DTYPE), pltpu.VMEM((tm, D_MODEL), F32)],
    )
    return pl.pallas_call(
        _expert_kernel,
        grid_spec=grid_spec,
        out_shape=jax.ShapeDtypeStruct((rows, D_MODEL), F32),
        compiler_params=_cparams("arbitrary", "arbitrary"),
        name="moe_experts",
    )(tile_expert, n_used, xs, w1, w3, w2)


def _combine_kernel(d1_ref, d2_ref, x_ref, info_ref, ys_ref, g_ref, b_ref, o_ref, y1_ref, y2_ref, sem):
    tm = x_ref.shape[0]

    def row_copy(src, r, buf):
        return pltpu.make_async_copy(ys_ref.at[pl.ds(src, 1), :], buf.at[pl.ds(r, 1), :], sem)

    def issue(r, carry):
        row_copy(d1_ref[0, 0, r], r, y1_ref).start()
        row_copy(d2_ref[0, 0, r], r, y2_ref).start()
        return carry

    lax.fori_loop(0, tm, issue, 0, unroll=DMA_LOOP_UNROLL)

    def drain(r, carry):
        row_copy(d1_ref[0, 0, r], r, y1_ref).wait()
        row_copy(d2_ref[0, 0, r], r, y2_ref).wait()
        return carry

    lax.fori_loop(0, tm, drain, 0, unroll=DMA_LOOP_UNROLL)
    info = info_ref[...]
    ff = info[:, 4:5] * y1_ref[...] + info[:, 5:6] * y2_ref[...]
    o_ref[...] = _layer_norm_rows(DEEPNORM_ALPHA * x_ref[...] + ff, g_ref[...], b_ref[...])


def _combine(xf, info, ys, d1, d2, g, bvec, tm):
    n = xf.shape[0]
    idx = lambda a: a.reshape(n // tm, 1, tm)
    smem = lambda: pl.BlockSpec((1, 1, tm), lambda i: (i, 0, 0), memory_space=pltpu.SMEM)
    return pl.pallas_call(
        _combine_kernel,
        grid=(n // tm,),
        in_specs=[smem(), smem(),
                  pl.BlockSpec((tm, D_MODEL), lambda i: (i, 0)),
                  pl.BlockSpec((tm, LANES), lambda i: (i, 0)),
                  pl.BlockSpec(memory_space=pl.ANY),
                  pl.BlockSpec((1, D_MODEL), lambda i: (0, 0)),
                  pl.BlockSpec((1, D_MODEL), lambda i: (0, 0))],
        out_specs=pl.BlockSpec((tm, D_MODEL), lambda i: (i, 0)),
        out_shape=jax.ShapeDtypeStruct((n, D_MODEL), F32),
        scratch_shapes=[pltpu.VMEM((tm, D_MODEL), F32), pltpu.VMEM((tm, D_MODEL), F32),
                        pltpu.SemaphoreType.DMA(())],
        compiler_params=_cparams("arbitrary"),
        name="moe_combine_ln",
    )(idx(d1), idx(d2), xf, info, ys, g, bvec)


MOE_TILE = 512


def _moe_ffn(xf, w_router, w1, w3, w2, g, bvec, tm_tok, tf):
    n = xf.shape[0]
    wr_pad = jnp.zeros((D_MODEL, LANES), MXU_DTYPE).at[:, :N_EXPERTS].set(w_router.astype(MXU_DTYPE))
    info, counts = _router(xf, wr_pad, tm_tok)
    sizes = counts[0, :N_EXPERTS].astype(jnp.int32)
    tiles_per = (sizes + MOE_TILE - 1) // MOE_TILE
    tile_end = jnp.cumsum(tiles_per)
    seg_start = (tile_end - tiles_per) * MOE_TILE
    e1 = info[:, 0].astype(jnp.int32)
    e2 = info[:, 1].astype(jnp.int32)
    d1 = seg_start[e1] + info[:, 2].astype(jnp.int32)
    d2 = seg_start[e2] + info[:, 3].astype(jnp.int32)
    n_tiles = (n * TOP_K) // MOE_TILE + N_EXPERTS
    rows = n_tiles * MOE_TILE
    tile_expert = jnp.minimum(jnp.searchsorted(tile_end, jnp.arange(n_tiles, dtype=jnp.int32), side='right'),
                              N_EXPERTS - 1).astype(jnp.int32)
    n_used = tile_end[N_EXPERTS - 1:].astype(jnp.int32)
    xs = _dispatch(xf, d1, d2, rows, tm_tok)
    ys = _experts(xs, tile_expert, n_used, w1, w3, w2, MOE_TILE, tf)
    return _combine(xf, info, ys, d1, d2, g, bvec, tm_tok)


def _pack_in_proj(w_in):
    cuts = np.cumsum([0, GDN_WIDTH, GDN_WIDTH, GDN_WIDTH, GDN_HEADS, GDN_HEADS, GDN_WIDTH,
                      2 * CONV_WIDTH, FOX_WIDTH, FOX_WIDTH, FOX_WIDTH, FOX_HEADS])
    seg = lambda i: w_in[:, cuts[i]:cuts[i + 1]]
    w_main = jnp.concatenate([seg(0), seg(1), seg(2), seg(5), seg(6), seg(7), seg(8), seg(9)], axis=1)
    pad = jnp.zeros((D_MODEL, SMALL_COLS - 2 * GDN_HEADS - FOX_HEADS), w_in.dtype)
    w_small = jnp.concatenate([seg(3), seg(4), seg(10), pad], axis=1).astype(F32)
    ws_hi = w_small.astype(MXU_DTYPE)
    ws_lo = (w_small - ws_hi.astype(F32)).astype(MXU_DTYPE)
    return w_main.astype(MXU_DTYPE), jnp.concatenate([ws_hi, ws_lo], axis=1)


def _lane_row(vals, offset):
    return jnp.zeros((1, LANES), F32).at[0, offset:offset + vals.shape[0]].set(vals.astype(F32))


def _mixer(xf, b, t, w_in, mix_scale, w_out, gdn_conv_w, gdn_a_log, gdn_dt_bias, gdn_norm_w,
           cnv_dw_w, cnv_dw_b, cnv_ln_g, cnv_ln_b, fox_f_bias, ln_g, ln_b, tiles):
    w_main, w_small = _pack_in_proj(w_in)
    ms = mix_scale.reshape(1, D_MODEL).astype(F32)
    row = lambda v: v.reshape(1, -1).astype(F32)
    qkv, gate, o_b, fq, fk, fv, small = _in_proj(
        xf, w_main, w_small, gdn_conv_w, cnv_dw_w, row(cnv_dw_b), row(cnv_ln_g), row(cnv_ln_b),
        ms[:, GDN_WIDTH:GDN_WIDTH + CONV_WIDTH], tiles["tm"], t // tiles["tm"])
    r3 = lambda a: a.reshape(b, t, a.shape[-1])
    assert tiles["tt"] == tiles["tq"]
    g_cum, beta, g_t, stats, qa, ka, va = _gates(
        r3(small), _lane_row(gdn_a_log, 0), _lane_row(gdn_dt_bias, 0), _lane_row(fox_f_bias, 2 * GDN_HEADS),
        r3(fq), r3(fk), r3(fv), tiles["tt"])
    norm_w2 = jnp.tile(gdn_norm_w.astype(F32), 2).reshape(1, PAIR)
    o_a = _gdn(r3(qkv), g_cum, beta, g_t, r3(gate), norm_w2, ms[:, :GDN_WIDTH], tiles["gdn_blk"])
    o_c = _fox(qa, ka, va, _fox_aux(stats), ms[:, GDN_WIDTH + CONV_WIDTH:], tiles["tq"])
    flat = lambda a: a.reshape(b * t, a.shape[-1])
    return _out_proj(flat(o_a), o_b, flat(o_c), xf, w_out.astype(MXU_DTYPE), row(ln_g), row(ln_b),
                     tiles["tm"])


def _tiles(t):
    return dict(tm=min(512, t), tt=min(512, t), gdn_blk=min(256, t), tq=min(512, t),
                tf_moe=1792)


def kernel(x, w_in, mix_scale, w_out, gdn_conv_w, gdn_a_log, gdn_dt_bias, gdn_norm_w, cnv_dw_w, cnv_dw_b,
           cnv_ln_g, cnv_ln_b, fox_f_bias, ln_mix_g, ln_mix_b, ln_ffn_g, ln_ffn_b, ffn_w1, ffn_w3, ffn_w2,
           moe_router, moe_w1, moe_w3, moe_w2):
    b, t, d = x.shape
    tiles = _tiles(t)
    row = lambda v: v.reshape(1, -1).astype(F32)
    xf = x.reshape(b * t, d)
    for l in range(DEPTH):
        xf = _mixer(xf, b, t, w_in[l], mix_scale[l], w_out[l], gdn_conv_w[l], gdn_a_log[l], gdn_dt_bias[l],
                    gdn_norm_w[l], cnv_dw_w[l], cnv_dw_b[l], cnv_ln_g[l], cnv_ln_b[l], fox_f_bias[l],
                    ln_mix_g[l], ln_mix_b[l], tiles)
        if l % 2 == 0:
            e = l // 2
            xf = _dense_ffn(xf, ffn_w1[e].astype(MXU_DTYPE), ffn_w3[e].astype(MXU_DTYPE),
                            ffn_w2[e].astype(MXU_DTYPE), row(ln_ffn_g[l]), row(ln_ffn_b[l]),
                            tiles["tm"])
        else:
            e = l // 2
            xf = _moe_ffn(xf, moe_router[e], moe_w1[e].astype(MXU_DTYPE), moe_w3[e].astype(MXU_DTYPE),
                          moe_w2[e].astype(MXU_DTYPE), row(ln_ffn_g[l]), row(ln_ffn_b[l]),
                          tiles["tm"], tiles["tf_moe"])
    return xf.reshape(b, t, d)
```

```python
import functools

import jax
import jax.numpy as jnp
import numpy as np
from jax import lax
from jax.experimental import pallas as pl
from jax.experimental.pallas import tpu as pltpu

D_MODEL = 1024
DEPTH = 2
HEAD_DIM = 64
GDN_WIDTH = 384
CONV_WIDTH = 256
FOX_WIDTH = 384
GDN_HEADS = 6
FOX_HEADS = 6
GDN_SHORT_CONV = 4
GDN_CHUNK = 64
CONV_KERNEL = 31
FFN_DENSE = 2816
N_EXPERTS = 8
TOP_K = 2
FFN_EXPERT = 3584
DEEPNORM_ALPHA = (2 * DEPTH) ** 0.25
LN_EPS = 1e-5
NORM_EPS = 1e-6

LANES = 128
SUBLANES = 8
PAIR = 2 * HEAD_DIM
N_PAIRS = GDN_WIDTH // PAIR
SMALL_COLS = LANES

_C_QKV = 0
_C_GATE = _C_QKV + 3 * GDN_WIDTH
_C_GLU = _C_GATE + GDN_WIDTH
_C_FQ = _C_GLU + 2 * CONV_WIDTH
_C_FK = _C_FQ + FOX_WIDTH
_C_FV = _C_FK + FOX_WIDTH
_C_END = _C_FV + FOX_WIDTH

MXU_DTYPE = jnp.bfloat16
F32 = jnp.float32

VMEM_LIMIT = 56 * 1024 * 1024


def _cparams(*sem):
    return pltpu.CompilerParams(dimension_semantics=sem, vmem_limit_bytes=VMEM_LIMIT)


def _dot(a, b):
    return jnp.dot(a.astype(MXU_DTYPE), b.astype(MXU_DTYPE), preferred_element_type=F32)


def _dot_nt(a, b):
    return lax.dot_general(a.astype(MXU_DTYPE), b.astype(MXU_DTYPE),
                           (((1,), (1,)), ((), ())), preferred_element_type=F32)


def _split(a):
    hi = a.astype(jnp.bfloat16)
    return hi, (a - hi.astype(F32)).astype(jnp.bfloat16)


def _split3(a):
    hi = a.astype(jnp.bfloat16)
    rest = a - hi.astype(F32)
    mid = rest.astype(jnp.bfloat16)
    lo = (rest - mid.astype(F32)).astype(jnp.bfloat16)
    return hi, mid, lo


def _dot_ones(a, ones_b16):
    ah, al = _split(a)
    d = lambda x: jnp.dot(x, ones_b16, preferred_element_type=F32)
    return d(ah) + d(al)


def _sel_rhs(a, sel_b16):
    return sum(jnp.dot(t, sel_b16, preferred_element_type=F32) for t in _split3(a))


def _sel_lhs(sel_b16, b):
    return sum(jnp.dot(sel_b16, t, preferred_element_type=F32) for t in _split3(b))


def _sigmoid(x):
    return 1.0 / (1.0 + jnp.exp(-x))


def _silu(x):
    return x * _sigmoid(x)


def _softplus(x):
    return jnp.maximum(x, 0.0) + jnp.log1p(jnp.exp(-jnp.abs(x)))


def _log_sigmoid(x):
    return -_softplus(-x)


def _layer_norm_rows(y, g, b):
    mu = jnp.mean(y, axis=-1, keepdims=True)
    d = y - mu
    var = jnp.mean(d * d, axis=-1, keepdims=True)
    return d * lax.rsqrt(var + LN_EPS) * g + b


def _head_ones():
    r = lax.broadcasted_iota(jnp.int32, (PAIR, PAIR), 0) // HEAD_DIM
    c = lax.broadcasted_iota(jnp.int32, (PAIR, PAIR), 1) // HEAD_DIM
    return (r == c).astype(F32)


CONV_HALO = 32
CONV_ROW_BLOCK = 128
SHORT_CONV_ROW_BLOCK = 128


def _in_proj_kernel(x_ref, w_ref, ws_ref, gw_ref, cw_ref, cb_ref, lg_ref, lb_ref, ms_ref,
                    qkv_ref, gate_ref, ob_ref, fq_ref, fk_ref, fv_ref, small_ref,
                    qbuf_ref, cbuf_ref, sh_ref, *, tiles_per_seq):
    first = pl.program_id(0) % tiles_per_seq == 0
    tm = x_ref.shape[0]
    x = x_ref[...]
    xb = x.astype(MXU_DTYPE)

    def mm(lo, hi):
        return jnp.dot(xb, w_ref[:, lo:hi], preferred_element_type=F32)

    @pl.when(first)
    def _():
        cbuf_ref[0:CONV_HALO, :] = jnp.zeros((CONV_HALO, CONV_WIDTH), F32)
        qbuf_ref[0:SUBLANES, :] = jnp.zeros((SUBLANES, 3 * GDN_WIDTH), F32)

    @pl.when(jnp.logical_not(first))
    def _():
        cbuf_ref[0:CONV_HALO, :] = cbuf_ref[tm:tm + CONV_HALO, :]
        qbuf_ref[0:SUBLANES, :] = qbuf_ref[tm:tm + SUBLANES, :]

    glu = mm(_C_GLU, _C_FQ)
    cbuf_ref[CONV_HALO:CONV_HALO + tm, :] = glu[:, 0:CONV_WIDTH] * _sigmoid(glu[:, CONV_WIDTH:2 * CONV_WIDTH])
    span = tm + CONV_HALO - SUBLANES
    for s in range(1, SUBLANES):
        sh_ref[s - 1] = cbuf_ref[s:s + span, :]
    for r0 in range(0, tm, CONV_ROW_BLOCK):
        acc = jnp.zeros((CONV_ROW_BLOCK, CONV_WIDTH), F32) + cb_ref[...]
        for j in range(CONV_KERNEL):
            lo = r0 + CONV_HALO - (CONV_KERNEL - 1) + j
            base, phase = lo - lo % SUBLANES, lo % SUBLANES
            tap = (cbuf_ref[base:base + CONV_ROW_BLOCK, :] if phase == 0
                   else sh_ref[phase - 1, base:base + CONV_ROW_BLOCK, :])
            acc = acc + cw_ref[j:j + 1, :] * tap
        y = _silu(_layer_norm_rows(acc, lg_ref[...], lb_ref[...])) * ms_ref[...]
        ob_ref[r0:r0 + CONV_ROW_BLOCK, :] = y.astype(ob_ref.dtype)

    qbuf_ref[SUBLANES:SUBLANES + tm, :] = mm(_C_QKV, _C_GATE)
    for r0 in range(0, tm, SHORT_CONV_ROW_BLOCK):
        acc = jnp.zeros((SHORT_CONV_ROW_BLOCK, 3 * GDN_WIDTH), F32)
        for j in range(GDN_SHORT_CONV):
            lo = r0 + SUBLANES - (GDN_SHORT_CONV - 1) + j
            acc = acc + gw_ref[j:j + 1, :] * qbuf_ref[lo:lo + SHORT_CONV_ROW_BLOCK, :]
        qkv_ref[r0:r0 + SHORT_CONV_ROW_BLOCK, :] = _silu(acc)

    gate_ref[...] = mm(_C_GATE, _C_GLU)
    fq_ref[...] = (mm(_C_FQ, _C_FK) * HEAD_DIM ** -0.5).astype(fq_ref.dtype)
    fk_ref[...] = mm(_C_FK, _C_FV).astype(fk_ref.dtype)
    fv_ref[...] = mm(_C_FV, _C_END).astype(fv_ref.dtype)
    x_lo = (x - xb.astype(F32)).astype(MXU_DTYPE)
    hh_hl = jnp.dot(xb, ws_ref[...], preferred_element_type=F32)
    lh = jnp.dot(x_lo, ws_ref[:, 0:SMALL_COLS], preferred_element_type=F32)
    small_ref[...] = hh_hl[:, 0:SMALL_COLS] + hh_hl[:, SMALL_COLS:2 * SMALL_COLS] + lh


def _in_proj(xf, w_main, w_small, gdn_conv_w, cnv_w, cnv_b, cnv_ln_g, cnv_ln_b, cnv_scale, tm, tiles_per_seq):
    n = xf.shape[0]
    row = lambda i: (i, 0)
    const = lambda shape: pl.BlockSpec(shape, lambda i: (0, 0))
    outs = [
        jax.ShapeDtypeStruct((n, 3 * GDN_WIDTH), F32),
        jax.ShapeDtypeStruct((n, GDN_WIDTH), F32),
        jax.ShapeDtypeStruct((n, CONV_WIDTH), MXU_DTYPE),
        jax.ShapeDtypeStruct((n, FOX_WIDTH), MXU_DTYPE),
        jax.ShapeDtypeStruct((n, FOX_WIDTH), MXU_DTYPE),
        jax.ShapeDtypeStruct((n, FOX_WIDTH), MXU_DTYPE),
        jax.ShapeDtypeStruct((n, SMALL_COLS), F32),
    ]
    vec = const((1, CONV_WIDTH))
    return pl.pallas_call(
        functools.partial(_in_proj_kernel, tiles_per_seq=tiles_per_seq),
        grid=(n // tm,),
        in_specs=[pl.BlockSpec((tm, D_MODEL), row),
                  const((D_MODEL, _C_END)), const((D_MODEL, 2 * SMALL_COLS)),
                  const((GDN_SHORT_CONV, 3 * GDN_WIDTH)), const((CONV_KERNEL, CONV_WIDTH)), vec, vec, vec, vec],
        out_specs=[pl.BlockSpec((tm, o.shape[1]), row) for o in outs],
        out_shape=outs,
        scratch_shapes=[pltpu.VMEM((tm + SUBLANES, 3 * GDN_WIDTH), F32),
                        pltpu.VMEM((tm + CONV_HALO, CONV_WIDTH), F32),
                        pltpu.VMEM((SUBLANES - 1, tm + CONV_HALO - SUBLANES, CONV_WIDTH), F32)],
        compiler_params=_cparams("arbitrary"),
        name="in_proj",
    )(xf, w_main, w_small, gdn_conv_w, cnv_w, cnv_b, cnv_ln_g, cnv_ln_b, cnv_scale)


AUG_BIAS_LANE = HEAD_DIM


def _gates_kernel(small_ref, a_ref, dtb_ref, fb_ref, fq_ref, fk_ref, fv_ref,
                  g_ref, beta_ref, gt_ref, st_ref, qa_ref, ka_ref, va_ref, carry_ref):
    t = pl.program_id(1)
    tt = small_ref.shape[1]
    b16 = jnp.bfloat16

    @pl.when(t == 0)
    def _():
        carry_ref[...] = jnp.zeros_like(carry_ref)

    s = small_ref[0]
    log_decay = -jnp.exp(a_ref[...]) * _softplus(s + dtb_ref[...])
    beta = _sigmoid(s)
    log_f = _log_sigmoid(s + fb_ref[...])

    lane = lax.broadcasted_iota(jnp.int32, (1, LANES), 1)
    rr = lax.broadcasted_iota(jnp.int32, (LANES, LANES), 0)
    cc = lax.broadcasted_iota(jnp.int32, (LANES, LANES), 1)
    tri_chunk = ((cc <= rr) & (rr // GDN_CHUNK == cc // GDN_CHUNK)).astype(b16)
    both = jnp.where(lane < 2 * GDN_HEADS, log_decay, log_f)
    in_chunk = jnp.concatenate([_sel_lhs(tri_chunk, both[r0:r0 + LANES]) for r0 in range(0, tt, LANES)], axis=0)
    g_cum = in_chunk
    carry = carry_ref[...]
    pieces = []
    for r0 in range(0, tt, GDN_CHUNK):
        chunk = in_chunk[r0:r0 + GDN_CHUNK]
        pieces.append(chunk + carry)
        carry = carry + chunk[GDN_CHUNK - 1:GDN_CHUNK]
    c_cum = jnp.concatenate(pieces, axis=0)
    carry_ref[...] = carry

    er = lax.broadcasted_iota(jnp.int32, (LANES, 2 * GDN_WIDTH), 0)
    ec = lax.broadcasted_iota(jnp.int32, (LANES, 2 * GDN_WIDTH), 1) // HEAD_DIM
    expanded = _sel_rhs(jnp.where(lane < GDN_HEADS, g_cum, beta), (er == ec).astype(b16))
    g_ref[0] = expanded[:, 0:GDN_WIDTH]
    beta_ref[0] = expanded[:, GDN_WIDTH:2 * GDN_WIDTH]
    gt_ref[0] = g_cum.T[0:SUBLANES, :]

    head_ones = _head_ones().astype(b16)
    stat_rows = []
    for p in range(N_PAIRS):
        k_pair = fk_ref[0, :, p * PAIR:(p + 1) * PAIR].astype(F32)
        stat_rows.append(jnp.max(_dot_ones(k_pair * k_pair, head_ones), axis=0, keepdims=True))
    stat_rows.append(c_cum[tt - 1:tt, :])
    stat_rows.append(jnp.zeros((SUBLANES - len(stat_rows), LANES), F32))
    st_ref[0, 0] = jnp.concatenate(stat_rows, axis=0)

    q_ones = ((lane >= AUG_BIAS_LANE) & (lane < AUG_BIAS_LANE + 3)).astype(F32)
    v_one = (lane == AUG_BIAS_LANE).astype(F32)
    br = lax.broadcasted_iota(jnp.int32, (3 * LANES, FOX_HEADS * LANES), 0)
    bc = lax.broadcasted_iota(jnp.int32, (3 * LANES, FOX_HEADS * LANES), 1)
    place = ((br % LANES == 2 * GDN_HEADS + bc // LANES) & (bc % LANES == AUG_BIAS_LANE + br // LANES)).astype(b16)
    bias_all = jnp.dot(jnp.concatenate(_split3(-c_cum), axis=1), place, preferred_element_type=F32)
    upper_to_lower = ((cc < HEAD_DIM) & (rr == cc + HEAD_DIM)).astype(b16)
    for h in range(FOX_HEADS):
        p, half = divmod(h, 2)
        lanes = slice(p * PAIR, (p + 1) * PAIR)

        def pick(ref):
            if half == 0:
                return jnp.where(lane < HEAD_DIM, ref[0, :, lanes].astype(F32), 0.0)
            return jnp.dot(ref[0, :, lanes], upper_to_lower, preferred_element_type=F32)

        qa_ref[0, h] = (pick(fq_ref) + q_ones).astype(b16)
        ka_ref[0, h] = (pick(fk_ref) + bias_all[:, h * LANES:(h + 1) * LANES]).astype(b16)
        va_ref[0, h] = (pick(fv_ref) + v_one).astype(b16)


def _gates(small, a_row, dtb_row, fb_row, fq, fk, fv, tt):
    b, t, _ = small.shape
    row = pl.BlockSpec((1, LANES), lambda i, j: (0, 0))
    tile = lambda w: pl.BlockSpec((1, tt, w), lambda i, j: (i, j, 0))
    aug = pl.BlockSpec((1, FOX_HEADS, tt, LANES), lambda i, j: (i, 0, j, 0))
    aug_shape = jax.ShapeDtypeStruct((b, FOX_HEADS, t, LANES), jnp.bfloat16)
    return pl.pallas_call(
        _gates_kernel,
        grid=(b, t // tt),
        in_specs=[tile(SMALL_COLS), row, row, row, tile(FOX_WIDTH), tile(FOX_WIDTH), tile(FOX_WIDTH)],
        out_specs=[tile(GDN_WIDTH), tile(GDN_WIDTH), pl.BlockSpec((1, SUBLANES, tt), lambda i, j: (i, 0, j)),
                   pl.BlockSpec((1, 1, SUBLANES, LANES), lambda i, j: (i, j, 0, 0)), aug, aug, aug],
        out_shape=[jax.ShapeDtypeStruct((b, t, GDN_WIDTH), F32), jax.ShapeDtypeStruct((b, t, GDN_WIDTH), F32),
                   jax.ShapeDtypeStruct((b, SUBLANES, t), F32),
                   jax.ShapeDtypeStruct((b, t // tt, SUBLANES, LANES), F32), aug_shape, aug_shape, aug_shape],
        scratch_shapes=[pltpu.VMEM((1, LANES), F32)],
        compiler_params=_cparams("parallel", "arbitrary"),
        name="gates",
    )(small, a_row, dtb_row, fb_row, fq, fk, fv)


def _unit_lower_inverse_many(lows):
    n = lows[0].shape[0]
    r = lax.broadcasted_iota(jnp.int32, (n, n), 0)
    c = lax.broadcasted_iota(jnp.int32, (n, n), 1)
    eye = (r == c).astype(F32)
    diag16 = r // 16 == c // 16
    cast = lambda xs: [x.astype(MXU_DTYPE) for x in xs]
    mm = lambda xs, ys: [jnp.dot(x, y, preferred_element_type=F32) for x, y in zip(xs, ys)]
    add = lambda xs, ys: [x + y for x, y in zip(xs, ys)]
    d = [jnp.where(diag16, low, 0.0) for low in lows]
    db = cast(d)
    d2b = cast(mm(db, db))
    p = [eye - x for x in d]
    p = add(p, mm(cast(p), d2b))
    d4b = cast(mm(d2b, d2b))
    p = add(p, mm(cast(p), d4b))
    d8b = cast(mm(d4b, d4b))
    x = add(p, mm(cast(p), d8b))
    for blk in (32, 64):
        sel = (r // blk == c // blk) & (r // (blk // 2) != c // (blk // 2))
        xb = cast(x)
        xo = mm(xb, cast([jnp.where(sel, low, 0.0) for low in lows]))
        x = [a - b for a, b in zip(x, mm(cast(xo), xb))]
    return x


def _gdn_kernel(qkv_ref, g_ref, beta_ref, gt_ref, gate_ref, nw_ref, ms_ref, o_ref, s_ref):
    t = pl.program_id(1)
    blk = qkv_ref.shape[1]
    nc = blk // GDN_CHUNK

    @pl.when(t == 0)
    def _():
        s_ref[...] = jnp.zeros_like(s_ref)

    qkv = qkv_ref[0]

    ones = _head_ones().astype(jnp.bfloat16)
    lane = lax.broadcasted_iota(jnp.int32, (1, PAIR), 1)
    head0 = lane < HEAD_DIM
    n2 = 2 * GDN_CHUNK
    r = lax.broadcasted_iota(jnp.int32, (n2, n2), 0)
    c = lax.broadcasted_iota(jnp.int32, (n2, n2), 1)
    same = (r // GDN_CHUNK) == (c // GDN_CHUNK)
    causal = same & (c <= r)
    strict = same & (c < r)

    def l2n(v):
        return v * lax.rsqrt(_dot_ones(v * v, ones) + NORM_EPS)

    def stack(v):
        return jnp.concatenate([jnp.where(head0, v, 0.0), jnp.where(head0, 0.0, v)], axis=0)

    pair = lambda p, grp: slice(grp * GDN_WIDTH + p * PAIR, grp * GDN_WIDTH + (p + 1) * PAIR)
    q_p = [l2n(qkv[:, pair(p, 0)]) * HEAD_DIM ** -0.5 for p in range(N_PAIRS)]
    k_p = [l2n(qkv[:, pair(p, 1)]) for p in range(N_PAIRS)]
    v_p = [qkv[:, pair(p, 2)] for p in range(N_PAIRS)]

    units = [(ci, p) for ci in range(nc) for p in range(N_PAIRS)]
    rows = lambda ci: slice(ci * GDN_CHUNK, (ci + 1) * GDN_CHUNK)
    gc = [g_ref[0, rows(ci), pair(p, 0)] for ci, p in units]
    bt = [beta_ref[0, rows(ci), pair(p, 0)] for ci, p in units]
    qn = [q_p[p][rows(ci)] for ci, p in units]
    kn = [k_p[p][rows(ci)] for ci, p in units]
    vv = [v_p[p][rows(ci)] for ci, p in units]
    g_last = [g[GDN_CHUNK - 1:GDN_CHUNK, :] for g in gc]
    eg = [jnp.exp(g) for g in gc]
    kb = [k * b for k, b in zip(kn, bt)]
    k2 = [stack(k) for k in kn]

    def decay_of(g, ci, p):
        g_col = jnp.concatenate([jnp.broadcast_to(g[:, 0:1], (GDN_CHUNK, PAIR)),
                                 jnp.broadcast_to(g[:, HEAD_DIM:HEAD_DIM + 1], (GDN_CHUNK, PAIR))], axis=0)
        g_row = jnp.concatenate([gt_ref[0, 2 * p:2 * p + 1, rows(ci)], gt_ref[0, 2 * p + 1:2 * p + 2, rows(ci)]],
                                axis=1)
        return jnp.where(causal, jnp.exp(jnp.where(causal, g_col - g_row, 0.0)), 0.0)

    decay = [decay_of(g, ci, p) for g, (ci, p) in zip(gc, units)]
    k2b = [x.astype(MXU_DTYPE) for x in k2]
    low = [jnp.where(strict, _dot_nt(stack(a), b) * d, 0.0) for a, b, d in zip(kb, k2b, decay)]
    a_in = [(_dot_nt(stack(a), b) * d).astype(MXU_DTYPE) for a, b, d in zip(qn, k2b, decay)]
    t_inv = _unit_lower_inverse_many(low)
    uw = [_dot(ti, jnp.concatenate([stack(v * b), stack(a * e)], axis=1)).astype(MXU_DTYPE)
          for ti, v, b, a, e in zip(t_inv, vv, bt, kb, eg)]
    ket = [stack(k * jnp.exp(gl - g)).T for k, gl, g in zip(kn, g_last, gc)]
    nm = [_dot(a, b) for a, b in zip(ket, uw)]
    raw = [jnp.dot(a, b, preferred_element_type=F32) for a, b in zip(a_in, uw)]
    p_mat = [stack(q * e) - x[:, PAIR:] for q, e, x in zip(qn, eg, raw)]

    state = [s_ref[p] for p in range(N_PAIRS)]
    for i, (ci, p) in enumerate(units):
        s = state[p]
        o2 = _dot(p_mat[i], s) + raw[i][:, :PAIR]
        state[p] = s * jnp.exp(g_last[i]) - _dot(nm[i][:, PAIR:], s) + nm[i][:, :PAIR]
        o = o2[0:GDN_CHUNK] + o2[GDN_CHUNK:n2]
        ms = _dot_ones(o * o, ones) * (1.0 / HEAD_DIM)
        on = o * lax.rsqrt(ms + NORM_EPS) * nw_ref[...]
        o_ref[0, rows(ci), pair(p, 0)] = (on * _silu(gate_ref[0, rows(ci), pair(p, 0)])
                                          * ms_ref[:, pair(p, 0)]).astype(o_ref.dtype)
    for p in range(N_PAIRS):
        s_ref[p] = state[p]


def _gdn(qkv, g, beta, g_t, gate, norm_w2, mscale, blk):
    b, t, _ = qkv.shape
    tile = lambda width: pl.BlockSpec((1, blk, width), lambda i, j: (i, j, 0))
    const = lambda shape: pl.BlockSpec(shape, lambda i, j: (0, 0))
    return pl.pallas_call(
        _gdn_kernel,
        grid=(b, t // blk),
        in_specs=[tile(3 * GDN_WIDTH),
                  tile(GDN_WIDTH), tile(GDN_WIDTH), pl.BlockSpec((1, SUBLANES, blk), lambda i, j: (i, 0, j)),
                  tile(GDN_WIDTH), const((1, PAIR)), const((1, GDN_WIDTH))],
        out_specs=tile(GDN_WIDTH),
        out_shape=jax.ShapeDtypeStruct((b, t, GDN_WIDTH), MXU_DTYPE),
        scratch_shapes=[pltpu.VMEM((N_PAIRS, PAIR, PAIR), F32)],
        compiler_params=_cparams("parallel", "arbitrary"),
        name="gdn",
    )(qkv, g, beta, g_t, gate, norm_w2, mscale)


FOX_ROW_GROUP = 32


FOX_SKIP_MARGIN = 106.0
KEY_NORM_SLACK = 1.01


def _fox_kernel(q_ref, k_ref, v_ref, aux_ref, ms_ref, o_ref, s0_ref, s1_ref, p0_ref, p1_ref, a0_ref, a1_ref,
                m_ref, acc_ref, *, tq):
    i = pl.program_id(2)
    heads = range(2)
    rg = FOX_ROW_GROUP
    m_ref[...] = jnp.full(m_ref.shape, -jnp.inf, F32)
    acc_ref[...] = jnp.zeros(acc_ref.shape, F32)

    def scores(j, s_ref):
        start = pl.multiple_of(j * tq, tq)
        for h in heads:
            s_ref[h] = lax.dot_general(q_ref[0, h], k_ref[0, h, pl.ds(start, tq), :],
                                       (((1,), (1,)), ((), ())), preferred_element_type=F32)

    def softmax(s_ref, p_ref, a_ref, masked):
        for h in heads:
            for g in range(tq // rg):
                rows = slice(g * rg, (g + 1) * rg)
                s = s_ref[h, rows, :]
                if masked:
                    row_id = g * rg + lax.broadcasted_iota(jnp.int32, (rg, tq), 0)
                    col_id = lax.broadcasted_iota(jnp.int32, (rg, tq), 1)
                    s = jnp.where(col_id <= row_id, s, -jnp.inf)
                m_old = m_ref[h, rows, :]
                m_new = jnp.maximum(m_old, jnp.max(s, axis=-1, keepdims=True))
                a_ref[h, rows, :] = jnp.exp(m_old - m_new)
                m_ref[h, rows, :] = m_new
                m_wide = jnp.concatenate([m_new] * (tq // LANES), axis=1)
                p_ref[h, rows, :] = jnp.exp(s - m_wide).astype(p_ref.dtype)

    def weighted_values(j, p_ref, a_ref):
        start = pl.multiple_of(j * tq, tq)
        for h in heads:
            acc_ref[h] = a_ref[h] * acc_ref[h] + jnp.dot(p_ref[h], v_ref[0, h, pl.ds(start, tq), :],
                                                         preferred_element_type=F32)

    scores(i, s0_ref)
    softmax(s0_ref, p0_ref, a0_ref, True)
    scores(jnp.maximum(i - 1, 0), s1_ref)
    weighted_values(i, p0_ref, a0_ref)

    lane = lax.broadcasted_iota(jnp.int32, (1, LANES), 1)
    n = jnp.int32(0)
    for h in heads:
        q = q_ref[0, h].astype(F32)
        q_norm = jnp.sqrt(jnp.sum(jnp.where(lane < HEAD_DIM, q * q, 0.0), axis=-1, keepdims=True))
        key_norm = aux_ref[0, h, 1:2, :]
        block_end_c = aux_ref[0, h, 0:1, :]
        slack = jnp.max(q_norm * key_norm - m_ref[h], axis=0, keepdims=True)
        keep = (lane < i) & (slack - block_end_c >= -FOX_SKIP_MARGIN)
        n = jnp.maximum(n, jnp.sum(keep.astype(jnp.int32)))

    def body(t, carry):
        j = i - 1 - 2 * t
        softmax(s1_ref, p1_ref, a1_ref, False)
        scores(jnp.maximum(j - 1, 0), s0_ref)
        weighted_values(j, p1_ref, a1_ref)
        softmax(s0_ref, p0_ref, a0_ref, False)
        scores(jnp.maximum(j - 2, 0), s1_ref)
        weighted_values(j - 1, p0_ref, a0_ref)
        return carry

    lax.fori_loop(0, n // 2, body, 0)

    @pl.when(n % 2 == 1)
    def _():
        softmax(s1_ref, p1_ref, a1_ref, False)
        weighted_values(i - n, p1_ref, a1_ref)

    o = [acc_ref[h] / acc_ref[h][:, AUG_BIAS_LANE:AUG_BIAS_LANE + 1] for h in heads]
    lane = lax.broadcasted_iota(jnp.int32, (1, PAIR), 1)
    o_pair = jnp.where(lane < HEAD_DIM, o[0], pltpu.roll(o[1], HEAD_DIM, axis=1))
    o_ref[0] = (o_pair * ms_ref[...]).astype(o_ref.dtype)


def _fox_aux(stats):
    b, nt = stats.shape[:2]
    k_sq = jnp.max(stats[:, :, :N_PAIRS, :], axis=1)
    k_sq = k_sq.reshape(b, N_PAIRS, 2, HEAD_DIM)[..., 0].reshape(b, FOX_HEADS)
    key_norm = jnp.sqrt(k_sq) * KEY_NORM_SLACK
    block_end_c = jnp.transpose(stats[:, :, N_PAIRS, 2 * GDN_HEADS:2 * GDN_HEADS + FOX_HEADS], (0, 2, 1))
    aux = jnp.zeros((b, FOX_HEADS, SUBLANES, LANES), F32)
    aux = aux.at[:, :, 0, :nt].set(block_end_c)
    return aux.at[:, :, 1, :].set(jnp.broadcast_to(key_norm[:, :, None], (b, FOX_HEADS, LANES)))


def _fox(qa, ka, va, aux, mscale, tq):
    b, _, t, _ = qa.shape
    kern = functools.partial(_fox_kernel, tq=tq)
    return pl.pallas_call(
        kern,
        grid=(b, N_PAIRS, t // tq),
        in_specs=[pl.BlockSpec((1, 2, tq, LANES), lambda bi, p, i: (bi, p, i, 0)),
                  pl.BlockSpec((1, 2, t, LANES), lambda bi, p, i: (bi, p, 0, 0)),
                  pl.BlockSpec((1, 2, t, LANES), lambda bi, p, i: (bi, p, 0, 0)),
                  pl.BlockSpec((1, 2, SUBLANES, LANES), lambda bi, p, i: (bi, p, 0, 0)),
                  pl.BlockSpec((1, PAIR), lambda bi, p, i: (0, p))],
        out_specs=pl.BlockSpec((1, tq, PAIR), lambda bi, p, i: (bi, i, p)),
        out_shape=jax.ShapeDtypeStruct((b, t, FOX_WIDTH), MXU_DTYPE),
        scratch_shapes=[pltpu.VMEM((2, tq, tq), F32), pltpu.VMEM((2, tq, tq), F32),
                        pltpu.VMEM((2, tq, tq), jnp.bfloat16), pltpu.VMEM((2, tq, tq), jnp.bfloat16),
                        pltpu.VMEM((2, tq, LANES), F32), pltpu.VMEM((2, tq, LANES), F32),
                        pltpu.VMEM((2, tq, LANES), F32), pltpu.VMEM((2, tq, LANES), F32)],
        compiler_params=_cparams("parallel", "parallel", "arbitrary"),
        name="fox_attention",
    )(qa, ka, va, aux, mscale)


def _out_proj_kernel(oa_ref, ob_ref, oc_ref, x_ref, w_ref, g_ref, b_ref, o_ref):
    mix = jnp.dot(oa_ref[...], w_ref[0:GDN_WIDTH, :], preferred_element_type=F32)
    mix = mix + jnp.dot(ob_ref[...], w_ref[GDN_WIDTH:GDN_WIDTH + CONV_WIDTH, :], preferred_element_type=F32)
    mix = mix + jnp.dot(oc_ref[...], w_ref[GDN_WIDTH + CONV_WIDTH:D_MODEL, :], preferred_element_type=F32)
    o_ref[...] = _layer_norm_rows(DEEPNORM_ALPHA * x_ref[...] + mix, g_ref[...], b_ref[...])


def _out_proj(oa, ob, oc, xf, w, g, bvec, tm):
    n = xf.shape[0]
    row = lambda i: (i, 0)
    const = lambda i: (0, 0)
    return pl.pallas_call(
        _out_proj_kernel,
        grid=(n // tm,),
        in_specs=[pl.BlockSpec((tm, GDN_WIDTH), row), pl.BlockSpec((tm, CONV_WIDTH), row),
                  pl.BlockSpec((tm, FOX_WIDTH), row), pl.BlockSpec((tm, D_MODEL), row),
                  pl.BlockSpec((D_MODEL, D_MODEL), const),
                  pl.BlockSpec((1, D_MODEL), const), pl.BlockSpec((1, D_MODEL), const)],
        out_specs=pl.BlockSpec((tm, D_MODEL), row),
        out_shape=jax.ShapeDtypeStruct((n, D_MODEL), F32),
        compiler_params=_cparams("parallel"),
        name="out_proj_ln",
    )(oa, ob, oc, xf, w, g, bvec)


FFN_CHUNK = 256


def _swiglu_chunks(xb, w1, w3, w2, width):
    assert width % FFN_CHUNK == 0
    acc = None
    for c in range(width // FFN_CHUNK):
        cols = slice(c * FFN_CHUNK, (c + 1) * FFN_CHUNK)
        h = _silu(jnp.dot(xb, w1(slice(None), cols), preferred_element_type=F32)) * jnp.dot(
            xb, w3(slice(None), cols), preferred_element_type=F32)
        part = jnp.dot(h.astype(xb.dtype), w2(cols, slice(None)), preferred_element_type=F32)
        acc = part if acc is None else acc + part
    return acc


def _ffn_kernel(x_ref, w1_ref, w3_ref, w2_ref, g_ref, b_ref, o_ref):
    x = x_ref[...]
    ff = _swiglu_chunks(x.astype(w1_ref.dtype), lambda r, c: w1_ref[r, c], lambda r, c: w3_ref[r, c],
                        lambda r, c: w2_ref[r, c], w1_ref.shape[1])
    o_ref[...] = _layer_norm_rows(DEEPNORM_ALPHA * x + ff, g_ref[...], b_ref[...])


def _dense_ffn(xf, w1, w3, w2, g, bvec, tm):
    n = xf.shape[0]
    ff = w1.shape[1]
    const = lambda shape: pl.BlockSpec(shape, lambda i: (0, 0), pipeline_mode=pl.Buffered(1))
    return pl.pallas_call(
        _ffn_kernel,
        grid=(n // tm,),
        in_specs=[pl.BlockSpec((tm, D_MODEL), lambda i: (i, 0)),
                  const((D_MODEL, ff)), const((D_MODEL, ff)), const((ff, D_MODEL)),
                  const((1, D_MODEL)), const((1, D_MODEL))],
        out_specs=pl.BlockSpec((tm, D_MODEL), lambda i: (i, 0)),
        out_shape=jax.ShapeDtypeStruct((n, D_MODEL), F32),
        compiler_params=_cparams("parallel"),
        name="dense_ffn_ln",
    )(xf, w1, w3, w2, g, bvec)


def _router_kernel(x_ref, wr_ref, info_ref, cnt_ref, run_ref):
    i = pl.program_id(0)
    tm = x_ref.shape[0]

    @pl.when(i == 0)
    def _():
        run_ref[...] = jnp.zeros_like(run_ref)

    logits = jnp.dot(x_ref[...].astype(MXU_DTYPE), wr_ref[...], preferred_element_type=F32)
    lane = lax.broadcasted_iota(jnp.int32, (tm, LANES), 1)
    logits = jnp.where(lane < N_EXPERTS, logits, -jnp.inf)
    m1 = jnp.max(logits, axis=-1, keepdims=True)
    e1 = jnp.min(jnp.where(logits == m1, lane, LANES), axis=-1, keepdims=True)
    rest = jnp.where(lane == e1, -jnp.inf, logits)
    m2 = jnp.max(rest, axis=-1, keepdims=True)
    e2 = jnp.min(jnp.where(rest == m2, lane, LANES), axis=-1, keepdims=True)
    z = jnp.exp(m2 - m1)
    g1 = 1.0 / (1.0 + z)
    g2 = z / (1.0 + z)
    onehot = ((lane == e1) | (lane == e2)).astype(F32)
    r = lax.broadcasted_iota(jnp.int32, (tm, tm), 0)
    c = lax.broadcasted_iota(jnp.int32, (tm, tm), 1)
    before = jnp.dot((c < r).astype(jnp.bfloat16), onehot.astype(jnp.bfloat16),
                     preferred_element_type=F32) + run_ref[...]
    rank1 = jnp.sum(jnp.where(lane == e1, before, 0.0), axis=-1, keepdims=True)
    rank2 = jnp.sum(jnp.where(lane == e2, before, 0.0), axis=-1, keepdims=True)
    run_ref[...] = run_ref[...] + jnp.sum(onehot, axis=0, keepdims=True)
    cnt_ref[...] = run_ref[...]
    info = jnp.where(lane == 0, e1.astype(F32),
                     jnp.where(lane == 1, e2.astype(F32),
                               jnp.where(lane == 2, rank1,
                                         jnp.where(lane == 3, rank2,
                                                   jnp.where(lane == 4, g1, jnp.where(lane == 5, g2, 0.0))))))
    info_ref[...] = info


def _router(xf, wr_pad, tm):
    n = xf.shape[0]
    return pl.pallas_call(
        _router_kernel,
        grid=(n // tm,),
        in_specs=[pl.BlockSpec((tm, D_MODEL), lambda i: (i, 0)),
                  pl.BlockSpec((D_MODEL, LANES), lambda i: (0, 0))],
        out_specs=[pl.BlockSpec((tm, LANES), lambda i: (i, 0)),
                   pl.BlockSpec((1, LANES), lambda i: (0, 0))],
        out_shape=[jax.ShapeDtypeStruct((n, LANES), F32), jax.ShapeDtypeStruct((1, LANES), F32)],
        scratch_shapes=[pltpu.VMEM((1, LANES), F32)],
        compiler_params=_cparams("arbitrary"),
        name="moe_router",
    )(xf, wr_pad)


DMA_LOOP_UNROLL = 8


def _dispatch_kernel(d1_ref, d2_ref, x_ref, zeros_ref, xs_ref, sem):
    del zeros_ref
    tm = x_ref.shape[0]

    def row_copy(r, dst):
        return pltpu.make_async_copy(x_ref.at[pl.ds(r, 1), :], xs_ref.at[pl.ds(dst, 1), :], sem)

    def issue(r, carry):
        row_copy(r, d1_ref[0, 0, r]).start()
        row_copy(r, d2_ref[0, 0, r]).start()
        return carry

    lax.fori_loop(0, tm, issue, 0, unroll=DMA_LOOP_UNROLL)

    def drain(r, carry):
        row_copy(r, d1_ref[0, 0, r]).wait()
        row_copy(r, d2_ref[0, 0, r]).wait()
        return carry

    lax.fori_loop(0, tm, drain, 0, unroll=DMA_LOOP_UNROLL)


def _dispatch(xf, d1, d2, rows, tm):
    n = xf.shape[0]
    idx = lambda a: a.reshape(n // tm, 1, tm)
    zeros = jnp.zeros((rows, D_MODEL), xf.dtype)
    smem = lambda: pl.BlockSpec((1, 1, tm), lambda i: (i, 0, 0), memory_space=pltpu.SMEM)
    return pl.pallas_call(
        _dispatch_kernel,
        grid=(n // tm,),
        in_specs=[smem(), smem(),
                  pl.BlockSpec((tm, D_MODEL), lambda i: (i, 0)),
                  pl.BlockSpec(memory_space=pl.ANY)],
        out_specs=pl.BlockSpec(memory_space=pl.ANY),
        out_shape=jax.ShapeDtypeStruct((rows, D_MODEL), xf.dtype),
        scratch_shapes=[pltpu.SemaphoreType.DMA(())],
        input_output_aliases={3: 0},
        compiler_params=_cparams("arbitrary"),
        name="moe_dispatch",
    )(idx(d1), idx(d2), xf, zeros)


def _expert_kernel(te_ref, nu_ref, xs_ref, w1_ref, w3_ref, w2_ref, ys_ref, xb_ref, acc_ref):
    del te_ref
    i = pl.program_id(0)
    f = pl.program_id(1)
    used = i < nu_ref[0]

    @pl.when(f == 0)
    def _():
        xb_ref[...] = xs_ref[...].astype(xb_ref.dtype)
        acc_ref[...] = jnp.zeros_like(acc_ref)

    @pl.when(used)
    def _():
        acc_ref[...] += _swiglu_chunks(xb_ref[...], lambda r, c: w1_ref[0, r, c], lambda r, c: w3_ref[0, r, c],
                                       lambda r, c: w2_ref[0, r, c], w1_ref.shape[2])

    @pl.when(f == pl.num_programs(1) - 1)
    def _():
        ys_ref[...] = acc_ref[...]


def _experts(xs, tile_expert, n_used, w1, w3, w2, tm, tf):
    rows = xs.shape[0]
    ff = w1.shape[2]
    n_tiles = rows // tm

    def x_map(i, f, te, nu):
        return (jnp.maximum(jnp.minimum(i, nu[0] - 1), 0), 0)

    def w13_map(i, f, te, nu):
        return (te[i], 0, jnp.where(i < nu[0], f, ff // tf - 1))

    def w2_map(i, f, te, nu):
        return (te[i], jnp.where(i < nu[0], f, ff // tf - 1), 0)

    grid_spec = pltpu.PrefetchScalarGridSpec(
        num_scalar_prefetch=2,
        grid=(n_tiles, ff // tf),
        in_specs=[pl.BlockSpec((tm, D_MODEL), x_map),
                  pl.BlockSpec((1, D_MODEL, tf), w13_map),
                  pl.BlockSpec((1, D_MODEL, tf), w13_map),
                  pl.BlockSpec((1, tf, D_MODEL), w2_map)],
        out_specs=pl.BlockSpec((tm, D_MODEL), lambda i, f, te, nu: (i, 0)),
        scratch_shapes=[pltpu.VMEM((tm, D_MODEL), MXU_DTYPE), pltpu.VMEM((tm, D_MODEL), F32)],
    )
    return pl.pallas_call(
        _expert_kernel,
        grid_spec=grid_spec,
        out_shape=jax.ShapeDtypeStruct((rows, D_MODEL), F32),
        compiler_params=_cparams("arbitrary", "arbitrary"),
        name="moe_experts",
    )(tile_expert, n_used, xs, w1, w3, w2)


def _combine_kernel(d1_ref, d2_ref, x_ref, info_ref, ys_ref, g_ref, b_ref, o_ref, y1_ref, y2_ref, sem):
    tm = x_ref.shape[0]

    def row_copy(src, r, buf):
        return pltpu.make_async_copy(ys_ref.at[pl.ds(src, 1), :], buf.at[pl.ds(r, 1), :], sem)

    def issue(r, carry):
        row_copy(d1_ref[0, 0, r], r, y1_ref).start()
        row_copy(d2_ref[0, 0, r], r, y2_ref).start()
        return carry

    lax.fori_loop(0, tm, issue, 0, unroll=DMA_LOOP_UNROLL)

    def drain(r, carry):
        row_copy(d1_ref[0, 0, r], r, y1_ref).wait()
        row_copy(d2_ref[0, 0, r], r, y2_ref).wait()
        return carry

    lax.fori_loop(0, tm, drain, 0, unroll=DMA_LOOP_UNROLL)
    info = info_ref[...]
    ff = info[:, 4:5] * y1_ref[...] + info[:, 5:6] * y2_ref[...]
    o_ref[...] = _layer_norm_rows(DEEPNORM_ALPHA * x_ref[...] + ff, g_ref[...], b_ref[...])


def _combine(xf, info, ys, d1, d2, g, bvec, tm):
    n = xf.shape[0]
    idx = lambda a: a.reshape(n // tm, 1, tm)
    smem = lambda: pl.BlockSpec((1, 1, tm), lambda i: (i, 0, 0), memory_space=pltpu.SMEM)
    return pl.pallas_call(
        _combine_kernel,
        grid=(n // tm,),
        in_specs=[smem(), smem(),
                  pl.BlockSpec((tm, D_MODEL), lambda i: (i, 0)),
                  pl.BlockSpec((tm, LANES), lambda i: (i, 0)),
                  pl.BlockSpec(memory_space=pl.ANY),
                  pl.BlockSpec((1, D_MODEL), lambda i: (0, 0)),
                  pl.BlockSpec((1, D_MODEL), lambda i: (0, 0))],
        out_specs=pl.BlockSpec((tm, D_MODEL), lambda i: (i, 0)),
        out_shape=jax.ShapeDtypeStruct((n, D_MODEL), F32),
        scratch_shapes=[pltpu.VMEM((tm, D_MODEL), F32), pltpu.VMEM((tm, D_MODEL), F32),
                        pltpu.SemaphoreType.DMA(())],
        compiler_params=_cparams("arbitrary"),
        name="moe_combine_ln",
    )(idx(d1), idx(d2), xf, info, ys, g, bvec)


MOE_TILE = 512


def _moe_ffn(xf, w_router, w1, w3, w2, g, bvec, tm_tok, tf):
    n = xf.shape[0]
    wr_pad = jnp.zeros((D_MODEL, LANES), MXU_DTYPE).at[:, :N_EXPERTS].set(w_router.astype(MXU_DTYPE))
    info, counts = _router(xf, wr_pad, tm_tok)
    sizes = counts[0, :N_EXPERTS].astype(jnp.int32)
    tiles_per = (sizes + MOE_TILE - 1) // MOE_TILE
    tile_end = jnp.cumsum(tiles_per)
    seg_start = (tile_end - tiles_per) * MOE_TILE
    e1 = info[:, 0].astype(jnp.int32)
    e2 = info[:, 1].astype(jnp.int32)
    d1 = seg_start[e1] + info[:, 2].astype(jnp.int32)
    d2 = seg_start[e2] + info[:, 3].astype(jnp.int32)
    n_tiles = (n * TOP_K) // MOE_TILE + N_EXPERTS
    rows = n_tiles * MOE_TILE
    tile_expert = jnp.minimum(jnp.searchsorted(tile_end, jnp.arange(n_tiles, dtype=jnp.int32), side='right'),
                              N_EXPERTS - 1).astype(jnp.int32)
    n_used = tile_end[N_EXPERTS - 1:].astype(jnp.int32)
    xs = _dispatch(xf, d1, d2, rows, tm_tok)
    ys = _experts(xs, tile_expert, n_used, w1, w3, w2, MOE_TILE, tf)
    return _combine(xf, info, ys, d1, d2, g, bvec, tm_tok)


def _pack_in_proj(w_in):
    cuts = np.cumsum([0, GDN_WIDTH, GDN_WIDTH, GDN_WIDTH, GDN_HEADS, GDN_HEADS, GDN_WIDTH,
                      2 * CONV_WIDTH, FOX_WIDTH, FOX_WIDTH, FOX_WIDTH, FOX_HEADS])
    seg = lambda i: w_in[:, cuts[i]:cuts[i + 1]]
    w_main = jnp.concatenate([seg(0), seg(1), seg(2), seg(5), seg(6), seg(7), seg(8), seg(9)], axis=1)
    pad = jnp.zeros((D_MODEL, SMALL_COLS - 2 * GDN_HEADS - FOX_HEADS), w_in.dtype)
    w_small = jnp.concatenate([seg(3), seg(4), seg(10), pad], axis=1).astype(F32)
    ws_hi = w_small.astype(MXU_DTYPE)
    ws_lo = (w_small - ws_hi.astype(F32)).astype(MXU_DTYPE)
    return w_main.astype(MXU_DTYPE), jnp.concatenate([ws_hi, ws_lo], axis=1)


def _lane_row(vals, offset):
    return jnp.zeros((1, LANES), F32).at[0, offset:offset + vals.shape[0]].set(vals.astype(F32))


def _mixer(xf, b, t, w_in, mix_scale, w_out, gdn_conv_w, gdn_a_log, gdn_dt_bias, gdn_norm_w,
           cnv_dw_w, cnv_dw_b, cnv_ln_g, cnv_ln_b, fox_f_bias, ln_g, ln_b, tiles):
    w_main, w_small = _pack_in_proj(w_in)
    ms = mix_scale.reshape(1, D_MODEL).astype(F32)
    row = lambda v: v.reshape(1, -1).astype(F32)
    qkv, gate, o_b, fq, fk, fv, small = _in_proj(
        xf, w_main, w_small, gdn_conv_w, cnv_dw_w, row(cnv_dw_b), row(cnv_ln_g), row(cnv_ln_b),
        ms[:, GDN_WIDTH:GDN_WIDTH + CONV_WIDTH], tiles["tm"], t // tiles["tm"])
    r3 = lambda a: a.reshape(b, t, a.shape[-1])
    assert tiles["tt"] == tiles["tq"]
    g_cum, beta, g_t, stats, qa, ka, va = _gates(
        r3(small), _lane_row(gdn_a_log, 0), _lane_row(gdn_dt_bias, 0), _lane_row(fox_f_bias, 2 * GDN_HEADS),
        r3(fq), r3(fk), r3(fv), tiles["tt"])
    norm_w2 = jnp.tile(gdn_norm_w.astype(F32), 2).reshape(1, PAIR)
    o_a = _gdn(r3(qkv), g_cum, beta, g_t, r3(gate), norm_w2, ms[:, :GDN_WIDTH], tiles["gdn_blk"])
    o_c = _fox(qa, ka, va, _fox_aux(stats), ms[:, GDN_WIDTH + CONV_WIDTH:], tiles["tq"])
    flat = lambda a: a.reshape(b * t, a.shape[-1])
    return _out_proj(flat(o_a), o_b, flat(o_c), xf, w_out.astype(MXU_DTYPE), row(ln_g), row(ln_b),
                     tiles["tm"])


def _tiles(t):
    return dict(tm=min(512, t), tt=min(512, t), gdn_blk=min(256, t), tq=min(512, t),
                tf_moe=1792)


def kernel(x, w_in, mix_scale, w_out, gdn_conv_w, gdn_a_log, gdn_dt_bias, gdn_norm_w, cnv_dw_w, cnv_dw_b,
           cnv_ln_g, cnv_ln_b, fox_f_bias, ln_mix_g, ln_mix_b, ln_ffn_g, ln_ffn_b, ffn_w1, ffn_w3, ffn_w2,
           moe_router, moe_w1, moe_w3, moe_w2):
    b, t, d = x.shape
    tiles = _tiles(t)
    row = lambda v: v.reshape(1, -1).astype(F32)
    xf = x.reshape(b * t, d)
    for l in range(DEPTH):
        xf = _mixer(xf, b, t, w_in[l], mix_scale[l], w_out[l], gdn_conv_w[l], gdn_a_log[l], gdn_dt_bias[l],
                    gdn_norm_w[l], cnv_dw_w[l], cnv_dw_b[l], cnv_ln_g[l], cnv_ln_b[l], fox_f_bias[l],
                    ln_mix_g[l], ln_mix_b[l], tiles)
        if l % 2 == 0:
            e = l // 2
            xf = _dense_ffn(xf, ffn_w1[e].astype(MXU_DTYPE), ffn_w3[e].astype(MXU_DTYPE),
                            ffn_w2[e].astype(MXU_DTYPE), row(ln_ffn_g[l]), row(ln_ffn_b[l]),
                            tiles["tm"])
        else:
            e = l // 2
            xf = _moe_ffn(xf, moe_router[e], moe_w1[e].astype(MXU_DTYPE), moe_w3[e].astype(MXU_DTYPE),
                          moe_w2[e].astype(MXU_DTYPE), row(ln_ffn_g[l]), row(ln_ffn_b[l]),
                          tiles["tm"], tiles["tf_moe"])
    return xf.reshape(b, t, d)
```

```python
import functools

import jax
import jax.numpy as jnp
import numpy as np
from jax import lax
from jax.experimental import pallas as pl
from jax.experimental.pallas import tpu as pltpu

D_MODEL = 1024
DEPTH = 2
HEAD_DIM = 64
GDN_WIDTH = 384
CONV_WIDTH = 256
FOX_WIDTH = 384
GDN_HEADS = 6
FOX_HEADS = 6
GDN_SHORT_CONV = 4
GDN_CHUNK = 64
CONV_KERNEL = 31
FFN_DENSE = 2816
N_EXPERTS = 8
TOP_K = 2
FFN_EXPERT = 3584
DEEPNORM_ALPHA = (2 * DEPTH) ** 0.25
LN_EPS = 1e-5
NORM_EPS = 1e-6

LANES = 128
SUBLANES = 8
PAIR = 2 * HEAD_DIM
N_PAIRS = GDN_WIDTH // PAIR
SMALL_COLS = LANES

_C_QKV = 0
_C_GATE = _C_QKV + 3 * GDN_WIDTH
_C_GLU = _C_GATE + GDN_WIDTH
_C_FQ = _C_GLU + 2 * CONV_WIDTH
_C_FK = _C_FQ + FOX_WIDTH
_C_FV = _C_FK + FOX_WIDTH
_C_END = _C_FV + FOX_WIDTH

MXU_DTYPE = jnp.bfloat16
F32 = jnp.float32

VMEM_LIMIT = 56 * 1024 * 1024


def _cparams(*sem):
    return pltpu.CompilerParams(dimension_semantics=sem, vmem_limit_bytes=VMEM_LIMIT)


def _dot(a, b):
    return jnp.dot(a.astype(MXU_DTYPE), b.astype(MXU_DTYPE), preferred_element_type=F32)


def _dot_nt(a, b):
    return lax.dot_general(a.astype(MXU_DTYPE), b.astype(MXU_DTYPE),
                           (((1,), (1,)), ((), ())), preferred_element_type=F32)


def _split(a):
    hi = a.astype(jnp.bfloat16)
    return hi, (a - hi.astype(F32)).astype(jnp.bfloat16)


def _split3(a):
    hi = a.astype(jnp.bfloat16)
    rest = a - hi.astype(F32)
    mid = rest.astype(jnp.bfloat16)
    lo = (rest - mid.astype(F32)).astype(jnp.bfloat16)
    return hi, mid, lo


def _dot_ones(a, ones_b16):
    ah, al = _split(a)
    d = lambda x: jnp.dot(x, ones_b16, preferred_element_type=F32)
    return d(ah) + d(al)


def _sel_rhs(a, sel_b16):
    return sum(jnp.dot(t, sel_b16, preferred_element_type=F32) for t in _split3(a))


def _sel_lhs(sel_b16, b):
    return sum(jnp.dot(sel_b16, t, preferred_element_type=F32) for t in _split3(b))


def _sigmoid(x):
    return 1.0 / (1.0 + jnp.exp(-x))


def _silu(x):
    return x * _sigmoid(x)


def _softplus(x):
    return jnp.maximum(x, 0.0) + jnp.log1p(jnp.exp(-jnp.abs(x)))


def _log_sigmoid(x):
    return -_softplus(-x)


def _layer_norm_rows(y, g, b):
    mu = jnp.mean(y, axis=-1, keepdims=True)
    d = y - mu
    var = jnp.mean(d * d, axis=-1, keepdims=True)
    return d * lax.rsqrt(var + LN_EPS) * g + b


def _head_ones():
    r = lax.broadcasted_iota(jnp.int32, (PAIR, PAIR), 0) // HEAD_DIM
    c = lax.broadcasted_iota(jnp.int32, (PAIR, PAIR), 1) // HEAD_DIM
    return (r == c).astype(F32)


CONV_HALO = 32
CONV_ROW_BLOCK = 128
SHORT_CONV_ROW_BLOCK = 128


def _in_proj_kernel(x_ref, w_ref, ws_ref, gw_ref, cw_ref, cb_ref, lg_ref, lb_ref, ms_ref,
                    qkv_ref, gate_ref, ob_ref, fq_ref, fk_ref, fv_ref, small_ref,
                    qbuf_ref, cbuf_ref, sh_ref, *, tiles_per_seq):
    first = pl.program_id(0) % tiles_per_seq == 0
    tm = x_ref.shape[0]
    x = x_ref[...]
    xb = x.astype(MXU_DTYPE)

    def mm(lo, hi):
        return jnp.dot(xb, w_ref[:, lo:hi], preferred_element_type=F32)

    @pl.when(first)
    def _():
        cbuf_ref[0:CONV_HALO, :] = jnp.zeros((CONV_HALO, CONV_WIDTH), F32)
        qbuf_ref[0:SUBLANES, :] = jnp.zeros((SUBLANES, 3 * GDN_WIDTH), F32)

    @pl.when(jnp.logical_not(first))
    def _():
        cbuf_ref[0:CONV_HALO, :] = cbuf_ref[tm:tm + CONV_HALO, :]
        qbuf_ref[0:SUBLANES, :] = qbuf_ref[tm:tm + SUBLANES, :]

    glu = mm(_C_GLU, _C_FQ)
    cbuf_ref[CONV_HALO:CONV_HALO + tm, :] = glu[:, 0:CONV_WIDTH] * _sigmoid(glu[:, CONV_WIDTH:2 * CONV_WIDTH])
    span = tm + CONV_HALO - SUBLANES
    for s in range(1, SUBLANES):
        sh_ref[s - 1] = cbuf_ref[s:s + span, :]
    for r0 in range(0, tm, CONV_ROW_BLOCK):
        acc = jnp.zeros((CONV_ROW_BLOCK, CONV_WIDTH), F32) + cb_ref[...]
        for j in range(CONV_KERNEL):
            lo = r0 + CONV_HALO - (CONV_KERNEL - 1) + j
            base, phase = lo - lo % SUBLANES, lo % SUBLANES
            tap = (cbuf_ref[base:base + CONV_ROW_BLOCK, :] if phase == 0
                   else sh_ref[phase - 1, base:base + CONV_ROW_BLOCK, :])
            acc = acc + cw_ref[j:j + 1, :] * tap
        y = _silu(_layer_norm_rows(acc, lg_ref[...], lb_ref[...])) * ms_ref[...]
        ob_ref[r0:r0 + CONV_ROW_BLOCK, :] = y.astype(ob_ref.dtype)

    qbuf_ref[SUBLANES:SUBLANES + tm, :] = mm(_C_QKV, _C_GATE)
    for r0 in range(0, tm, SHORT_CONV_ROW_BLOCK):
        acc = jnp.zeros((SHORT_CONV_ROW_BLOCK, 3 * GDN_WIDTH), F32)
        for j in range(GDN_SHORT_CONV):
            lo = r0 + SUBLANES - (GDN_SHORT_CONV - 1) + j
            acc = acc + gw_ref[j:j + 1, :] * qbuf_ref[lo:lo + SHORT_CONV_ROW_BLOCK, :]
        qkv_ref[r0:r0 + SHORT_CONV_ROW_BLOCK, :] = _silu(acc)

    gate_ref[...] = mm(_C_GATE, _C_GLU)
    fq_ref[...] = (mm(_C_FQ, _C_FK) * HEAD_DIM ** -0.5).astype(fq_ref.dtype)
    fk_ref[...] = mm(_C_FK, _C_FV).astype(fk_ref.dtype)
    fv_ref[...] = mm(_C_FV, _C_END).astype(fv_ref.dtype)
    x_lo = (x - xb.astype(F32)).astype(MXU_DTYPE)
    hh_hl = jnp.dot(xb, ws_ref[...], preferred_element_type=F32)
    lh = jnp.dot(x_lo, ws_ref[:, 0:SMALL_COLS], preferred_element_type=F32)
    small_ref[...] = hh_hl[:, 0:SMALL_COLS] + hh_hl[:, SMALL_COLS:2 * SMALL_COLS] + lh


def _in_proj(xf, w_main, w_small, gdn_conv_w, cnv_w, cnv_b, cnv_ln_g, cnv_ln_b, cnv_scale, tm, tiles_per_seq):
    n = xf.shape[0]
    row = lambda i: (i, 0)
    const = lambda shape: pl.BlockSpec(shape, lambda i: (0, 0))
    outs = [
        jax.ShapeDtypeStruct((n, 3 * GDN_WIDTH), F32),
        jax.ShapeDtypeStruct((n, GDN_WIDTH), F32),
        jax.ShapeDtypeStruct((n, CONV_WIDTH), MXU_DTYPE),
        jax.ShapeDtypeStruct((n, FOX_WIDTH), MXU_DTYPE),
        jax.ShapeDtypeStruct((n, FOX_WIDTH), MXU_DTYPE),
        jax.ShapeDtypeStruct((n, FOX_WIDTH), MXU_DTYPE),
        jax.ShapeDtypeStruct((n, SMALL_COLS), F32),
    ]
    vec = const((1, CONV_WIDTH))
    return pl.pallas_call(
        functools.partial(_in_proj_kernel, tiles_per_seq=tiles_per_seq),
        grid=(n // tm,),
        in_specs=[pl.BlockSpec((tm, D_MODEL), row),
                  const((D_MODEL, _C_END)), const((D_MODEL, 2 * SMALL_COLS)),
                  const((GDN_SHORT_CONV, 3 * GDN_WIDTH)), const((CONV_KERNEL, CONV_WIDTH)), vec, vec, vec, vec],
        out_specs=[pl.BlockSpec((tm, o.shape[1]), row) for o in outs],
        out_shape=outs,
        scratch_shapes=[pltpu.VMEM((tm + SUBLANES, 3 * GDN_WIDTH), F32),
                        pltpu.VMEM((tm + CONV_HALO, CONV_WIDTH), F32),
                        pltpu.VMEM((SUBLANES - 1, tm + CONV_HALO - SUBLANES, CONV_WIDTH), F32)],
        compiler_params=_cparams("arbitrary"),
        name="in_proj",
    )(xf, w_main, w_small, gdn_conv_w, cnv_w, cnv_b, cnv_ln_g, cnv_ln_b, cnv_scale)


AUG_BIAS_LANE = HEAD_DIM


def _gates_kernel(small_ref, a_ref, dtb_ref, fb_ref, fq_ref, fk_ref, fv_ref,
                  g_ref, beta_ref, gt_ref, st_ref, qa_ref, ka_ref, va_ref, carry_ref):
    t = pl.program_id(1)
    tt = small_ref.shape[1]
    b16 = jnp.bfloat16

    @pl.when(t == 0)
    def _():
        carry_ref[...] = jnp.zeros_like(carry_ref)

    s = small_ref[0]
    log_decay = -jnp.exp(a_ref[...]) * _softplus(s + dtb_ref[...])
    beta = _sigmoid(s)
    log_f = _log_sigmoid(s + fb_ref[...])

    lane = lax.broadcasted_iota(jnp.int32, (1, LANES), 1)
    rr = lax.broadcasted_iota(jnp.int32, (LANES, LANES), 0)
    cc = lax.broadcasted_iota(jnp.int32, (LANES, LANES), 1)
    tri_chunk = ((cc <= rr) & (rr // GDN_CHUNK == cc // GDN_CHUNK)).astype(b16)
    both = jnp.where(lane < 2 * GDN_HEADS, log_decay, log_f)
    in_chunk = jnp.concatenate([_sel_lhs(tri_chunk, both[r0:r0 + LANES]) for r0 in range(0, tt, LANES)], axis=0)
    g_cum = in_chunk
    carry = carry_ref[...]
    pieces = []
    for r0 in range(0, tt, GDN_CHUNK):
        chunk = in_chunk[r0:r0 + GDN_CHUNK]
        pieces.append(chunk + carry)
        carry = carry + chunk[GDN_CHUNK - 1:GDN_CHUNK]
    c_cum = jnp.concatenate(pieces, axis=0)
    carry_ref[...] = carry

    er = lax.broadcasted_iota(jnp.int32, (LANES, 2 * GDN_WIDTH), 0)
    ec = lax.broadcasted_iota(jnp.int32, (LANES, 2 * GDN_WIDTH), 1) // HEAD_DIM
    expanded = _sel_rhs(jnp.where(lane < GDN_HEADS, g_cum, beta), (er == ec).astype(b16))
    g_ref[0] = expanded[:, 0:GDN_WIDTH]
    beta_ref[0] = expanded[:, GDN_WIDTH:2 * GDN_WIDTH]
    gt_ref[0] = g_cum.T[0:SUBLANES, :]

    head_ones = _head_ones().astype(b16)
    stat_rows = []
    for p in range(N_PAIRS):
        k_pair = fk_ref[0, :, p * PAIR:(p + 1) * PAIR].astype(F32)
        stat_rows.append(jnp.max(_dot_ones(k_pair * k_pair, head_ones), axis=0, keepdims=True))
    stat_rows.append(c_cum[tt - 1:tt, :])
    stat_rows.append(jnp.zeros((SUBLANES - len(stat_rows), LANES), F32))
    st_ref[0, 0] = jnp.concatenate(stat_rows, axis=0)

    q_ones = ((lane >= AUG_BIAS_LANE) & (lane < AUG_BIAS_LANE + 3)).astype(F32)
    v_one = (lane == AUG_BIAS_LANE).astype(F32)
    br = lax.broadcasted_iota(jnp.int32, (3 * LANES, FOX_HEADS * LANES), 0)
    bc = lax.broadcasted_iota(jnp.int32, (3 * LANES, FOX_HEADS * LANES), 1)
    place = ((br % LANES == 2 * GDN_HEADS + bc // LANES) & (bc % LANES == AUG_BIAS_LANE + br // LANES)).astype(b16)
    bias_all = jnp.dot(jnp.concatenate(_split3(-c_cum), axis=1), place, preferred_element_type=F32)
    upper_to_lower = ((cc < HEAD_DIM) & (rr == cc + HEAD_DIM)).astype(b16)
    for h in range(FOX_HEADS):
        p, half = divmod(h, 2)
        lanes = slice(p * PAIR, (p + 1) * PAIR)

        def pick(ref):
            if half == 0:
                return jnp.where(lane < HEAD_DIM, ref[0, :, lanes].astype(F32), 0.0)
            return jnp.dot(ref[0, :, lanes], upper_to_lower, preferred_element_type=F32)

        qa_ref[0, h] = (pick(fq_ref) + q_ones).astype(b16)
        ka_ref[0, h] = (pick(fk_ref) + bias_all[:, h * LANES:(h + 1) * LANES]).astype(b16)
        va_ref[0, h] = (pick(fv_ref) + v_one).astype(b16)


def _gates(small, a_row, dtb_row, fb_row, fq, fk, fv, tt):
    b, t, _ = small.shape
    row = pl.BlockSpec((1, LANES), lambda i, j: (0, 0))
    tile = lambda w: pl.BlockSpec((1, tt, w), lambda i, j: (i, j, 0))
    aug = pl.BlockSpec((1, FOX_HEADS, tt, LANES), lambda i, j: (i, 0, j, 0))
    aug_shape = jax.ShapeDtypeStruct((b, FOX_HEADS, t, LANES), jnp.bfloat16)
    return pl.pallas_call(
        _gates_kernel,
        grid=(b, t // tt),
        in_specs=[tile(SMALL_COLS), row, row, row, tile(FOX_WIDTH), tile(FOX_WIDTH), tile(FOX_WIDTH)],
        out_specs=[tile(GDN_WIDTH), tile(GDN_WIDTH), pl.BlockSpec((1, SUBLANES, tt), lambda i, j: (i, 0, j)),
                   pl.BlockSpec((1, 1, SUBLANES, LANES), lambda i, j: (i, j, 0, 0)), aug, aug, aug],
        out_shape=[jax.ShapeDtypeStruct((b, t, GDN_WIDTH), F32), jax.ShapeDtypeStruct((b, t, GDN_WIDTH), F32),
                   jax.ShapeDtypeStruct((b, SUBLANES, t), F32),
                   jax.ShapeDtypeStruct((b, t // tt, SUBLANES, LANES), F32), aug_shape, aug_shape, aug_shape],
        scratch_shapes=[pltpu.VMEM((1, LANES), F32)],
        compiler_params=_cparams("parallel", "arbitrary"),
        name="gates",
    )(small, a_row, dtb_row, fb_row, fq, fk, fv)


def _unit_lower_inverse_many(lows):
    n = lows[0].shape[0]
    r = lax.broadcasted_iota(jnp.int32, (n, n), 0)
    c = lax.broadcasted_iota(jnp.int32, (n, n), 1)
    eye = (r == c).astype(F32)
    base = GDN_CHUNK // 4
    diag = r // base == c // base
    cast = lambda xs: [x.astype(MXU_DTYPE) for x in xs]
    mm = lambda xs, ys: [jnp.dot(x, y, preferred_element_type=F32) for x, y in zip(xs, ys)]
    add = lambda xs, ys: [x + y for x, y in zip(xs, ys)]
    d = [jnp.where(diag, low, 0.0) for low in lows]
    db = cast(d)
    d2b = cast(mm(db, db))
    p = [eye - x for x in d]
    p = add(p, mm(cast(p), d2b))
    d4b = cast(mm(d2b, d2b))
    p = add(p, mm(cast(p), d4b))
    d8b = cast(mm(d4b, d4b))
    x = add(p, mm(cast(p), d8b))
    for blk in (2 * base, 4 * base):
        sel = (r // blk == c // blk) & (r // (blk // 2) != c // (blk // 2))
        xb = cast(x)
        xo = mm(xb, cast([jnp.where(sel, low, 0.0) for low in lows]))
        x = [a - b for a, b in zip(x, mm(cast(xo), xb))]
    return x


def _gdn_kernel(qkv_ref, g_ref, beta_ref, gt_ref, gate_ref, nw_ref, ms_ref, o_ref, s_ref):
    t = pl.program_id(1)
    blk = qkv_ref.shape[1]
    nc = blk // GDN_CHUNK

    @pl.when(t == 0)
    def _():
        s_ref[...] = jnp.zeros_like(s_ref)

    qkv = qkv_ref[0]

    ones = _head_ones().astype(jnp.bfloat16)
    lane = lax.broadcasted_iota(jnp.int32, (1, PAIR), 1)
    head0 = lane < HEAD_DIM
    n2 = 2 * GDN_CHUNK
    r = lax.broadcasted_iota(jnp.int32, (n2, n2), 0)
    c = lax.broadcasted_iota(jnp.int32, (n2, n2), 1)
    same = (r // GDN_CHUNK) == (c // GDN_CHUNK)
    causal = same & (c <= r)
    strict = same & (c < r)

    def l2n(v):
        return v * lax.rsqrt(_dot_ones(v * v, ones) + NORM_EPS)

    def stack(v):
        return jnp.concatenate([jnp.where(head0, v, 0.0), jnp.where(head0, 0.0, v)], axis=0)

    pair = lambda p, grp: slice(grp * GDN_WIDTH + p * PAIR, grp * GDN_WIDTH + (p + 1) * PAIR)
    q_p = [l2n(qkv[:, pair(p, 0)]) * HEAD_DIM ** -0.5 for p in range(N_PAIRS)]
    k_p = [l2n(qkv[:, pair(p, 1)]) for p in range(N_PAIRS)]
    v_p = [qkv[:, pair(p, 2)] for p in range(N_PAIRS)]

    units = [(ci, p) for ci in range(nc) for p in range(N_PAIRS)]
    rows = lambda ci: slice(ci * GDN_CHUNK, (ci + 1) * GDN_CHUNK)
    gc = [g_ref[0, rows(ci), pair(p, 0)] for ci, p in units]
    bt = [beta_ref[0, rows(ci), pair(p, 0)] for ci, p in units]
    qn = [q_p[p][rows(ci)] for ci, p in units]
    kn = [k_p[p][rows(ci)] for ci, p in units]
    vv = [v_p[p][rows(ci)] for ci, p in units]
    g_last = [g[GDN_CHUNK - 1:GDN_CHUNK, :] for g in gc]
    eg = [jnp.exp(g) for g in gc]
    kb = [k * b for k, b in zip(kn, bt)]
    k2 = [stack(k) for k in kn]

    def decay_of(g, ci, p):
        g_col = jnp.concatenate([jnp.broadcast_to(g[:, 0:1], (GDN_CHUNK, PAIR)),
                                 jnp.broadcast_to(g[:, HEAD_DIM:HEAD_DIM + 1], (GDN_CHUNK, PAIR))], axis=0)
        g_row = jnp.concatenate([gt_ref[0, 2 * p:2 * p + 1, rows(ci)], gt_ref[0, 2 * p + 1:2 * p + 2, rows(ci)]],
                                axis=1)
        return jnp.where(causal, jnp.exp(jnp.where(causal, g_col - g_row, 0.0)), 0.0)

    decay = [decay_of(g, ci, p) for g, (ci, p) in zip(gc, units)]
    k2b = [x.astype(MXU_DTYPE) for x in k2]
    low = [jnp.where(strict, _dot_nt(stack(a), b) * d, 0.0) for a, b, d in zip(kb, k2b, decay)]
    a_in = [(_dot_nt(stack(a), b) * d).astype(MXU_DTYPE) for a, b, d in zip(qn, k2b, decay)]
    t_inv = _unit_lower_inverse_many(low)
    uw = [_dot(ti, jnp.concatenate([stack(v * b), stack(a * e)], axis=1)).astype(MXU_DTYPE)
          for ti, v, b, a, e in zip(t_inv, vv, bt, kb, eg)]
    ket = [stack(k * jnp.exp(gl - g)).T for k, gl, g in zip(kn, g_last, gc)]
    nm = [_dot(a, b) for a, b in zip(ket, uw)]
    raw = [jnp.dot(a, b, preferred_element_type=F32) for a, b in zip(a_in, uw)]
    p_mat = [stack(q * e) - x[:, PAIR:] for q, e, x in zip(qn, eg, raw)]

    state = [s_ref[p] for p in range(N_PAIRS)]
    for i, (ci, p) in enumerate(units):
        s = state[p]
        o2 = _dot(p_mat[i], s) + raw[i][:, :PAIR]
        state[p] = s * jnp.exp(g_last[i]) - _dot(nm[i][:, PAIR:], s) + nm[i][:, :PAIR]
        o = o2[0:GDN_CHUNK] + o2[GDN_CHUNK:n2]
        ms = _dot_ones(o * o, ones) * (1.0 / HEAD_DIM)
        on = o * lax.rsqrt(ms + NORM_EPS) * nw_ref[...]
        o_ref[0, rows(ci), pair(p, 0)] = (on * _silu(gate_ref[0, rows(ci), pair(p, 0)])
                                          * ms_ref[:, pair(p, 0)]).astype(o_ref.dtype)
    for p in range(N_PAIRS):
        s_ref[p] = state[p]


def _gdn(qkv, g, beta, g_t, gate, norm_w2, mscale, blk):
    b, t, _ = qkv.shape
    tile = lambda width: pl.BlockSpec((1, blk, width), lambda i, j: (i, j, 0))
    const = lambda shape: pl.BlockSpec(shape, lambda i, j: (0, 0))
    return pl.pallas_call(
        _gdn_kernel,
        grid=(b, t // blk),
        in_specs=[tile(3 * GDN_WIDTH),
                  tile(GDN_WIDTH), tile(GDN_WIDTH), pl.BlockSpec((1, SUBLANES, blk), lambda i, j: (i, 0, j)),
                  tile(GDN_WIDTH), const((1, PAIR)), const((1, GDN_WIDTH))],
        out_specs=tile(GDN_WIDTH),
        out_shape=jax.ShapeDtypeStruct((b, t, GDN_WIDTH), MXU_DTYPE),
        scratch_shapes=[pltpu.VMEM((N_PAIRS, PAIR, PAIR), F32)],
        compiler_params=_cparams("parallel", "arbitrary"),
        name="gdn",
    )(qkv, g, beta, g_t, gate, norm_w2, mscale)


FOX_ROW_GROUP = 32


FOX_SKIP_MARGIN = 106.0
KEY_NORM_SLACK = 1.01


def _fox_kernel(q_ref, k_ref, v_ref, aux_ref, ms_ref, o_ref, s0_ref, s1_ref, p0_ref, p1_ref, a0_ref, a1_ref,
                m_ref, acc_ref, *, tq):
    i = pl.program_id(2)
    heads = range(2)
    rg = FOX_ROW_GROUP
    m_ref[...] = jnp.full(m_ref.shape, -jnp.inf, F32)
    acc_ref[...] = jnp.zeros(acc_ref.shape, F32)

    def scores(j, s_ref):
        start = pl.multiple_of(j * tq, tq)
        for h in heads:
            s_ref[h] = lax.dot_general(q_ref[0, h], k_ref[0, h, pl.ds(start, tq), :],
                                       (((1,), (1,)), ((), ())), preferred_element_type=F32)

    def softmax(s_ref, p_ref, a_ref, masked):
        for h in heads:
            for g in range(tq // rg):
                rows = slice(g * rg, (g + 1) * rg)
                width = LANES * (((g + 1) * rg - 1) // LANES + 1) if masked else tq
                s = s_ref[h, rows, 0:width]
                if masked:
                    row_id = g * rg + lax.broadcasted_iota(jnp.int32, (rg, width), 0)
                    col_id = lax.broadcasted_iota(jnp.int32, (rg, width), 1)
                    s = jnp.where(col_id <= row_id, s, -jnp.inf)
                    if width < tq:
                        p_ref[h, rows, width:tq] = jnp.zeros((rg, tq - width), p_ref.dtype)
                m_old = m_ref[h, rows, :]
                m_new = jnp.maximum(m_old, jnp.max(s, axis=-1, keepdims=True))
                a_ref[h, rows, :] = jnp.exp(m_old - m_new)
                m_ref[h, rows, :] = m_new
                m_wide = jnp.concatenate([m_new] * (width // LANES), axis=1)
                p_ref[h, rows, 0:width] = jnp.exp(s - m_wide).astype(p_ref.dtype)

    def weighted_values(j, p_ref, a_ref):
        start = pl.multiple_of(j * tq, tq)
        for h in heads:
            acc_ref[h] = a_ref[h] * acc_ref[h] + jnp.dot(p_ref[h], v_ref[0, h, pl.ds(start, tq), :],
                                                         preferred_element_type=F32)

    scores(i, s0_ref)
    softmax(s0_ref, p0_ref, a0_ref, True)
    scores(jnp.maximum(i - 1, 0), s1_ref)
    weighted_values(i, p0_ref, a0_ref)

    lane = lax.broadcasted_iota(jnp.int32, (1, LANES), 1)
    n = jnp.int32(0)
    for h in heads:
        q = q_ref[0, h].astype(F32)
        q_norm = jnp.sqrt(jnp.sum(jnp.where(lane < HEAD_DIM, q * q, 0.0), axis=-1, keepdims=True))
        key_norm = aux_ref[0, h, 1:2, :]
        block_end_c = aux_ref[0, h, 0:1, :]
        slack = jnp.max(q_norm * key_norm - m_ref[h], axis=0, keepdims=True)
        keep = (lane < i) & (slack - block_end_c >= -FOX_SKIP_MARGIN)
        n = jnp.maximum(n, jnp.sum(keep.astype(jnp.int32)))

    def body(t, carry):
        j = i - 1 - 2 * t
        softmax(s1_ref, p1_ref, a1_ref, False)
        scores(jnp.maximum(j - 1, 0), s0_ref)
        weighted_values(j, p1_ref, a1_ref)
        softmax(s0_ref, p0_ref, a0_ref, False)
        scores(jnp.maximum(j - 2, 0), s1_ref)
        weighted_values(j - 1, p0_ref, a0_ref)
        return carry

    lax.fori_loop(0, n // 2, body, 0)

    @pl.when(n % 2 == 1)
    def _():
        softmax(s1_ref, p1_ref, a1_ref, False)
        weighted_values(i - n, p1_ref, a1_ref)

    o = [acc_ref[h] / acc_ref[h][:, AUG_BIAS_LANE:AUG_BIAS_LANE + 1] for h in heads]
    lane = lax.broadcasted_iota(jnp.int32, (1, PAIR), 1)
    o_pair = jnp.where(lane < HEAD_DIM, o[0], pltpu.roll(o[1], HEAD_DIM, axis=1))
    o_ref[0] = (o_pair * ms_ref[...]).astype(o_ref.dtype)


def _fox_aux(stats):
    b, nt = stats.shape[:2]
    k_sq = jnp.max(stats[:, :, :N_PAIRS, :], axis=1)
    k_sq = k_sq.reshape(b, N_PAIRS, 2, HEAD_DIM)[..., 0].reshape(b, FOX_HEADS)
    key_norm = jnp.sqrt(k_sq) * KEY_NORM_SLACK
    block_end_c = jnp.transpose(stats[:, :, N_PAIRS, 2 * GDN_HEADS:2 * GDN_HEADS + FOX_HEADS], (0, 2, 1))
    aux = jnp.zeros((b, FOX_HEADS, SUBLANES, LANES), F32)
    aux = aux.at[:, :, 0, :nt].set(block_end_c)
    return aux.at[:, :, 1, :].set(jnp.broadcast_to(key_norm[:, :, None], (b, FOX_HEADS, LANES)))


def _fox(qa, ka, va, aux, mscale, tq):
    b, _, t, _ = qa.shape
    kern = functools.partial(_fox_kernel, tq=tq)
    return pl.pallas_call(
        kern,
        grid=(b, N_PAIRS, t // tq),
        in_specs=[pl.BlockSpec((1, 2, tq, LANES), lambda bi, p, i: (bi, p, i, 0)),
                  pl.BlockSpec((1, 2, t, LANES), lambda bi, p, i: (bi, p, 0, 0)),
                  pl.BlockSpec((1, 2, t, LANES), lambda bi, p, i: (bi, p, 0, 0)),
                  pl.BlockSpec((1, 2, SUBLANES, LANES), lambda bi, p, i: (bi, p, 0, 0)),
                  pl.BlockSpec((1, PAIR), lambda bi, p, i: (0, p))],
        out_specs=pl.BlockSpec((1, tq, PAIR), lambda bi, p, i: (bi, i, p)),
        out_shape=jax.ShapeDtypeStruct((b, t, FOX_WIDTH), MXU_DTYPE),
        scratch_shapes=[pltpu.VMEM((2, tq, tq), F32), pltpu.VMEM((2, tq, tq), F32),
                        pltpu.VMEM((2, tq, tq), jnp.bfloat16), pltpu.VMEM((2, tq, tq), jnp.bfloat16),
                        pltpu.VMEM((2, tq, LANES), F32), pltpu.VMEM((2, tq, LANES), F32),
                        pltpu.VMEM((2, tq, LANES), F32), pltpu.VMEM((2, tq, LANES), F32)],
        compiler_params=_cparams("parallel", "parallel", "arbitrary"),
        name="fox_attention",
    )(qa, ka, va, aux, mscale)


def _out_proj_kernel(oa_ref, ob_ref, oc_ref, x_ref, w_ref, g_ref, b_ref, o_ref):
    mix = jnp.dot(oa_ref[...], w_ref[0:GDN_WIDTH, :], preferred_element_type=F32)
    mix = mix + jnp.dot(ob_ref[...], w_ref[GDN_WIDTH:GDN_WIDTH + CONV_WIDTH, :], preferred_element_type=F32)
    mix = mix + jnp.dot(oc_ref[...], w_ref[GDN_WIDTH + CONV_WIDTH:D_MODEL, :], preferred_element_type=F32)
    o_ref[...] = _layer_norm_rows(DEEPNORM_ALPHA * x_ref[...] + mix, g_ref[...], b_ref[...])


def _out_proj(oa, ob, oc, xf, w, g, bvec, tm):
    n = xf.shape[0]
    row = lambda i: (i, 0)
    const = lambda i: (0, 0)
    return pl.pallas_call(
        _out_proj_kernel,
        grid=(n // tm,),
        in_specs=[pl.BlockSpec((tm, GDN_WIDTH), row), pl.BlockSpec((tm, CONV_WIDTH), row),
                  pl.BlockSpec((tm, FOX_WIDTH), row), pl.BlockSpec((tm, D_MODEL), row),
                  pl.BlockSpec((D_MODEL, D_MODEL), const),
                  pl.BlockSpec((1, D_MODEL), const), pl.BlockSpec((1, D_MODEL), const)],
        out_specs=pl.BlockSpec((tm, D_MODEL), row),
        out_shape=jax.ShapeDtypeStruct((n, D_MODEL), F32),
        compiler_params=_cparams("parallel"),
        name="out_proj_ln",
    )(oa, ob, oc, xf, w, g, bvec)


FFN_CHUNK = 256


def _swiglu_chunks(xb, w1, w3, w2, width):
    assert width % FFN_CHUNK == 0
    acc = None
    for c in range(width // FFN_CHUNK):
        cols = slice(c * FFN_CHUNK, (c + 1) * FFN_CHUNK)
        h = _silu(jnp.dot(xb, w1(slice(None), cols), preferred_element_type=F32)) * jnp.dot(
            xb, w3(slice(None), cols), preferred_element_type=F32)
        part = jnp.dot(h.astype(xb.dtype), w2(cols, slice(None)), preferred_element_type=F32)
        acc = part if acc is None else acc + part
    return acc


def _ffn_kernel(x_ref, w1_ref, w3_ref, w2_ref, g_ref, b_ref, o_ref):
    x = x_ref[...]
    ff = _swiglu_chunks(x.astype(w1_ref.dtype), lambda r, c: w1_ref[r, c], lambda r, c: w3_ref[r, c],
                        lambda r, c: w2_ref[r, c], w1_ref.shape[1])
    o_ref[...] = _layer_norm_rows(DEEPNORM_ALPHA * x + ff, g_ref[...], b_ref[...])


def _dense_ffn(xf, w1, w3, w2, g, bvec, tm):
    n = xf.shape[0]
    ff = w1.shape[1]
    const = lambda shape: pl.BlockSpec(shape, lambda i: (0, 0), pipeline_mode=pl.Buffered(1))
    return pl.pallas_call(
        _ffn_kernel,
        grid=(n // tm,),
        in_specs=[pl.BlockSpec((tm, D_MODEL), lambda i: (i, 0)),
                  const((D_MODEL, ff)), const((D_MODEL, ff)), const((ff, D_MODEL)),
                  const((1, D_MODEL)), const((1, D_MODEL))],
        out_specs=pl.BlockSpec((tm, D_MODEL), lambda i: (i, 0)),
        out_shape=jax.ShapeDtypeStruct((n, D_MODEL), F32),
        compiler_params=_cparams("parallel"),
        name="dense_ffn_ln",
    )(xf, w1, w3, w2, g, bvec)


def _router_kernel(x_ref, wr_ref, info_ref, cnt_ref, run_ref):
    i = pl.program_id(0)
    tm = x_ref.shape[0]

    @pl.when(i == 0)
    def _():
        run_ref[...] = jnp.zeros_like(run_ref)

    logits = jnp.dot(x_ref[...].astype(MXU_DTYPE), wr_ref[...], preferred_element_type=F32)
    lane = lax.broadcasted_iota(jnp.int32, (tm, LANES), 1)
    logits = jnp.where(lane < N_EXPERTS, logits, -jnp.inf)
    m1 = jnp.max(logits, axis=-1, keepdims=True)
    e1 = jnp.min(jnp.where(logits == m1, lane, LANES), axis=-1, keepdims=True)
    rest = jnp.where(lane == e1, -jnp.inf, logits)
    m2 = jnp.max(rest, axis=-1, keepdims=True)
    e2 = jnp.min(jnp.where(rest == m2, lane, LANES), axis=-1, keepdims=True)
    z = jnp.exp(m2 - m1)
    g1 = 1.0 / (1.0 + z)
    g2 = z / (1.0 + z)
    onehot = ((lane == e1) | (lane == e2)).astype(F32)
    r = lax.broadcasted_iota(jnp.int32, (tm, tm), 0)
    c = lax.broadcasted_iota(jnp.int32, (tm, tm), 1)
    before = jnp.dot((c < r).astype(jnp.bfloat16), onehot.astype(jnp.bfloat16),
                     preferred_element_type=F32) + run_ref[...]
    rank1 = jnp.sum(jnp.where(lane == e1, before, 0.0), axis=-1, keepdims=True)
    rank2 = jnp.sum(jnp.where(lane == e2, before, 0.0), axis=-1, keepdims=True)
    run_ref[...] = run_ref[...] + jnp.sum(onehot, axis=0, keepdims=True)
    cnt_ref[...] = run_ref[...]
    info = jnp.where(lane == 0, e1.astype(F32),
                     jnp.where(lane == 1, e2.astype(F32),
                               jnp.where(lane == 2, rank1,
                                         jnp.where(lane == 3, rank2,
                                                   jnp.where(lane == 4, g1, jnp.where(lane == 5, g2, 0.0))))))
    info_ref[...] = info


def _router(xf, wr_pad, tm):
    n = xf.shape[0]
    return pl.pallas_call(
        _router_kernel,
        grid=(n // tm,),
        in_specs=[pl.BlockSpec((tm, D_MODEL), lambda i: (i, 0)),
                  pl.BlockSpec((D_MODEL, LANES), lambda i: (0, 0))],
        out_specs=[pl.BlockSpec((tm, LANES), lambda i: (i, 0)),
                   pl.BlockSpec((1, LANES), lambda i: (0, 0))],
        out_shape=[jax.ShapeDtypeStruct((n, LANES), F32), jax.ShapeDtypeStruct((1, LANES), F32)],
        scratch_shapes=[pltpu.VMEM((1, LANES), F32)],
        compiler_params=_cparams("arbitrary"),
        name="moe_router",
    )(xf, wr_pad)


DMA_LOOP_UNROLL = 8


def _dispatch_kernel(fill_ref, d1_ref, d2_ref, x_ref, xs_ref, zero_ref, sem):
    tm = x_ref.shape[0]

    @pl.when(pl.program_id(0) == 0)
    def _():
        zero_ref[...] = jnp.zeros_like(zero_ref)

        def strip_copy(e, k):
            start = pl.multiple_of(fill_ref[e] - SUBLANES * (k + 1), SUBLANES)
            return pltpu.make_async_copy(zero_ref, xs_ref.at[pl.ds(start, SUBLANES), :], sem)

        for e in range(N_EXPERTS):
            lax.fori_loop(0, fill_ref[N_EXPERTS + e], lambda k, c, e=e: (strip_copy(e, k).start(), c)[1], 0)
        for e in range(N_EXPERTS):
            lax.fori_loop(0, fill_ref[N_EXPERTS + e], lambda k, c, e=e: (strip_copy(e, k).wait(), c)[1], 0)

    def row_copy(r, dst):
        return pltpu.make_async_copy(x_ref.at[pl.ds(r, 1), :], xs_ref.at[pl.ds(dst, 1), :], sem)

    def issue(r, carry):
        row_copy(r, d1_ref[0, 0, r]).start()
        row_copy(r, d2_ref[0, 0, r]).start()
        return carry

    lax.fori_loop(0, tm, issue, 0, unroll=DMA_LOOP_UNROLL)

    def drain(r, carry):
        row_copy(r, d1_ref[0, 0, r]).wait()
        row_copy(r, d2_ref[0, 0, r]).wait()
        return carry

    lax.fori_loop(0, tm, drain, 0, unroll=DMA_LOOP_UNROLL)


def _dispatch(xf, d1, d2, gap_start, gap_end, rows, tm):
    n = xf.shape[0]
    idx = lambda a: a.reshape(n // tm, 1, tm)
    n_strips = (gap_end - gap_start + SUBLANES - 1) // SUBLANES
    fill = jnp.concatenate([gap_end, n_strips]).astype(jnp.int32)
    smem = lambda: pl.BlockSpec((1, 1, tm), lambda i, fill: (i, 0, 0), memory_space=pltpu.SMEM)
    grid_spec = pltpu.PrefetchScalarGridSpec(
        num_scalar_prefetch=1,
        grid=(n // tm,),
        in_specs=[smem(), smem(), pl.BlockSpec((tm, D_MODEL), lambda i, fill: (i, 0))],
        out_specs=pl.BlockSpec(memory_space=pl.ANY),
        scratch_shapes=[pltpu.VMEM((SUBLANES, D_MODEL), xf.dtype), pltpu.SemaphoreType.DMA(())],
    )
    return pl.pallas_call(
        _dispatch_kernel,
        grid_spec=grid_spec,
        out_shape=jax.ShapeDtypeStruct((rows, D_MODEL), xf.dtype),
        compiler_params=_cparams("arbitrary"),
        name="moe_dispatch",
    )(fill, idx(d1), idx(d2), xf)


def _expert_kernel(te_ref, nu_ref, xs_ref, w1_ref, w3_ref, w2_ref, ys_ref, xb_ref, acc_ref):
    del te_ref
    i = pl.program_id(0)
    f = pl.program_id(1)
    used = i < nu_ref[0]

    @pl.when(f == 0)
    def _():
        xb_ref[...] = xs_ref[...].astype(xb_ref.dtype)
        acc_ref[...] = jnp.zeros_like(acc_ref)

    @pl.when(used)
    def _():
        acc_ref[...] += _swiglu_chunks(xb_ref[...], lambda r, c: w1_ref[0, r, c], lambda r, c: w3_ref[0, r, c],
                                       lambda r, c: w2_ref[0, r, c], w1_ref.shape[2])

    @pl.when(f == pl.num_programs(1) - 1)
    def _():
        ys_ref[...] = acc_ref[...]


def _experts(xs, tile_expert, n_used, w1, w3, w2, tm, tf):
    rows = xs.shape[0]
    ff = w1.shape[2]
    n_tiles = rows // tm

    def x_map(i, f, te, nu):
        return (jnp.maximum(jnp.minimum(i, nu[0] - 1), 0), 0)

    def w13_map(i, f, te, nu):
        return (te[i], 0, jnp.where(i < nu[0], f, ff // tf - 1))

    def w2_map(i, f, te, nu):
        return (te[i], jnp.where(i < nu[0], f, ff // tf - 1), 0)

    grid_spec = pltpu.PrefetchScalarGridSpec(
        num_scalar_prefetch=2,
        grid=(n_tiles, ff // tf),
        in_specs=[pl.BlockSpec((tm, D_MODEL), x_map),
                  pl.BlockSpec((1, D_MODEL, tf), w13_map),
                  pl.BlockSpec((1, D_MODEL, tf), w13_map),
                  pl.BlockSpec((1, tf, D_MODEL), w2_map)],
        out_specs=pl.BlockSpec((tm, D_MODEL), lambda i, f, te, nu: (i, 0)),
        scratch_shapes=[pltpu.VMEM((tm, D_MODEL), MXU_DTYPE), pltpu.VMEM((tm, D_MODEL), F32)],
    )
    return pl.pallas_call(
        _expert_kernel,
        grid_spec=grid_spec,
        out_shape=jax.ShapeDtypeStruct((rows, D_MODEL), F32),
        compiler_params=_cparams("arbitrary", "arbitrary"),
        name="moe_experts",
    )(tile_expert, n_used, xs, w1, w3, w2)


def _combine_kernel(d1_ref, d2_ref, x_ref, info_ref, ys_ref, g_ref, b_ref, o_ref, y1_ref, y2_ref, sem):
    tm = x_ref.shape[0]

    def row_copy(src, r, buf):
        return pltpu.make_async_copy(ys_ref.at[pl.ds(src, 1), :], buf.at[pl.ds(r, 1), :], sem)

    def issue(r, carry):
        row_copy(d1_ref[0, 0, r], r, y1_ref).start()
        row_copy(d2_ref[0, 0, r], r, y2_ref).start()
        return carry

    lax.fori_loop(0, tm, issue, 0, unroll=DMA_LOOP_UNROLL)

    def drain(r, carry):
        row_copy(d1_ref[0, 0, r], r, y1_ref).wait()
        row_copy(d2_ref[0, 0, r], r, y2_ref).wait()
        return carry

    lax.fori_loop(0, tm, drain, 0, unroll=DMA_LOOP_UNROLL)
    info = info_ref[...]
    ff = info[:, 4:5] * y1_ref[...] + info[:, 5:6] * y2_ref[...]
    o_ref[...] = _layer_norm_rows(DEEPNORM_ALPHA * x_ref[...] + ff, g_ref[...], b_ref[...])


def _combine(xf, info, ys, d1, d2, g, bvec, tm):
    n = xf.shape[0]
    idx = lambda a: a.reshape(n // tm, 1, tm)
    smem = lambda: pl.BlockSpec((1, 1, tm), lambda i: (i, 0, 0), memory_space=pltpu.SMEM)
    return pl.pallas_call(
        _combine_kernel,
        grid=(n // tm,),
        in_specs=[smem(), smem(),
                  pl.BlockSpec((tm, D_MODEL), lambda i: (i, 0)),
                  pl.BlockSpec((tm, LANES), lambda i: (i, 0)),
                  pl.BlockSpec(memory_space=pl.ANY),
                  pl.BlockSpec((1, D_MODEL), lambda i: (0, 0)),
                  pl.BlockSpec((1, D_MODEL), lambda i: (0, 0))],
        out_specs=pl.BlockSpec((tm, D_MODEL), lambda i: (i, 0)),
        out_shape=jax.ShapeDtypeStruct((n, D_MODEL), F32),
        scratch_shapes=[pltpu.VMEM((tm, D_MODEL), F32), pltpu.VMEM((tm, D_MODEL), F32),
                        pltpu.SemaphoreType.DMA(())],
        compiler_params=_cparams("arbitrary"),
        name="moe_combine_ln",
    )(idx(d1), idx(d2), xf, info, ys, g, bvec)


MOE_TILE = 512


def _moe_ffn(xf, w_router, w1, w3, w2, g, bvec, tm_tok, tf):
    n = xf.shape[0]
    wr_pad = jnp.zeros((D_MODEL, LANES), MXU_DTYPE).at[:, :N_EXPERTS].set(w_router.astype(MXU_DTYPE))
    info, counts = _router(xf, wr_pad, tm_tok)
    sizes = counts[0, :N_EXPERTS].astype(jnp.int32)
    tiles_per = (sizes + MOE_TILE - 1) // MOE_TILE
    tile_end = jnp.cumsum(tiles_per)
    seg_start = (tile_end - tiles_per) * MOE_TILE
    e1 = info[:, 0].astype(jnp.int32)
    e2 = info[:, 1].astype(jnp.int32)
    d1 = seg_start[e1] + info[:, 2].astype(jnp.int32)
    d2 = seg_start[e2] + info[:, 3].astype(jnp.int32)
    n_tiles = (n * TOP_K) // MOE_TILE + N_EXPERTS
    rows = n_tiles * MOE_TILE
    tile_expert = jnp.minimum(jnp.searchsorted(tile_end, jnp.arange(n_tiles, dtype=jnp.int32), side='right'),
                              N_EXPERTS - 1).astype(jnp.int32)
    n_used = tile_end[N_EXPERTS - 1:].astype(jnp.int32)
    gap_end = jnp.concatenate([seg_start[1:], jnp.full((1,), rows, jnp.int32)])
    xs = _dispatch(xf, d1, d2, seg_start + sizes, gap_end, rows, tm_tok)
    ys = _experts(xs, tile_expert, n_used, w1, w3, w2, MOE_TILE, tf)
    return _combine(xf, info, ys, d1, d2, g, bvec, tm_tok)


def _pack_in_proj(w_in):
    cuts = np.cumsum([0, GDN_WIDTH, GDN_WIDTH, GDN_WIDTH, GDN_HEADS, GDN_HEADS, GDN_WIDTH,
                      2 * CONV_WIDTH, FOX_WIDTH, FOX_WIDTH, FOX_WIDTH, FOX_HEADS])
    seg = lambda i: w_in[:, cuts[i]:cuts[i + 1]]
    w_main = jnp.concatenate([seg(0), seg(1), seg(2), seg(5), seg(6), seg(7), seg(8), seg(9)], axis=1)
    pad = jnp.zeros((D_MODEL, SMALL_COLS - 2 * GDN_HEADS - FOX_HEADS), w_in.dtype)
    w_small = jnp.concatenate([seg(3), seg(4), seg(10), pad], axis=1).astype(F32)
    ws_hi = w_small.astype(MXU_DTYPE)
    ws_lo = (w_small - ws_hi.astype(F32)).astype(MXU_DTYPE)
    return w_main.astype(MXU_DTYPE), jnp.concatenate([ws_hi, ws_lo], axis=1)


def _lane_row(vals, offset):
    return jnp.zeros((1, LANES), F32).at[0, offset:offset + vals.shape[0]].set(vals.astype(F32))


def _mixer(xf, b, t, w_in, mix_scale, w_out, gdn_conv_w, gdn_a_log, gdn_dt_bias, gdn_norm_w,
           cnv_dw_w, cnv_dw_b, cnv_ln_g, cnv_ln_b, fox_f_bias, ln_g, ln_b, tiles):
    w_main, w_small = _pack_in_proj(w_in)
    ms = mix_scale.reshape(1, D_MODEL).astype(F32)
    row = lambda v: v.reshape(1, -1).astype(F32)
    qkv, gate, o_b, fq, fk, fv, small = _in_proj(
        xf, w_main, w_small, gdn_conv_w, cnv_dw_w, row(cnv_dw_b), row(cnv_ln_g), row(cnv_ln_b),
        ms[:, GDN_WIDTH:GDN_WIDTH + CONV_WIDTH], tiles["tm"], t // tiles["tm"])
    r3 = lambda a: a.reshape(b, t, a.shape[-1])
    assert tiles["tt"] == tiles["tq"]
    g_cum, beta, g_t, stats, qa, ka, va = _gates(
        r3(small), _lane_row(gdn_a_log, 0), _lane_row(gdn_dt_bias, 0), _lane_row(fox_f_bias, 2 * GDN_HEADS),
        r3(fq), r3(fk), r3(fv), tiles["tt"])
    norm_w2 = jnp.tile(gdn_norm_w.astype(F32), 2).reshape(1, PAIR)
    o_a = _gdn(r3(qkv), g_cum, beta, g_t, r3(gate), norm_w2, ms[:, :GDN_WIDTH], tiles["gdn_blk"])
    o_c = _fox(qa, ka, va, _fox_aux(stats), ms[:, GDN_WIDTH + CONV_WIDTH:], tiles["tq"])
    flat = lambda a: a.reshape(b * t, a.shape[-1])
    return _out_proj(flat(o_a), o_b, flat(o_c), xf, w_out.astype(MXU_DTYPE), row(ln_g), row(ln_b),
                     tiles["tm"])


def _tiles(t):
    return dict(tm=min(512, t), tt=min(512, t), gdn_blk=min(512, t), tq=min(512, t),
                tf_moe=1792)


def kernel(x, w_in, mix_scale, w_out, gdn_conv_w, gdn_a_log, gdn_dt_bias, gdn_norm_w, cnv_dw_w, cnv_dw_b,
           cnv_ln_g, cnv_ln_b, fox_f_bias, ln_mix_g, ln_mix_b, ln_ffn_g, ln_ffn_b, ffn_w1, ffn_w3, ffn_w2,
           moe_router, moe_w1, moe_w3, moe_w2):
    b, t, d = x.shape
    tiles = _tiles(t)
    row = lambda v: v.reshape(1, -1).astype(F32)
    xf = x.reshape(b * t, d)
    for l in range(DEPTH):
        xf = _mixer(xf, b, t, w_in[l], mix_scale[l], w_out[l], gdn_conv_w[l], gdn_a_log[l], gdn_dt_bias[l],
                    gdn_norm_w[l], cnv_dw_w[l], cnv_dw_b[l], cnv_ln_g[l], cnv_ln_b[l], fox_f_bias[l],
                    ln_mix_g[l], ln_mix_b[l], tiles)
        if l % 2 == 0:
            e = l // 2
            xf = _dense_ffn(xf, ffn_w1[e].astype(MXU_DTYPE), ffn_w3[e].astype(MXU_DTYPE),
                            ffn_w2[e].astype(MXU_DTYPE), row(ln_ffn_g[l]), row(ln_ffn_b[l]),
                            tiles["tm"])
        else:
            e = l // 2
            xf = _moe_ffn(xf, moe_router[e], moe_w1[e].astype(MXU_DTYPE), moe_w3[e].astype(MXU_DTYPE),
                          moe_w2[e].astype(MXU_DTYPE), row(ln_ffn_g[l]), row(ln_ffn_b[l]),
                          tiles["tm"], tiles["tf_moe"])
    return xf.reshape(b, t, d)
```

```python
import functools

import jax
import jax.numpy as jnp
import numpy as np
from jax import lax
from jax.experimental import pallas as pl
from jax.experimental.pallas import tpu as pltpu

D_MODEL = 1024
DEPTH = 2
HEAD_DIM = 64
GDN_WIDTH = 384
CONV_WIDTH = 256
FOX_WIDTH = 384
GDN_HEADS = 6
FOX_HEADS = 6
GDN_SHORT_CONV = 4
GDN_CHUNK = 64
CONV_KERNEL = 31
FFN_DENSE = 2816
N_EXPERTS = 8
TOP_K = 2
FFN_EXPERT = 3584
DEEPNORM_ALPHA = (2 * DEPTH) ** 0.25
LN_EPS = 1e-5
NORM_EPS = 1e-6

LANES = 128
SUBLANES = 8
PAIR = 2 * HEAD_DIM
N_PAIRS = GDN_WIDTH // PAIR
SMALL_COLS = LANES

_C_QKV = 0
_C_GATE = _C_QKV + 3 * GDN_WIDTH
_C_GLU = _C_GATE + GDN_WIDTH
_C_FQ = _C_GLU + 2 * CONV_WIDTH
_C_FK = _C_FQ + FOX_WIDTH
_C_FV = _C_FK + FOX_WIDTH
_C_END = _C_FV + FOX_WIDTH

MXU_DTYPE = jnp.bfloat16
F32 = jnp.float32

VMEM_LIMIT = 56 * 1024 * 1024


def _cparams(*sem):
    return pltpu.CompilerParams(dimension_semantics=sem, vmem_limit_bytes=VMEM_LIMIT)


def _dot(a, b):
    return jnp.dot(a.astype(MXU_DTYPE), b.astype(MXU_DTYPE), preferred_element_type=F32)


def _dot_nt(a, b):
    return lax.dot_general(a.astype(MXU_DTYPE), b.astype(MXU_DTYPE),
                           (((1,), (1,)), ((), ())), preferred_element_type=F32)


def _split(a):
    hi = a.astype(jnp.bfloat16)
    return hi, (a - hi.astype(F32)).astype(jnp.bfloat16)


def _split3(a):
    hi = a.astype(jnp.bfloat16)
    rest = a - hi.astype(F32)
    mid = rest.astype(jnp.bfloat16)
    lo = (rest - mid.astype(F32)).astype(jnp.bfloat16)
    return hi, mid, lo


def _dot_ones(a, ones_b16):
    ah, al = _split(a)
    d = lambda x: jnp.dot(x, ones_b16, preferred_element_type=F32)
    return d(ah) + d(al)


def _sel_rhs(a, sel_b16):
    return sum(jnp.dot(t, sel_b16, preferred_element_type=F32) for t in _split3(a))


def _sel_lhs(sel_b16, b):
    return sum(jnp.dot(sel_b16, t, preferred_element_type=F32) for t in _split3(b))


def _sigmoid(x):
    return 1.0 / (1.0 + jnp.exp(-x))


def _silu(x):
    return x * _sigmoid(x)


def _softplus(x):
    return jnp.maximum(x, 0.0) + jnp.log1p(jnp.exp(-jnp.abs(x)))


def _log_sigmoid(x):
    return -_softplus(-x)


def _layer_norm_rows(y, g, b):
    mu = jnp.mean(y, axis=-1, keepdims=True)
    d = y - mu
    var = jnp.mean(d * d, axis=-1, keepdims=True)
    return d * lax.rsqrt(var + LN_EPS) * g + b


def _head_ones():
    r = lax.broadcasted_iota(jnp.int32, (PAIR, PAIR), 0) // HEAD_DIM
    c = lax.broadcasted_iota(jnp.int32, (PAIR, PAIR), 1) // HEAD_DIM
    return (r == c).astype(F32)


CONV_HALO = 32
CONV_ROW_BLOCK = 128
SHORT_CONV_ROW_BLOCK = 128


def _in_proj_kernel(x_ref, w_ref, ws_ref, gw_ref, cw_ref, cb_ref, lg_ref, lb_ref, ms_ref,
                    qkv_ref, gate_ref, ob_ref, fq_ref, fk_ref, fv_ref, small_ref,
                    qbuf_ref, cbuf_ref, sh_ref, *, tiles_per_seq):
    first = pl.program_id(0) % tiles_per_seq == 0
    tm = x_ref.shape[0]
    x = x_ref[...]
    xb = x.astype(MXU_DTYPE)

    def mm(lo, hi):
        return jnp.dot(xb, w_ref[:, lo:hi], preferred_element_type=F32)

    @pl.when(first)
    def _():
        cbuf_ref[0:CONV_HALO, :] = jnp.zeros((CONV_HALO, CONV_WIDTH), F32)
        qbuf_ref[0:SUBLANES, :] = jnp.zeros((SUBLANES, 3 * GDN_WIDTH), F32)

    @pl.when(jnp.logical_not(first))
    def _():
        cbuf_ref[0:CONV_HALO, :] = cbuf_ref[tm:tm + CONV_HALO, :]
        qbuf_ref[0:SUBLANES, :] = qbuf_ref[tm:tm + SUBLANES, :]

    glu = mm(_C_GLU, _C_FQ)
    cbuf_ref[CONV_HALO:CONV_HALO + tm, :] = glu[:, 0:CONV_WIDTH] * _sigmoid(glu[:, CONV_WIDTH:2 * CONV_WIDTH])
    span = tm + CONV_HALO - SUBLANES
    for s in range(1, SUBLANES):
        sh_ref[s - 1] = cbuf_ref[s:s + span, :]
    for r0 in range(0, tm, CONV_ROW_BLOCK):
        acc = jnp.zeros((CONV_ROW_BLOCK, CONV_WIDTH), F32) + cb_ref[...]
        for j in range(CONV_KERNEL):
            lo = r0 + CONV_HALO - (CONV_KERNEL - 1) + j
            base, phase = lo - lo % SUBLANES, lo % SUBLANES
            tap = (cbuf_ref[base:base + CONV_ROW_BLOCK, :] if phase == 0
                   else sh_ref[phase - 1, base:base + CONV_ROW_BLOCK, :])
            acc = acc + cw_ref[j:j + 1, :] * tap
        y = _silu(_layer_norm_rows(acc, lg_ref[...], lb_ref[...])) * ms_ref[...]
        ob_ref[r0:r0 + CONV_ROW_BLOCK, :] = y.astype(ob_ref.dtype)

    qbuf_ref[SUBLANES:SUBLANES + tm, :] = mm(_C_QKV, _C_GATE)
    for r0 in range(0, tm, SHORT_CONV_ROW_BLOCK):
        acc = jnp.zeros((SHORT_CONV_ROW_BLOCK, 3 * GDN_WIDTH), F32)
        for j in range(GDN_SHORT_CONV):
            lo = r0 + SUBLANES - (GDN_SHORT_CONV - 1) + j
            acc = acc + gw_ref[j:j + 1, :] * qbuf_ref[lo:lo + SHORT_CONV_ROW_BLOCK, :]
        qkv_ref[r0:r0 + SHORT_CONV_ROW_BLOCK, :] = _silu(acc)

    gate_ref[...] = mm(_C_GATE, _C_GLU)
    fq_ref[...] = (mm(_C_FQ, _C_FK) * HEAD_DIM ** -0.5).astype(fq_ref.dtype)
    fk_ref[...] = mm(_C_FK, _C_FV).astype(fk_ref.dtype)
    fv_ref[...] = mm(_C_FV, _C_END).astype(fv_ref.dtype)
    x_lo = (x - xb.astype(F32)).astype(MXU_DTYPE)
    hh_hl = jnp.dot(xb, ws_ref[...], preferred_element_type=F32)
    lh = jnp.dot(x_lo, ws_ref[:, 0:SMALL_COLS], preferred_element_type=F32)
    small_ref[...] = hh_hl[:, 0:SMALL_COLS] + hh_hl[:, SMALL_COLS:2 * SMALL_COLS] + lh


def _in_proj(xf, w_main, w_small, gdn_conv_w, cnv_w, cnv_b, cnv_ln_g, cnv_ln_b, cnv_scale, tm, tiles_per_seq):
    n = xf.shape[0]
    row = lambda i: (i, 0)
    const = lambda shape: pl.BlockSpec(shape, lambda i: (0, 0))
    outs = [
        jax.ShapeDtypeStruct((n, 3 * GDN_WIDTH), F32),
        jax.ShapeDtypeStruct((n, GDN_WIDTH), F32),
        jax.ShapeDtypeStruct((n, CONV_WIDTH), MXU_DTYPE),
        jax.ShapeDtypeStruct((n, FOX_WIDTH), MXU_DTYPE),
        jax.ShapeDtypeStruct((n, FOX_WIDTH), MXU_DTYPE),
        jax.ShapeDtypeStruct((n, FOX_WIDTH), MXU_DTYPE),
        jax.ShapeDtypeStruct((n, SMALL_COLS), F32),
    ]
    vec = const((1, CONV_WIDTH))
    return pl.pallas_call(
        functools.partial(_in_proj_kernel, tiles_per_seq=tiles_per_seq),
        grid=(n // tm,),
        in_specs=[pl.BlockSpec((tm, D_MODEL), row),
                  const((D_MODEL, _C_END)), const((D_MODEL, 2 * SMALL_COLS)),
                  const((GDN_SHORT_CONV, 3 * GDN_WIDTH)), const((CONV_KERNEL, CONV_WIDTH)), vec, vec, vec, vec],
        out_specs=[pl.BlockSpec((tm, o.shape[1]), row) for o in outs],
        out_shape=outs,
        scratch_shapes=[pltpu.VMEM((tm + SUBLANES, 3 * GDN_WIDTH), F32),
                        pltpu.VMEM((tm + CONV_HALO, CONV_WIDTH), F32),
                        pltpu.VMEM((SUBLANES - 1, tm + CONV_HALO - SUBLANES, CONV_WIDTH), F32)],
        compiler_params=_cparams("arbitrary"),
        name="in_proj",
    )(xf, w_main, w_small, gdn_conv_w, cnv_w, cnv_b, cnv_ln_g, cnv_ln_b, cnv_scale)


AUG_BIAS_LANE = HEAD_DIM


def _gates_kernel(small_ref, a_ref, dtb_ref, fb_ref, fq_ref, fk_ref, fv_ref,
                  g_ref, beta_ref, gt_ref, st_ref, qa_ref, ka_ref, va_ref, carry_ref):
    t = pl.program_id(1)
    tt = small_ref.shape[1]
    b16 = jnp.bfloat16

    @pl.when(t == 0)
    def _():
        carry_ref[...] = jnp.zeros_like(carry_ref)

    s = small_ref[0]
    log_decay = -jnp.exp(a_ref[...]) * _softplus(s + dtb_ref[...])
    beta = _sigmoid(s)
    log_f = _log_sigmoid(s + fb_ref[...])

    lane = lax.broadcasted_iota(jnp.int32, (1, LANES), 1)
    rr = lax.broadcasted_iota(jnp.int32, (LANES, LANES), 0)
    cc = lax.broadcasted_iota(jnp.int32, (LANES, LANES), 1)
    tri_chunk = ((cc <= rr) & (rr // GDN_CHUNK == cc // GDN_CHUNK)).astype(b16)
    both = jnp.where(lane < 2 * GDN_HEADS, log_decay, log_f)
    in_chunk = jnp.concatenate([_sel_lhs(tri_chunk, both[r0:r0 + LANES]) for r0 in range(0, tt, LANES)], axis=0)
    g_cum = in_chunk
    carry = carry_ref[...]
    pieces = []
    for r0 in range(0, tt, GDN_CHUNK):
        chunk = in_chunk[r0:r0 + GDN_CHUNK]
        pieces.append(chunk + carry)
        carry = carry + chunk[GDN_CHUNK - 1:GDN_CHUNK]
    c_cum = jnp.concatenate(pieces, axis=0)
    carry_ref[...] = carry

    er = lax.broadcasted_iota(jnp.int32, (LANES, 2 * GDN_WIDTH), 0)
    ec = lax.broadcasted_iota(jnp.int32, (LANES, 2 * GDN_WIDTH), 1) // HEAD_DIM
    expanded = _sel_rhs(jnp.where(lane < GDN_HEADS, g_cum, beta), (er == ec).astype(b16))
    g_ref[0] = expanded[:, 0:GDN_WIDTH]
    beta_ref[0] = expanded[:, GDN_WIDTH:2 * GDN_WIDTH]
    gt_ref[0] = g_cum.T[0:SUBLANES, :]

    head_ones = _head_ones().astype(b16)
    stat_rows = []
    for p in range(N_PAIRS):
        k_pair = fk_ref[0, :, p * PAIR:(p + 1) * PAIR].astype(F32)
        stat_rows.append(jnp.max(_dot_ones(k_pair * k_pair, head_ones), axis=0, keepdims=True))
    stat_rows.append(c_cum[tt - 1:tt, :])
    stat_rows.append(jnp.zeros((SUBLANES - len(stat_rows), LANES), F32))
    st_ref[0, 0] = jnp.concatenate(stat_rows, axis=0)

    q_ones = ((lane >= AUG_BIAS_LANE) & (lane < AUG_BIAS_LANE + 3)).astype(F32)
    v_one = (lane == AUG_BIAS_LANE).astype(F32)
    br = lax.broadcasted_iota(jnp.int32, (3 * LANES, FOX_HEADS * LANES), 0)
    bc = lax.broadcasted_iota(jnp.int32, (3 * LANES, FOX_HEADS * LANES), 1)
    place = ((br % LANES == 2 * GDN_HEADS + bc // LANES) & (bc % LANES == AUG_BIAS_LANE + br // LANES)).astype(b16)
    bias_all = jnp.dot(jnp.concatenate(_split3(-c_cum), axis=1), place, preferred_element_type=F32)
    upper_to_lower = ((cc < HEAD_DIM) & (rr == cc + HEAD_DIM)).astype(b16)
    for h in range(FOX_HEADS):
        p, half = divmod(h, 2)
        lanes = slice(p * PAIR, (p + 1) * PAIR)

        def pick(ref):
            if half == 0:
                return jnp.where(lane < HEAD_DIM, ref[0, :, lanes].astype(F32), 0.0)
            return jnp.dot(ref[0, :, lanes], upper_to_lower, preferred_element_type=F32)

        qa_ref[0, h] = (pick(fq_ref) + q_ones).astype(b16)
        ka_ref[0, h] = (pick(fk_ref) + bias_all[:, h * LANES:(h + 1) * LANES]).astype(b16)
        va_ref[0, h] = (pick(fv_ref) + v_one).astype(b16)


def _gates(small, a_row, dtb_row, fb_row, fq, fk, fv, tt):
    b, t, _ = small.shape
    row = pl.BlockSpec((1, LANES), lambda i, j: (0, 0))
    tile = lambda w: pl.BlockSpec((1, tt, w), lambda i, j: (i, j, 0))
    aug = pl.BlockSpec((1, FOX_HEADS, tt, LANES), lambda i, j: (i, 0, j, 0))
    aug_shape = jax.ShapeDtypeStruct((b, FOX_HEADS, t, LANES), jnp.bfloat16)
    return pl.pallas_call(
        _gates_kernel,
        grid=(b, t // tt),
        in_specs=[tile(SMALL_COLS), row, row, row, tile(FOX_WIDTH), tile(FOX_WIDTH), tile(FOX_WIDTH)],
        out_specs=[tile(GDN_WIDTH), tile(GDN_WIDTH), pl.BlockSpec((1, SUBLANES, tt), lambda i, j: (i, 0, j)),
                   pl.BlockSpec((1, 1, SUBLANES, LANES), lambda i, j: (i, j, 0, 0)), aug, aug, aug],
        out_shape=[jax.ShapeDtypeStruct((b, t, GDN_WIDTH), F32), jax.ShapeDtypeStruct((b, t, GDN_WIDTH), F32),
                   jax.ShapeDtypeStruct((b, SUBLANES, t), F32),
                   jax.ShapeDtypeStruct((b, t // tt, SUBLANES, LANES), F32), aug_shape, aug_shape, aug_shape],
        scratch_shapes=[pltpu.VMEM((1, LANES), F32)],
        compiler_params=_cparams("parallel", "arbitrary"),
        name="gates",
    )(small, a_row, dtb_row, fb_row, fq, fk, fv)


def _unit_lower_inverse_many(lows):
    n = lows[0].shape[0]
    r = lax.broadcasted_iota(jnp.int32, (n, n), 0)
    c = lax.broadcasted_iota(jnp.int32, (n, n), 1)
    eye = (r == c).astype(F32)
    base = GDN_CHUNK // 4
    diag = r // base == c // base
    cast = lambda xs: [x.astype(MXU_DTYPE) for x in xs]
    mm = lambda xs, ys: [jnp.dot(x, y, preferred_element_type=F32) for x, y in zip(xs, ys)]
    add = lambda xs, ys: [x + y for x, y in zip(xs, ys)]
    d = [jnp.where(diag, low, 0.0) for low in lows]
    db = cast(d)
    d2b = cast(mm(db, db))
    p = [eye - x for x in d]
    p = add(p, mm(cast(p), d2b))
    d4b = cast(mm(d2b, d2b))
    p = add(p, mm(cast(p), d4b))
    d8b = cast(mm(d4b, d4b))
    x = add(p, mm(cast(p), d8b))
    for blk in (2 * base, 4 * base):
        sel = (r // blk == c // blk) & (r // (blk // 2) != c // (blk // 2))
        xb = cast(x)
        xo = mm(xb, cast([jnp.where(sel, low, 0.0) for low in lows]))
        x = [a - b for a, b in zip(x, mm(cast(xo), xb))]
    return x


def _gdn_kernel(qkv_ref, g_ref, beta_ref, gt_ref, gate_ref, nw_ref, ms_ref, o_ref, s_ref):
    t = pl.program_id(1)
    blk = qkv_ref.shape[1]
    nc = blk // GDN_CHUNK

    @pl.when(t == 0)
    def _():
        s_ref[...] = jnp.zeros_like(s_ref)

    qkv = qkv_ref[0]

    ones = _head_ones().astype(jnp.bfloat16)
    lane = lax.broadcasted_iota(jnp.int32, (1, PAIR), 1)
    head0 = lane < HEAD_DIM
    n2 = 2 * GDN_CHUNK
    r = lax.broadcasted_iota(jnp.int32, (n2, n2), 0)
    c = lax.broadcasted_iota(jnp.int32, (n2, n2), 1)
    same = (r // GDN_CHUNK) == (c // GDN_CHUNK)
    causal = same & (c <= r)
    strict = same & (c < r)

    def l2n(v):
        return v * lax.rsqrt(_dot_ones(v * v, ones) + NORM_EPS)

    def stack(v):
        return jnp.concatenate([jnp.where(head0, v, 0.0), jnp.where(head0, 0.0, v)], axis=0)

    pair = lambda p, grp: slice(grp * GDN_WIDTH + p * PAIR, grp * GDN_WIDTH + (p + 1) * PAIR)
    q_p = [l2n(qkv[:, pair(p, 0)]) * HEAD_DIM ** -0.5 for p in range(N_PAIRS)]
    k_p = [l2n(qkv[:, pair(p, 1)]) for p in range(N_PAIRS)]
    v_p = [qkv[:, pair(p, 2)] for p in range(N_PAIRS)]

    units = [(ci, p) for ci in range(nc) for p in range(N_PAIRS)]
    rows = lambda ci: slice(ci * GDN_CHUNK, (ci + 1) * GDN_CHUNK)
    gc = [g_ref[0, rows(ci), pair(p, 0)] for ci, p in units]
    bt = [beta_ref[0, rows(ci), pair(p, 0)] for ci, p in units]
    qn = [q_p[p][rows(ci)] for ci, p in units]
    kn = [k_p[p][rows(ci)] for ci, p in units]
    vv = [v_p[p][rows(ci)] for ci, p in units]
    g_last = [g[GDN_CHUNK - 1:GDN_CHUNK, :] for g in gc]
    eg = [jnp.exp(g) for g in gc]
    kb = [k * b for k, b in zip(kn, bt)]
    k2 = [stack(k) for k in kn]

    def decay_of(g, ci, p):
        g_col = jnp.concatenate([jnp.broadcast_to(g[:, 0:1], (GDN_CHUNK, PAIR)),
                                 jnp.broadcast_to(g[:, HEAD_DIM:HEAD_DIM + 1], (GDN_CHUNK, PAIR))], axis=0)
        g_row = jnp.concatenate([gt_ref[0, 2 * p:2 * p + 1, rows(ci)], gt_ref[0, 2 * p + 1:2 * p + 2, rows(ci)]],
                                axis=1)
        return jnp.where(causal, jnp.exp(jnp.where(causal, g_col - g_row, 0.0)), 0.0)

    decay = [decay_of(g, ci, p) for g, (ci, p) in zip(gc, units)]
    k2b = [x.astype(MXU_DTYPE) for x in k2]
    low = [jnp.where(strict, _dot_nt(stack(a), b) * d, 0.0) for a, b, d in zip(kb, k2b, decay)]
    a_in = [(_dot_nt(stack(a), b) * d).astype(MXU_DTYPE) for a, b, d in zip(qn, k2b, decay)]
    t_inv = _unit_lower_inverse_many(low)
    uw = [_dot(ti, jnp.concatenate([stack(v * b), stack(a * e)], axis=1)).astype(MXU_DTYPE)
          for ti, v, b, a, e in zip(t_inv, vv, bt, kb, eg)]
    ket = [stack(k * jnp.exp(gl - g)).T for k, gl, g in zip(kn, g_last, gc)]
    nm = [_dot(a, b) for a, b in zip(ket, uw)]
    raw = [jnp.dot(a, b, preferred_element_type=F32) for a, b in zip(a_in, uw)]
    p_mat = [stack(q * e) - x[:, PAIR:] for q, e, x in zip(qn, eg, raw)]

    state = [s_ref[p] for p in range(N_PAIRS)]
    for i, (ci, p) in enumerate(units):
        s = state[p]
        o2 = _dot(p_mat[i], s) + raw[i][:, :PAIR]
        state[p] = s * jnp.exp(g_last[i]) - _dot(nm[i][:, PAIR:], s) + nm[i][:, :PAIR]
        o = o2[0:GDN_CHUNK] + o2[GDN_CHUNK:n2]
        ms = _dot_ones(o * o, ones) * (1.0 / HEAD_DIM)
        on = o * lax.rsqrt(ms + NORM_EPS) * nw_ref[...]
        o_ref[0, rows(ci), pair(p, 0)] = (on * _silu(gate_ref[0, rows(ci), pair(p, 0)])
                                          * ms_ref[:, pair(p, 0)]).astype(o_ref.dtype)
    for p in range(N_PAIRS):
        s_ref[p] = state[p]


def _gdn(qkv, g, beta, g_t, gate, norm_w2, mscale, blk):
    b, t, _ = qkv.shape
    tile = lambda width: pl.BlockSpec((1, blk, width), lambda i, j: (i, j, 0))
    const = lambda shape: pl.BlockSpec(shape, lambda i, j: (0, 0))
    return pl.pallas_call(
        _gdn_kernel,
        grid=(b, t // blk),
        in_specs=[tile(3 * GDN_WIDTH),
                  tile(GDN_WIDTH), tile(GDN_WIDTH), pl.BlockSpec((1, SUBLANES, blk), lambda i, j: (i, 0, j)),
                  tile(GDN_WIDTH), const((1, PAIR)), const((1, GDN_WIDTH))],
        out_specs=tile(GDN_WIDTH),
        out_shape=jax.ShapeDtypeStruct((b, t, GDN_WIDTH), MXU_DTYPE),
        scratch_shapes=[pltpu.VMEM((N_PAIRS, PAIR, PAIR), F32)],
        compiler_params=_cparams("parallel", "arbitrary"),
        name="gdn",
    )(qkv, g, beta, g_t, gate, norm_w2, mscale)


FOX_ROW_GROUP = 32


FOX_SKIP_MARGIN = 106.0
KEY_NORM_SLACK = 1.01


def _fox_kernel(q_ref, k_ref, v_ref, aux_ref, ms_ref, o_ref, s0_ref, s1_ref, p0_ref, p1_ref, a0_ref, a1_ref,
                m_ref, acc_ref, *, tq):
    i = pl.program_id(2)
    heads = range(2)
    rg = FOX_ROW_GROUP
    m_ref[...] = jnp.full(m_ref.shape, -jnp.inf, F32)
    acc_ref[...] = jnp.zeros(acc_ref.shape, F32)

    def scores(j, s_ref):
        start = pl.multiple_of(j * tq, tq)
        for h in heads:
            s_ref[h] = lax.dot_general(q_ref[0, h], k_ref[0, h, pl.ds(start, tq), :],
                                       (((1,), (1,)), ((), ())), preferred_element_type=F32)

    def softmax(s_ref, p_ref, a_ref, masked):
        for h in heads:
            for g in range(tq // rg):
                rows = slice(g * rg, (g + 1) * rg)
                width = LANES * (((g + 1) * rg - 1) // LANES + 1) if masked else tq
                s = s_ref[h, rows, 0:width]
                if masked:
                    row_id = g * rg + lax.broadcasted_iota(jnp.int32, (rg, width), 0)
                    col_id = lax.broadcasted_iota(jnp.int32, (rg, width), 1)
                    s = jnp.where(col_id <= row_id, s, -jnp.inf)
                    if width < tq:
                        p_ref[h, rows, width:tq] = jnp.zeros((rg, tq - width), p_ref.dtype)
                m_old = m_ref[h, rows, :]
                m_new = jnp.maximum(m_old, jnp.max(s, axis=-1, keepdims=True))
                a_ref[h, rows, :] = jnp.exp(m_old - m_new)
                m_ref[h, rows, :] = m_new
                m_wide = jnp.concatenate([m_new] * (width // LANES), axis=1)
                p_ref[h, rows, 0:width] = jnp.exp(s - m_wide).astype(p_ref.dtype)

    def weighted_values(j, p_ref, a_ref):
        start = pl.multiple_of(j * tq, tq)
        for h in heads:
            acc_ref[h] = a_ref[h] * acc_ref[h] + jnp.dot(p_ref[h], v_ref[0, h, pl.ds(start, tq), :],
                                                         preferred_element_type=F32)

    scores(i, s0_ref)
    softmax(s0_ref, p0_ref, a0_ref, True)
    scores(jnp.maximum(i - 1, 0), s1_ref)
    weighted_values(i, p0_ref, a0_ref)

    lane = lax.broadcasted_iota(jnp.int32, (1, LANES), 1)
    n = jnp.int32(0)
    for h in heads:
        q = q_ref[0, h].astype(F32)
        q_norm = jnp.sqrt(jnp.sum(jnp.where(lane < HEAD_DIM, q * q, 0.0), axis=-1, keepdims=True))
        key_norm = aux_ref[0, h, 1:2, :]
        block_end_c = aux_ref[0, h, 0:1, :]
        slack = jnp.max(q_norm * key_norm - m_ref[h], axis=0, keepdims=True)
        keep = (lane < i) & (slack - block_end_c >= -FOX_SKIP_MARGIN)
        n = jnp.maximum(n, jnp.sum(keep.astype(jnp.int32)))

    def body(t, carry):
        j = i - 1 - 2 * t
        softmax(s1_ref, p1_ref, a1_ref, False)
        scores(jnp.maximum(j - 1, 0), s0_ref)
        weighted_values(j, p1_ref, a1_ref)
        softmax(s0_ref, p0_ref, a0_ref, False)
        scores(jnp.maximum(j - 2, 0), s1_ref)
        weighted_values(j - 1, p0_ref, a0_ref)
        return carry

    lax.fori_loop(0, n // 2, body, 0)

    @pl.when(n % 2 == 1)
    def _():
        softmax(s1_ref, p1_ref, a1_ref, False)
        weighted_values(i - n, p1_ref, a1_ref)

    o = [acc_ref[h] / acc_ref[h][:, AUG_BIAS_LANE:AUG_BIAS_LANE + 1] for h in heads]
    lane = lax.broadcasted_iota(jnp.int32, (1, PAIR), 1)
    o_pair = jnp.where(lane < HEAD_DIM, o[0], pltpu.roll(o[1], HEAD_DIM, axis=1))
    o_ref[0] = (o_pair * ms_ref[...]).astype(o_ref.dtype)


def _fox_aux(stats):
    b, nt = stats.shape[:2]
    k_sq = jnp.max(stats[:, :, :N_PAIRS, :], axis=1)
    k_sq = k_sq.reshape(b, N_PAIRS, 2, HEAD_DIM)[..., 0].reshape(b, FOX_HEADS)
    key_norm = jnp.sqrt(k_sq) * KEY_NORM_SLACK
    block_end_c = jnp.transpose(stats[:, :, N_PAIRS, 2 * GDN_HEADS:2 * GDN_HEADS + FOX_HEADS], (0, 2, 1))
    aux = jnp.zeros((b, FOX_HEADS, SUBLANES, LANES), F32)
    aux = aux.at[:, :, 0, :nt].set(block_end_c)
    return aux.at[:, :, 1, :].set(jnp.broadcast_to(key_norm[:, :, None], (b, FOX_HEADS, LANES)))


def _fox(qa, ka, va, aux, mscale, tq):
    b, _, t, _ = qa.shape
    kern = functools.partial(_fox_kernel, tq=tq)
    return pl.pallas_call(
        kern,
        grid=(b, N_PAIRS, t // tq),
        in_specs=[pl.BlockSpec((1, 2, tq, LANES), lambda bi, p, i: (bi, p, i, 0)),
                  pl.BlockSpec((1, 2, t, LANES), lambda bi, p, i: (bi, p, 0, 0)),
                  pl.BlockSpec((1, 2, t, LANES), lambda bi, p, i: (bi, p, 0, 0)),
                  pl.BlockSpec((1, 2, SUBLANES, LANES), lambda bi, p, i: (bi, p, 0, 0)),
                  pl.BlockSpec((1, PAIR), lambda bi, p, i: (0, p))],
        out_specs=pl.BlockSpec((1, tq, PAIR), lambda bi, p, i: (bi, i, p)),
        out_shape=jax.ShapeDtypeStruct((b, t, FOX_WIDTH), MXU_DTYPE),
        scratch_shapes=[pltpu.VMEM((2, tq, tq), F32), pltpu.VMEM((2, tq, tq), F32),
                        pltpu.VMEM((2, tq, tq), jnp.bfloat16), pltpu.VMEM((2, tq, tq), jnp.bfloat16),
                        pltpu.VMEM((2, tq, LANES), F32), pltpu.VMEM((2, tq, LANES), F32),
                        pltpu.VMEM((2, tq, LANES), F32), pltpu.VMEM((2, tq, LANES), F32)],
        compiler_params=_cparams("parallel", "parallel", "arbitrary"),
        name="fox_attention",
    )(qa, ka, va, aux, mscale)


def _out_proj_kernel(oa_ref, ob_ref, oc_ref, x_ref, w_ref, g_ref, b_ref, o_ref):
    mix = jnp.dot(oa_ref[...], w_ref[0:GDN_WIDTH, :], preferred_element_type=F32)
    mix = mix + jnp.dot(ob_ref[...], w_ref[GDN_WIDTH:GDN_WIDTH + CONV_WIDTH, :], preferred_element_type=F32)
    mix = mix + jnp.dot(oc_ref[...], w_ref[GDN_WIDTH + CONV_WIDTH:D_MODEL, :], preferred_element_type=F32)
    o_ref[...] = _layer_norm_rows(DEEPNORM_ALPHA * x_ref[...] + mix, g_ref[...], b_ref[...])


def _out_proj(oa, ob, oc, xf, w, g, bvec, tm):
    n = xf.shape[0]
    row = lambda i: (i, 0)
    const = lambda i: (0, 0)
    return pl.pallas_call(
        _out_proj_kernel,
        grid=(n // tm,),
        in_specs=[pl.BlockSpec((tm, GDN_WIDTH), row), pl.BlockSpec((tm, CONV_WIDTH), row),
                  pl.BlockSpec((tm, FOX_WIDTH), row), pl.BlockSpec((tm, D_MODEL), row),
                  pl.BlockSpec((D_MODEL, D_MODEL), const),
                  pl.BlockSpec((1, D_MODEL), const), pl.BlockSpec((1, D_MODEL), const)],
        out_specs=pl.BlockSpec((tm, D_MODEL), row),
        out_shape=jax.ShapeDtypeStruct((n, D_MODEL), F32),
        compiler_params=_cparams("parallel"),
        name="out_proj_ln",
    )(oa, ob, oc, xf, w, g, bvec)


FFN_CHUNK = 256


def _swiglu_chunks(xb, w1, w3, w2, width):
    assert width % FFN_CHUNK == 0
    acc = None
    for c in range(width // FFN_CHUNK):
        cols = slice(c * FFN_CHUNK, (c + 1) * FFN_CHUNK)
        h = _silu(jnp.dot(xb, w1(slice(None), cols), preferred_element_type=F32)) * jnp.dot(
            xb, w3(slice(None), cols), preferred_element_type=F32)
        part = jnp.dot(h.astype(xb.dtype), w2(cols, slice(None)), preferred_element_type=F32)
        acc = part if acc is None else acc + part
    return acc


def _ffn_kernel(x_ref, w1_ref, w3_ref, w2_ref, g_ref, b_ref, o_ref):
    x = x_ref[...]
    ff = _swiglu_chunks(x.astype(w1_ref.dtype), lambda r, c: w1_ref[r, c], lambda r, c: w3_ref[r, c],
                        lambda r, c: w2_ref[r, c], w1_ref.shape[1])
    o_ref[...] = _layer_norm_rows(DEEPNORM_ALPHA * x + ff, g_ref[...], b_ref[...])


def _dense_ffn(xf, w1, w3, w2, g, bvec, tm):
    n = xf.shape[0]
    ff = w1.shape[1]
    const = lambda shape: pl.BlockSpec(shape, lambda i: (0, 0), pipeline_mode=pl.Buffered(1))
    return pl.pallas_call(
        _ffn_kernel,
        grid=(n // tm,),
        in_specs=[pl.BlockSpec((tm, D_MODEL), lambda i: (i, 0)),
                  const((D_MODEL, ff)), const((D_MODEL, ff)), const((ff, D_MODEL)),
                  const((1, D_MODEL)), const((1, D_MODEL))],
        out_specs=pl.BlockSpec((tm, D_MODEL), lambda i: (i, 0)),
        out_shape=jax.ShapeDtypeStruct((n, D_MODEL), F32),
        compiler_params=_cparams("parallel"),
        name="dense_ffn_ln",
    )(xf, w1, w3, w2, g, bvec)


def _router_kernel(x_ref, wr_ref, info_ref, cnt_ref, run_ref):
    i = pl.program_id(0)
    tm = x_ref.shape[0]

    @pl.when(i == 0)
    def _():
        run_ref[...] = jnp.zeros_like(run_ref)

    logits = jnp.dot(x_ref[...].astype(MXU_DTYPE), wr_ref[...], preferred_element_type=F32)
    lane = lax.broadcasted_iota(jnp.int32, (tm, LANES), 1)
    logits = jnp.where(lane < N_EXPERTS, logits, -jnp.inf)
    m1 = jnp.max(logits, axis=-1, keepdims=True)
    e1 = jnp.min(jnp.where(logits == m1, lane, LANES), axis=-1, keepdims=True)
    rest = jnp.where(lane == e1, -jnp.inf, logits)
    m2 = jnp.max(rest, axis=-1, keepdims=True)
    e2 = jnp.min(jnp.where(rest == m2, lane, LANES), axis=-1, keepdims=True)
    z = jnp.exp(m2 - m1)
    g1 = 1.0 / (1.0 + z)
    g2 = z / (1.0 + z)
    onehot = ((lane == e1) | (lane == e2)).astype(F32)
    r = lax.broadcasted_iota(jnp.int32, (tm, tm), 0)
    c = lax.broadcasted_iota(jnp.int32, (tm, tm), 1)
    before = jnp.dot((c < r).astype(jnp.bfloat16), onehot.astype(jnp.bfloat16),
                     preferred_element_type=F32) + run_ref[...]
    rank1 = jnp.sum(jnp.where(lane == e1, before, 0.0), axis=-1, keepdims=True)
    rank2 = jnp.sum(jnp.where(lane == e2, before, 0.0), axis=-1, keepdims=True)
    run_ref[...] = run_ref[...] + jnp.sum(onehot, axis=0, keepdims=True)
    cnt_ref[...] = run_ref[...]
    info = jnp.where(lane == 0, e1.astype(F32),
                     jnp.where(lane == 1, e2.astype(F32),
                               jnp.where(lane == 2, rank1,
                                         jnp.where(lane == 3, rank2,
                                                   jnp.where(lane == 4, g1, jnp.where(lane == 5, g2, 0.0))))))
    info_ref[...] = info


def _router(xf, wr_pad, tm):
    n = xf.shape[0]
    return pl.pallas_call(
        _router_kernel,
        grid=(n // tm,),
        in_specs=[pl.BlockSpec((tm, D_MODEL), lambda i: (i, 0)),
                  pl.BlockSpec((D_MODEL, LANES), lambda i: (0, 0))],
        out_specs=[pl.BlockSpec((tm, LANES), lambda i: (i, 0)),
                   pl.BlockSpec((1, LANES), lambda i: (0, 0))],
        out_shape=[jax.ShapeDtypeStruct((n, LANES), F32), jax.ShapeDtypeStruct((1, LANES), F32)],
        scratch_shapes=[pltpu.VMEM((1, LANES), F32)],
        compiler_params=_cparams("arbitrary"),
        name="moe_router",
    )(xf, wr_pad)


DMA_LOOP_UNROLL = 8


def _dispatch_kernel(fill_ref, d1_ref, d2_ref, x_ref, xs_ref, zero_ref, sem):
    tm = x_ref.shape[0]

    @pl.when(pl.program_id(0) == 0)
    def _():
        zero_ref[...] = jnp.zeros_like(zero_ref)

        def strip_copy(e, k):
            start = pl.multiple_of(fill_ref[e] - SUBLANES * (k + 1), SUBLANES)
            return pltpu.make_async_copy(zero_ref, xs_ref.at[pl.ds(start, SUBLANES), :], sem)

        for e in range(N_EXPERTS):
            lax.fori_loop(0, fill_ref[N_EXPERTS + e], lambda k, c, e=e: (strip_copy(e, k).start(), c)[1], 0)
        for e in range(N_EXPERTS):
            lax.fori_loop(0, fill_ref[N_EXPERTS + e], lambda k, c, e=e: (strip_copy(e, k).wait(), c)[1], 0)

    def row_copy(r, dst):
        return pltpu.make_async_copy(x_ref.at[pl.ds(r, 1), :], xs_ref.at[pl.ds(dst, 1), :], sem)

    def issue(r, carry):
        row_copy(r, d1_ref[0, 0, r]).start()
        row_copy(r, d2_ref[0, 0, r]).start()
        return carry

    lax.fori_loop(0, tm, issue, 0, unroll=DMA_LOOP_UNROLL)

    def drain(r, carry):
        row_copy(r, d1_ref[0, 0, r]).wait()
        row_copy(r, d2_ref[0, 0, r]).wait()
        return carry

    lax.fori_loop(0, tm, drain, 0, unroll=DMA_LOOP_UNROLL)


def _dispatch(xf, d1, d2, gap_start, gap_end, rows, tm):
    n = xf.shape[0]
    idx = lambda a: a.reshape(n // tm, 1, tm)
    n_strips = (gap_end - gap_start + SUBLANES - 1) // SUBLANES
    fill = jnp.concatenate([gap_end, n_strips]).astype(jnp.int32)
    smem = lambda: pl.BlockSpec((1, 1, tm), lambda i, fill: (i, 0, 0), memory_space=pltpu.SMEM)
    grid_spec = pltpu.PrefetchScalarGridSpec(
        num_scalar_prefetch=1,
        grid=(n // tm,),
        in_specs=[smem(), smem(), pl.BlockSpec((tm, D_MODEL), lambda i, fill: (i, 0))],
        out_specs=pl.BlockSpec(memory_space=pl.ANY),
        scratch_shapes=[pltpu.VMEM((SUBLANES, D_MODEL), xf.dtype), pltpu.SemaphoreType.DMA(())],
    )
    return pl.pallas_call(
        _dispatch_kernel,
        grid_spec=grid_spec,
        out_shape=jax.ShapeDtypeStruct((rows, D_MODEL), xf.dtype),
        compiler_params=_cparams("arbitrary"),
        name="moe_dispatch",
    )(fill, idx(d1), idx(d2), xf)


def _expert_kernel(te_ref, nu_ref, xs_ref, w1_ref, w3_ref, w2_ref, ys_ref, xb_ref):
    del te_ref
    i = pl.program_id(0)
    f = pl.program_id(1)
    used = i < nu_ref[0]

    def partial_ffn(xb):
        return _swiglu_chunks(xb, lambda r, c: w1_ref[0, r, c], lambda r, c: w3_ref[0, r, c],
                              lambda r, c: w2_ref[0, r, c], w1_ref.shape[2])

    @pl.when(used & (f == 0))
    def _():
        xb = xs_ref[...].astype(xb_ref.dtype)
        xb_ref[...] = xb
        ys_ref[...] = partial_ffn(xb)

    @pl.when(used & (f > 0))
    def _():
        ys_ref[...] += partial_ffn(xb_ref[...])

    @pl.when(jnp.logical_not(used) & (f == 0))
    def _():
        ys_ref[...] = jnp.zeros_like(ys_ref)


def _experts(xs, tile_expert, n_used, w1, w3, w2, tm, tf):
    rows = xs.shape[0]
    ff = w1.shape[2]
    n_tiles = rows // tm

    def x_map(i, f, te, nu):
        return (jnp.maximum(jnp.minimum(i, nu[0] - 1), 0), 0)

    def w13_map(i, f, te, nu):
        return (te[i], 0, jnp.where(i < nu[0], f, ff // tf - 1))

    def w2_map(i, f, te, nu):
        return (te[i], jnp.where(i < nu[0], f, ff // tf - 1), 0)

    grid_spec = pltpu.PrefetchScalarGridSpec(
        num_scalar_prefetch=2,
        grid=(n_tiles, ff // tf),
        in_specs=[pl.BlockSpec((tm, D_MODEL), x_map),
                  pl.BlockSpec((1, D_MODEL, tf), w13_map),
                  pl.BlockSpec((1, D_MODEL, tf), w13_map),
                  pl.BlockSpec((1, tf, D_MODEL), w2_map)],
        out_specs=pl.BlockSpec((tm, D_MODEL), lambda i, f, te, nu: (i, 0)),
        scratch_shapes=[pltpu.VMEM((tm, D_MODEL), MXU_DTYPE)],
    )
    return pl.pallas_call(
        _expert_kernel,
        grid_spec=grid_spec,
        out_shape=jax.ShapeDtypeStruct((rows, D_MODEL), F32),
        compiler_params=_cparams("arbitrary", "arbitrary"),
        name="moe_experts",
    )(tile_expert, n_used, xs, w1, w3, w2)


def _combine_kernel(d1_ref, d2_ref, x_ref, info_ref, ys_ref, g_ref, b_ref, o_ref, y1_ref, y2_ref, sem):
    tm = x_ref.shape[0]

    def row_copy(src, r, buf):
        return pltpu.make_async_copy(ys_ref.at[pl.ds(src, 1), :], buf.at[pl.ds(r, 1), :], sem)

    def issue(r, carry):
        row_copy(d1_ref[0, 0, r], r, y1_ref).start()
        row_copy(d2_ref[0, 0, r], r, y2_ref).start()
        return carry

    lax.fori_loop(0, tm, issue, 0, unroll=DMA_LOOP_UNROLL)

    def drain(r, carry):
        row_copy(d1_ref[0, 0, r], r, y1_ref).wait()
        row_copy(d2_ref[0, 0, r], r, y2_ref).wait()
        return carry

    lax.fori_loop(0, tm, drain, 0, unroll=DMA_LOOP_UNROLL)
    info = info_ref[...]
    ff = info[:, 4:5] * y1_ref[...] + info[:, 5:6] * y2_ref[...]
    o_ref[...] = _layer_norm_rows(DEEPNORM_ALPHA * x_ref[...] + ff, g_ref[...], b_ref[...])


def _combine(xf, info, ys, d1, d2, g, bvec, tm):
    n = xf.shape[0]
    idx = lambda a: a.reshape(n // tm, 1, tm)
    smem = lambda: pl.BlockSpec((1, 1, tm), lambda i: (i, 0, 0), memory_space=pltpu.SMEM)
    return pl.pallas_call(
        _combine_kernel,
        grid=(n // tm,),
        in_specs=[smem(), smem(),
                  pl.BlockSpec((tm, D_MODEL), lambda i: (i, 0)),
                  pl.BlockSpec((tm, LANES), lambda i: (i, 0)),
                  pl.BlockSpec(memory_space=pl.ANY),
                  pl.BlockSpec((1, D_MODEL), lambda i: (0, 0)),
                  pl.BlockSpec((1, D_MODEL), lambda i: (0, 0))],
        out_specs=pl.BlockSpec((tm, D_MODEL), lambda i: (i, 0)),
        out_shape=jax.ShapeDtypeStruct((n, D_MODEL), F32),
        scratch_shapes=[pltpu.VMEM((tm, D_MODEL), F32), pltpu.VMEM((tm, D_MODEL), F32),
                        pltpu.SemaphoreType.DMA(())],
        compiler_params=_cparams("arbitrary"),
        name="moe_combine_ln",
    )(idx(d1), idx(d2), xf, info, ys, g, bvec)


MOE_TILE = 512


def _moe_ffn(xf, w_router, w1, w3, w2, g, bvec, tm_tok, tf):
    n = xf.shape[0]
    wr_pad = jnp.zeros((D_MODEL, LANES), MXU_DTYPE).at[:, :N_EXPERTS].set(w_router.astype(MXU_DTYPE))
    info, counts = _router(xf, wr_pad, tm_tok)
    sizes = counts[0, :N_EXPERTS].astype(jnp.int32)
    tiles_per = (sizes + MOE_TILE - 1) // MOE_TILE
    tile_end = jnp.cumsum(tiles_per)
    seg_start = (tile_end - tiles_per) * MOE_TILE
    e1 = info[:, 0].astype(jnp.int32)
    e2 = info[:, 1].astype(jnp.int32)
    d1 = seg_start[e1] + info[:, 2].astype(jnp.int32)
    d2 = seg_start[e2] + info[:, 3].astype(jnp.int32)
    n_tiles = (n * TOP_K) // MOE_TILE + N_EXPERTS
    rows = n_tiles * MOE_TILE
    tile_expert = jnp.minimum(jnp.searchsorted(tile_end, jnp.arange(n_tiles, dtype=jnp.int32), side='right'),
                              N_EXPERTS - 1).astype(jnp.int32)
    n_used = tile_end[N_EXPERTS - 1:].astype(jnp.int32)
    gap_end = jnp.concatenate([seg_start[1:], jnp.full((1,), rows, jnp.int32)])
    xs = _dispatch(xf, d1, d2, seg_start + sizes, gap_end, rows, tm_tok)
    ys = _experts(xs, tile_expert, n_used, w1, w3, w2, MOE_TILE, tf)
    return _combine(xf, info, ys, d1, d2, g, bvec, tm_tok)


def _pack_in_proj(w_in):
    cuts = np.cumsum([0, GDN_WIDTH, GDN_WIDTH, GDN_WIDTH, GDN_HEADS, GDN_HEADS, GDN_WIDTH,
                      2 * CONV_WIDTH, FOX_WIDTH, FOX_WIDTH, FOX_WIDTH, FOX_HEADS])
    seg = lambda i: w_in[:, cuts[i]:cuts[i + 1]]
    w_main = jnp.concatenate([seg(0), seg(1), seg(2), seg(5), seg(6), seg(7), seg(8), seg(9)], axis=1)
    pad = jnp.zeros((D_MODEL, SMALL_COLS - 2 * GDN_HEADS - FOX_HEADS), w_in.dtype)
    w_small = jnp.concatenate([seg(3), seg(4), seg(10), pad], axis=1).astype(F32)
    ws_hi = w_small.astype(MXU_DTYPE)
    ws_lo = (w_small - ws_hi.astype(F32)).astype(MXU_DTYPE)
    return w_main.astype(MXU_DTYPE), jnp.concatenate([ws_hi, ws_lo], axis=1)


def _lane_row(vals, offset):
    return jnp.zeros((1, LANES), F32).at[0, offset:offset + vals.shape[0]].set(vals.astype(F32))


def _mixer(xf, b, t, w_in, mix_scale, w_out, gdn_conv_w, gdn_a_log, gdn_dt_bias, gdn_norm_w,
           cnv_dw_w, cnv_dw_b, cnv_ln_g, cnv_ln_b, fox_f_bias, ln_g, ln_b, tiles):
    w_main, w_small = _pack_in_proj(w_in)
    ms = mix_scale.reshape(1, D_MODEL).astype(F32)
    row = lambda v: v.reshape(1, -1).astype(F32)
    qkv, gate, o_b, fq, fk, fv, small = _in_proj(
        xf, w_main, w_small, gdn_conv_w, cnv_dw_w, row(cnv_dw_b), row(cnv_ln_g), row(cnv_ln_b),
        ms[:, GDN_WIDTH:GDN_WIDTH + CONV_WIDTH], tiles["tm"], t // tiles["tm"])
    r3 = lambda a: a.reshape(b, t, a.shape[-1])
    assert tiles["tt"] == tiles["tq"]
    g_cum, beta, g_t, stats, qa, ka, va = _gates(
        r3(small), _lane_row(gdn_a_log, 0), _lane_row(gdn_dt_bias, 0), _lane_row(fox_f_bias, 2 * GDN_HEADS),
        r3(fq), r3(fk), r3(fv), tiles["tt"])
    norm_w2 = jnp.tile(gdn_norm_w.astype(F32), 2).reshape(1, PAIR)
    o_a = _gdn(r3(qkv), g_cum, beta, g_t, r3(gate), norm_w2, ms[:, :GDN_WIDTH], tiles["gdn_blk"])
    o_c = _fox(qa, ka, va, _fox_aux(stats), ms[:, GDN_WIDTH + CONV_WIDTH:], tiles["tq"])
    flat = lambda a: a.reshape(b * t, a.shape[-1])
    return _out_proj(flat(o_a), o_b, flat(o_c), xf, w_out.astype(MXU_DTYPE), row(ln_g), row(ln_b),
                     tiles["tm"])


def _tiles(t):
    return dict(tm=min(512, t), tt=min(512, t), gdn_blk=min(512, t), tq=min(512, t),
                tf_moe=1792)


def kernel(x, w_in, mix_scale, w_out, gdn_conv_w, gdn_a_log, gdn_dt_bias, gdn_norm_w, cnv_dw_w, cnv_dw_b,
           cnv_ln_g, cnv_ln_b, fox_f_bias, ln_mix_g, ln_mix_b, ln_ffn_g, ln_ffn_b, ffn_w1, ffn_w3, ffn_w2,
           moe_router, moe_w1, moe_w3, moe_w2):
    b, t, d = x.shape
    tiles = _tiles(t)
    row = lambda v: v.reshape(1, -1).astype(F32)
    xf = x.reshape(b * t, d)
    for l in range(DEPTH):
        xf = _mixer(xf, b, t, w_in[l], mix_scale[l], w_out[l], gdn_conv_w[l], gdn_a_log[l], gdn_dt_bias[l],
                    gdn_norm_w[l], cnv_dw_w[l], cnv_dw_b[l], cnv_ln_g[l], cnv_ln_b[l], fox_f_bias[l],
                    ln_mix_g[l], ln_mix_b[l], tiles)
        if l % 2 == 0:
            e = l // 2
            xf = _dense_ffn(xf, ffn_w1[e].astype(MXU_DTYPE), ffn_w3[e].astype(MXU_DTYPE),
                            ffn_w2[e].astype(MXU_DTYPE), row(ln_ffn_g[l]), row(ln_ffn_b[l]),
                            tiles["tm"])
        else:
            e = l // 2
            xf = _moe_ffn(xf, moe_router[e], moe_w1[e].astype(MXU_DTYPE), moe_w3[e].astype(MXU_DTYPE),
                          moe_w2[e].astype(MXU_DTYPE), row(ln_ffn_g[l]), row(ln_ffn_b[l]),
                          tiles["tm"], tiles["tf_moe"])
    return xf.reshape(b, t, d)
```

```python
import functools

import jax
import jax.numpy as jnp
import numpy as np
from jax import lax
from jax.experimental import pallas as pl
from jax.experimental.pallas import tpu as pltpu

D_MODEL = 1024
DEPTH = 2
HEAD_DIM = 64
GDN_WIDTH = 384
CONV_WIDTH = 256
FOX_WIDTH = 384
GDN_HEADS = 6
FOX_HEADS = 6
GDN_SHORT_CONV = 4
GDN_CHUNK = 64
CONV_KERNEL = 31
FFN_DENSE = 2816
N_EXPERTS = 8
TOP_K = 2
FFN_EXPERT = 3584
DEEPNORM_ALPHA = (2 * DEPTH) ** 0.25
LN_EPS = 1e-5
NORM_EPS = 1e-6

LANES = 128
SUBLANES = 8
PAIR = 2 * HEAD_DIM
N_PAIRS = GDN_WIDTH // PAIR
SMALL_COLS = LANES

_C_QKV = 0
_C_GATE = _C_QKV + 3 * GDN_WIDTH
_C_GLU = _C_GATE + GDN_WIDTH
_C_FQ = _C_GLU + 2 * CONV_WIDTH
_C_FK = _C_FQ + FOX_WIDTH
_C_FV = _C_FK + FOX_WIDTH
_C_END = _C_FV + FOX_WIDTH

MXU_DTYPE = jnp.bfloat16
F32 = jnp.float32

VMEM_LIMIT = 56 * 1024 * 1024


def _cparams(*sem):
    return pltpu.CompilerParams(dimension_semantics=sem, vmem_limit_bytes=VMEM_LIMIT)


def _dot(a, b):
    return jnp.dot(a.astype(MXU_DTYPE), b.astype(MXU_DTYPE), preferred_element_type=F32)


def _dot_nt(a, b):
    return lax.dot_general(a.astype(MXU_DTYPE), b.astype(MXU_DTYPE),
                           (((1,), (1,)), ((), ())), preferred_element_type=F32)


def _split(a):
    hi = a.astype(jnp.bfloat16)
    return hi, (a - hi.astype(F32)).astype(jnp.bfloat16)


def _split3(a):
    hi = a.astype(jnp.bfloat16)
    rest = a - hi.astype(F32)
    mid = rest.astype(jnp.bfloat16)
    lo = (rest - mid.astype(F32)).astype(jnp.bfloat16)
    return hi, mid, lo


def _dot_ones(a, ones_b16):
    ah, al = _split(a)
    d = lambda x: jnp.dot(x, ones_b16, preferred_element_type=F32)
    return d(ah) + d(al)


def _sel_rhs(a, sel_b16):
    return sum(jnp.dot(t, sel_b16, preferred_element_type=F32) for t in _split3(a))


def _sel_lhs(sel_b16, b):
    return sum(jnp.dot(sel_b16, t, preferred_element_type=F32) for t in _split3(b))


def _sigmoid(x):
    return 1.0 / (1.0 + jnp.exp(-x))


def _silu(x):
    return x * _sigmoid(x)


def _softplus(x):
    return jnp.maximum(x, 0.0) + jnp.log1p(jnp.exp(-jnp.abs(x)))


def _log_sigmoid(x):
    return -_softplus(-x)


def _layer_norm_rows(y, g, b):
    mu = jnp.mean(y, axis=-1, keepdims=True)
    d = y - mu
    var = jnp.mean(d * d, axis=-1, keepdims=True)
    return d * lax.rsqrt(var + LN_EPS) * g + b


def _head_ones():
    r = lax.broadcasted_iota(jnp.int32, (PAIR, PAIR), 0) // HEAD_DIM
    c = lax.broadcasted_iota(jnp.int32, (PAIR, PAIR), 1) // HEAD_DIM
    return (r == c).astype(F32)


CONV_HALO = 32
CONV_ROW_BLOCK = 128
SHORT_CONV_ROW_BLOCK = 128


def _in_proj_kernel(x_ref, w_ref, ws_ref, gw_ref, cw_ref, cb_ref, lg_ref, lb_ref, ms_ref,
                    qkv_ref, gate_ref, ob_ref, fq_ref, fk_ref, fv_ref, small_ref,
                    qbuf_ref, cbuf_ref, sh_ref, *, tiles_per_seq):
    first = pl.program_id(0) % tiles_per_seq == 0
    tm = x_ref.shape[0]
    x = x_ref[...]
    xb = x.astype(MXU_DTYPE)

    def mm(lo, hi):
        return jnp.dot(xb, w_ref[:, lo:hi], preferred_element_type=F32)

    @pl.when(first)
    def _():
        cbuf_ref[0:CONV_HALO, :] = jnp.zeros((CONV_HALO, CONV_WIDTH), F32)
        qbuf_ref[0:SUBLANES, :] = jnp.zeros((SUBLANES, 3 * GDN_WIDTH), F32)

    @pl.when(jnp.logical_not(first))
    def _():
        cbuf_ref[0:CONV_HALO, :] = cbuf_ref[tm:tm + CONV_HALO, :]
        qbuf_ref[0:SUBLANES, :] = qbuf_ref[tm:tm + SUBLANES, :]

    glu = mm(_C_GLU, _C_FQ)
    cbuf_ref[CONV_HALO:CONV_HALO + tm, :] = glu[:, 0:CONV_WIDTH] * _sigmoid(glu[:, CONV_WIDTH:2 * CONV_WIDTH])
    span = tm + CONV_HALO - SUBLANES
    for s in range(1, SUBLANES):
        sh_ref[s - 1] = cbuf_ref[s:s + span, :]
    for r0 in range(0, tm, CONV_ROW_BLOCK):
        acc = jnp.zeros((CONV_ROW_BLOCK, CONV_WIDTH), F32) + cb_ref[...]
        for j in range(CONV_KERNEL):
            lo = r0 + CONV_HALO - (CONV_KERNEL - 1) + j
            base, phase = lo - lo % SUBLANES, lo % SUBLANES
            tap = (cbuf_ref[base:base + CONV_ROW_BLOCK, :] if phase == 0
                   else sh_ref[phase - 1, base:base + CONV_ROW_BLOCK, :])
            acc = acc + cw_ref[j:j + 1, :] * tap
        y = _silu(_layer_norm_rows(acc, lg_ref[...], lb_ref[...])) * ms_ref[...]
        ob_ref[r0:r0 + CONV_ROW_BLOCK, :] = y.astype(ob_ref.dtype)

    qbuf_ref[SUBLANES:SUBLANES + tm, :] = mm(_C_QKV, _C_GATE)
    for r0 in range(0, tm, SHORT_CONV_ROW_BLOCK):
        acc = jnp.zeros((SHORT_CONV_ROW_BLOCK, 3 * GDN_WIDTH), F32)
        for j in range(GDN_SHORT_CONV):
            lo = r0 + SUBLANES - (GDN_SHORT_CONV - 1) + j
            acc = acc + gw_ref[j:j + 1, :] * qbuf_ref[lo:lo + SHORT_CONV_ROW_BLOCK, :]
        qkv_ref[r0:r0 + SHORT_CONV_ROW_BLOCK, :] = _silu(acc)

    gate_ref[...] = mm(_C_GATE, _C_GLU)
    fq_ref[...] = (mm(_C_FQ, _C_FK) * HEAD_DIM ** -0.5).astype(fq_ref.dtype)
    fk_ref[...] = mm(_C_FK, _C_FV).astype(fk_ref.dtype)
    fv_ref[...] = mm(_C_FV, _C_END).astype(fv_ref.dtype)
    x_lo = (x - xb.astype(F32)).astype(MXU_DTYPE)
    hh_hl = jnp.dot(xb, ws_ref[...], preferred_element_type=F32)
    lh = jnp.dot(x_lo, ws_ref[:, 0:SMALL_COLS], preferred_element_type=F32)
    small_ref[...] = hh_hl[:, 0:SMALL_COLS] + hh_hl[:, SMALL_COLS:2 * SMALL_COLS] + lh


def _in_proj(xf, w_main, w_small, gdn_conv_w, cnv_w, cnv_b, cnv_ln_g, cnv_ln_b, cnv_scale, tm, tiles_per_seq):
    n = xf.shape[0]
    row = lambda i: (i, 0)
    const = lambda shape: pl.BlockSpec(shape, lambda i: (0, 0))
    outs = [
        jax.ShapeDtypeStruct((n, 3 * GDN_WIDTH), F32),
        jax.ShapeDtypeStruct((n, GDN_WIDTH), F32),
        jax.ShapeDtypeStruct((n, CONV_WIDTH), MXU_DTYPE),
        jax.ShapeDtypeStruct((n, FOX_WIDTH), MXU_DTYPE),
        jax.ShapeDtypeStruct((n, FOX_WIDTH), MXU_DTYPE),
        jax.ShapeDtypeStruct((n, FOX_WIDTH), MXU_DTYPE),
        jax.ShapeDtypeStruct((n, SMALL_COLS), F32),
    ]
    vec = const((1, CONV_WIDTH))
    return pl.pallas_call(
        functools.partial(_in_proj_kernel, tiles_per_seq=tiles_per_seq),
        grid=(n // tm,),
        in_specs=[pl.BlockSpec((tm, D_MODEL), row),
                  const((D_MODEL, _C_END)), const((D_MODEL, 2 * SMALL_COLS)),
                  const((GDN_SHORT_CONV, 3 * GDN_WIDTH)), const((CONV_KERNEL, CONV_WIDTH)), vec, vec, vec, vec],
        out_specs=[pl.BlockSpec((tm, o.shape[1]), row) for o in outs],
        out_shape=outs,
        scratch_shapes=[pltpu.VMEM((tm + SUBLANES, 3 * GDN_WIDTH), F32),
                        pltpu.VMEM((tm + CONV_HALO, CONV_WIDTH), F32),
                        pltpu.VMEM((SUBLANES - 1, tm + CONV_HALO - SUBLANES, CONV_WIDTH), F32)],
        compiler_params=_cparams("arbitrary"),
        name="in_proj",
    )(xf, w_main, w_small, gdn_conv_w, cnv_w, cnv_b, cnv_ln_g, cnv_ln_b, cnv_scale)


AUG_BIAS_LANE = HEAD_DIM
Q_NORM_LANE = AUG_BIAS_LANE + 3
KEY_NORM_SLACK = 1.01


def _gates_kernel(small_ref, a_ref, dtb_ref, fb_ref, fq_ref, fk_ref, fv_ref,
                  g_ref, beta_ref, gt_ref, st_ref, qa_ref, ka_ref, va_ref, carry_ref):
    t = pl.program_id(1)
    tt = small_ref.shape[1]
    b16 = jnp.bfloat16

    @pl.when(t == 0)
    def _():
        carry_ref[...] = jnp.zeros_like(carry_ref)

    s = small_ref[0]
    log_decay = -jnp.exp(a_ref[...]) * _softplus(s + dtb_ref[...])
    beta = _sigmoid(s)
    log_f = _log_sigmoid(s + fb_ref[...])

    lane = lax.broadcasted_iota(jnp.int32, (1, LANES), 1)
    rr = lax.broadcasted_iota(jnp.int32, (LANES, LANES), 0)
    cc = lax.broadcasted_iota(jnp.int32, (LANES, LANES), 1)
    tri_chunk = ((cc <= rr) & (rr // GDN_CHUNK == cc // GDN_CHUNK)).astype(b16)
    both = jnp.where(lane < 2 * GDN_HEADS, log_decay, log_f)
    in_chunk = jnp.concatenate([_sel_lhs(tri_chunk, both[r0:r0 + LANES]) for r0 in range(0, tt, LANES)], axis=0)
    g_cum = in_chunk
    carry = carry_ref[...]
    pieces = []
    for r0 in range(0, tt, GDN_CHUNK):
        chunk = in_chunk[r0:r0 + GDN_CHUNK]
        pieces.append(chunk + carry)
        carry = carry + chunk[GDN_CHUNK - 1:GDN_CHUNK]
    c_cum = jnp.concatenate(pieces, axis=0)
    carry_ref[...] = carry

    er = lax.broadcasted_iota(jnp.int32, (LANES, 2 * GDN_WIDTH), 0)
    ec = lax.broadcasted_iota(jnp.int32, (LANES, 2 * GDN_WIDTH), 1) // HEAD_DIM
    expanded = _sel_rhs(jnp.where(lane < GDN_HEADS, g_cum, beta), (er == ec).astype(b16))
    g_ref[0] = expanded[:, 0:GDN_WIDTH]
    beta_ref[0] = expanded[:, GDN_WIDTH:2 * GDN_WIDTH]
    gt_ref[0] = g_cum.T[0:SUBLANES, :]

    head_ones = _head_ones().astype(b16)
    stat_rows = []
    for p in range(N_PAIRS):
        k_pair = fk_ref[0, :, p * PAIR:(p + 1) * PAIR].astype(F32)
        stat_rows.append(jnp.max(_dot_ones(k_pair * k_pair, head_ones), axis=0, keepdims=True))
    stat_rows.append(c_cum[tt - 1:tt, :])
    stat_rows.append(jnp.zeros((SUBLANES - len(stat_rows), LANES), F32))
    st_ref[0, 0] = jnp.concatenate(stat_rows, axis=0)

    q_ones = ((lane >= AUG_BIAS_LANE) & (lane < AUG_BIAS_LANE + 3)).astype(F32)
    v_one = (lane == AUG_BIAS_LANE).astype(F32)
    br = lax.broadcasted_iota(jnp.int32, (3 * LANES, FOX_HEADS * LANES), 0)
    bc = lax.broadcasted_iota(jnp.int32, (3 * LANES, FOX_HEADS * LANES), 1)
    place = ((br % LANES == 2 * GDN_HEADS + bc // LANES) & (bc % LANES == AUG_BIAS_LANE + br // LANES)).astype(b16)
    bias_all = jnp.dot(jnp.concatenate(_split3(-c_cum), axis=1), place, preferred_element_type=F32)
    upper_to_lower = ((cc < HEAD_DIM) & (rr == cc + HEAD_DIM)).astype(b16)
    for h in range(FOX_HEADS):
        p, half = divmod(h, 2)
        lanes = slice(p * PAIR, (p + 1) * PAIR)

        def pick(ref):
            if half == 0:
                return jnp.where(lane < HEAD_DIM, ref[0, :, lanes].astype(F32), 0.0)
            return jnp.dot(ref[0, :, lanes], upper_to_lower, preferred_element_type=F32)

        q_h = pick(fq_ref)
        q_norm = jnp.sqrt(jnp.sum(q_h * q_h, axis=-1, keepdims=True)) * KEY_NORM_SLACK
        qa_ref[0, h] = (q_h + q_ones + jnp.where(lane == Q_NORM_LANE, q_norm, 0.0)).astype(b16)
        ka_ref[0, h] = (pick(fk_ref) + bias_all[:, h * LANES:(h + 1) * LANES]).astype(b16)
        va_ref[0, h] = (pick(fv_ref) + v_one).astype(b16)


def _gates(small, a_row, dtb_row, fb_row, fq, fk, fv, tt):
    b, t, _ = small.shape
    row = pl.BlockSpec((1, LANES), lambda i, j: (0, 0))
    tile = lambda w: pl.BlockSpec((1, tt, w), lambda i, j: (i, j, 0))
    aug = pl.BlockSpec((1, FOX_HEADS, tt, LANES), lambda i, j: (i, 0, j, 0))
    aug_shape = jax.ShapeDtypeStruct((b, FOX_HEADS, t, LANES), jnp.bfloat16)
    return pl.pallas_call(
        _gates_kernel,
        grid=(b, t // tt),
        in_specs=[tile(SMALL_COLS), row, row, row, tile(FOX_WIDTH), tile(FOX_WIDTH), tile(FOX_WIDTH)],
        out_specs=[tile(GDN_WIDTH), tile(GDN_WIDTH), pl.BlockSpec((1, SUBLANES, tt), lambda i, j: (i, 0, j)),
                   pl.BlockSpec((1, 1, SUBLANES, LANES), lambda i, j: (i, j, 0, 0)), aug, aug, aug],
        out_shape=[jax.ShapeDtypeStruct((b, t, GDN_WIDTH), F32), jax.ShapeDtypeStruct((b, t, GDN_WIDTH), F32),
                   jax.ShapeDtypeStruct((b, SUBLANES, t), F32),
                   jax.ShapeDtypeStruct((b, t // tt, SUBLANES, LANES), F32), aug_shape, aug_shape, aug_shape],
        scratch_shapes=[pltpu.VMEM((1, LANES), F32)],
        compiler_params=_cparams("parallel", "arbitrary"),
        name="gates",
    )(small, a_row, dtb_row, fb_row, fq, fk, fv)


def _unit_lower_inverse_many(lows):
    n = lows[0].shape[0]
    r = lax.broadcasted_iota(jnp.int32, (n, n), 0)
    c = lax.broadcasted_iota(jnp.int32, (n, n), 1)
    eye = (r == c).astype(F32)
    base = GDN_CHUNK // 4
    diag = r // base == c // base
    cast = lambda xs: [x.astype(MXU_DTYPE) for x in xs]
    mm = lambda xs, ys: [jnp.dot(x, y, preferred_element_type=F32) for x, y in zip(xs, ys)]
    add = lambda xs, ys: [x + y for x, y in zip(xs, ys)]
    d = [jnp.where(diag, low, 0.0) for low in lows]
    db = cast(d)
    d2b = cast(mm(db, db))
    p = [eye - x for x in d]
    p = add(p, mm(cast(p), d2b))
    d4b = cast(mm(d2b, d2b))
    p = add(p, mm(cast(p), d4b))
    d8b = cast(mm(d4b, d4b))
    x = add(p, mm(cast(p), d8b))
    for blk in (2 * base, 4 * base):
        sel = (r // blk == c // blk) & (r // (blk // 2) != c // (blk // 2))
        xb = cast(x)
        xo = mm(xb, cast([jnp.where(sel, low, 0.0) for low in lows]))
        x = [a - b for a, b in zip(x, mm(cast(xo), xb))]
    return x


def _gdn_kernel(qkv_ref, g_ref, beta_ref, gt_ref, gate_ref, nw_ref, ms_ref, o_ref, s_ref):
    t = pl.program_id(1)
    blk = qkv_ref.shape[1]
    nc = blk // GDN_CHUNK

    @pl.when(t == 0)
    def _():
        s_ref[...] = jnp.zeros_like(s_ref)

    qkv = qkv_ref[0]

    ones = _head_ones().astype(jnp.bfloat16)
    lane = lax.broadcasted_iota(jnp.int32, (1, PAIR), 1)
    head0 = lane < HEAD_DIM
    n2 = 2 * GDN_CHUNK
    r = lax.broadcasted_iota(jnp.int32, (n2, n2), 0)
    c = lax.broadcasted_iota(jnp.int32, (n2, n2), 1)
    same = (r // GDN_CHUNK) == (c // GDN_CHUNK)
    causal = same & (c <= r)
    strict = same & (c < r)

    def l2n(v):
        return v * lax.rsqrt(_dot_ones(v * v, ones) + NORM_EPS)

    def stack(v):
        return jnp.concatenate([jnp.where(head0, v, 0.0), jnp.where(head0, 0.0, v)], axis=0)

    pair = lambda p, grp: slice(grp * GDN_WIDTH + p * PAIR, grp * GDN_WIDTH + (p + 1) * PAIR)
    q_p = [l2n(qkv[:, pair(p, 0)]) * HEAD_DIM ** -0.5 for p in range(N_PAIRS)]
    k_p = [l2n(qkv[:, pair(p, 1)]) for p in range(N_PAIRS)]
    v_p = [qkv[:, pair(p, 2)] for p in range(N_PAIRS)]

    units = [(ci, p) for ci in range(nc) for p in range(N_PAIRS)]
    rows = lambda ci: slice(ci * GDN_CHUNK, (ci + 1) * GDN_CHUNK)
    gc = [g_ref[0, rows(ci), pair(p, 0)] for ci, p in units]
    bt = [beta_ref[0, rows(ci), pair(p, 0)] for ci, p in units]
    qn = [q_p[p][rows(ci)] for ci, p in units]
    kn = [k_p[p][rows(ci)] for ci, p in units]
    vv = [v_p[p][rows(ci)] for ci, p in units]
    g_last = [g[GDN_CHUNK - 1:GDN_CHUNK, :] for g in gc]
    eg = [jnp.exp(g) for g in gc]
    kb = [k * b for k, b in zip(kn, bt)]
    k2 = [stack(k) for k in kn]

    def decay_of(g, ci, p):
        g_col = jnp.concatenate([jnp.broadcast_to(g[:, 0:1], (GDN_CHUNK, PAIR)),
                                 jnp.broadcast_to(g[:, HEAD_DIM:HEAD_DIM + 1], (GDN_CHUNK, PAIR))], axis=0)
        g_row = jnp.concatenate([gt_ref[0, 2 * p:2 * p + 1, rows(ci)], gt_ref[0, 2 * p + 1:2 * p + 2, rows(ci)]],
                                axis=1)
        return jnp.where(causal, jnp.exp(jnp.where(causal, g_col - g_row, 0.0)), 0.0)

    decay = [decay_of(g, ci, p) for g, (ci, p) in zip(gc, units)]
    k2b = [x.astype(MXU_DTYPE) for x in k2]
    low = [jnp.where(strict, _dot_nt(stack(a), b) * d, 0.0) for a, b, d in zip(kb, k2b, decay)]
    a_in = [(_dot_nt(stack(a), b) * d).astype(MXU_DTYPE) for a, b, d in zip(qn, k2b, decay)]
    t_inv = _unit_lower_inverse_many(low)
    uw = [_dot(ti, jnp.concatenate([stack(v * b), stack(a * e)], axis=1)).astype(MXU_DTYPE)
          for ti, v, b, a, e in zip(t_inv, vv, bt, kb, eg)]
    ket = [stack(k * jnp.exp(gl - g)).T for k, gl, g in zip(kn, g_last, gc)]
    nm = [_dot(a, b) for a, b in zip(ket, uw)]
    raw = [jnp.dot(a, b, preferred_element_type=F32) for a, b in zip(a_in, uw)]
    p_mat = [stack(q * e) - x[:, PAIR:] for q, e, x in zip(qn, eg, raw)]

    state = [s_ref[p] for p in range(N_PAIRS)]
    for i, (ci, p) in enumerate(units):
        s = state[p]
        o2 = _dot(p_mat[i], s) + raw[i][:, :PAIR]
        state[p] = s * jnp.exp(g_last[i]) - _dot(nm[i][:, PAIR:], s) + nm[i][:, :PAIR]
        o = o2[0:GDN_CHUNK] + o2[GDN_CHUNK:n2]
        ms = _dot_ones(o * o, ones) * (1.0 / HEAD_DIM)
        on = o * lax.rsqrt(ms + NORM_EPS) * nw_ref[...]
        o_ref[0, rows(ci), pair(p, 0)] = (on * _silu(gate_ref[0, rows(ci), pair(p, 0)])
                                          * ms_ref[:, pair(p, 0)]).astype(o_ref.dtype)
    for p in range(N_PAIRS):
        s_ref[p] = state[p]


def _gdn(qkv, g, beta, g_t, gate, norm_w2, mscale, blk):
    b, t, _ = qkv.shape
    tile = lambda width: pl.BlockSpec((1, blk, width), lambda i, j: (i, j, 0))
    const = lambda shape: pl.BlockSpec(shape, lambda i, j: (0, 0))
    return pl.pallas_call(
        _gdn_kernel,
        grid=(b, t // blk),
        in_specs=[tile(3 * GDN_WIDTH),
                  tile(GDN_WIDTH), tile(GDN_WIDTH), pl.BlockSpec((1, SUBLANES, blk), lambda i, j: (i, 0, j)),
                  tile(GDN_WIDTH), const((1, PAIR)), const((1, GDN_WIDTH))],
        out_specs=tile(GDN_WIDTH),
        out_shape=jax.ShapeDtypeStruct((b, t, GDN_WIDTH), MXU_DTYPE),
        scratch_shapes=[pltpu.VMEM((N_PAIRS, PAIR, PAIR), F32)],
        compiler_params=_cparams("parallel", "arbitrary"),
        name="gdn",
    )(qkv, g, beta, g_t, gate, norm_w2, mscale)


FOX_ROW_GROUP = 32


FOX_SKIP_MARGIN = 106.0


def _fox_kernel(q_ref, k_ref, v_ref, aux_ref, ms_ref, o_ref, s0_ref, s1_ref, p0_ref, p1_ref, a0_ref, a1_ref,
                m_ref, acc_ref, *, tq):
    i = pl.program_id(2)
    heads = range(2)
    rg = FOX_ROW_GROUP
    m_ref[...] = jnp.full(m_ref.shape, -jnp.inf, F32)
    acc_ref[...] = jnp.zeros(acc_ref.shape, F32)

    def scores(j, s_ref):
        start = pl.multiple_of(j * tq, tq)
        for h in heads:
            s_ref[h] = lax.dot_general(q_ref[0, h], k_ref[0, h, pl.ds(start, tq), :],
                                       (((1,), (1,)), ((), ())), preferred_element_type=F32)

    def softmax(s_ref, p_ref, a_ref, masked):
        for h in heads:
            for g in range(tq // rg):
                rows = slice(g * rg, (g + 1) * rg)
                width = LANES * (((g + 1) * rg - 1) // LANES + 1) if masked else tq
                s = s_ref[h, rows, 0:width]
                if masked:
                    row_id = g * rg + lax.broadcasted_iota(jnp.int32, (rg, width), 0)
                    col_id = lax.broadcasted_iota(jnp.int32, (rg, width), 1)
                    s = jnp.where(col_id <= row_id, s, -jnp.inf)
                    if width < tq:
                        p_ref[h, rows, width:tq] = jnp.zeros((rg, tq - width), p_ref.dtype)
                m_old = m_ref[h, rows, :]
                m_new = jnp.maximum(m_old, jnp.max(s, axis=-1, keepdims=True))
                a_ref[h, rows, :] = jnp.exp(m_old - m_new)
                m_ref[h, rows, :] = m_new
                m_wide = jnp.concatenate([m_new] * (width // LANES), axis=1)
                p_ref[h, rows, 0:width] = jnp.exp(s - m_wide).astype(p_ref.dtype)

    def weighted_values(j, p_ref, a_ref):
        start = pl.multiple_of(j * tq, tq)
        for h in heads:
            acc_ref[h] = a_ref[h] * acc_ref[h] + jnp.dot(p_ref[h], v_ref[0, h, pl.ds(start, tq), :],
                                                         preferred_element_type=F32)

    scores(i, s0_ref)
    softmax(s0_ref, p0_ref, a0_ref, True)
    scores(jnp.maximum(i - 1, 0), s1_ref)
    weighted_values(i, p0_ref, a0_ref)

    lane = lax.broadcasted_iota(jnp.int32, (1, LANES), 1)
    n = jnp.int32(0)
    for h in heads:
        q_norm = q_ref[0, h, :, Q_NORM_LANE:Q_NORM_LANE + 1].astype(F32)
        key_norm = aux_ref[0, h, 1:2, :]
        block_end_c = aux_ref[0, h, 0:1, :]
        slack = jnp.max(q_norm * key_norm - m_ref[h], axis=0, keepdims=True)
        keep = (lane < i) & (slack - block_end_c >= -FOX_SKIP_MARGIN)
        n = jnp.maximum(n, jnp.sum(keep.astype(jnp.int32)))

    def body(t, carry):
        j = i - 1 - 2 * t
        softmax(s1_ref, p1_ref, a1_ref, False)
        scores(jnp.maximum(j - 1, 0), s0_ref)
        weighted_values(j, p1_ref, a1_ref)
        softmax(s0_ref, p0_ref, a0_ref, False)
        scores(jnp.maximum(j - 2, 0), s1_ref)
        weighted_values(j - 1, p0_ref, a0_ref)
        return carry

    lax.fori_loop(0, n // 2, body, 0)

    @pl.when(n % 2 == 1)
    def _():
        softmax(s1_ref, p1_ref, a1_ref, False)
        weighted_values(i - n, p1_ref, a1_ref)

    o = [acc_ref[h] / acc_ref[h][:, AUG_BIAS_LANE:AUG_BIAS_LANE + 1] for h in heads]
    lane = lax.broadcasted_iota(jnp.int32, (1, PAIR), 1)
    o_pair = jnp.where(lane < HEAD_DIM, o[0], pltpu.roll(o[1], HEAD_DIM, axis=1))
    o_ref[0] = (o_pair * ms_ref[...]).astype(o_ref.dtype)


def _fox_aux(stats):
    b, nt = stats.shape[:2]
    k_sq = jnp.max(stats[:, :, :N_PAIRS, :], axis=1)
    k_sq = k_sq.reshape(b, N_PAIRS, 2, HEAD_DIM)[..., 0].reshape(b, FOX_HEADS)
    key_norm = jnp.sqrt(k_sq) * KEY_NORM_SLACK
    block_end_c = jnp.transpose(stats[:, :, N_PAIRS, 2 * GDN_HEADS:2 * GDN_HEADS + FOX_HEADS], (0, 2, 1))
    aux = jnp.zeros((b, FOX_HEADS, SUBLANES, LANES), F32)
    aux = aux.at[:, :, 0, :nt].set(block_end_c)
    return aux.at[:, :, 1, :].set(jnp.broadcast_to(key_norm[:, :, None], (b, FOX_HEADS, LANES)))


def _fox(qa, ka, va, aux, mscale, tq):
    b, _, t, _ = qa.shape
    kern = functools.partial(_fox_kernel, tq=tq)
    return pl.pallas_call(
        kern,
        grid=(b, N_PAIRS, t // tq),
        in_specs=[pl.BlockSpec((1, 2, tq, LANES), lambda bi, p, i: (bi, p, i, 0)),
                  pl.BlockSpec((1, 2, t, LANES), lambda bi, p, i: (bi, p, 0, 0)),
                  pl.BlockSpec((1, 2, t, LANES), lambda bi, p, i: (bi, p, 0, 0)),
                  pl.BlockSpec((1, 2, SUBLANES, LANES), lambda bi, p, i: (bi, p, 0, 0)),
                  pl.BlockSpec((1, PAIR), lambda bi, p, i: (0, p))],
        out_specs=pl.BlockSpec((1, tq, PAIR), lambda bi, p, i: (bi, i, p)),
        out_shape=jax.ShapeDtypeStruct((b, t, FOX_WIDTH), MXU_DTYPE),
        scratch_shapes=[pltpu.VMEM((2, tq, tq), F32), pltpu.VMEM((2, tq, tq), F32),
                        pltpu.VMEM((2, tq, tq), jnp.bfloat16), pltpu.VMEM((2, tq, tq), jnp.bfloat16),
                        pltpu.VMEM((2, tq, LANES), F32), pltpu.VMEM((2, tq, LANES), F32),
                        pltpu.VMEM((2, tq, LANES), F32), pltpu.VMEM((2, tq, LANES), F32)],
        compiler_params=_cparams("parallel", "parallel", "arbitrary"),
        name="fox_attention",
    )(qa, ka, va, aux, mscale)


def _out_proj_kernel(oa_ref, ob_ref, oc_ref, x_ref, w_ref, g_ref, b_ref, o_ref):
    mix = jnp.dot(oa_ref[...], w_ref[0:GDN_WIDTH, :], preferred_element_type=F32)
    mix = mix + jnp.dot(ob_ref[...], w_ref[GDN_WIDTH:GDN_WIDTH + CONV_WIDTH, :], preferred_element_type=F32)
    mix = mix + jnp.dot(oc_ref[...], w_ref[GDN_WIDTH + CONV_WIDTH:D_MODEL, :], preferred_element_type=F32)
    o_ref[...] = _layer_norm_rows(DEEPNORM_ALPHA * x_ref[...] + mix, g_ref[...], b_ref[...])


def _out_proj(oa, ob, oc, xf, w, g, bvec, tm):
    n = xf.shape[0]
    row = lambda i: (i, 0)
    const = lambda i: (0, 0)
    return pl.pallas_call(
        _out_proj_kernel,
        grid=(n // tm,),
        in_specs=[pl.BlockSpec((tm, GDN_WIDTH), row), pl.BlockSpec((tm, CONV_WIDTH), row),
                  pl.BlockSpec((tm, FOX_WIDTH), row), pl.BlockSpec((tm, D_MODEL), row),
                  pl.BlockSpec((D_MODEL, D_MODEL), const),
                  pl.BlockSpec((1, D_MODEL), const), pl.BlockSpec((1, D_MODEL), const)],
        out_specs=pl.BlockSpec((tm, D_MODEL), row),
        out_shape=jax.ShapeDtypeStruct((n, D_MODEL), F32),
        compiler_params=_cparams("parallel"),
        name="out_proj_ln",
    )(oa, ob, oc, xf, w, g, bvec)


FFN_CHUNK = 256


def _swiglu_chunks(xb, w1, w3, w2, width):
    assert width % FFN_CHUNK == 0
    acc = None
    for c in range(width // FFN_CHUNK):
        cols = slice(c * FFN_CHUNK, (c + 1) * FFN_CHUNK)
        h = _silu(jnp.dot(xb, w1(slice(None), cols), preferred_element_type=F32)) * jnp.dot(
            xb, w3(slice(None), cols), preferred_element_type=F32)
        part = jnp.dot(h.astype(xb.dtype), w2(cols, slice(None)), preferred_element_type=F32)
        acc = part if acc is None else acc + part
    return acc


def _ffn_kernel(x_ref, w1_ref, w3_ref, w2_ref, g_ref, b_ref, o_ref):
    x = x_ref[...]
    ff = _swiglu_chunks(x.astype(w1_ref.dtype), lambda r, c: w1_ref[r, c], lambda r, c: w3_ref[r, c],
                        lambda r, c: w2_ref[r, c], w1_ref.shape[1])
    o_ref[...] = _layer_norm_rows(DEEPNORM_ALPHA * x + ff, g_ref[...], b_ref[...])


def _dense_ffn(xf, w1, w3, w2, g, bvec, tm):
    n = xf.shape[0]
    ff = w1.shape[1]
    const = lambda shape: pl.BlockSpec(shape, lambda i: (0, 0), pipeline_mode=pl.Buffered(1))
    return pl.pallas_call(
        _ffn_kernel,
        grid=(n // tm,),
        in_specs=[pl.BlockSpec((tm, D_MODEL), lambda i: (i, 0)),
                  const((D_MODEL, ff)), const((D_MODEL, ff)), const((ff, D_MODEL)),
                  const((1, D_MODEL)), const((1, D_MODEL))],
        out_specs=pl.BlockSpec((tm, D_MODEL), lambda i: (i, 0)),
        out_shape=jax.ShapeDtypeStruct((n, D_MODEL), F32),
        compiler_params=_cparams("parallel"),
        name="dense_ffn_ln",
    )(xf, w1, w3, w2, g, bvec)


def _router_kernel(x_ref, wr_ref, info_ref, info_t_ref, cnt_ref, run_ref):
    i = pl.program_id(0)
    tm = x_ref.shape[0]

    @pl.when(i == 0)
    def _():
        run_ref[...] = jnp.zeros_like(run_ref)

    logits = jnp.dot(x_ref[...].astype(MXU_DTYPE), wr_ref[...], preferred_element_type=F32)
    lane = lax.broadcasted_iota(jnp.int32, (tm, LANES), 1)
    logits = jnp.where(lane < N_EXPERTS, logits, -jnp.inf)
    m1 = jnp.max(logits, axis=-1, keepdims=True)
    e1 = jnp.min(jnp.where(logits == m1, lane, LANES), axis=-1, keepdims=True)
    rest = jnp.where(lane == e1, -jnp.inf, logits)
    m2 = jnp.max(rest, axis=-1, keepdims=True)
    e2 = jnp.min(jnp.where(rest == m2, lane, LANES), axis=-1, keepdims=True)
    z = jnp.exp(m2 - m1)
    g1 = 1.0 / (1.0 + z)
    g2 = z / (1.0 + z)
    onehot = ((lane == e1) | (lane == e2)).astype(F32)
    r = lax.broadcasted_iota(jnp.int32, (tm, tm), 0)
    c = lax.broadcasted_iota(jnp.int32, (tm, tm), 1)
    before = jnp.dot((c < r).astype(jnp.bfloat16), onehot.astype(jnp.bfloat16),
                     preferred_element_type=F32) + run_ref[...]
    rank1 = jnp.sum(jnp.where(lane == e1, before, 0.0), axis=-1, keepdims=True)
    rank2 = jnp.sum(jnp.where(lane == e2, before, 0.0), axis=-1, keepdims=True)
    run_ref[...] = run_ref[...] + jnp.sum(onehot, axis=0, keepdims=True)
    cnt_ref[...] = run_ref[...]
    info = jnp.where(lane == 0, e1.astype(F32),
                     jnp.where(lane == 1, e2.astype(F32),
                               jnp.where(lane == 2, rank1,
                                         jnp.where(lane == 3, rank2,
                                                   jnp.where(lane == 4, g1, jnp.where(lane == 5, g2, 0.0))))))
    info_ref[...] = info
    info_t_ref[...] = info.T[0:SUBLANES, :]


def _router(xf, wr_pad, tm):
    n = xf.shape[0]
    return pl.pallas_call(
        _router_kernel,
        grid=(n // tm,),
        in_specs=[pl.BlockSpec((tm, D_MODEL), lambda i: (i, 0)),
                  pl.BlockSpec((D_MODEL, LANES), lambda i: (0, 0))],
        out_specs=[pl.BlockSpec((tm, LANES), lambda i: (i, 0)),
                   pl.BlockSpec((SUBLANES, tm), lambda i: (0, i)),
                   pl.BlockSpec((1, LANES), lambda i: (0, 0))],
        out_shape=[jax.ShapeDtypeStruct((n, LANES), F32), jax.ShapeDtypeStruct((SUBLANES, n), F32),
                   jax.ShapeDtypeStruct((1, LANES), F32)],
        scratch_shapes=[pltpu.VMEM((1, LANES), F32)],
        compiler_params=_cparams("arbitrary"),
        name="moe_router",
    )(xf, wr_pad)


DMA_LOOP_UNROLL = 8


def _dispatch_kernel(fill_ref, d1_ref, d2_ref, x_ref, xs_ref, zero_ref, sem):
    tm = x_ref.shape[0]

    @pl.when(pl.program_id(0) == 0)
    def _():
        zero_ref[...] = jnp.zeros_like(zero_ref)

        def strip_copy(e, k):
            start = pl.multiple_of(fill_ref[e] - SUBLANES * (k + 1), SUBLANES)
            return pltpu.make_async_copy(zero_ref, xs_ref.at[pl.ds(start, SUBLANES), :], sem)

        for e in range(N_EXPERTS):
            lax.fori_loop(0, fill_ref[N_EXPERTS + e], lambda k, c, e=e: (strip_copy(e, k).start(), c)[1], 0)
        for e in range(N_EXPERTS):
            lax.fori_loop(0, fill_ref[N_EXPERTS + e], lambda k, c, e=e: (strip_copy(e, k).wait(), c)[1], 0)

    def row_copy(r, dst):
        return pltpu.make_async_copy(x_ref.at[pl.ds(r, 1), :], xs_ref.at[pl.ds(dst, 1), :], sem)

    def issue(r, carry):
        row_copy(r, d1_ref[0, 0, r]).start()
        row_copy(r, d2_ref[0, 0, r]).start()
        return carry

    lax.fori_loop(0, tm, issue, 0, unroll=DMA_LOOP_UNROLL)

    def drain(r, carry):
        row_copy(r, d1_ref[0, 0, r]).wait()
        row_copy(r, d2_ref[0, 0, r]).wait()
        return carry

    lax.fori_loop(0, tm, drain, 0, unroll=DMA_LOOP_UNROLL)


def _dispatch(xf, d1, d2, gap_start, gap_end, rows, tm):
    n = xf.shape[0]
    idx = lambda a: a.reshape(n // tm, 1, tm)
    n_strips = (gap_end - gap_start + SUBLANES - 1) // SUBLANES
    fill = jnp.concatenate([gap_end, n_strips]).astype(jnp.int32)
    smem = lambda: pl.BlockSpec((1, 1, tm), lambda i, fill: (i, 0, 0), memory_space=pltpu.SMEM)
    grid_spec = pltpu.PrefetchScalarGridSpec(
        num_scalar_prefetch=1,
        grid=(n // tm,),
        in_specs=[smem(), smem(), pl.BlockSpec((tm, D_MODEL), lambda i, fill: (i, 0))],
        out_specs=pl.BlockSpec(memory_space=pl.ANY),
        scratch_shapes=[pltpu.VMEM((SUBLANES, D_MODEL), xf.dtype), pltpu.SemaphoreType.DMA(())],
    )
    return pl.pallas_call(
        _dispatch_kernel,
        grid_spec=grid_spec,
        out_shape=jax.ShapeDtypeStruct((rows, D_MODEL), xf.dtype),
        compiler_params=_cparams("arbitrary"),
        name="moe_dispatch",
    )(fill, idx(d1), idx(d2), xf)


def _expert_kernel(te_ref, nu_ref, xs_ref, w1_ref, w3_ref, w2_ref, ys_ref, xb_ref):
    del te_ref
    i = pl.program_id(0)
    f = pl.program_id(1)
    used = i < nu_ref[0]

    def partial_ffn(xb):
        return _swiglu_chunks(xb, lambda r, c: w1_ref[0, r, c], lambda r, c: w3_ref[0, r, c],
                              lambda r, c: w2_ref[0, r, c], w1_ref.shape[2])

    @pl.when(used & (f == 0))
    def _():
        xb = xs_ref[...].astype(xb_ref.dtype)
        xb_ref[...] = xb
        ys_ref[...] = partial_ffn(xb)

    @pl.when(used & (f > 0))
    def _():
        ys_ref[...] += partial_ffn(xb_ref[...])

    @pl.when(jnp.logical_not(used) & (f == 0))
    def _():
        ys_ref[...] = jnp.zeros_like(ys_ref)


def _experts(xs, tile_expert, n_used, w1, w3, w2, tm, tf):
    rows = xs.shape[0]
    ff = w1.shape[2]
    n_tiles = rows // tm

    def x_map(i, f, te, nu):
        return (jnp.maximum(jnp.minimum(i, nu[0] - 1), 0), 0)

    def w13_map(i, f, te, nu):
        return (te[i], 0, jnp.where(i < nu[0], f, ff // tf - 1))

    def w2_map(i, f, te, nu):
        return (te[i], jnp.where(i < nu[0], f, ff // tf - 1), 0)

    grid_spec = pltpu.PrefetchScalarGridSpec(
        num_scalar_prefetch=2,
        grid=(n_tiles, ff // tf),
        in_specs=[pl.BlockSpec((tm, D_MODEL), x_map),
                  pl.BlockSpec((1, D_MODEL, tf), w13_map),
                  pl.BlockSpec((1, D_MODEL, tf), w13_map),
                  pl.BlockSpec((1, tf, D_MODEL), w2_map)],
        out_specs=pl.BlockSpec((tm, D_MODEL), lambda i, f, te, nu: (i, 0)),
        scratch_shapes=[pltpu.VMEM((tm, D_MODEL), MXU_DTYPE)],
    )
    return pl.pallas_call(
        _expert_kernel,
        grid_spec=grid_spec,
        out_shape=jax.ShapeDtypeStruct((rows, D_MODEL), F32),
        compiler_params=_cparams("arbitrary", "arbitrary"),
        name="moe_experts",
    )(tile_expert, n_used, xs, w1, w3, w2)


def _combine_kernel(d1_ref, d2_ref, x_ref, info_ref, ys_ref, g_ref, b_ref, o_ref, y1_ref, y2_ref, sem):
    tm = x_ref.shape[0]

    def row_copy(src, r, buf):
        return pltpu.make_async_copy(ys_ref.at[pl.ds(src, 1), :], buf.at[pl.ds(r, 1), :], sem)

    def issue(r, carry):
        row_copy(d1_ref[0, 0, r], r, y1_ref).start()
        row_copy(d2_ref[0, 0, r], r, y2_ref).start()
        return carry

    lax.fori_loop(0, tm, issue, 0, unroll=DMA_LOOP_UNROLL)

    def drain(r, carry):
        row_copy(d1_ref[0, 0, r], r, y1_ref).wait()
        row_copy(d2_ref[0, 0, r], r, y2_ref).wait()
        return carry

    lax.fori_loop(0, tm, drain, 0, unroll=DMA_LOOP_UNROLL)
    info = info_ref[...]
    ff = info[:, 4:5] * y1_ref[...] + info[:, 5:6] * y2_ref[...]
    o_ref[...] = _layer_norm_rows(DEEPNORM_ALPHA * x_ref[...] + ff, g_ref[...], b_ref[...])


def _combine(xf, info, ys, d1, d2, g, bvec, tm):
    n = xf.shape[0]
    idx = lambda a: a.reshape(n // tm, 1, tm)
    smem = lambda: pl.BlockSpec((1, 1, tm), lambda i: (i, 0, 0), memory_space=pltpu.SMEM)
    return pl.pallas_call(
        _combine_kernel,
        grid=(n // tm,),
        in_specs=[smem(), smem(),
                  pl.BlockSpec((tm, D_MODEL), lambda i: (i, 0)),
                  pl.BlockSpec((tm, LANES), lambda i: (i, 0)),
                  pl.BlockSpec(memory_space=pl.ANY),
                  pl.BlockSpec((1, D_MODEL), lambda i: (0, 0)),
                  pl.BlockSpec((1, D_MODEL), lambda i: (0, 0))],
        out_specs=pl.BlockSpec((tm, D_MODEL), lambda i: (i, 0)),
        out_shape=jax.ShapeDtypeStruct((n, D_MODEL), F32),
        scratch_shapes=[pltpu.VMEM((tm, D_MODEL), F32), pltpu.VMEM((tm, D_MODEL), F32),
                        pltpu.SemaphoreType.DMA(())],
        compiler_params=_cparams("arbitrary"),
        name="moe_combine_ln",
    )(idx(d1), idx(d2), xf, info, ys, g, bvec)


MOE_TILE = 512


def _moe_ffn(xf, w_router, w1, w3, w2, g, bvec, tm_tok, tf):
    n = xf.shape[0]
    wr_pad = jnp.zeros((D_MODEL, LANES), MXU_DTYPE).at[:, :N_EXPERTS].set(w_router.astype(MXU_DTYPE))
    info, info_t, counts = _router(xf, wr_pad, tm_tok)
    sizes = counts[0, :N_EXPERTS].astype(jnp.int32)
    tiles_per = (sizes + MOE_TILE - 1) // MOE_TILE
    tile_end = jnp.cumsum(tiles_per)
    seg_start = (tile_end - tiles_per) * MOE_TILE
    experts = jnp.arange(N_EXPERTS, dtype=jnp.int32)[:, None]

    def sorted_row(expert_row, rank_row):
        start = jnp.sum(jnp.where(expert_row.astype(jnp.int32)[None, :] == experts, seg_start[:, None], 0), axis=0)
        return start + rank_row.astype(jnp.int32)

    d1 = sorted_row(info_t[0], info_t[2])
    d2 = sorted_row(info_t[1], info_t[3])
    n_tiles = (n * TOP_K) // MOE_TILE + N_EXPERTS
    rows = n_tiles * MOE_TILE
    tile_ids = jnp.arange(n_tiles, dtype=jnp.int32)
    tile_expert = jnp.minimum(jnp.sum((tile_end[None, :] <= tile_ids[:, None]).astype(jnp.int32), axis=1),
                              N_EXPERTS - 1)
    n_used = tile_end[N_EXPERTS - 1:].astype(jnp.int32)
    gap_end = jnp.concatenate([seg_start[1:], jnp.full((1,), rows, jnp.int32)])
    xs = _dispatch(xf, d1, d2, seg_start + sizes, gap_end, rows, tm_tok)
    ys = _experts(xs, tile_expert, n_used, w1, w3, w2, MOE_TILE, tf)
    return _combine(xf, info, ys, d1, d2, g, bvec, tm_tok)


def _pack_in_proj(w_in):
    cuts = np.cumsum([0, GDN_WIDTH, GDN_WIDTH, GDN_WIDTH, GDN_HEADS, GDN_HEADS, GDN_WIDTH,
                      2 * CONV_WIDTH, FOX_WIDTH, FOX_WIDTH, FOX_WIDTH, FOX_HEADS])
    seg = lambda i: w_in[:, cuts[i]:cuts[i + 1]]
    w_main = jnp.concatenate([seg(0), seg(1), seg(2), seg(5), seg(6), seg(7), seg(8), seg(9)], axis=1)
    pad = jnp.zeros((D_MODEL, SMALL_COLS - 2 * GDN_HEADS - FOX_HEADS), w_in.dtype)
    w_small = jnp.concatenate([seg(3), seg(4), seg(10), pad], axis=1).astype(F32)
    ws_hi = w_small.astype(MXU_DTYPE)
    ws_lo = (w_small - ws_hi.astype(F32)).astype(MXU_DTYPE)
    return w_main.astype(MXU_DTYPE), jnp.concatenate([ws_hi, ws_lo], axis=1)


def _lane_row(vals, offset):
    return jnp.zeros((1, LANES), F32).at[0, offset:offset + vals.shape[0]].set(vals.astype(F32))


def _mixer(xf, b, t, w_in, mix_scale, w_out, gdn_conv_w, gdn_a_log, gdn_dt_bias, gdn_norm_w,
           cnv_dw_w, cnv_dw_b, cnv_ln_g, cnv_ln_b, fox_f_bias, ln_g, ln_b, tiles):
    w_main, w_small = _pack_in_proj(w_in)
    ms = mix_scale.reshape(1, D_MODEL).astype(F32)
    row = lambda v: v.reshape(1, -1).astype(F32)
    qkv, gate, o_b, fq, fk, fv, small = _in_proj(
        xf, w_main, w_small, gdn_conv_w, cnv_dw_w, row(cnv_dw_b), row(cnv_ln_g), row(cnv_ln_b),
        ms[:, GDN_WIDTH:GDN_WIDTH + CONV_WIDTH], tiles["tm"], t // tiles["tm"])
    r3 = lambda a: a.reshape(b, t, a.shape[-1])
    assert tiles["tt"] == tiles["tq"]
    g_cum, beta, g_t, stats, qa, ka, va = _gates(
        r3(small), _lane_row(gdn_a_log, 0), _lane_row(gdn_dt_bias, 0), _lane_row(fox_f_bias, 2 * GDN_HEADS),
        r3(fq), r3(fk), r3(fv), tiles["tt"])
    norm_w2 = jnp.tile(gdn_norm_w.astype(F32), 2).reshape(1, PAIR)
    o_a = _gdn(r3(qkv), g_cum, beta, g_t, r3(gate), norm_w2, ms[:, :GDN_WIDTH], tiles["gdn_blk"])
    o_c = _fox(qa, ka, va, _fox_aux(stats), ms[:, GDN_WIDTH + CONV_WIDTH:], tiles["tq"])
    flat = lambda a: a.reshape(b * t, a.shape[-1])
    return _out_proj(flat(o_a), o_b, flat(o_c), xf, w_out.astype(MXU_DTYPE), row(ln_g), row(ln_b),
                     tiles["tm"])


def _tiles(t):
    return dict(tm=min(512, t), tt=min(512, t), gdn_blk=min(512, t), tq=min(512, t),
                tf_moe=1792)


def kernel(x, w_in, mix_scale, w_out, gdn_conv_w, gdn_a_log, gdn_dt_bias, gdn_norm_w, cnv_dw_w, cnv_dw_b,
           cnv_ln_g, cnv_ln_b, fox_f_bias, ln_mix_g, ln_mix_b, ln_ffn_g, ln_ffn_b, ffn_w1, ffn_w3, ffn_w2,
           moe_router, moe_w1, moe_w3, moe_w2):
    b, t, d = x.shape
    tiles = _tiles(t)
    row = lambda v: v.reshape(1, -1).astype(F32)
    xf = x.reshape(b * t, d)
    for l in range(DEPTH):
        xf = _mixer(xf, b, t, w_in[l], mix_scale[l], w_out[l], gdn_conv_w[l], gdn_a_log[l], gdn_dt_bias[l],
                    gdn_norm_w[l], cnv_dw_w[l], cnv_dw_b[l], cnv_ln_g[l], cnv_ln_b[l], fox_f_bias[l],
                    ln_mix_g[l], ln_mix_b[l], tiles)
        if l % 2 == 0:
            e = l // 2
            xf = _dense_ffn(xf, ffn_w1[e].astype(MXU_DTYPE), ffn_w3[e].astype(MXU_DTYPE),
                            ffn_w2[e].astype(MXU_DTYPE), row(ln_ffn_g[l]), row(ln_ffn_b[l]),
                            tiles["tm"])
        else:
            e = l // 2
            xf = _moe_ffn(xf, moe_router[e], moe_w1[e].astype(MXU_DTYPE), moe_w3[e].astype(MXU_DTYPE),
                          moe_w2[e].astype(MXU_DTYPE), row(ln_ffn_g[l]), row(ln_ffn_b[l]),
                          tiles["tm"], tiles["tf_moe"])
    return xf.reshape(b, t, d)
```

```python
import functools

import jax
import jax.numpy as jnp
import numpy as np
from jax import lax
from jax.experimental import pallas as pl
from jax.experimental.pallas import tpu as pltpu

D_MODEL = 1024
DEPTH = 2
HEAD_DIM = 64
GDN_WIDTH = 384
CONV_WIDTH = 256
FOX_WIDTH = 384
GDN_HEADS = 6
FOX_HEADS = 6
GDN_SHORT_CONV = 4
GDN_CHUNK = 64
CONV_KERNEL = 31
FFN_DENSE = 2816
N_EXPERTS = 8
TOP_K = 2
FFN_EXPERT = 3584
DEEPNORM_ALPHA = (2 * DEPTH) ** 0.25
LN_EPS = 1e-5
NORM_EPS = 1e-6

LANES = 128
SUBLANES = 8
PAIR = 2 * HEAD_DIM
N_PAIRS = GDN_WIDTH // PAIR
SMALL_COLS = LANES

_C_QKV = 0
_C_GATE = _C_QKV + 3 * GDN_WIDTH
_C_GLU = _C_GATE + GDN_WIDTH
_C_FQ = _C_GLU + 2 * CONV_WIDTH
_C_FK = _C_FQ + FOX_WIDTH
_C_FV = _C_FK + FOX_WIDTH
_C_END = _C_FV + FOX_WIDTH

MXU_DTYPE = jnp.bfloat16
F32 = jnp.float32

VMEM_LIMIT = 56 * 1024 * 1024


def _cparams(*sem):
    return pltpu.CompilerParams(dimension_semantics=sem, vmem_limit_bytes=VMEM_LIMIT)


def _dot(a, b):
    return jnp.dot(a.astype(MXU_DTYPE), b.astype(MXU_DTYPE), preferred_element_type=F32)


def _dot_nt(a, b):
    return lax.dot_general(a.astype(MXU_DTYPE), b.astype(MXU_DTYPE),
                           (((1,), (1,)), ((), ())), preferred_element_type=F32)


def _split(a):
    hi = a.astype(jnp.bfloat16)
    return hi, (a - hi.astype(F32)).astype(jnp.bfloat16)


def _split3(a):
    hi = a.astype(jnp.bfloat16)
    rest = a - hi.astype(F32)
    mid = rest.astype(jnp.bfloat16)
    lo = (rest - mid.astype(F32)).astype(jnp.bfloat16)
    return hi, mid, lo


def _dot_ones(a, ones_b16):
    ah, al = _split(a)
    d = lambda x: jnp.dot(x, ones_b16, preferred_element_type=F32)
    return d(ah) + d(al)


def _sel_rhs(a, sel_b16):
    return sum(jnp.dot(t, sel_b16, preferred_element_type=F32) for t in _split3(a))


def _sel_lhs(sel_b16, b):
    return sum(jnp.dot(sel_b16, t, preferred_element_type=F32) for t in _split3(b))


def _sigmoid(x):
    return 1.0 / (1.0 + jnp.exp(-x))


def _silu(x):
    return x * _sigmoid(x)


def _softplus(x):
    return jnp.maximum(x, 0.0) + jnp.log1p(jnp.exp(-jnp.abs(x)))


def _log_sigmoid(x):
    return -_softplus(-x)


def _layer_norm_rows(y, g, b):
    mu = jnp.mean(y, axis=-1, keepdims=True)
    d = y - mu
    var = jnp.mean(d * d, axis=-1, keepdims=True)
    return d * lax.rsqrt(var + LN_EPS) * g + b


def _head_ones():
    r = lax.broadcasted_iota(jnp.int32, (PAIR, PAIR), 0) // HEAD_DIM
    c = lax.broadcasted_iota(jnp.int32, (PAIR, PAIR), 1) // HEAD_DIM
    return (r == c).astype(F32)


CONV_HALO = 32
CONV_ROW_BLOCK = 128
SHORT_CONV_ROW_BLOCK = 128


def _in_proj_kernel(x_ref, w_ref, ws_ref, gw_ref, cw_ref, cb_ref, lg_ref, lb_ref, ms_ref,
                    qkv_ref, gate_ref, ob_ref, fq_ref, fk_ref, fv_ref, small_ref,
                    qbuf_ref, cbuf_ref, sh_ref, *, tiles_per_seq):
    first = pl.program_id(0) % tiles_per_seq == 0
    tm = x_ref.shape[0]
    x = x_ref[...]
    xb = x.astype(MXU_DTYPE)

    def mm(lo, hi):
        return jnp.dot(xb, w_ref[:, lo:hi], preferred_element_type=F32)

    @pl.when(first)
    def _():
        cbuf_ref[0:CONV_HALO, :] = jnp.zeros((CONV_HALO, CONV_WIDTH), F32)
        qbuf_ref[0:SUBLANES, :] = jnp.zeros((SUBLANES, 3 * GDN_WIDTH), F32)

    @pl.when(jnp.logical_not(first))
    def _():
        cbuf_ref[0:CONV_HALO, :] = cbuf_ref[tm:tm + CONV_HALO, :]
        qbuf_ref[0:SUBLANES, :] = qbuf_ref[tm:tm + SUBLANES, :]

    glu = mm(_C_GLU, _C_FQ)
    cbuf_ref[CONV_HALO:CONV_HALO + tm, :] = glu[:, 0:CONV_WIDTH] * _sigmoid(glu[:, CONV_WIDTH:2 * CONV_WIDTH])
    span = tm + CONV_HALO - SUBLANES
    for s in range(1, SUBLANES):
        sh_ref[s - 1] = cbuf_ref[s:s + span, :]
    for r0 in range(0, tm, CONV_ROW_BLOCK):
        acc = jnp.zeros((CONV_ROW_BLOCK, CONV_WIDTH), F32) + cb_ref[...]
        for j in range(CONV_KERNEL):
            lo = r0 + CONV_HALO - (CONV_KERNEL - 1) + j
            base, phase = lo - lo % SUBLANES, lo % SUBLANES
            tap = (cbuf_ref[base:base + CONV_ROW_BLOCK, :] if phase == 0
                   else sh_ref[phase - 1, base:base + CONV_ROW_BLOCK, :])
            acc = acc + cw_ref[j:j + 1, :] * tap
        y = _silu(_layer_norm_rows(acc, lg_ref[...], lb_ref[...])) * ms_ref[...]
        ob_ref[r0:r0 + CONV_ROW_BLOCK, :] = y.astype(ob_ref.dtype)

    qbuf_ref[SUBLANES:SUBLANES + tm, :] = mm(_C_QKV, _C_GATE)
    for r0 in range(0, tm, SHORT_CONV_ROW_BLOCK):
        acc = jnp.zeros((SHORT_CONV_ROW_BLOCK, 3 * GDN_WIDTH), F32)
        for j in range(GDN_SHORT_CONV):
            lo = r0 + SUBLANES - (GDN_SHORT_CONV - 1) + j
            acc = acc + gw_ref[j:j + 1, :] * qbuf_ref[lo:lo + SHORT_CONV_ROW_BLOCK, :]
        qkv_ref[r0:r0 + SHORT_CONV_ROW_BLOCK, :] = _silu(acc)

    gate_ref[...] = mm(_C_GATE, _C_GLU)
    fq_ref[...] = (mm(_C_FQ, _C_FK) * HEAD_DIM ** -0.5).astype(fq_ref.dtype)
    fk_ref[...] = mm(_C_FK, _C_FV).astype(fk_ref.dtype)
    fv_ref[...] = mm(_C_FV, _C_END).astype(fv_ref.dtype)
    x_lo = (x - xb.astype(F32)).astype(MXU_DTYPE)
    hh_hl = jnp.dot(xb, ws_ref[...], preferred_element_type=F32)
    lh = jnp.dot(x_lo, ws_ref[:, 0:SMALL_COLS], preferred_element_type=F32)
    small_ref[...] = hh_hl[:, 0:SMALL_COLS] + hh_hl[:, SMALL_COLS:2 * SMALL_COLS] + lh


def _in_proj(xf, w_main, w_small, gdn_conv_w, cnv_w, cnv_b, cnv_ln_g, cnv_ln_b, cnv_scale, tm, tiles_per_seq):
    n = xf.shape[0]
    row = lambda i: (i, 0)
    const = lambda shape: pl.BlockSpec(shape, lambda i: (0, 0))
    outs = [
        jax.ShapeDtypeStruct((n, 3 * GDN_WIDTH), F32),
        jax.ShapeDtypeStruct((n, GDN_WIDTH), F32),
        jax.ShapeDtypeStruct((n, CONV_WIDTH), MXU_DTYPE),
        jax.ShapeDtypeStruct((n, FOX_WIDTH), MXU_DTYPE),
        jax.ShapeDtypeStruct((n, FOX_WIDTH), MXU_DTYPE),
        jax.ShapeDtypeStruct((n, FOX_WIDTH), MXU_DTYPE),
        jax.ShapeDtypeStruct((n, SMALL_COLS), F32),
    ]
    vec = const((1, CONV_WIDTH))
    return pl.pallas_call(
        functools.partial(_in_proj_kernel, tiles_per_seq=tiles_per_seq),
        grid=(n // tm,),
        in_specs=[pl.BlockSpec((tm, D_MODEL), row),
                  const((D_MODEL, _C_END)), const((D_MODEL, 2 * SMALL_COLS)),
                  const((GDN_SHORT_CONV, 3 * GDN_WIDTH)), const((CONV_KERNEL, CONV_WIDTH)), vec, vec, vec, vec],
        out_specs=[pl.BlockSpec((tm, o.shape[1]), row) for o in outs],
        out_shape=outs,
        scratch_shapes=[pltpu.VMEM((tm + SUBLANES, 3 * GDN_WIDTH), F32),
                        pltpu.VMEM((tm + CONV_HALO, CONV_WIDTH), F32),
                        pltpu.VMEM((SUBLANES - 1, tm + CONV_HALO - SUBLANES, CONV_WIDTH), F32)],
        compiler_params=_cparams("arbitrary"),
        name="in_proj",
    )(xf, w_main, w_small, gdn_conv_w, cnv_w, cnv_b, cnv_ln_g, cnv_ln_b, cnv_scale)


AUG_BIAS_LANE = HEAD_DIM
KEY_NORM_SLACK = 1.01


def _gates_kernel(small_ref, a_ref, dtb_ref, fb_ref, fq_ref, fk_ref, fv_ref,
                  g_ref, beta_ref, gt_ref, st_ref, qa_ref, ka_ref, va_ref, carry_ref):
    t = pl.program_id(1)
    tt = small_ref.shape[1]
    b16 = jnp.bfloat16

    @pl.when(t == 0)
    def _():
        carry_ref[...] = jnp.zeros_like(carry_ref)

    s = small_ref[0]
    log_decay = -jnp.exp(a_ref[...]) * _softplus(s + dtb_ref[...])
    beta = _sigmoid(s)
    log_f = _log_sigmoid(s + fb_ref[...])

    lane = lax.broadcasted_iota(jnp.int32, (1, LANES), 1)
    rr = lax.broadcasted_iota(jnp.int32, (LANES, LANES), 0)
    cc = lax.broadcasted_iota(jnp.int32, (LANES, LANES), 1)
    tri_chunk = ((cc <= rr) & (rr // GDN_CHUNK == cc // GDN_CHUNK)).astype(b16)
    both = jnp.where(lane < 2 * GDN_HEADS, log_decay, log_f)
    in_chunk = jnp.concatenate([_sel_lhs(tri_chunk, both[r0:r0 + LANES]) for r0 in range(0, tt, LANES)], axis=0)
    g_cum = in_chunk
    carry = carry_ref[...]
    pieces = []
    for r0 in range(0, tt, GDN_CHUNK):
        chunk = in_chunk[r0:r0 + GDN_CHUNK]
        pieces.append(chunk + carry)
        carry = carry + chunk[GDN_CHUNK - 1:GDN_CHUNK]
    c_cum = jnp.concatenate(pieces, axis=0)
    carry_ref[...] = carry

    er = lax.broadcasted_iota(jnp.int32, (LANES, 2 * GDN_WIDTH), 0)
    ec = lax.broadcasted_iota(jnp.int32, (LANES, 2 * GDN_WIDTH), 1) // HEAD_DIM
    expanded = _sel_rhs(jnp.where(lane < GDN_HEADS, g_cum, beta), (er == ec).astype(b16))
    g_ref[0] = expanded[:, 0:GDN_WIDTH]
    beta_ref[0] = expanded[:, GDN_WIDTH:2 * GDN_WIDTH]
    gt_ref[0] = g_cum.T[0:SUBLANES, :]

    head_ones = _head_ones().astype(b16)
    stat_rows = []
    for p in range(N_PAIRS):
        k_pair = fk_ref[0, :, p * PAIR:(p + 1) * PAIR].astype(F32)
        k_sq = jnp.dot((k_pair * k_pair).astype(b16), head_ones, preferred_element_type=F32)
        stat_rows.append(jnp.max(k_sq, axis=0, keepdims=True))
    stat_rows.append(c_cum[tt - 1:tt, :])
    stat_rows.append(jnp.zeros((SUBLANES - len(stat_rows), LANES), F32))
    st_ref[0, 0] = jnp.concatenate(stat_rows, axis=0)

    q_ones = ((lane >= AUG_BIAS_LANE) & (lane < AUG_BIAS_LANE + 3)).astype(F32)
    v_one = (lane == AUG_BIAS_LANE).astype(F32)
    br = lax.broadcasted_iota(jnp.int32, (3 * LANES, FOX_HEADS * LANES), 0)
    bc = lax.broadcasted_iota(jnp.int32, (3 * LANES, FOX_HEADS * LANES), 1)
    place = ((br % LANES == 2 * GDN_HEADS + bc // LANES) & (bc % LANES == AUG_BIAS_LANE + br // LANES)).astype(b16)
    bias_all = jnp.dot(jnp.concatenate(_split3(-c_cum), axis=1), place, preferred_element_type=F32)
    upper_to_lower = ((cc < HEAD_DIM) & (rr == cc + HEAD_DIM)).astype(b16)
    for h in range(FOX_HEADS):
        p, half = divmod(h, 2)
        lanes = slice(p * PAIR, (p + 1) * PAIR)

        def pick(ref):
            if half == 0:
                return jnp.where(lane < HEAD_DIM, ref[0, :, lanes].astype(F32), 0.0)
            return jnp.dot(ref[0, :, lanes], upper_to_lower, preferred_element_type=F32)

        qa_ref[0, h] = (pick(fq_ref) + q_ones).astype(b16)
        ka_ref[0, h] = (pick(fk_ref) + bias_all[:, h * LANES:(h + 1) * LANES]).astype(b16)
        va_ref[0, h] = (pick(fv_ref) + v_one).astype(b16)


def _gates(small, a_row, dtb_row, fb_row, fq, fk, fv, tt):
    b, t, _ = small.shape
    row = pl.BlockSpec((1, LANES), lambda i, j: (0, 0))
    tile = lambda w: pl.BlockSpec((1, tt, w), lambda i, j: (i, j, 0))
    aug = pl.BlockSpec((1, FOX_HEADS, tt, LANES), lambda i, j: (i, 0, j, 0))
    aug_shape = jax.ShapeDtypeStruct((b, FOX_HEADS, t, LANES), jnp.bfloat16)
    return pl.pallas_call(
        _gates_kernel,
        grid=(b, t // tt),
        in_specs=[tile(SMALL_COLS), row, row, row, tile(FOX_WIDTH), tile(FOX_WIDTH), tile(FOX_WIDTH)],
        out_specs=[tile(GDN_WIDTH), tile(GDN_WIDTH), pl.BlockSpec((1, SUBLANES, tt), lambda i, j: (i, 0, j)),
                   pl.BlockSpec((1, 1, SUBLANES, LANES), lambda i, j: (i, j, 0, 0)), aug, aug, aug],
        out_shape=[jax.ShapeDtypeStruct((b, t, GDN_WIDTH), F32), jax.ShapeDtypeStruct((b, t, GDN_WIDTH), F32),
                   jax.ShapeDtypeStruct((b, SUBLANES, t), F32),
                   jax.ShapeDtypeStruct((b, t // tt, SUBLANES, LANES), F32), aug_shape, aug_shape, aug_shape],
        scratch_shapes=[pltpu.VMEM((1, LANES), F32)],
        compiler_params=_cparams("parallel", "arbitrary"),
        name="gates",
    )(small, a_row, dtb_row, fb_row, fq, fk, fv)


def _unit_lower_inverse_many(lows):
    n = lows[0].shape[0]
    r = lax.broadcasted_iota(jnp.int32, (n, n), 0)
    c = lax.broadcasted_iota(jnp.int32, (n, n), 1)
    eye = (r == c).astype(F32)
    base = GDN_CHUNK // 4
    diag = r // base == c // base
    cast = lambda xs: [x.astype(MXU_DTYPE) for x in xs]
    mm = lambda xs, ys: [jnp.dot(x, y, preferred_element_type=F32) for x, y in zip(xs, ys)]
    add = lambda xs, ys: [x + y for x, y in zip(xs, ys)]
    d = [jnp.where(diag, low, 0.0) for low in lows]
    db = cast(d)
    d2b = cast(mm(db, db))
    p = [eye - x for x in d]
    p = add(p, mm(cast(p), d2b))
    d4b = cast(mm(d2b, d2b))
    p = add(p, mm(cast(p), d4b))
    d8b = cast(mm(d4b, d4b))
    x = add(p, mm(cast(p), d8b))
    for blk in (2 * base, 4 * base):
        sel = (r // blk == c // blk) & (r // (blk // 2) != c // (blk // 2))
        xb = cast(x)
        xo = mm(xb, cast([jnp.where(sel, low, 0.0) for low in lows]))
        x = [a - b for a, b in zip(x, mm(cast(xo), xb))]
    return x


def _gdn_kernel(qkv_ref, g_ref, beta_ref, gt_ref, gate_ref, nw_ref, ms_ref, o_ref, s_ref):
    t = pl.program_id(1)
    blk = qkv_ref.shape[1]
    nc = blk // GDN_CHUNK

    @pl.when(t == 0)
    def _():
        s_ref[...] = jnp.zeros_like(s_ref)

    qkv = qkv_ref[0]

    ones = _head_ones().astype(jnp.bfloat16)
    lane = lax.broadcasted_iota(jnp.int32, (1, PAIR), 1)
    head0 = lane < HEAD_DIM
    n2 = 2 * GDN_CHUNK
    r = lax.broadcasted_iota(jnp.int32, (n2, n2), 0)
    c = lax.broadcasted_iota(jnp.int32, (n2, n2), 1)
    same = (r // GDN_CHUNK) == (c // GDN_CHUNK)
    causal = same & (c <= r)
    strict = same & (c < r)

    def l2n(v):
        return v * lax.rsqrt(_dot_ones(v * v, ones) + NORM_EPS)

    def stack(v):
        return jnp.concatenate([jnp.where(head0, v, 0.0), jnp.where(head0, 0.0, v)], axis=0)

    pair = lambda p, grp: slice(grp * GDN_WIDTH + p * PAIR, grp * GDN_WIDTH + (p + 1) * PAIR)
    q_p = [l2n(qkv[:, pair(p, 0)]) * HEAD_DIM ** -0.5 for p in range(N_PAIRS)]
    k_p = [l2n(qkv[:, pair(p, 1)]) for p in range(N_PAIRS)]
    v_p = [qkv[:, pair(p, 2)] for p in range(N_PAIRS)]

    units = [(ci, p) for ci in range(nc) for p in range(N_PAIRS)]
    rows = lambda ci: slice(ci * GDN_CHUNK, (ci + 1) * GDN_CHUNK)
    gc = [g_ref[0, rows(ci), pair(p, 0)] for ci, p in units]
    bt = [beta_ref[0, rows(ci), pair(p, 0)] for ci, p in units]
    qn = [q_p[p][rows(ci)] for ci, p in units]
    kn = [k_p[p][rows(ci)] for ci, p in units]
    vv = [v_p[p][rows(ci)] for ci, p in units]
    g_last = [g[GDN_CHUNK - 1:GDN_CHUNK, :] for g in gc]
    eg = [jnp.exp(g) for g in gc]
    kb = [k * b for k, b in zip(kn, bt)]
    k2 = [stack(k) for k in kn]

    def decay_of(g, ci, p):
        g_col = jnp.concatenate([jnp.broadcast_to(g[:, 0:1], (GDN_CHUNK, PAIR)),
                                 jnp.broadcast_to(g[:, HEAD_DIM:HEAD_DIM + 1], (GDN_CHUNK, PAIR))], axis=0)
        g_row = jnp.concatenate([gt_ref[0, 2 * p:2 * p + 1, rows(ci)], gt_ref[0, 2 * p + 1:2 * p + 2, rows(ci)]],
                                axis=1)
        return jnp.where(causal, jnp.exp(jnp.where(causal, g_col - g_row, 0.0)), 0.0)

    decay = [decay_of(g, ci, p) for g, (ci, p) in zip(gc, units)]
    k2b = [x.astype(MXU_DTYPE) for x in k2]
    low = [jnp.where(strict, _dot_nt(stack(a), b) * d, 0.0) for a, b, d in zip(kb, k2b, decay)]
    a_in = [(_dot_nt(stack(a), b) * d).astype(MXU_DTYPE) for a, b, d in zip(qn, k2b, decay)]
    t_inv = _unit_lower_inverse_many(low)
    uw = [_dot(ti, jnp.concatenate([stack(v * b), stack(a * e)], axis=1)).astype(MXU_DTYPE)
          for ti, v, b, a, e in zip(t_inv, vv, bt, kb, eg)]
    ket = [stack(k * jnp.exp(gl - g)).T for k, gl, g in zip(kn, g_last, gc)]
    nm = [_dot(a, b) for a, b in zip(ket, uw)]
    raw = [jnp.dot(a, b, preferred_element_type=F32) for a, b in zip(a_in, uw)]
    p_mat = [stack(q * e) - x[:, PAIR:] for q, e, x in zip(qn, eg, raw)]

    state = [s_ref[p] for p in range(N_PAIRS)]
    for i, (ci, p) in enumerate(units):
        s = state[p]
        o2 = _dot(p_mat[i], s) + raw[i][:, :PAIR]
        state[p] = s * jnp.exp(g_last[i]) - _dot(nm[i][:, PAIR:], s) + nm[i][:, :PAIR]
        o = o2[0:GDN_CHUNK] + o2[GDN_CHUNK:n2]
        ms = _dot_ones(o * o, ones) * (1.0 / HEAD_DIM)
        on = o * lax.rsqrt(ms + NORM_EPS) * nw_ref[...]
        o_ref[0, rows(ci), pair(p, 0)] = (on * _silu(gate_ref[0, rows(ci), pair(p, 0)])
                                          * ms_ref[:, pair(p, 0)]).astype(o_ref.dtype)
    for p in range(N_PAIRS):
        s_ref[p] = state[p]


def _gdn(qkv, g, beta, g_t, gate, norm_w2, mscale, blk):
    b, t, _ = qkv.shape
    tile = lambda width: pl.BlockSpec((1, blk, width), lambda i, j: (i, j, 0))
    const = lambda shape: pl.BlockSpec(shape, lambda i, j: (0, 0))
    return pl.pallas_call(
        _gdn_kernel,
        grid=(b, t // blk),
        in_specs=[tile(3 * GDN_WIDTH),
                  tile(GDN_WIDTH), tile(GDN_WIDTH), pl.BlockSpec((1, SUBLANES, blk), lambda i, j: (i, 0, j)),
                  tile(GDN_WIDTH), const((1, PAIR)), const((1, GDN_WIDTH))],
        out_specs=tile(GDN_WIDTH),
        out_shape=jax.ShapeDtypeStruct((b, t, GDN_WIDTH), MXU_DTYPE),
        scratch_shapes=[pltpu.VMEM((N_PAIRS, PAIR, PAIR), F32)],
        compiler_params=_cparams("parallel", "arbitrary"),
        name="gdn",
    )(qkv, g, beta, g_t, gate, norm_w2, mscale)


FOX_ROW_GROUP = 32


FOX_SKIP_MARGIN = 106.0


def _fox_kernel(q_ref, k_ref, v_ref, aux_ref, ms_ref, o_ref, s0_ref, s1_ref, p0_ref, p1_ref, a0_ref, a1_ref,
                m_ref, acc_ref, *, tq):
    i = pl.program_id(2)
    heads = range(2)
    rg = FOX_ROW_GROUP
    m_ref[...] = jnp.full(m_ref.shape, -jnp.inf, F32)
    acc_ref[...] = jnp.zeros(acc_ref.shape, F32)

    def scores(j, s_ref):
        start = pl.multiple_of(j * tq, tq)
        for h in heads:
            s_ref[h] = lax.dot_general(q_ref[0, h], k_ref[0, h, pl.ds(start, tq), :],
                                       (((1,), (1,)), ((), ())), preferred_element_type=F32)

    def softmax(s_ref, p_ref, a_ref, masked):
        for h in heads:
            for g in range(tq // rg):
                rows = slice(g * rg, (g + 1) * rg)
                width = LANES * (((g + 1) * rg - 1) // LANES + 1) if masked else tq
                s = s_ref[h, rows, 0:width]
                if masked:
                    row_id = g * rg + lax.broadcasted_iota(jnp.int32, (rg, width), 0)
                    col_id = lax.broadcasted_iota(jnp.int32, (rg, width), 1)
                    s = jnp.where(col_id <= row_id, s, -jnp.inf)
                    if width < tq:
                        p_ref[h, rows, width:tq] = jnp.zeros((rg, tq - width), p_ref.dtype)
                m_old = m_ref[h, rows, :]
                m_new = jnp.maximum(m_old, jnp.max(s, axis=-1, keepdims=True))
                a_ref[h, rows, :] = jnp.exp(m_old - m_new)
                m_ref[h, rows, :] = m_new
                m_wide = jnp.concatenate([m_new] * (width // LANES), axis=1)
                p_ref[h, rows, 0:width] = jnp.exp(s - m_wide).astype(p_ref.dtype)

    def weighted_values(j, p_ref, a_ref):
        start = pl.multiple_of(j * tq, tq)
        for h in heads:
            acc_ref[h] = a_ref[h] * acc_ref[h] + jnp.dot(p_ref[h], v_ref[0, h, pl.ds(start, tq), :],
                                                         preferred_element_type=F32)

    scores(i, s0_ref)
    softmax(s0_ref, p0_ref, a0_ref, True)
    scores(jnp.maximum(i - 1, 0), s1_ref)
    weighted_values(i, p0_ref, a0_ref)

    lane = lax.broadcasted_iota(jnp.int32, (1, LANES), 1)
    n = jnp.int32(0)
    for h in heads:
        q = q_ref[0, h].astype(F32)
        q_norm = jnp.sqrt(jnp.sum(jnp.where(lane < HEAD_DIM, q * q, 0.0), axis=-1, keepdims=True))
        key_norm = aux_ref[0, h, 1:2, :]
        block_end_c = aux_ref[0, h, 0:1, :]
        slack = jnp.max(q_norm * key_norm - m_ref[h], axis=0, keepdims=True)
        keep = (lane < i) & (slack - block_end_c >= -FOX_SKIP_MARGIN)
        n = jnp.maximum(n, jnp.sum(keep.astype(jnp.int32)))

    def body(t, carry):
        j = i - 1 - 2 * t
        softmax(s1_ref, p1_ref, a1_ref, False)
        scores(jnp.maximum(j - 1, 0), s0_ref)
        weighted_values(j, p1_ref, a1_ref)
        softmax(s0_ref, p0_ref, a0_ref, False)
        scores(jnp.maximum(j - 2, 0), s1_ref)
        weighted_values(j - 1, p0_ref, a0_ref)
        return carry

    lax.fori_loop(0, n // 2, body, 0)

    @pl.when(n % 2 == 1)
    def _():
        softmax(s1_ref, p1_ref, a1_ref, False)
        weighted_values(i - n, p1_ref, a1_ref)

    o = [acc_ref[h] / acc_ref[h][:, AUG_BIAS_LANE:AUG_BIAS_LANE + 1] for h in heads]
    lane = lax.broadcasted_iota(jnp.int32, (1, PAIR), 1)
    o_pair = jnp.where(lane < HEAD_DIM, o[0], pltpu.roll(o[1], HEAD_DIM, axis=1))
    o_ref[0] = (o_pair * ms_ref[...]).astype(o_ref.dtype)


def _fox_aux(stats):
    b, nt = stats.shape[:2]
    k_sq = jnp.max(stats[:, :, :N_PAIRS, :], axis=1)
    k_sq = k_sq.reshape(b, N_PAIRS, 2, HEAD_DIM)[..., 0].reshape(b, FOX_HEADS)
    key_norm = jnp.sqrt(k_sq) * KEY_NORM_SLACK
    block_end_c = jnp.transpose(stats[:, :, N_PAIRS, 2 * GDN_HEADS:2 * GDN_HEADS + FOX_HEADS], (0, 2, 1))
    aux = jnp.zeros((b, FOX_HEADS, SUBLANES, LANES), F32)
    aux = aux.at[:, :, 0, :nt].set(block_end_c)
    return aux.at[:, :, 1, :].set(jnp.broadcast_to(key_norm[:, :, None], (b, FOX_HEADS, LANES)))


def _fox(qa, ka, va, aux, mscale, tq):
    b, _, t, _ = qa.shape
    kern = functools.partial(_fox_kernel, tq=tq)
    return pl.pallas_call(
        kern,
        grid=(b, N_PAIRS, t // tq),
        in_specs=[pl.BlockSpec((1, 2, tq, LANES), lambda bi, p, i: (bi, p, i, 0)),
                  pl.BlockSpec((1, 2, t, LANES), lambda bi, p, i: (bi, p, 0, 0)),
                  pl.BlockSpec((1, 2, t, LANES), lambda bi, p, i: (bi, p, 0, 0)),
                  pl.BlockSpec((1, 2, SUBLANES, LANES), lambda bi, p, i: (bi, p, 0, 0)),
                  pl.BlockSpec((1, PAIR), lambda bi, p, i: (0, p))],
        out_specs=pl.BlockSpec((1, tq, PAIR), lambda bi, p, i: (bi, i, p)),
        out_shape=jax.ShapeDtypeStruct((b, t, FOX_WIDTH), MXU_DTYPE),
        scratch_shapes=[pltpu.VMEM((2, tq, tq), F32), pltpu.VMEM((2, tq, tq), F32),
                        pltpu.VMEM((2, tq, tq), jnp.bfloat16), pltpu.VMEM((2, tq, tq), jnp.bfloat16),
                        pltpu.VMEM((2, tq, LANES), F32), pltpu.VMEM((2, tq, LANES), F32),
                        pltpu.VMEM((2, tq, LANES), F32), pltpu.VMEM((2, tq, LANES), F32)],
        compiler_params=_cparams("parallel", "parallel", "arbitrary"),
        name="fox_attention",
    )(qa, ka, va, aux, mscale)


def _out_proj_kernel(oa_ref, ob_ref, oc_ref, x_ref, w_ref, g_ref, b_ref, o_ref):
    mix = jnp.dot(oa_ref[...], w_ref[0:GDN_WIDTH, :], preferred_element_type=F32)
    mix = mix + jnp.dot(ob_ref[...], w_ref[GDN_WIDTH:GDN_WIDTH + CONV_WIDTH, :], preferred_element_type=F32)
    mix = mix + jnp.dot(oc_ref[...], w_ref[GDN_WIDTH + CONV_WIDTH:D_MODEL, :], preferred_element_type=F32)
    o_ref[...] = _layer_norm_rows(DEEPNORM_ALPHA * x_ref[...] + mix, g_ref[...], b_ref[...])


def _out_proj(oa, ob, oc, xf, w, g, bvec, tm):
    n = xf.shape[0]
    row = lambda i: (i, 0)
    const = lambda i: (0, 0)
    return pl.pallas_call(
        _out_proj_kernel,
        grid=(n // tm,),
        in_specs=[pl.BlockSpec((tm, GDN_WIDTH), row), pl.BlockSpec((tm, CONV_WIDTH), row),
                  pl.BlockSpec((tm, FOX_WIDTH), row), pl.BlockSpec((tm, D_MODEL), row),
                  pl.BlockSpec((D_MODEL, D_MODEL), const),
                  pl.BlockSpec((1, D_MODEL), const), pl.BlockSpec((1, D_MODEL), const)],
        out_specs=pl.BlockSpec((tm, D_MODEL), row),
        out_shape=jax.ShapeDtypeStruct((n, D_MODEL), F32),
        compiler_params=_cparams("parallel"),
        name="out_proj_ln",
    )(oa, ob, oc, xf, w, g, bvec)


FFN_CHUNK = 256


def _swiglu_chunks(xb, w1, w3, w2, width):
    assert width % FFN_CHUNK == 0
    acc = None
    for c in range(width // FFN_CHUNK):
        cols = slice(c * FFN_CHUNK, (c + 1) * FFN_CHUNK)
        h = _silu(jnp.dot(xb, w1(slice(None), cols), preferred_element_type=F32)) * jnp.dot(
            xb, w3(slice(None), cols), preferred_element_type=F32)
        part = jnp.dot(h.astype(xb.dtype), w2(cols, slice(None)), preferred_element_type=F32)
        acc = part if acc is None else acc + part
    return acc


def _ffn_kernel(x_ref, w1_ref, w3_ref, w2_ref, g_ref, b_ref, o_ref):
    x = x_ref[...]
    ff = _swiglu_chunks(x.astype(w1_ref.dtype), lambda r, c: w1_ref[r, c], lambda r, c: w3_ref[r, c],
                        lambda r, c: w2_ref[r, c], w1_ref.shape[1])
    o_ref[...] = _layer_norm_rows(DEEPNORM_ALPHA * x + ff, g_ref[...], b_ref[...])


def _dense_ffn(xf, w1, w3, w2, g, bvec, tm):
    n = xf.shape[0]
    ff = w1.shape[1]
    const = lambda shape: pl.BlockSpec(shape, lambda i: (0, 0), pipeline_mode=pl.Buffered(1))
    return pl.pallas_call(
        _ffn_kernel,
        grid=(n // tm,),
        in_specs=[pl.BlockSpec((tm, D_MODEL), lambda i: (i, 0)),
                  const((D_MODEL, ff)), const((D_MODEL, ff)), const((ff, D_MODEL)),
                  const((1, D_MODEL)), const((1, D_MODEL))],
        out_specs=pl.BlockSpec((tm, D_MODEL), lambda i: (i, 0)),
        out_shape=jax.ShapeDtypeStruct((n, D_MODEL), F32),
        compiler_params=_cparams("parallel"),
        name="dense_ffn_ln",
    )(xf, w1, w3, w2, g, bvec)


def _router_kernel(x_ref, wr_ref, info_ref, info_t_ref, cnt_ref, run_ref):
    i = pl.program_id(0)
    tm = x_ref.shape[0]

    @pl.when(i == 0)
    def _():
        run_ref[...] = jnp.zeros_like(run_ref)

    logits = jnp.dot(x_ref[...].astype(MXU_DTYPE), wr_ref[...], preferred_element_type=F32)
    lane = lax.broadcasted_iota(jnp.int32, (tm, LANES), 1)
    logits = jnp.where(lane < N_EXPERTS, logits, -jnp.inf)
    m1 = jnp.max(logits, axis=-1, keepdims=True)
    e1 = jnp.min(jnp.where(logits == m1, lane, LANES), axis=-1, keepdims=True)
    rest = jnp.where(lane == e1, -jnp.inf, logits)
    m2 = jnp.max(rest, axis=-1, keepdims=True)
    e2 = jnp.min(jnp.where(rest == m2, lane, LANES), axis=-1, keepdims=True)
    z = jnp.exp(m2 - m1)
    g1 = 1.0 / (1.0 + z)
    g2 = z / (1.0 + z)
    onehot = ((lane == e1) | (lane == e2)).astype(F32)
    r = lax.broadcasted_iota(jnp.int32, (tm, tm), 0)
    c = lax.broadcasted_iota(jnp.int32, (tm, tm), 1)
    before = jnp.dot((c < r).astype(jnp.bfloat16), onehot.astype(jnp.bfloat16),
                     preferred_element_type=F32) + run_ref[...]
    rank1 = jnp.sum(jnp.where(lane == e1, before, 0.0), axis=-1, keepdims=True)
    rank2 = jnp.sum(jnp.where(lane == e2, before, 0.0), axis=-1, keepdims=True)
    run_ref[...] = run_ref[...] + jnp.sum(onehot, axis=0, keepdims=True)
    cnt_ref[...] = run_ref[...]
    info = jnp.where(lane == 0, e1.astype(F32),
                     jnp.where(lane == 1, e2.astype(F32),
                               jnp.where(lane == 2, rank1,
                                         jnp.where(lane == 3, rank2,
                                                   jnp.where(lane == 4, g1, jnp.where(lane == 5, g2, 0.0))))))
    info_ref[...] = info
    info_t_ref[...] = info.T[0:SUBLANES, :]


def _router(xf, wr_pad, tm):
    n = xf.shape[0]
    return pl.pallas_call(
        _router_kernel,
        grid=(n // tm,),
        in_specs=[pl.BlockSpec((tm, D_MODEL), lambda i: (i, 0)),
                  pl.BlockSpec((D_MODEL, LANES), lambda i: (0, 0))],
        out_specs=[pl.BlockSpec((tm, LANES), lambda i: (i, 0)),
                   pl.BlockSpec((SUBLANES, tm), lambda i: (0, i)),
                   pl.BlockSpec((1, LANES), lambda i: (0, 0))],
        out_shape=[jax.ShapeDtypeStruct((n, LANES), F32), jax.ShapeDtypeStruct((SUBLANES, n), F32),
                   jax.ShapeDtypeStruct((1, LANES), F32)],
        scratch_shapes=[pltpu.VMEM((1, LANES), F32)],
        compiler_params=_cparams("arbitrary"),
        name="moe_router",
    )(xf, wr_pad)


DMA_LOOP_UNROLL = 8


def _dispatch_kernel(fill_ref, d1_ref, d2_ref, x_ref, xs_ref, zero_ref, sem):
    tm = x_ref.shape[0]

    @pl.when(pl.program_id(0) == 0)
    def _():
        zero_ref[...] = jnp.zeros_like(zero_ref)

        def strip_copy(e, k):
            start = pl.multiple_of(fill_ref[e] - SUBLANES * (k + 1), SUBLANES)
            return pltpu.make_async_copy(zero_ref, xs_ref.at[pl.ds(start, SUBLANES), :], sem)

        for e in range(N_EXPERTS):
            lax.fori_loop(0, fill_ref[N_EXPERTS + e], lambda k, c, e=e: (strip_copy(e, k).start(), c)[1], 0)
        for e in range(N_EXPERTS):
            lax.fori_loop(0, fill_ref[N_EXPERTS + e], lambda k, c, e=e: (strip_copy(e, k).wait(), c)[1], 0)

    def row_copy(r, dst):
        return pltpu.make_async_copy(x_ref.at[pl.ds(r, 1), :], xs_ref.at[pl.ds(dst, 1), :], sem)

    def issue(r, carry):
        row_copy(r, d1_ref[0, 0, r]).start()
        row_copy(r, d2_ref[0, 0, r]).start()
        return carry

    lax.fori_loop(0, tm, issue, 0, unroll=DMA_LOOP_UNROLL)

    def drain(r, carry):
        row_copy(r, d1_ref[0, 0, r]).wait()
        row_copy(r, d2_ref[0, 0, r]).wait()
        return carry

    lax.fori_loop(0, tm, drain, 0, unroll=DMA_LOOP_UNROLL)


def _dispatch(xf, d1, d2, gap_start, gap_end, rows, tm):
    n = xf.shape[0]
    idx = lambda a: a.reshape(n // tm, 1, tm)
    n_strips = (gap_end - gap_start + SUBLANES - 1) // SUBLANES
    fill = jnp.concatenate([gap_end, n_strips]).astype(jnp.int32)
    smem = lambda: pl.BlockSpec((1, 1, tm), lambda i, fill: (i, 0, 0), memory_space=pltpu.SMEM)
    grid_spec = pltpu.PrefetchScalarGridSpec(
        num_scalar_prefetch=1,
        grid=(n // tm,),
        in_specs=[smem(), smem(), pl.BlockSpec((tm, D_MODEL), lambda i, fill: (i, 0))],
        out_specs=pl.BlockSpec(memory_space=pl.ANY),
        scratch_shapes=[pltpu.VMEM((SUBLANES, D_MODEL), xf.dtype), pltpu.SemaphoreType.DMA(())],
    )
    return pl.pallas_call(
        _dispatch_kernel,
        grid_spec=grid_spec,
        out_shape=jax.ShapeDtypeStruct((rows, D_MODEL), xf.dtype),
        compiler_params=_cparams("arbitrary"),
        name="moe_dispatch",
    )(fill, idx(d1), idx(d2), xf)


def _expert_kernel(te_ref, nu_ref, xs_ref, w1_ref, w3_ref, w2_ref, ys_ref, xb_ref):
    del te_ref
    i = pl.program_id(0)
    f = pl.program_id(1)
    used = i < nu_ref[0]

    def partial_ffn(xb):
        return _swiglu_chunks(xb, lambda r, c: w1_ref[0, r, c], lambda r, c: w3_ref[0, r, c],
                              lambda r, c: w2_ref[0, r, c], w1_ref.shape[2])

    @pl.when(used & (f == 0))
    def _():
        xb = xs_ref[...].astype(xb_ref.dtype)
        xb_ref[...] = xb
        ys_ref[...] = partial_ffn(xb)

    @pl.when(used & (f > 0))
    def _():
        ys_ref[...] += partial_ffn(xb_ref[...])

    @pl.when(jnp.logical_not(used) & (f == 0))
    def _():
        ys_ref[...] = jnp.zeros_like(ys_ref)


def _experts(xs, tile_expert, n_used, w1, w3, w2, tm, tf):
    rows = xs.shape[0]
    ff = w1.shape[2]
    n_tiles = rows // tm

    def x_map(i, f, te, nu):
        return (jnp.maximum(jnp.minimum(i, nu[0] - 1), 0), 0)

    def w13_map(i, f, te, nu):
        return (te[i], 0, jnp.where(i < nu[0], f, ff // tf - 1))

    def w2_map(i, f, te, nu):
        return (te[i], jnp.where(i < nu[0], f, ff // tf - 1), 0)

    grid_spec = pltpu.PrefetchScalarGridSpec(
        num_scalar_prefetch=2,
        grid=(n_tiles, ff // tf),
        in_specs=[pl.BlockSpec((tm, D_MODEL), x_map),
                  pl.BlockSpec((1, D_MODEL, tf), w13_map),
                  pl.BlockSpec((1, D_MODEL, tf), w13_map),
                  pl.BlockSpec((1, tf, D_MODEL), w2_map)],
        out_specs=pl.BlockSpec((tm, D_MODEL), lambda i, f, te, nu: (i, 0)),
        scratch_shapes=[pltpu.VMEM((tm, D_MODEL), MXU_DTYPE)],
    )
    return pl.pallas_call(
        _expert_kernel,
        grid_spec=grid_spec,
        out_shape=jax.ShapeDtypeStruct((rows, D_MODEL), F32),
        compiler_params=_cparams("arbitrary", "arbitrary"),
        name="moe_experts",
    )(tile_expert, n_used, xs, w1, w3, w2)


def _combine_kernel(d1_ref, d2_ref, x_ref, info_ref, ys_ref, g_ref, b_ref, o_ref, y1_ref, y2_ref, sem):
    tm = x_ref.shape[0]

    def row_copy(src, r, buf):
        return pltpu.make_async_copy(ys_ref.at[pl.ds(src, 1), :], buf.at[pl.ds(r, 1), :], sem)

    def issue(r, carry):
        row_copy(d1_ref[0, 0, r], r, y1_ref).start()
        row_copy(d2_ref[0, 0, r], r, y2_ref).start()
        return carry

    lax.fori_loop(0, tm, issue, 0, unroll=DMA_LOOP_UNROLL)

    def drain(r, carry):
        row_copy(d1_ref[0, 0, r], r, y1_ref).wait()
        row_copy(d2_ref[0, 0, r], r, y2_ref).wait()
        return carry

    lax.fori_loop(0, tm, drain, 0, unroll=DMA_LOOP_UNROLL)
    info = info_ref[...]
    ff = info[:, 4:5] * y1_ref[...] + info[:, 5:6] * y2_ref[...]
    o_ref[...] = _layer_norm_rows(DEEPNORM_ALPHA * x_ref[...] + ff, g_ref[...], b_ref[...])


def _combine(xf, info, ys, d1, d2, g, bvec, tm):
    n = xf.shape[0]
    idx = lambda a: a.reshape(n // tm, 1, tm)
    smem = lambda: pl.BlockSpec((1, 1, tm), lambda i: (i, 0, 0), memory_space=pltpu.SMEM)
    return pl.pallas_call(
        _combine_kernel,
        grid=(n // tm,),
        in_specs=[smem(), smem(),
                  pl.BlockSpec((tm, D_MODEL), lambda i: (i, 0)),
                  pl.BlockSpec((tm, LANES), lambda i: (i, 0)),
                  pl.BlockSpec(memory_space=pl.ANY),
                  pl.BlockSpec((1, D_MODEL), lambda i: (0, 0)),
                  pl.BlockSpec((1, D_MODEL), lambda i: (0, 0))],
        out_specs=pl.BlockSpec((tm, D_MODEL), lambda i: (i, 0)),
        out_shape=jax.ShapeDtypeStruct((n, D_MODEL), F32),
        scratch_shapes=[pltpu.VMEM((tm, D_MODEL), F32), pltpu.VMEM((tm, D_MODEL), F32),
                        pltpu.SemaphoreType.DMA(())],
        compiler_params=_cparams("arbitrary"),
        name="moe_combine_ln",
    )(idx(d1), idx(d2), xf, info, ys, g, bvec)


MOE_TILE = 512


def _moe_ffn(xf, w_router, w1, w3, w2, g, bvec, tm_tok, tf):
    n = xf.shape[0]
    wr_pad = jnp.zeros((D_MODEL, LANES), MXU_DTYPE).at[:, :N_EXPERTS].set(w_router.astype(MXU_DTYPE))
    info, info_t, counts = _router(xf, wr_pad, tm_tok)
    sizes = counts[0, :N_EXPERTS].astype(jnp.int32)
    tiles_per = (sizes + MOE_TILE - 1) // MOE_TILE
    tile_end = jnp.cumsum(tiles_per)
    seg_start = (tile_end - tiles_per) * MOE_TILE
    experts = jnp.arange(N_EXPERTS, dtype=jnp.int32)[:, None]

    def sorted_row(expert_row, rank_row):
        start = jnp.sum(jnp.where(expert_row.astype(jnp.int32)[None, :] == experts, seg_start[:, None], 0), axis=0)
        return start + rank_row.astype(jnp.int32)

    d1 = sorted_row(info_t[0], info_t[2])
    d2 = sorted_row(info_t[1], info_t[3])
    n_tiles = (n * TOP_K) // MOE_TILE + N_EXPERTS
    rows = n_tiles * MOE_TILE
    tile_ids = jnp.arange(n_tiles, dtype=jnp.int32)
    tile_expert = jnp.minimum(jnp.sum((tile_end[None, :] <= tile_ids[:, None]).astype(jnp.int32), axis=1),
                              N_EXPERTS - 1)
    n_used = tile_end[N_EXPERTS - 1:].astype(jnp.int32)
    gap_end = jnp.concatenate([seg_start[1:], jnp.full((1,), rows, jnp.int32)])
    xs = _dispatch(xf, d1, d2, seg_start + sizes, gap_end, rows, tm_tok)
    ys = _experts(xs, tile_expert, n_used, w1, w3, w2, MOE_TILE, tf)
    return _combine(xf, info, ys, d1, d2, g, bvec, tm_tok)


def _pack_in_proj(w_in):
    cuts = np.cumsum([0, GDN_WIDTH, GDN_WIDTH, GDN_WIDTH, GDN_HEADS, GDN_HEADS, GDN_WIDTH,
                      2 * CONV_WIDTH, FOX_WIDTH, FOX_WIDTH, FOX_WIDTH, FOX_HEADS])
    seg = lambda i: w_in[:, cuts[i]:cuts[i + 1]]
    w_main = jnp.concatenate([seg(0), seg(1), seg(2), seg(5), seg(6), seg(7), seg(8), seg(9)], axis=1)
    pad = jnp.zeros((D_MODEL, SMALL_COLS - 2 * GDN_HEADS - FOX_HEADS), w_in.dtype)
    w_small = jnp.concatenate([seg(3), seg(4), seg(10), pad], axis=1).astype(F32)
    ws_hi = w_small.astype(MXU_DTYPE)
    ws_lo = (w_small - ws_hi.astype(F32)).astype(MXU_DTYPE)
    return w_main.astype(MXU_DTYPE), jnp.concatenate([ws_hi, ws_lo], axis=1)


def _lane_row(vals, offset):
    return jnp.zeros((1, LANES), F32).at[0, offset:offset + vals.shape[0]].set(vals.astype(F32))


def _mixer(xf, b, t, w_in, mix_scale, w_out, gdn_conv_w, gdn_a_log, gdn_dt_bias, gdn_norm_w,
           cnv_dw_w, cnv_dw_b, cnv_ln_g, cnv_ln_b, fox_f_bias, ln_g, ln_b, tiles):
    w_main, w_small = _pack_in_proj(w_in)
    ms = mix_scale.reshape(1, D_MODEL).astype(F32)
    row = lambda v: v.reshape(1, -1).astype(F32)
    qkv, gate, o_b, fq, fk, fv, small = _in_proj(
        xf, w_main, w_small, gdn_conv_w, cnv_dw_w, row(cnv_dw_b), row(cnv_ln_g), row(cnv_ln_b),
        ms[:, GDN_WIDTH:GDN_WIDTH + CONV_WIDTH], tiles["tm"], t // tiles["tm"])
    r3 = lambda a: a.reshape(b, t, a.shape[-1])
    assert tiles["tt"] == tiles["tq"]
    g_cum, beta, g_t, stats, qa, ka, va = _gates(
        r3(small), _lane_row(gdn_a_log, 0), _lane_row(gdn_dt_bias, 0), _lane_row(fox_f_bias, 2 * GDN_HEADS),
        r3(fq), r3(fk), r3(fv), tiles["tt"])
    norm_w2 = jnp.tile(gdn_norm_w.astype(F32), 2).reshape(1, PAIR)
    o_a = _gdn(r3(qkv), g_cum, beta, g_t, r3(gate), norm_w2, ms[:, :GDN_WIDTH], tiles["gdn_blk"])
    o_c = _fox(qa, ka, va, _fox_aux(stats), ms[:, GDN_WIDTH + CONV_WIDTH:], tiles["tq"])
    flat = lambda a: a.reshape(b * t, a.shape[-1])
    return _out_proj(flat(o_a), o_b, flat(o_c), xf, w_out.astype(MXU_DTYPE), row(ln_g), row(ln_b),
                     tiles["tm"])


def _tiles(t):
    return dict(tm=min(512, t), tt=min(512, t), gdn_blk=min(512, t), tq=min(512, t),
                tf_moe=1792)


def kernel(x, w_in, mix_scale, w_out, gdn_conv_w, gdn_a_log, gdn_dt_bias, gdn_norm_w, cnv_dw_w, cnv_dw_b,
           cnv_ln_g, cnv_ln_b, fox_f_bias, ln_mix_g, ln_mix_b, ln_ffn_g, ln_ffn_b, ffn_w1, ffn_w3, ffn_w2,
           moe_router, moe_w1, moe_w3, moe_w2):
    b, t, d = x.shape
    tiles = _tiles(t)
    row = lambda v: v.reshape(1, -1).astype(F32)
    xf = x.reshape(b * t, d)
    for l in range(DEPTH):
        xf = _mixer(xf, b, t, w_in[l], mix_scale[l], w_out[l], gdn_conv_w[l], gdn_a_log[l], gdn_dt_bias[l],
                    gdn_norm_w[l], cnv_dw_w[l], cnv_dw_b[l], cnv_ln_g[l], cnv_ln_b[l], fox_f_bias[l],
                    ln_mix_g[l], ln_mix_b[l], tiles)
        if l % 2 == 0:
            e = l // 2
            xf = _dense_ffn(xf, ffn_w1[e].astype(MXU_DTYPE), ffn_w3[e].astype(MXU_DTYPE),
                            ffn_w2[e].astype(MXU_DTYPE), row(ln_ffn_g[l]), row(ln_ffn_b[l]),
                            tiles["tm"])
        else:
            e = l // 2
            xf = _moe_ffn(xf, moe_router[e], moe_w1[e].astype(MXU_DTYPE), moe_w3[e].astype(MXU_DTYPE),
                          moe_w2[e].astype(MXU_DTYPE), row(ln_ffn_g[l]), row(ln_ffn_b[l]),
                          tiles["tm"], tiles["tf_moe"])
    return xf.reshape(b, t, d)
```

```python
import functools

import jax
import jax.numpy as jnp
import numpy as np
from jax import lax
from jax.experimental import pallas as pl
from jax.experimental.pallas import tpu as pltpu

D_MODEL = 1024
DEPTH = 2
HEAD_DIM = 64
GDN_WIDTH = 384
CONV_WIDTH = 256
FOX_WIDTH = 384
GDN_HEADS = 6
FOX_HEADS = 6
GDN_SHORT_CONV = 4
GDN_CHUNK = 64
CONV_KERNEL = 31
FFN_DENSE = 2816
N_EXPERTS = 8
TOP_K = 2
FFN_EXPERT = 3584
DEEPNORM_ALPHA = (2 * DEPTH) ** 0.25
LN_EPS = 1e-5
NORM_EPS = 1e-6

LANES = 128
SUBLANES = 8
PAIR = 2 * HEAD_DIM
N_PAIRS = GDN_WIDTH // PAIR
SMALL_COLS = LANES

_C_QKV = 0
_C_GATE = _C_QKV + 3 * GDN_WIDTH
_C_GLU = _C_GATE + GDN_WIDTH
_C_FQ = _C_GLU + 2 * CONV_WIDTH
_C_FK = _C_FQ + FOX_WIDTH
_C_FV = _C_FK + FOX_WIDTH
_C_END = _C_FV + FOX_WIDTH

MXU_DTYPE = jnp.bfloat16
F32 = jnp.float32

VMEM_LIMIT = 56 * 1024 * 1024


def _cparams(*sem):
    return pltpu.CompilerParams(dimension_semantics=sem, vmem_limit_bytes=VMEM_LIMIT)


def _dot(a, b):
    return jnp.dot(a.astype(MXU_DTYPE), b.astype(MXU_DTYPE), preferred_element_type=F32)


def _dot_nt(a, b):
    return lax.dot_general(a.astype(MXU_DTYPE), b.astype(MXU_DTYPE),
                           (((1,), (1,)), ((), ())), preferred_element_type=F32)


def _split(a):
    hi = a.astype(jnp.bfloat16)
    return hi, (a - hi.astype(F32)).astype(jnp.bfloat16)


def _split3(a):
    hi = a.astype(jnp.bfloat16)
    rest = a - hi.astype(F32)
    mid = rest.astype(jnp.bfloat16)
    lo = (rest - mid.astype(F32)).astype(jnp.bfloat16)
    return hi, mid, lo


def _dot_ones(a, ones_b16):
    ah, al = _split(a)
    d = lambda x: jnp.dot(x, ones_b16, preferred_element_type=F32)
    return d(ah) + d(al)


def _sel_rhs(a, sel_b16):
    return sum(jnp.dot(t, sel_b16, preferred_element_type=F32) for t in _split3(a))


def _sel_lhs(sel_b16, b):
    return sum(jnp.dot(sel_b16, t, preferred_element_type=F32) for t in _split3(b))


def _sigmoid(x):
    return 1.0 / (1.0 + jnp.exp(-x))


def _silu(x):
    return x * _sigmoid(x)


def _softplus(x):
    return jnp.maximum(x, 0.0) + jnp.log1p(jnp.exp(-jnp.abs(x)))


def _log_sigmoid(x):
    return -_softplus(-x)


def _layer_norm_rows(y, g, b):
    mu = jnp.mean(y, axis=-1, keepdims=True)
    d = y - mu
    var = jnp.mean(d * d, axis=-1, keepdims=True)
    return d * lax.rsqrt(var + LN_EPS) * g + b


def _head_ones():
    r = lax.broadcasted_iota(jnp.int32, (PAIR, PAIR), 0) // HEAD_DIM
    c = lax.broadcasted_iota(jnp.int32, (PAIR, PAIR), 1) // HEAD_DIM
    return (r == c).astype(F32)


CONV_HALO = 32
CONV_ROW_BLOCK = 128
SHORT_CONV_ROW_BLOCK = 128


def _in_proj_kernel(x_ref, w_ref, ws_ref, gw_ref, cw_ref, cb_ref, lg_ref, lb_ref, ms_ref,
                    qkv_ref, gate_ref, ob_ref, fq_ref, fk_ref, fv_ref, small_ref,
                    qbuf_ref, cbuf_ref, sh_ref, *, tiles_per_seq):
    first = pl.program_id(0) % tiles_per_seq == 0
    tm = x_ref.shape[0]
    x = x_ref[...]
    xb = x.astype(MXU_DTYPE)

    def mm(lo, hi):
        return jnp.dot(xb, w_ref[:, lo:hi], preferred_element_type=F32)

    @pl.when(first)
    def _():
        cbuf_ref[0:CONV_HALO, :] = jnp.zeros((CONV_HALO, CONV_WIDTH), F32)
        qbuf_ref[0:SUBLANES, :] = jnp.zeros((SUBLANES, 3 * GDN_WIDTH), F32)

    @pl.when(jnp.logical_not(first))
    def _():
        cbuf_ref[0:CONV_HALO, :] = cbuf_ref[tm:tm + CONV_HALO, :]
        qbuf_ref[0:SUBLANES, :] = qbuf_ref[tm:tm + SUBLANES, :]

    glu = mm(_C_GLU, _C_FQ)
    cbuf_ref[CONV_HALO:CONV_HALO + tm, :] = glu[:, 0:CONV_WIDTH] * _sigmoid(glu[:, CONV_WIDTH:2 * CONV_WIDTH])
    span = tm + CONV_HALO - SUBLANES
    for s in range(1, SUBLANES):
        sh_ref[s - 1] = cbuf_ref[s:s + span, :]
    for r0 in range(0, tm, CONV_ROW_BLOCK):
        acc = jnp.zeros((CONV_ROW_BLOCK, CONV_WIDTH), F32) + cb_ref[...]
        for j in range(CONV_KERNEL):
            lo = r0 + CONV_HALO - (CONV_KERNEL - 1) + j
            base, phase = lo - lo % SUBLANES, lo % SUBLANES
            tap = (cbuf_ref[base:base + CONV_ROW_BLOCK, :] if phase == 0
                   else sh_ref[phase - 1, base:base + CONV_ROW_BLOCK, :])
            acc = acc + cw_ref[j:j + 1, :] * tap
        y = _silu(_layer_norm_rows(acc, lg_ref[...], lb_ref[...])) * ms_ref[...]
        ob_ref[r0:r0 + CONV_ROW_BLOCK, :] = y.astype(ob_ref.dtype)

    qbuf_ref[SUBLANES:SUBLANES + tm, :] = mm(_C_QKV, _C_GATE)
    for r0 in range(0, tm, SHORT_CONV_ROW_BLOCK):
        acc = jnp.zeros((SHORT_CONV_ROW_BLOCK, 3 * GDN_WIDTH), F32)
        for j in range(GDN_SHORT_CONV):
            lo = r0 + SUBLANES - (GDN_SHORT_CONV - 1) + j
            acc = acc + gw_ref[j:j + 1, :] * qbuf_ref[lo:lo + SHORT_CONV_ROW_BLOCK, :]
        qkv_ref[r0:r0 + SHORT_CONV_ROW_BLOCK, :] = _silu(acc)

    gate_ref[...] = mm(_C_GATE, _C_GLU)
    fq_ref[...] = (mm(_C_FQ, _C_FK) * HEAD_DIM ** -0.5).astype(fq_ref.dtype)
    fk_ref[...] = mm(_C_FK, _C_FV).astype(fk_ref.dtype)
    fv_ref[...] = mm(_C_FV, _C_END).astype(fv_ref.dtype)
    x_lo = (x - xb.astype(F32)).astype(MXU_DTYPE)
    hh_hl = jnp.dot(xb, ws_ref[...], preferred_element_type=F32)
    lh = jnp.dot(x_lo, ws_ref[:, 0:SMALL_COLS], preferred_element_type=F32)
    small_ref[...] = hh_hl[:, 0:SMALL_COLS] + hh_hl[:, SMALL_COLS:2 * SMALL_COLS] + lh


def _in_proj(xf, w_main, w_small, gdn_conv_w, cnv_w, cnv_b, cnv_ln_g, cnv_ln_b, cnv_scale, tm, tiles_per_seq):
    n = xf.shape[0]
    row = lambda i: (i, 0)
    const = lambda shape: pl.BlockSpec(shape, lambda i: (0, 0))
    outs = [
        jax.ShapeDtypeStruct((n, 3 * GDN_WIDTH), F32),
        jax.ShapeDtypeStruct((n, GDN_WIDTH), F32),
        jax.ShapeDtypeStruct((n, CONV_WIDTH), MXU_DTYPE),
        jax.ShapeDtypeStruct((n, FOX_WIDTH), MXU_DTYPE),
        jax.ShapeDtypeStruct((n, FOX_WIDTH), MXU_DTYPE),
        jax.ShapeDtypeStruct((n, FOX_WIDTH), MXU_DTYPE),
        jax.ShapeDtypeStruct((n, SMALL_COLS), F32),
    ]
    vec = const((1, CONV_WIDTH))
    return pl.pallas_call(
        functools.partial(_in_proj_kernel, tiles_per_seq=tiles_per_seq),
        grid=(n // tm,),
        in_specs=[pl.BlockSpec((tm, D_MODEL), row),
                  const((D_MODEL, _C_END)), const((D_MODEL, 2 * SMALL_COLS)),
                  const((GDN_SHORT_CONV, 3 * GDN_WIDTH)), const((CONV_KERNEL, CONV_WIDTH)), vec, vec, vec, vec],
        out_specs=[pl.BlockSpec((tm, o.shape[1]), row) for o in outs],
        out_shape=outs,
        scratch_shapes=[pltpu.VMEM((tm + SUBLANES, 3 * GDN_WIDTH), F32),
                        pltpu.VMEM((tm + CONV_HALO, CONV_WIDTH), F32),
                        pltpu.VMEM((SUBLANES - 1, tm + CONV_HALO - SUBLANES, CONV_WIDTH), F32)],
        compiler_params=_cparams("arbitrary"),
        name="in_proj",
    )(xf, w_main, w_small, gdn_conv_w, cnv_w, cnv_b, cnv_ln_g, cnv_ln_b, cnv_scale)


AUG_BIAS_LANE = HEAD_DIM
KEY_NORM_SLACK = 1.01


def _gates_kernel(small_ref, a_ref, dtb_ref, fb_ref, fq_ref, fk_ref, fv_ref,
                  g_ref, beta_ref, gt_ref, st_ref, qa_ref, ka_ref, va_ref, carry_ref):
    t = pl.program_id(1)
    tt = small_ref.shape[1]
    b16 = jnp.bfloat16

    @pl.when(t == 0)
    def _():
        carry_ref[...] = jnp.zeros_like(carry_ref)

    s = small_ref[0]
    log_decay = -jnp.exp(a_ref[...]) * _softplus(s + dtb_ref[...])
    beta = _sigmoid(s)
    log_f = _log_sigmoid(s + fb_ref[...])

    lane = lax.broadcasted_iota(jnp.int32, (1, LANES), 1)
    rr = lax.broadcasted_iota(jnp.int32, (LANES, LANES), 0)
    cc = lax.broadcasted_iota(jnp.int32, (LANES, LANES), 1)
    tri_chunk = ((cc <= rr) & (rr // GDN_CHUNK == cc // GDN_CHUNK)).astype(b16)
    both = jnp.where(lane < 2 * GDN_HEADS, log_decay, log_f)
    in_chunk = jnp.concatenate([_sel_lhs(tri_chunk, both[r0:r0 + LANES]) for r0 in range(0, tt, LANES)], axis=0)
    g_cum = in_chunk
    carry = carry_ref[...]
    pieces = []
    for r0 in range(0, tt, GDN_CHUNK):
        chunk = in_chunk[r0:r0 + GDN_CHUNK]
        pieces.append(chunk + carry)
        carry = carry + chunk[GDN_CHUNK - 1:GDN_CHUNK]
    c_cum = jnp.concatenate(pieces, axis=0)
    carry_ref[...] = carry

    er = lax.broadcasted_iota(jnp.int32, (LANES, 2 * GDN_WIDTH), 0)
    ec = lax.broadcasted_iota(jnp.int32, (LANES, 2 * GDN_WIDTH), 1) // HEAD_DIM
    expanded = _sel_rhs(jnp.where(lane < GDN_HEADS, g_cum, beta), (er == ec).astype(b16))
    g_ref[0] = expanded[:, 0:GDN_WIDTH]
    beta_ref[0] = expanded[:, GDN_WIDTH:2 * GDN_WIDTH]
    gt_ref[0] = g_cum.T[0:SUBLANES, :]

    head_ones = _head_ones().astype(b16)
    stat_rows = []
    for p in range(N_PAIRS):
        k_pair = fk_ref[0, :, p * PAIR:(p + 1) * PAIR].astype(F32)
        k_sq = jnp.dot((k_pair * k_pair).astype(b16), head_ones, preferred_element_type=F32)
        stat_rows.append(jnp.max(k_sq, axis=0, keepdims=True))
    stat_rows.append(c_cum[tt - 1:tt, :])
    stat_rows.append(jnp.zeros((SUBLANES - len(stat_rows), LANES), F32))
    st_ref[0, 0] = jnp.concatenate(stat_rows, axis=0)

    q_ones = ((lane >= AUG_BIAS_LANE) & (lane < AUG_BIAS_LANE + 3)).astype(F32)
    v_one = (lane == AUG_BIAS_LANE).astype(F32)
    br = lax.broadcasted_iota(jnp.int32, (3 * LANES, FOX_HEADS * LANES), 0)
    bc = lax.broadcasted_iota(jnp.int32, (3 * LANES, FOX_HEADS * LANES), 1)
    place = ((br % LANES == 2 * GDN_HEADS + bc // LANES) & (bc % LANES == AUG_BIAS_LANE + br // LANES)).astype(b16)
    bias_all = jnp.dot(jnp.concatenate(_split3(-c_cum), axis=1), place, preferred_element_type=F32)
    upper_to_lower = ((cc < HEAD_DIM) & (rr == cc + HEAD_DIM)).astype(b16)
    for h in range(FOX_HEADS):
        p, half = divmod(h, 2)
        lanes = slice(p * PAIR, (p + 1) * PAIR)

        def pick(ref):
            if half == 0:
                return jnp.where(lane < HEAD_DIM, ref[0, :, lanes].astype(F32), 0.0)
            return jnp.dot(ref[0, :, lanes], upper_to_lower, preferred_element_type=F32)

        qa_ref[0, h] = (pick(fq_ref) + q_ones).astype(b16)
        ka_ref[0, h] = (pick(fk_ref) + bias_all[:, h * LANES:(h + 1) * LANES]).astype(b16)
        va_ref[0, h] = (pick(fv_ref) + v_one).astype(b16)


def _gates(small, a_row, dtb_row, fb_row, fq, fk, fv, tt):
    b, t, _ = small.shape
    row = pl.BlockSpec((1, LANES), lambda i, j: (0, 0))
    tile = lambda w: pl.BlockSpec((1, tt, w), lambda i, j: (i, j, 0))
    aug = pl.BlockSpec((1, FOX_HEADS, tt, LANES), lambda i, j: (i, 0, j, 0))
    aug_shape = jax.ShapeDtypeStruct((b, FOX_HEADS, t, LANES), jnp.bfloat16)
    return pl.pallas_call(
        _gates_kernel,
        grid=(b, t // tt),
        in_specs=[tile(SMALL_COLS), row, row, row, tile(FOX_WIDTH), tile(FOX_WIDTH), tile(FOX_WIDTH)],
        out_specs=[tile(GDN_WIDTH), tile(GDN_WIDTH), pl.BlockSpec((1, SUBLANES, tt), lambda i, j: (i, 0, j)),
                   pl.BlockSpec((1, 1, SUBLANES, LANES), lambda i, j: (i, j, 0, 0)), aug, aug, aug],
        out_shape=[jax.ShapeDtypeStruct((b, t, GDN_WIDTH), F32), jax.ShapeDtypeStruct((b, t, GDN_WIDTH), F32),
                   jax.ShapeDtypeStruct((b, SUBLANES, t), F32),
                   jax.ShapeDtypeStruct((b, t // tt, SUBLANES, LANES), F32), aug_shape, aug_shape, aug_shape],
        scratch_shapes=[pltpu.VMEM((1, LANES), F32)],
        compiler_params=_cparams("parallel", "arbitrary"),
        name="gates",
    )(small, a_row, dtb_row, fb_row, fq, fk, fv)


def _unit_lower_inverse_many(lows):
    n = lows[0].shape[0]
    r = lax.broadcasted_iota(jnp.int32, (n, n), 0)
    c = lax.broadcasted_iota(jnp.int32, (n, n), 1)
    eye = (r == c).astype(F32)
    base = GDN_CHUNK // 4
    diag = r // base == c // base
    cast = lambda xs: [x.astype(MXU_DTYPE) for x in xs]
    mm = lambda xs, ys: [jnp.dot(x, y, preferred_element_type=F32) for x, y in zip(xs, ys)]
    add = lambda xs, ys: [x + y for x, y in zip(xs, ys)]
    d = [jnp.where(diag, low, 0.0) for low in lows]
    db = cast(d)
    d2b = cast(mm(db, db))
    p = [eye - x for x in d]
    p = add(p, mm(cast(p), d2b))
    d4b = cast(mm(d2b, d2b))
    p = add(p, mm(cast(p), d4b))
    d8b = cast(mm(d4b, d4b))
    x = add(p, mm(cast(p), d8b))
    for blk in (2 * base, 4 * base):
        sel = (r // blk == c // blk) & (r // (blk // 2) != c // (blk // 2))
        xb = cast(x)
        xo = mm(xb, cast([jnp.where(sel, low, 0.0) for low in lows]))
        x = [a - b for a, b in zip(x, mm(cast(xo), xb))]
    return x


def _gdn_kernel(qkv_ref, g_ref, beta_ref, gt_ref, gate_ref, nw_ref, ms_ref, o_ref, s_ref):
    t = pl.program_id(1)
    blk = qkv_ref.shape[1]
    nc = blk // GDN_CHUNK

    @pl.when(t == 0)
    def _():
        s_ref[...] = jnp.zeros_like(s_ref)

    qkv = qkv_ref[0]

    ones = _head_ones().astype(jnp.bfloat16)
    lane = lax.broadcasted_iota(jnp.int32, (1, PAIR), 1)
    head0 = lane < HEAD_DIM
    n2 = 2 * GDN_CHUNK
    r = lax.broadcasted_iota(jnp.int32, (n2, n2), 0)
    c = lax.broadcasted_iota(jnp.int32, (n2, n2), 1)
    same = (r // GDN_CHUNK) == (c // GDN_CHUNK)
    causal = same & (c <= r)
    strict = same & (c < r)

    def l2n(v):
        return v * lax.rsqrt(_dot_ones(v * v, ones) + NORM_EPS)

    def stack(v):
        return jnp.concatenate([jnp.where(head0, v, 0.0), jnp.where(head0, 0.0, v)], axis=0)

    pair = lambda p, grp: slice(grp * GDN_WIDTH + p * PAIR, grp * GDN_WIDTH + (p + 1) * PAIR)
    q_p = [l2n(qkv[:, pair(p, 0)]) * HEAD_DIM ** -0.5 for p in range(N_PAIRS)]
    k_p = [l2n(qkv[:, pair(p, 1)]) for p in range(N_PAIRS)]
    v_p = [qkv[:, pair(p, 2)] for p in range(N_PAIRS)]

    units = [(ci, p) for ci in range(nc) for p in range(N_PAIRS)]
    rows = lambda ci: slice(ci * GDN_CHUNK, (ci + 1) * GDN_CHUNK)
    gc = [g_ref[0, rows(ci), pair(p, 0)] for ci, p in units]
    bt = [beta_ref[0, rows(ci), pair(p, 0)] for ci, p in units]
    qn = [q_p[p][rows(ci)] for ci, p in units]
    kn = [k_p[p][rows(ci)] for ci, p in units]
    vv = [v_p[p][rows(ci)] for ci, p in units]
    g_last = [g[GDN_CHUNK - 1:GDN_CHUNK, :] for g in gc]
    eg = [jnp.exp(g) for g in gc]
    kb = [k * b for k, b in zip(kn, bt)]
    k2 = [stack(k) for k in kn]

    def decay_of(g, ci, p):
        g_col = jnp.concatenate([jnp.broadcast_to(g[:, 0:1], (GDN_CHUNK, PAIR)),
                                 jnp.broadcast_to(g[:, HEAD_DIM:HEAD_DIM + 1], (GDN_CHUNK, PAIR))], axis=0)
        g_row = jnp.concatenate([gt_ref[0, 2 * p:2 * p + 1, rows(ci)], gt_ref[0, 2 * p + 1:2 * p + 2, rows(ci)]],
                                axis=1)
        return jnp.where(causal, jnp.exp(jnp.where(causal, g_col - g_row, 0.0)), 0.0)

    decay = [decay_of(g, ci, p) for g, (ci, p) in zip(gc, units)]
    k2b = [x.astype(MXU_DTYPE) for x in k2]
    low = [jnp.where(strict, _dot_nt(stack(a), b) * d, 0.0) for a, b, d in zip(kb, k2b, decay)]
    a_in = [(_dot_nt(stack(a), b) * d).astype(MXU_DTYPE) for a, b, d in zip(qn, k2b, decay)]
    t_inv = _unit_lower_inverse_many(low)
    uw = [_dot(ti, jnp.concatenate([stack(v * b), stack(a * e)], axis=1)).astype(MXU_DTYPE)
          for ti, v, b, a, e in zip(t_inv, vv, bt, kb, eg)]
    ket = [stack(k * jnp.exp(gl - g)).T for k, gl, g in zip(kn, g_last, gc)]
    nm = [_dot(a, b) for a, b in zip(ket, uw)]
    raw = [jnp.dot(a, b, preferred_element_type=F32) for a, b in zip(a_in, uw)]
    p_mat = [stack(q * e) - x[:, PAIR:] for q, e, x in zip(qn, eg, raw)]

    state = [s_ref[p] for p in range(N_PAIRS)]
    for i, (ci, p) in enumerate(units):
        s = state[p]
        o2 = _dot(p_mat[i], s) + raw[i][:, :PAIR]
        state[p] = s * jnp.exp(g_last[i]) - _dot(nm[i][:, PAIR:], s) + nm[i][:, :PAIR]
        o = o2[0:GDN_CHUNK] + o2[GDN_CHUNK:n2]
        ms = _dot_ones(o * o, ones) * (1.0 / HEAD_DIM)
        on = o * lax.rsqrt(ms + NORM_EPS) * nw_ref[...]
        o_ref[0, rows(ci), pair(p, 0)] = (on * _silu(gate_ref[0, rows(ci), pair(p, 0)])
                                          * ms_ref[:, pair(p, 0)]).astype(o_ref.dtype)
    for p in range(N_PAIRS):
        s_ref[p] = state[p]


def _gdn(qkv, g, beta, g_t, gate, norm_w2, mscale, blk):
    b, t, _ = qkv.shape
    tile = lambda width: pl.BlockSpec((1, blk, width), lambda i, j: (i, j, 0))
    const = lambda shape: pl.BlockSpec(shape, lambda i, j: (0, 0))
    return pl.pallas_call(
        _gdn_kernel,
        grid=(b, t // blk),
        in_specs=[tile(3 * GDN_WIDTH),
                  tile(GDN_WIDTH), tile(GDN_WIDTH), pl.BlockSpec((1, SUBLANES, blk), lambda i, j: (i, 0, j)),
                  tile(GDN_WIDTH), const((1, PAIR)), const((1, GDN_WIDTH))],
        out_specs=tile(GDN_WIDTH),
        out_shape=jax.ShapeDtypeStruct((b, t, GDN_WIDTH), MXU_DTYPE),
        scratch_shapes=[pltpu.VMEM((N_PAIRS, PAIR, PAIR), F32)],
        compiler_params=_cparams("parallel", "arbitrary"),
        name="gdn",
    )(qkv, g, beta, g_t, gate, norm_w2, mscale)


FOX_ROW_GROUP = 32


FOX_SKIP_MARGIN = 106.0


def _fox_kernel(q_ref, k_ref, v_ref, aux_ref, ms_ref, o_ref, s0_ref, s1_ref, p0_ref, p1_ref, a0_ref, a1_ref,
                m_ref, acc_ref, *, tq):
    i = pl.program_id(2)
    heads = range(2)
    rg = FOX_ROW_GROUP
    m_ref[...] = jnp.full(m_ref.shape, -jnp.inf, F32)
    acc_ref[...] = jnp.zeros(acc_ref.shape, F32)

    def scores(j, s_ref):
        start = pl.multiple_of(j * tq, tq)
        for h in heads:
            s_ref[h] = lax.dot_general(q_ref[0, h], k_ref[0, h, pl.ds(start, tq), :],
                                       (((1,), (1,)), ((), ())), preferred_element_type=F32)

    def softmax(s_ref, p_ref, a_ref, masked):
        for h in heads:
            for g in range(tq // rg):
                rows = slice(g * rg, (g + 1) * rg)
                width = LANES * (((g + 1) * rg - 1) // LANES + 1) if masked else tq
                s = s_ref[h, rows, 0:width]
                if masked:
                    row_id = g * rg + lax.broadcasted_iota(jnp.int32, (rg, width), 0)
                    col_id = lax.broadcasted_iota(jnp.int32, (rg, width), 1)
                    s = jnp.where(col_id <= row_id, s, -jnp.inf)
                    if width < tq:
                        p_ref[h, rows, width:tq] = jnp.zeros((rg, tq - width), p_ref.dtype)
                m_old = m_ref[h, rows, :]
                m_new = jnp.maximum(m_old, jnp.max(s, axis=-1, keepdims=True))
                a_ref[h, rows, :] = jnp.exp(m_old - m_new)
                m_ref[h, rows, :] = m_new
                m_wide = jnp.concatenate([m_new] * (width // LANES), axis=1)
                p_ref[h, rows, 0:width] = jnp.exp(s - m_wide).astype(p_ref.dtype)

    def weighted_values(j, p_ref, a_ref):
        start = pl.multiple_of(j * tq, tq)
        for h in heads:
            acc_ref[h] = a_ref[h] * acc_ref[h] + jnp.dot(p_ref[h], v_ref[0, h, pl.ds(start, tq), :],
                                                         preferred_element_type=F32)

    scores(i, s0_ref)
    softmax(s0_ref, p0_ref, a0_ref, True)
    scores(jnp.maximum(i - 1, 0), s1_ref)
    weighted_values(i, p0_ref, a0_ref)

    lane = lax.broadcasted_iota(jnp.int32, (1, LANES), 1)
    n = jnp.int32(0)
    for h in heads:
        q = q_ref[0, h].astype(F32)
        q_norm = jnp.sqrt(jnp.sum(jnp.where(lane < HEAD_DIM, q * q, 0.0), axis=-1, keepdims=True))
        key_norm = aux_ref[0, h, 1:2, :]
        block_end_c = aux_ref[0, h, 0:1, :]
        slack = jnp.max(q_norm * key_norm - m_ref[h], axis=0, keepdims=True)
        keep = (lane < i) & (slack - block_end_c >= -FOX_SKIP_MARGIN)
        n = jnp.maximum(n, jnp.sum(keep.astype(jnp.int32)))

    def body(t, carry):
        j = i - 1 - 2 * t
        softmax(s1_ref, p1_ref, a1_ref, False)
        scores(jnp.maximum(j - 1, 0), s0_ref)
        weighted_values(j, p1_ref, a1_ref)
        softmax(s0_ref, p0_ref, a0_ref, False)
        scores(jnp.maximum(j - 2, 0), s1_ref)
        weighted_values(j - 1, p0_ref, a0_ref)
        return carry

    lax.fori_loop(0, n // 2, body, 0)

    @pl.when(n % 2 == 1)
    def _():
        softmax(s1_ref, p1_ref, a1_ref, False)
        weighted_values(i - n, p1_ref, a1_ref)

    o = [acc_ref[h] / acc_ref[h][:, AUG_BIAS_LANE:AUG_BIAS_LANE + 1] for h in heads]
    lane = lax.broadcasted_iota(jnp.int32, (1, PAIR), 1)
    o_pair = jnp.where(lane < HEAD_DIM, o[0], pltpu.roll(o[1], HEAD_DIM, axis=1))
    o_ref[0] = (o_pair * ms_ref[...]).astype(o_ref.dtype)


def _fox_aux(stats):
    b, nt = stats.shape[:2]
    k_sq = jnp.max(stats[:, :, :N_PAIRS, :], axis=1)
    k_sq = k_sq.reshape(b, N_PAIRS, 2, HEAD_DIM)[..., 0].reshape(b, FOX_HEADS)
    key_norm = jnp.sqrt(k_sq) * KEY_NORM_SLACK
    block_end_c = jnp.transpose(stats[:, :, N_PAIRS, 2 * GDN_HEADS:2 * GDN_HEADS + FOX_HEADS], (0, 2, 1))
    aux = jnp.zeros((b, FOX_HEADS, SUBLANES, LANES), F32)
    aux = aux.at[:, :, 0, :nt].set(block_end_c)
    return aux.at[:, :, 1, :].set(jnp.broadcast_to(key_norm[:, :, None], (b, FOX_HEADS, LANES)))


def _fox(qa, ka, va, aux, mscale, tq):
    b, _, t, _ = qa.shape
    kern = functools.partial(_fox_kernel, tq=tq)
    return pl.pallas_call(
        kern,
        grid=(b, N_PAIRS, t // tq),
        in_specs=[pl.BlockSpec((1, 2, tq, LANES), lambda bi, p, i: (bi, p, i, 0)),
                  pl.BlockSpec((1, 2, t, LANES), lambda bi, p, i: (bi, p, 0, 0)),
                  pl.BlockSpec((1, 2, t, LANES), lambda bi, p, i: (bi, p, 0, 0)),
                  pl.BlockSpec((1, 2, SUBLANES, LANES), lambda bi, p, i: (bi, p, 0, 0)),
                  pl.BlockSpec((1, PAIR), lambda bi, p, i: (0, p))],
        out_specs=pl.BlockSpec((1, tq, PAIR), lambda bi, p, i: (bi, i, p)),
        out_shape=jax.ShapeDtypeStruct((b, t, FOX_WIDTH), MXU_DTYPE),
        scratch_shapes=[pltpu.VMEM((2, tq, tq), F32), pltpu.VMEM((2, tq, tq), F32),
                        pltpu.VMEM((2, tq, tq), jnp.bfloat16), pltpu.VMEM((2, tq, tq), jnp.bfloat16),
                        pltpu.VMEM((2, tq, LANES), F32), pltpu.VMEM((2, tq, LANES), F32),
                        pltpu.VMEM((2, tq, LANES), F32), pltpu.VMEM((2, tq, LANES), F32)],
        compiler_params=_cparams("parallel", "parallel", "arbitrary"),
        name="fox_attention",
    )(qa, ka, va, aux, mscale)


def _out_proj_kernel(oa_ref, ob_ref, oc_ref, x_ref, w_ref, g_ref, b_ref, o_ref):
    mixed = jnp.concatenate([oa_ref[...], ob_ref[...], oc_ref[...]], axis=1)
    mix = jnp.dot(mixed, w_ref[...], preferred_element_type=F32)
    o_ref[...] = _layer_norm_rows(DEEPNORM_ALPHA * x_ref[...] + mix, g_ref[...], b_ref[...])


def _out_proj(oa, ob, oc, xf, w, g, bvec, tm):
    n = xf.shape[0]
    row = lambda i: (i, 0)
    const = lambda i: (0, 0)
    return pl.pallas_call(
        _out_proj_kernel,
        grid=(n // tm,),
        in_specs=[pl.BlockSpec((tm, GDN_WIDTH), row), pl.BlockSpec((tm, CONV_WIDTH), row),
                  pl.BlockSpec((tm, FOX_WIDTH), row), pl.BlockSpec((tm, D_MODEL), row),
                  pl.BlockSpec((D_MODEL, D_MODEL), const),
                  pl.BlockSpec((1, D_MODEL), const), pl.BlockSpec((1, D_MODEL), const)],
        out_specs=pl.BlockSpec((tm, D_MODEL), row),
        out_shape=jax.ShapeDtypeStruct((n, D_MODEL), F32),
        compiler_params=_cparams("parallel"),
        name="out_proj_ln",
    )(oa, ob, oc, xf, w, g, bvec)


FFN_CHUNK = 256


def _swiglu_chunks(xb, w1, w3, w2, width):
    assert width % FFN_CHUNK == 0
    acc = None
    for c in range(width // FFN_CHUNK):
        cols = slice(c * FFN_CHUNK, (c + 1) * FFN_CHUNK)
        h = _silu(jnp.dot(xb, w1(slice(None), cols), preferred_element_type=F32)) * jnp.dot(
            xb, w3(slice(None), cols), preferred_element_type=F32)
        part = jnp.dot(h.astype(xb.dtype), w2(cols, slice(None)), preferred_element_type=F32)
        acc = part if acc is None else acc + part
    return acc


def _ffn_kernel(x_ref, w1_ref, w3_ref, w2_ref, g_ref, b_ref, o_ref):
    x = x_ref[...]
    ff = _swiglu_chunks(x.astype(w1_ref.dtype), lambda r, c: w1_ref[r, c], lambda r, c: w3_ref[r, c],
                        lambda r, c: w2_ref[r, c], w1_ref.shape[1])
    o_ref[...] = _layer_norm_rows(DEEPNORM_ALPHA * x + ff, g_ref[...], b_ref[...])


def _dense_ffn(xf, w1, w3, w2, g, bvec, tm):
    n = xf.shape[0]
    ff = w1.shape[1]
    const = lambda shape: pl.BlockSpec(shape, lambda i: (0, 0), pipeline_mode=pl.Buffered(1))
    return pl.pallas_call(
        _ffn_kernel,
        grid=(n // tm,),
        in_specs=[pl.BlockSpec((tm, D_MODEL), lambda i: (i, 0)),
                  const((D_MODEL, ff)), const((D_MODEL, ff)), const((ff, D_MODEL)),
                  const((1, D_MODEL)), const((1, D_MODEL))],
        out_specs=pl.BlockSpec((tm, D_MODEL), lambda i: (i, 0)),
        out_shape=jax.ShapeDtypeStruct((n, D_MODEL), F32),
        compiler_params=_cparams("parallel"),
        name="dense_ffn_ln",
    )(xf, w1, w3, w2, g, bvec)


def _router_kernel(x_ref, wr_ref, info_ref, info_t_ref, cnt_ref, run_ref):
    i = pl.program_id(0)
    tm = x_ref.shape[0]

    @pl.when(i == 0)
    def _():
        run_ref[...] = jnp.zeros_like(run_ref)

    logits = jnp.dot(x_ref[...].astype(MXU_DTYPE), wr_ref[...], preferred_element_type=F32)
    lane = lax.broadcasted_iota(jnp.int32, (tm, LANES), 1)
    logits = jnp.where(lane < N_EXPERTS, logits, -jnp.inf)
    m1 = jnp.max(logits, axis=-1, keepdims=True)
    e1 = jnp.min(jnp.where(logits == m1, lane, LANES), axis=-1, keepdims=True)
    rest = jnp.where(lane == e1, -jnp.inf, logits)
    m2 = jnp.max(rest, axis=-1, keepdims=True)
    e2 = jnp.min(jnp.where(rest == m2, lane, LANES), axis=-1, keepdims=True)
    z = jnp.exp(m2 - m1)
    g1 = 1.0 / (1.0 + z)
    g2 = z / (1.0 + z)
    onehot = ((lane == e1) | (lane == e2)).astype(F32)
    r = lax.broadcasted_iota(jnp.int32, (tm, tm), 0)
    c = lax.broadcasted_iota(jnp.int32, (tm, tm), 1)
    before = jnp.dot((c < r).astype(jnp.bfloat16), onehot.astype(jnp.bfloat16),
                     preferred_element_type=F32) + run_ref[...]
    rank1 = jnp.sum(jnp.where(lane == e1, before, 0.0), axis=-1, keepdims=True)
    rank2 = jnp.sum(jnp.where(lane == e2, before, 0.0), axis=-1, keepdims=True)
    run_ref[...] = run_ref[...] + jnp.sum(onehot, axis=0, keepdims=True)
    cnt_ref[...] = run_ref[...]
    info = jnp.where(lane == 0, e1.astype(F32),
                     jnp.where(lane == 1, e2.astype(F32),
                               jnp.where(lane == 2, rank1,
                                         jnp.where(lane == 3, rank2,
                                                   jnp.where(lane == 4, g1, jnp.where(lane == 5, g2, 0.0))))))
    info_ref[...] = info
    info_t_ref[...] = info.T[0:SUBLANES, :]


def _router(xf, wr_pad, tm):
    n = xf.shape[0]
    return pl.pallas_call(
        _router_kernel,
        grid=(n // tm,),
        in_specs=[pl.BlockSpec((tm, D_MODEL), lambda i: (i, 0)),
                  pl.BlockSpec((D_MODEL, LANES), lambda i: (0, 0))],
        out_specs=[pl.BlockSpec((tm, LANES), lambda i: (i, 0)),
                   pl.BlockSpec((SUBLANES, tm), lambda i: (0, i)),
                   pl.BlockSpec((1, LANES), lambda i: (0, 0))],
        out_shape=[jax.ShapeDtypeStruct((n, LANES), F32), jax.ShapeDtypeStruct((SUBLANES, n), F32),
                   jax.ShapeDtypeStruct((1, LANES), F32)],
        scratch_shapes=[pltpu.VMEM((1, LANES), F32)],
        compiler_params=_cparams("arbitrary"),
        name="moe_router",
    )(xf, wr_pad)


DMA_LOOP_UNROLL = 8


def _dispatch_kernel(fill_ref, d1_ref, d2_ref, x_ref, xs_ref, zero_ref, sem):
    tm = x_ref.shape[0]

    @pl.when(pl.program_id(0) == 0)
    def _():
        zero_ref[...] = jnp.zeros_like(zero_ref)

        def strip_copy(e, k):
            start = pl.multiple_of(fill_ref[e] - SUBLANES * (k + 1), SUBLANES)
            return pltpu.make_async_copy(zero_ref, xs_ref.at[pl.ds(start, SUBLANES), :], sem)

        for e in range(N_EXPERTS):
            lax.fori_loop(0, fill_ref[N_EXPERTS + e], lambda k, c, e=e: (strip_copy(e, k).start(), c)[1], 0)
        for e in range(N_EXPERTS):
            lax.fori_loop(0, fill_ref[N_EXPERTS + e], lambda k, c, e=e: (strip_copy(e, k).wait(), c)[1], 0)

    def row_copy(r, dst):
        return pltpu.make_async_copy(x_ref.at[pl.ds(r, 1), :], xs_ref.at[pl.ds(dst, 1), :], sem)

    def issue(r, carry):
        row_copy(r, d1_ref[0, 0, r]).start()
        row_copy(r, d2_ref[0, 0, r]).start()
        return carry

    lax.fori_loop(0, tm, issue, 0, unroll=DMA_LOOP_UNROLL)

    def drain(r, carry):
        row_copy(r, d1_ref[0, 0, r]).wait()
        row_copy(r, d2_ref[0, 0, r]).wait()
        return carry

    lax.fori_loop(0, tm, drain, 0, unroll=DMA_LOOP_UNROLL)


def _dispatch(xf, d1, d2, gap_start, gap_end, rows, tm):
    n = xf.shape[0]
    idx = lambda a: a.reshape(n // tm, 1, tm)
    n_strips = (gap_end - gap_start + SUBLANES - 1) // SUBLANES
    fill = jnp.concatenate([gap_end, n_strips]).astype(jnp.int32)
    smem = lambda: pl.BlockSpec((1, 1, tm), lambda i, fill: (i, 0, 0), memory_space=pltpu.SMEM)
    grid_spec = pltpu.PrefetchScalarGridSpec(
        num_scalar_prefetch=1,
        grid=(n // tm,),
        in_specs=[smem(), smem(), pl.BlockSpec((tm, D_MODEL), lambda i, fill: (i, 0))],
        out_specs=pl.BlockSpec(memory_space=pl.ANY),
        scratch_shapes=[pltpu.VMEM((SUBLANES, D_MODEL), xf.dtype), pltpu.SemaphoreType.DMA(())],
    )
    return pl.pallas_call(
        _dispatch_kernel,
        grid_spec=grid_spec,
        out_shape=jax.ShapeDtypeStruct((rows, D_MODEL), xf.dtype),
        compiler_params=_cparams("arbitrary"),
        name="moe_dispatch",
    )(fill, idx(d1), idx(d2), xf)


def _expert_kernel(te_ref, nu_ref, xs_ref, w1_ref, w3_ref, w2_ref, ys_ref, xb_ref):
    del te_ref
    i = pl.program_id(0)
    f = pl.program_id(1)
    used = i < nu_ref[0]

    def partial_ffn(xb):
        return _swiglu_chunks(xb, lambda r, c: w1_ref[0, r, c], lambda r, c: w3_ref[0, r, c],
                              lambda r, c: w2_ref[0, r, c], w1_ref.shape[2])

    @pl.when(used & (f == 0))
    def _():
        xb = xs_ref[...].astype(xb_ref.dtype)
        xb_ref[...] = xb
        ys_ref[...] = partial_ffn(xb)

    @pl.when(used & (f > 0))
    def _():
        ys_ref[...] += partial_ffn(xb_ref[...])

    @pl.when(jnp.logical_not(used) & (f == 0))
    def _():
        ys_ref[...] = jnp.zeros_like(ys_ref)


def _experts(xs, tile_expert, n_used, w1, w3, w2, tm, tf):
    rows = xs.shape[0]
    ff = w1.shape[2]
    n_tiles = rows // tm

    def x_map(i, f, te, nu):
        return (jnp.maximum(jnp.minimum(i, nu[0] - 1), 0), 0)

    def w13_map(i, f, te, nu):
        return (te[i], 0, jnp.where(i < nu[0], f, ff // tf - 1))

    def w2_map(i, f, te, nu):
        return (te[i], jnp.where(i < nu[0], f, ff // tf - 1), 0)

    grid_spec = pltpu.PrefetchScalarGridSpec(
        num_scalar_prefetch=2,
        grid=(n_tiles, ff // tf),
        in_specs=[pl.BlockSpec((tm, D_MODEL), x_map),
                  pl.BlockSpec((1, D_MODEL, tf), w13_map),
                  pl.BlockSpec((1, D_MODEL, tf), w13_map),
                  pl.BlockSpec((1, tf, D_MODEL), w2_map)],
        out_specs=pl.BlockSpec((tm, D_MODEL), lambda i, f, te, nu: (i, 0)),
        scratch_shapes=[pltpu.VMEM((tm, D_MODEL), MXU_DTYPE)],
    )
    return pl.pallas_call(
        _expert_kernel,
        grid_spec=grid_spec,
        out_shape=jax.ShapeDtypeStruct((rows, D_MODEL), F32),
        compiler_params=_cparams("arbitrary", "arbitrary"),
        name="moe_experts",
    )(tile_expert, n_used, xs, w1, w3, w2)


def _combine_kernel(d1_ref, d2_ref, x_ref, info_ref, ys_ref, g_ref, b_ref, o_ref, y1_ref, y2_ref, sem):
    tm = x_ref.shape[0]

    def row_copy(src, r, buf):
        return pltpu.make_async_copy(ys_ref.at[pl.ds(src, 1), :], buf.at[pl.ds(r, 1), :], sem)

    def issue(r, carry):
        row_copy(d1_ref[0, 0, r], r, y1_ref).start()
        row_copy(d2_ref[0, 0, r], r, y2_ref).start()
        return carry

    lax.fori_loop(0, tm, issue, 0, unroll=DMA_LOOP_UNROLL)

    def drain(r, carry):
        row_copy(d1_ref[0, 0, r], r, y1_ref).wait()
        row_copy(d2_ref[0, 0, r], r, y2_ref).wait()
        return carry

    lax.fori_loop(0, tm, drain, 0, unroll=DMA_LOOP_UNROLL)
    info = info_ref[...]
    ff = info[:, 4:5] * y1_ref[...] + info[:, 5:6] * y2_ref[...]
    o_ref[...] = _layer_norm_rows(DEEPNORM_ALPHA * x_ref[...] + ff, g_ref[...], b_ref[...])


def _combine(xf, info, ys, d1, d2, g, bvec, tm):
    n = xf.shape[0]
    idx = lambda a: a.reshape(n // tm, 1, tm)
    smem = lambda: pl.BlockSpec((1, 1, tm), lambda i: (i, 0, 0), memory_space=pltpu.SMEM)
    return pl.pallas_call(
        _combine_kernel,
        grid=(n // tm,),
        in_specs=[smem(), smem(),
                  pl.BlockSpec((tm, D_MODEL), lambda i: (i, 0)),
                  pl.BlockSpec((tm, LANES), lambda i: (i, 0)),
                  pl.BlockSpec(memory_space=pl.ANY),
                  pl.BlockSpec((1, D_MODEL), lambda i: (0, 0)),
                  pl.BlockSpec((1, D_MODEL), lambda i: (0, 0))],
        out_specs=pl.BlockSpec((tm, D_MODEL), lambda i: (i, 0)),
        out_shape=jax.ShapeDtypeStruct((n, D_MODEL), F32),
        scratch_shapes=[pltpu.VMEM((tm, D_MODEL), F32), pltpu.VMEM((tm, D_MODEL), F32),
                        pltpu.SemaphoreType.DMA(())],
        compiler_params=_cparams("arbitrary"),
        name="moe_combine_ln",
    )(idx(d1), idx(d2), xf, info, ys, g, bvec)


MOE_TILE = 512


def _moe_ffn(xf, w_router, w1, w3, w2, g, bvec, tm_tok, tf):
    n = xf.shape[0]
    wr_pad = jnp.zeros((D_MODEL, LANES), MXU_DTYPE).at[:, :N_EXPERTS].set(w_router.astype(MXU_DTYPE))
    info, info_t, counts = _router(xf, wr_pad, tm_tok)
    sizes = counts[0, :N_EXPERTS].astype(jnp.int32)
    tiles_per = (sizes + MOE_TILE - 1) // MOE_TILE
    tile_end = jnp.cumsum(tiles_per)
    seg_start = (tile_end - tiles_per) * MOE_TILE
    experts = jnp.arange(N_EXPERTS, dtype=jnp.int32)[:, None]

    def sorted_row(expert_row, rank_row):
        start = jnp.sum(jnp.where(expert_row.astype(jnp.int32)[None, :] == experts, seg_start[:, None], 0), axis=0)
        return start + rank_row.astype(jnp.int32)

    d1 = sorted_row(info_t[0], info_t[2])
    d2 = sorted_row(info_t[1], info_t[3])
    n_tiles = (n * TOP_K) // MOE_TILE + N_EXPERTS
    rows = n_tiles * MOE_TILE
    tile_ids = jnp.arange(n_tiles, dtype=jnp.int32)
    tile_expert = jnp.minimum(jnp.sum((tile_end[None, :] <= tile_ids[:, None]).astype(jnp.int32), axis=1),
                              N_EXPERTS - 1)
    n_used = tile_end[N_EXPERTS - 1:].astype(jnp.int32)
    gap_end = jnp.concatenate([seg_start[1:], jnp.full((1,), rows, jnp.int32)])
    xs = _dispatch(xf, d1, d2, seg_start + sizes, gap_end, rows, tm_tok)
    ys = _experts(xs, tile_expert, n_used, w1, w3, w2, MOE_TILE, tf)
    return _combine(xf, info, ys, d1, d2, g, bvec, tm_tok)


def _pack_in_proj(w_in):
    cuts = np.cumsum([0, GDN_WIDTH, GDN_WIDTH, GDN_WIDTH, GDN_HEADS, GDN_HEADS, GDN_WIDTH,
                      2 * CONV_WIDTH, FOX_WIDTH, FOX_WIDTH, FOX_WIDTH, FOX_HEADS])
    seg = lambda i: w_in[:, cuts[i]:cuts[i + 1]]
    w_main = jnp.concatenate([seg(0), seg(1), seg(2), seg(5), seg(6), seg(7), seg(8), seg(9)], axis=1)
    pad = jnp.zeros((D_MODEL, SMALL_COLS - 2 * GDN_HEADS - FOX_HEADS), w_in.dtype)
    w_small = jnp.concatenate([seg(3), seg(4), seg(10), pad], axis=1).astype(F32)
    ws_hi = w_small.astype(MXU_DTYPE)
    ws_lo = (w_small - ws_hi.astype(F32)).astype(MXU_DTYPE)
    return w_main.astype(MXU_DTYPE), jnp.concatenate([ws_hi, ws_lo], axis=1)


def _lane_row(vals, offset):
    return jnp.zeros((1, LANES), F32).at[0, offset:offset + vals.shape[0]].set(vals.astype(F32))


def _mixer(xf, b, t, w_in, mix_scale, w_out, gdn_conv_w, gdn_a_log, gdn_dt_bias, gdn_norm_w,
           cnv_dw_w, cnv_dw_b, cnv_ln_g, cnv_ln_b, fox_f_bias, ln_g, ln_b, tiles):
    w_main, w_small = _pack_in_proj(w_in)
    ms = mix_scale.reshape(1, D_MODEL).astype(F32)
    row = lambda v: v.reshape(1, -1).astype(F32)
    qkv, gate, o_b, fq, fk, fv, small = _in_proj(
        xf, w_main, w_small, gdn_conv_w, cnv_dw_w, row(cnv_dw_b), row(cnv_ln_g), row(cnv_ln_b),
        ms[:, GDN_WIDTH:GDN_WIDTH + CONV_WIDTH], tiles["tm"], t // tiles["tm"])
    r3 = lambda a: a.reshape(b, t, a.shape[-1])
    assert tiles["tt"] == tiles["tq"]
    g_cum, beta, g_t, stats, qa, ka, va = _gates(
        r3(small), _lane_row(gdn_a_log, 0), _lane_row(gdn_dt_bias, 0), _lane_row(fox_f_bias, 2 * GDN_HEADS),
        r3(fq), r3(fk), r3(fv), tiles["tt"])
    norm_w2 = jnp.tile(gdn_norm_w.astype(F32), 2).reshape(1, PAIR)
    o_a = _gdn(r3(qkv), g_cum, beta, g_t, r3(gate), norm_w2, ms[:, :GDN_WIDTH], tiles["gdn_blk"])
    o_c = _fox(qa, ka, va, _fox_aux(stats), ms[:, GDN_WIDTH + CONV_WIDTH:], tiles["tq"])
    flat = lambda a: a.reshape(b * t, a.shape[-1])
    return _out_proj(flat(o_a), o_b, flat(o_c), xf, w_out.astype(MXU_DTYPE), row(ln_g), row(ln_b),
                     tiles["tm"])


def _tiles(t):
    return dict(tm=min(512, t), tt=min(512, t), gdn_blk=min(512, t), tq=min(512, t),
                tf_moe=1792)


def kernel(x, w_in, mix_scale, w_out, gdn_conv_w, gdn_a_log, gdn_dt_bias, gdn_norm_w, cnv_dw_w, cnv_dw_b,
           cnv_ln_g, cnv_ln_b, fox_f_bias, ln_mix_g, ln_mix_b, ln_ffn_g, ln_ffn_b, ffn_w1, ffn_w3, ffn_w2,
           moe_router, moe_w1, moe_w3, moe_w2):
    b, t, d = x.shape
    tiles = _tiles(t)
    row = lambda v: v.reshape(1, -1).astype(F32)
    xf = x.reshape(b * t, d)
    for l in range(DEPTH):
        xf = _mixer(xf, b, t, w_in[l], mix_scale[l], w_out[l], gdn_conv_w[l], gdn_a_log[l], gdn_dt_bias[l],
                    gdn_norm_w[l], cnv_dw_w[l], cnv_dw_b[l], cnv_ln_g[l], cnv_ln_b[l], fox_f_bias[l],
                    ln_mix_g[l], ln_mix_b[l], tiles)
        if l % 2 == 0:
            e = l // 2
            xf = _dense_ffn(xf, ffn_w1[e].astype(MXU_DTYPE), ffn_w3[e].astype(MXU_DTYPE),
                            ffn_w2[e].astype(MXU_DTYPE), row(ln_ffn_g[l]), row(ln_ffn_b[l]),
                            tiles["tm"])
        else:
            e = l // 2
            xf = _moe_ffn(xf, moe_router[e], moe_w1[e].astype(MXU_DTYPE), moe_w3[e].astype(MXU_DTYPE),
                          moe_w2[e].astype(MXU_DTYPE), row(ln_ffn_g[l]), row(ln_ffn_b[l]),
                          tiles["tm"], tiles["tf_moe"])
    return xf.reshape(b, t, d)
```

```python
import functools

import jax
import jax.numpy as jnp
import numpy as np
from jax import lax
from jax.experimental import pallas as pl
from jax.experimental.pallas import tpu as pltpu

D_MODEL = 1024
DEPTH = 2
HEAD_DIM = 64
GDN_WIDTH = 384
CONV_WIDTH = 256
FOX_WIDTH = 384
GDN_HEADS = 6
FOX_HEADS = 6
GDN_SHORT_CONV = 4
GDN_CHUNK = 64
CONV_KERNEL = 31
FFN_DENSE = 2816
N_EXPERTS = 8
TOP_K = 2
FFN_EXPERT = 3584
DEEPNORM_ALPHA = (2 * DEPTH) ** 0.25
LN_EPS = 1e-5
NORM_EPS = 1e-6

LANES = 128
SUBLANES = 8
PAIR = 2 * HEAD_DIM
N_PAIRS = GDN_WIDTH // PAIR
SMALL_COLS = LANES

_C_QKV = 0
_C_GATE = _C_QKV + 3 * GDN_WIDTH
_C_GLU = _C_GATE + GDN_WIDTH
_C_FQ = _C_GLU + 2 * CONV_WIDTH
_C_FK = _C_FQ + FOX_WIDTH
_C_FV = _C_FK + FOX_WIDTH
_C_END = _C_FV + FOX_WIDTH

MXU_DTYPE = jnp.bfloat16
F32 = jnp.float32

VMEM_LIMIT = 56 * 1024 * 1024


def _cparams(*sem):
    return pltpu.CompilerParams(dimension_semantics=sem, vmem_limit_bytes=VMEM_LIMIT)


def _dot(a, b):
    return jnp.dot(a.astype(MXU_DTYPE), b.astype(MXU_DTYPE), preferred_element_type=F32)


def _dot_nt(a, b):
    return lax.dot_general(a.astype(MXU_DTYPE), b.astype(MXU_DTYPE),
                           (((1,), (1,)), ((), ())), preferred_element_type=F32)


def _split(a):
    hi = a.astype(jnp.bfloat16)
    return hi, (a - hi.astype(F32)).astype(jnp.bfloat16)


def _split3(a):
    hi = a.astype(jnp.bfloat16)
    rest = a - hi.astype(F32)
    mid = rest.astype(jnp.bfloat16)
    lo = (rest - mid.astype(F32)).astype(jnp.bfloat16)
    return hi, mid, lo


def _dot_ones(a, ones_b16):
    ah, al = _split(a)
    d = lambda x: jnp.dot(x, ones_b16, preferred_element_type=F32)
    return d(ah) + d(al)


def _sel_rhs(a, sel_b16):
    return sum(jnp.dot(t, sel_b16, preferred_element_type=F32) for t in _split3(a))


def _sel_lhs(sel_b16, b):
    return sum(jnp.dot(sel_b16, t, preferred_element_type=F32) for t in _split3(b))


def _sigmoid(x):
    return 1.0 / (1.0 + jnp.exp(-x))


def _silu(x):
    return x * _sigmoid(x)


def _softplus(x):
    return jnp.maximum(x, 0.0) + jnp.log1p(jnp.exp(-jnp.abs(x)))


def _log_sigmoid(x):
    return -_softplus(-x)


def _layer_norm_rows(y, g, b):
    mu = jnp.mean(y, axis=-1, keepdims=True)
    d = y - mu
    var = jnp.mean(d * d, axis=-1, keepdims=True)
    return d * lax.rsqrt(var + LN_EPS) * g + b


def _head_ones():
    r = lax.broadcasted_iota(jnp.int32, (PAIR, PAIR), 0) // HEAD_DIM
    c = lax.broadcasted_iota(jnp.int32, (PAIR, PAIR), 1) // HEAD_DIM
    return (r == c).astype(F32)


CONV_HALO = 32
CONV_ROW_BLOCK = 128
SHORT_CONV_ROW_BLOCK = 128


def _in_proj_kernel(x_ref, w_ref, ws_ref, gw_ref, cw_ref, cb_ref, lg_ref, lb_ref, ms_ref,
                    qkv_ref, gate_ref, ob_ref, fq_ref, fk_ref, fv_ref, small_ref,
                    qbuf_ref, cbuf_ref, sh_ref, *, tiles_per_seq):
    first = pl.program_id(0) % tiles_per_seq == 0
    tm = x_ref.shape[0]
    x = x_ref[...]
    xb = x.astype(MXU_DTYPE)

    def mm(lo, hi):
        return jnp.dot(xb, w_ref[:, lo:hi], preferred_element_type=F32)

    @pl.when(first)
    def _():
        cbuf_ref[0:CONV_HALO, :] = jnp.zeros((CONV_HALO, CONV_WIDTH), F32)
        qbuf_ref[0:SUBLANES, :] = jnp.zeros((SUBLANES, 3 * GDN_WIDTH), F32)

    @pl.when(jnp.logical_not(first))
    def _():
        cbuf_ref[0:CONV_HALO, :] = cbuf_ref[tm:tm + CONV_HALO, :]
        qbuf_ref[0:SUBLANES, :] = qbuf_ref[tm:tm + SUBLANES, :]

    glu = mm(_C_GLU, _C_FQ)
    cbuf_ref[CONV_HALO:CONV_HALO + tm, :] = glu[:, 0:CONV_WIDTH] * _sigmoid(glu[:, CONV_WIDTH:2 * CONV_WIDTH])
    span = tm + CONV_HALO - SUBLANES
    for s in range(1, SUBLANES):
        sh_ref[s - 1] = cbuf_ref[s:s + span, :]
    for r0 in range(0, tm, CONV_ROW_BLOCK):
        acc = jnp.zeros((CONV_ROW_BLOCK, CONV_WIDTH), F32) + cb_ref[...]
        for j in range(CONV_KERNEL):
            lo = r0 + CONV_HALO - (CONV_KERNEL - 1) + j
            base, phase = lo - lo % SUBLANES, lo % SUBLANES
            tap = (cbuf_ref[base:base + CONV_ROW_BLOCK, :] if phase == 0
                   else sh_ref[phase - 1, base:base + CONV_ROW_BLOCK, :])
            acc = acc + cw_ref[j:j + 1, :] * tap
        y = _silu(_layer_norm_rows(acc, lg_ref[...], lb_ref[...])) * ms_ref[...]
        ob_ref[r0:r0 + CONV_ROW_BLOCK, :] = y.astype(ob_ref.dtype)

    qbuf_ref[SUBLANES:SUBLANES + tm, :] = mm(_C_QKV, _C_GATE)
    for r0 in range(0, tm, SHORT_CONV_ROW_BLOCK):
        acc = jnp.zeros((SHORT_CONV_ROW_BLOCK, 3 * GDN_WIDTH), F32)
        for j in range(GDN_SHORT_CONV):
            lo = r0 + SUBLANES - (GDN_SHORT_CONV - 1) + j
            acc = acc + gw_ref[j:j + 1, :] * qbuf_ref[lo:lo + SHORT_CONV_ROW_BLOCK, :]
        qkv_ref[r0:r0 + SHORT_CONV_ROW_BLOCK, :] = _silu(acc)

    gate_ref[...] = mm(_C_GATE, _C_GLU)
    fq_ref[...] = (mm(_C_FQ, _C_FK) * HEAD_DIM ** -0.5).astype(fq_ref.dtype)
    fk_ref[...] = mm(_C_FK, _C_FV).astype(fk_ref.dtype)
    fv_ref[...] = mm(_C_FV, _C_END).astype(fv_ref.dtype)
    x_lo = (x - xb.astype(F32)).astype(MXU_DTYPE)
    hh_hl = jnp.dot(xb, ws_ref[...], preferred_element_type=F32)
    lh = jnp.dot(x_lo, ws_ref[:, 0:SMALL_COLS], preferred_element_type=F32)
    small_ref[...] = hh_hl[:, 0:SMALL_COLS] + hh_hl[:, SMALL_COLS:2 * SMALL_COLS] + lh


def _in_proj(xf, w_main, w_small, gdn_conv_w, cnv_w, cnv_b, cnv_ln_g, cnv_ln_b, cnv_scale, tm, tiles_per_seq):
    n = xf.shape[0]
    row = lambda i: (i, 0)
    const = lambda shape: pl.BlockSpec(shape, lambda i: (0, 0))
    outs = [
        jax.ShapeDtypeStruct((n, 3 * GDN_WIDTH), F32),
        jax.ShapeDtypeStruct((n, GDN_WIDTH), F32),
        jax.ShapeDtypeStruct((n, CONV_WIDTH), MXU_DTYPE),
        jax.ShapeDtypeStruct((n, FOX_WIDTH), MXU_DTYPE),
        jax.ShapeDtypeStruct((n, FOX_WIDTH), MXU_DTYPE),
        jax.ShapeDtypeStruct((n, FOX_WIDTH), MXU_DTYPE),
        jax.ShapeDtypeStruct((n, SMALL_COLS), F32),
    ]
    vec = const((1, CONV_WIDTH))
    return pl.pallas_call(
        functools.partial(_in_proj_kernel, tiles_per_seq=tiles_per_seq),
        grid=(n // tm,),
        in_specs=[pl.BlockSpec((tm, D_MODEL), row),
                  const((D_MODEL, _C_END)), const((D_MODEL, 2 * SMALL_COLS)),
                  const((GDN_SHORT_CONV, 3 * GDN_WIDTH)), const((CONV_KERNEL, CONV_WIDTH)), vec, vec, vec, vec],
        out_specs=[pl.BlockSpec((tm, o.shape[1]), row) for o in outs],
        out_shape=outs,
        scratch_shapes=[pltpu.VMEM((tm + SUBLANES, 3 * GDN_WIDTH), F32),
                        pltpu.VMEM((tm + CONV_HALO, CONV_WIDTH), F32),
                        pltpu.VMEM((SUBLANES - 1, tm + CONV_HALO - SUBLANES, CONV_WIDTH), F32)],
        compiler_params=_cparams("arbitrary"),
        name="in_proj",
    )(xf, w_main, w_small, gdn_conv_w, cnv_w, cnv_b, cnv_ln_g, cnv_ln_b, cnv_scale)


AUG_BIAS_LANE = HEAD_DIM
KEY_NORM_SLACK = 1.01


def _gates_kernel(small_ref, a_ref, dtb_ref, fb_ref, fq_ref, fk_ref, fv_ref,
                  g_ref, beta_ref, gt_ref, st_ref, qa_ref, ka_ref, va_ref, carry_ref):
    t = pl.program_id(1)
    tt = small_ref.shape[1]
    b16 = jnp.bfloat16

    @pl.when(t == 0)
    def _():
        carry_ref[...] = jnp.zeros_like(carry_ref)

    s = small_ref[0]
    log_decay = -jnp.exp(a_ref[...]) * _softplus(s + dtb_ref[...])
    beta = _sigmoid(s)
    log_f = _log_sigmoid(s + fb_ref[...])

    lane = lax.broadcasted_iota(jnp.int32, (1, LANES), 1)
    rr = lax.broadcasted_iota(jnp.int32, (LANES, LANES), 0)
    cc = lax.broadcasted_iota(jnp.int32, (LANES, LANES), 1)
    tri_chunk = ((cc <= rr) & (rr // GDN_CHUNK == cc // GDN_CHUNK)).astype(b16)
    both = jnp.where(lane < 2 * GDN_HEADS, log_decay, log_f)
    in_chunk = jnp.concatenate([_sel_lhs(tri_chunk, both[r0:r0 + LANES]) for r0 in range(0, tt, LANES)], axis=0)
    g_cum = in_chunk
    carry = carry_ref[...]
    pieces = []
    for r0 in range(0, tt, GDN_CHUNK):
        chunk = in_chunk[r0:r0 + GDN_CHUNK]
        pieces.append(chunk + carry)
        carry = carry + chunk[GDN_CHUNK - 1:GDN_CHUNK]
    c_cum = jnp.concatenate(pieces, axis=0)
    carry_ref[...] = carry

    er = lax.broadcasted_iota(jnp.int32, (LANES, 2 * GDN_WIDTH), 0)
    ec = lax.broadcasted_iota(jnp.int32, (LANES, 2 * GDN_WIDTH), 1) // HEAD_DIM
    expanded = _sel_rhs(jnp.where(lane < GDN_HEADS, g_cum, beta), (er == ec).astype(b16))
    g_ref[0] = expanded[:, 0:GDN_WIDTH]
    beta_ref[0] = expanded[:, GDN_WIDTH:2 * GDN_WIDTH]
    gt_ref[0] = g_cum.T[0:SUBLANES, :]

    head_ones = _head_ones().astype(b16)
    stat_rows = []
    for p in range(N_PAIRS):
        k_pair = fk_ref[0, :, p * PAIR:(p + 1) * PAIR].astype(F32)
        k_sq = jnp.dot((k_pair * k_pair).astype(b16), head_ones, preferred_element_type=F32)
        stat_rows.append(jnp.max(k_sq, axis=0, keepdims=True))
    stat_rows.append(c_cum[tt - 1:tt, :])
    stat_rows.append(jnp.zeros((SUBLANES - len(stat_rows), LANES), F32))
    st_ref[0, 0] = jnp.concatenate(stat_rows, axis=0)

    q_ones = ((lane >= AUG_BIAS_LANE) & (lane < AUG_BIAS_LANE + 3)).astype(F32)
    v_one = (lane == AUG_BIAS_LANE).astype(F32)
    br = lax.broadcasted_iota(jnp.int32, (3 * LANES, FOX_HEADS * LANES), 0)
    bc = lax.broadcasted_iota(jnp.int32, (3 * LANES, FOX_HEADS * LANES), 1)
    place = ((br % LANES == 2 * GDN_HEADS + bc // LANES) & (bc % LANES == AUG_BIAS_LANE + br // LANES)).astype(b16)
    bias_all = jnp.dot(jnp.concatenate(_split3(-c_cum), axis=1), place, preferred_element_type=F32)
    upper_to_lower = ((cc < HEAD_DIM) & (rr == cc + HEAD_DIM)).astype(b16)
    for h in range(FOX_HEADS):
        p, half = divmod(h, 2)
        lanes = slice(p * PAIR, (p + 1) * PAIR)

        def pick(ref):
            if half == 0:
                return jnp.where(lane < HEAD_DIM, ref[0, :, lanes].astype(F32), 0.0)
            return jnp.dot(ref[0, :, lanes], upper_to_lower, preferred_element_type=F32)

        qa_ref[0, h] = (pick(fq_ref) + q_ones).astype(b16)
        ka_ref[0, h] = (pick(fk_ref) + bias_all[:, h * LANES:(h + 1) * LANES]).astype(b16)
        va_ref[0, h] = (pick(fv_ref) + v_one).astype(b16)


def _gates(small, a_row, dtb_row, fb_row, fq, fk, fv, tt):
    b, t, _ = small.shape
    row = pl.BlockSpec((1, LANES), lambda i, j: (0, 0))
    tile = lambda w: pl.BlockSpec((1, tt, w), lambda i, j: (i, j, 0))
    aug = pl.BlockSpec((1, FOX_HEADS, tt, LANES), lambda i, j: (i, 0, j, 0))
    aug_shape = jax.ShapeDtypeStruct((b, FOX_HEADS, t, LANES), jnp.bfloat16)
    return pl.pallas_call(
        _gates_kernel,
        grid=(b, t // tt),
        in_specs=[tile(SMALL_COLS), row, row, row, tile(FOX_WIDTH), tile(FOX_WIDTH), tile(FOX_WIDTH)],
        out_specs=[tile(GDN_WIDTH), tile(GDN_WIDTH), pl.BlockSpec((1, SUBLANES, tt), lambda i, j: (i, 0, j)),
                   pl.BlockSpec((1, 1, SUBLANES, LANES), lambda i, j: (i, j, 0, 0)), aug, aug, aug],
        out_shape=[jax.ShapeDtypeStruct((b, t, GDN_WIDTH), F32), jax.ShapeDtypeStruct((b, t, GDN_WIDTH), F32),
                   jax.ShapeDtypeStruct((b, SUBLANES, t), F32),
                   jax.ShapeDtypeStruct((b, t // tt, SUBLANES, LANES), F32), aug_shape, aug_shape, aug_shape],
        scratch_shapes=[pltpu.VMEM((1, LANES), F32)],
        compiler_params=_cparams("parallel", "arbitrary"),
        name="gates",
    )(small, a_row, dtb_row, fb_row, fq, fk, fv)


def _unit_lower_inverse_many(lows):
    n = lows[0].shape[0]
    r = lax.broadcasted_iota(jnp.int32, (n, n), 0)
    c = lax.broadcasted_iota(jnp.int32, (n, n), 1)
    eye = (r == c).astype(F32)
    base = GDN_CHUNK // 4
    diag = r // base == c // base
    cast = lambda xs: [x.astype(MXU_DTYPE) for x in xs]
    mm = lambda xs, ys: [jnp.dot(x, y, preferred_element_type=F32) for x, y in zip(xs, ys)]
    add = lambda xs, ys: [x + y for x, y in zip(xs, ys)]
    d = [jnp.where(diag, low, 0.0) for low in lows]
    db = cast(d)
    d2b = cast(mm(db, db))
    p = [eye - x for x in d]
    p = add(p, mm(cast(p), d2b))
    d4b = cast(mm(d2b, d2b))
    p = add(p, mm(cast(p), d4b))
    d8b = cast(mm(d4b, d4b))
    x = add(p, mm(cast(p), d8b))
    for blk in (2 * base, 4 * base):
        sel = (r // blk == c // blk) & (r // (blk // 2) != c // (blk // 2))
        xb = cast(x)
        xo = mm(xb, cast([jnp.where(sel, low, 0.0) for low in lows]))
        x = [a - b for a, b in zip(x, mm(cast(xo), xb))]
    return x


def _gdn_kernel(qkv_ref, g_ref, beta_ref, gt_ref, gate_ref, nw_ref, ms_ref, o_ref, s_ref):
    t = pl.program_id(1)
    blk = qkv_ref.shape[1]
    nc = blk // GDN_CHUNK

    @pl.when(t == 0)
    def _():
        s_ref[...] = jnp.zeros_like(s_ref)

    qkv = qkv_ref[0]

    ones = _head_ones().astype(jnp.bfloat16)
    lane = lax.broadcasted_iota(jnp.int32, (1, PAIR), 1)
    head0 = lane < HEAD_DIM
    n2 = 2 * GDN_CHUNK
    r = lax.broadcasted_iota(jnp.int32, (n2, n2), 0)
    c = lax.broadcasted_iota(jnp.int32, (n2, n2), 1)
    same = (r // GDN_CHUNK) == (c // GDN_CHUNK)
    causal = same & (c <= r)
    strict = same & (c < r)

    def l2n(v):
        return v * lax.rsqrt(_dot_ones(v * v, ones) + NORM_EPS)

    def stack(v):
        return jnp.concatenate([jnp.where(head0, v, 0.0), jnp.where(head0, 0.0, v)], axis=0)

    pair = lambda p, grp: slice(grp * GDN_WIDTH + p * PAIR, grp * GDN_WIDTH + (p + 1) * PAIR)
    q_p = [l2n(qkv[:, pair(p, 0)]) * HEAD_DIM ** -0.5 for p in range(N_PAIRS)]
    k_p = [l2n(qkv[:, pair(p, 1)]) for p in range(N_PAIRS)]
    v_p = [qkv[:, pair(p, 2)] for p in range(N_PAIRS)]

    units = [(ci, p) for ci in range(nc) for p in range(N_PAIRS)]
    rows = lambda ci: slice(ci * GDN_CHUNK, (ci + 1) * GDN_CHUNK)
    gc = [g_ref[0, rows(ci), pair(p, 0)] for ci, p in units]
    bt = [beta_ref[0, rows(ci), pair(p, 0)] for ci, p in units]
    qn = [q_p[p][rows(ci)] for ci, p in units]
    kn = [k_p[p][rows(ci)] for ci, p in units]
    vv = [v_p[p][rows(ci)] for ci, p in units]
    g_last = [g[GDN_CHUNK - 1:GDN_CHUNK, :] for g in gc]
    eg = [jnp.exp(g) for g in gc]
    kb = [k * b for k, b in zip(kn, bt)]
    k2 = [stack(k) for k in kn]

    def decay_of(g, ci, p):
        g_col = jnp.concatenate([jnp.broadcast_to(g[:, 0:1], (GDN_CHUNK, PAIR)),
                                 jnp.broadcast_to(g[:, HEAD_DIM:HEAD_DIM + 1], (GDN_CHUNK, PAIR))], axis=0)
        g_row = jnp.concatenate([gt_ref[0, 2 * p:2 * p + 1, rows(ci)], gt_ref[0, 2 * p + 1:2 * p + 2, rows(ci)]],
                                axis=1)
        return jnp.where(causal, jnp.exp(jnp.where(causal, g_col - g_row, 0.0)), 0.0)

    decay = [decay_of(g, ci, p) for g, (ci, p) in zip(gc, units)]
    k2b = [x.astype(MXU_DTYPE) for x in k2]
    low = [jnp.where(strict, _dot_nt(stack(a), b) * d, 0.0) for a, b, d in zip(kb, k2b, decay)]
    a_in = [(_dot_nt(stack(a), b) * d).astype(MXU_DTYPE) for a, b, d in zip(qn, k2b, decay)]
    t_inv = _unit_lower_inverse_many(low)
    uw = [_dot(ti, jnp.concatenate([stack(v * b), stack(a * e)], axis=1)).astype(MXU_DTYPE)
          for ti, v, b, a, e in zip(t_inv, vv, bt, kb, eg)]
    ket = [stack(k * jnp.exp(gl - g)).T for k, gl, g in zip(kn, g_last, gc)]
    nm = [_dot(a, b) for a, b in zip(ket, uw)]
    raw = [jnp.dot(a, b, preferred_element_type=F32) for a, b in zip(a_in, uw)]
    p_mat = [stack(q * e) - x[:, PAIR:] for q, e, x in zip(qn, eg, raw)]

    state = [s_ref[p] for p in range(N_PAIRS)]
    for i, (ci, p) in enumerate(units):
        s = state[p]
        o2 = _dot(p_mat[i], s) + raw[i][:, :PAIR]
        state[p] = s * jnp.exp(g_last[i]) - _dot(nm[i][:, PAIR:], s) + nm[i][:, :PAIR]
        o = o2[0:GDN_CHUNK] + o2[GDN_CHUNK:n2]
        ms = _dot_ones(o * o, ones) * (1.0 / HEAD_DIM)
        on = o * lax.rsqrt(ms + NORM_EPS) * nw_ref[...]
        o_ref[0, rows(ci), pair(p, 0)] = (on * _silu(gate_ref[0, rows(ci), pair(p, 0)])
                                          * ms_ref[:, pair(p, 0)]).astype(o_ref.dtype)
    for p in range(N_PAIRS):
        s_ref[p] = state[p]


def _gdn(qkv, g, beta, g_t, gate, norm_w2, mscale, blk):
    b, t, _ = qkv.shape
    tile = lambda width: pl.BlockSpec((1, blk, width), lambda i, j: (i, j, 0))
    const = lambda shape: pl.BlockSpec(shape, lambda i, j: (0, 0))
    return pl.pallas_call(
        _gdn_kernel,
        grid=(b, t // blk),
        in_specs=[tile(3 * GDN_WIDTH),
                  tile(GDN_WIDTH), tile(GDN_WIDTH), pl.BlockSpec((1, SUBLANES, blk), lambda i, j: (i, 0, j)),
                  tile(GDN_WIDTH), const((1, PAIR)), const((1, GDN_WIDTH))],
        out_specs=tile(GDN_WIDTH),
        out_shape=jax.ShapeDtypeStruct((b, t, GDN_WIDTH), MXU_DTYPE),
        scratch_shapes=[pltpu.VMEM((N_PAIRS, PAIR, PAIR), F32)],
        compiler_params=_cparams("parallel", "arbitrary"),
        name="gdn",
    )(qkv, g, beta, g_t, gate, norm_w2, mscale)


FOX_ROW_GROUP = 32


FOX_SKIP_MARGIN = 106.0


def _fox_kernel(q_ref, k_ref, v_ref, aux_ref, ms_ref, o_ref, s0_ref, s1_ref, p0_ref, p1_ref, a0_ref, a1_ref,
                m_ref, acc_ref, *, tq):
    i = pl.program_id(2)
    heads = range(2)
    rg = FOX_ROW_GROUP
    m_ref[...] = jnp.full(m_ref.shape, -jnp.inf, F32)
    acc_ref[...] = jnp.zeros(acc_ref.shape, F32)

    def scores(j, s_ref):
        start = pl.multiple_of(j * tq, tq)
        for h in heads:
            s_ref[h] = lax.dot_general(q_ref[0, h], k_ref[0, h, pl.ds(start, tq), :],
                                       (((1,), (1,)), ((), ())), preferred_element_type=F32)

    def softmax(s_ref, p_ref, a_ref, masked):
        for h in heads:
            for g in range(tq // rg):
                rows = slice(g * rg, (g + 1) * rg)
                width = LANES * (((g + 1) * rg - 1) // LANES + 1) if masked else tq
                s = s_ref[h, rows, 0:width]
                if masked:
                    row_id = g * rg + lax.broadcasted_iota(jnp.int32, (rg, width), 0)
                    col_id = lax.broadcasted_iota(jnp.int32, (rg, width), 1)
                    s = jnp.where(col_id <= row_id, s, -jnp.inf)
                    if width < tq:
                        p_ref[h, rows, width:tq] = jnp.zeros((rg, tq - width), p_ref.dtype)
                m_old = m_ref[h, rows, :]
                m_new = jnp.maximum(m_old, jnp.max(s, axis=-1, keepdims=True))
                a_ref[h, rows, :] = jnp.exp(m_old - m_new)
                m_ref[h, rows, :] = m_new
                m_wide = jnp.concatenate([m_new] * (width // LANES), axis=1)
                p_ref[h, rows, 0:width] = jnp.exp(s - m_wide).astype(p_ref.dtype)

    def weighted_values(j, p_ref, a_ref):
        start = pl.multiple_of(j * tq, tq)
        for h in heads:
            acc_ref[h] = a_ref[h] * acc_ref[h] + jnp.dot(p_ref[h], v_ref[0, h, pl.ds(start, tq), :],
                                                         preferred_element_type=F32)

    scores(i, s0_ref)
    softmax(s0_ref, p0_ref, a0_ref, True)
    scores(jnp.maximum(i - 1, 0), s1_ref)
    weighted_values(i, p0_ref, a0_ref)

    lane = lax.broadcasted_iota(jnp.int32, (1, LANES), 1)
    n = jnp.int32(0)
    for h in heads:
        q = q_ref[0, h].astype(F32)
        q_norm = jnp.sqrt(jnp.sum(jnp.where(lane < HEAD_DIM, q * q, 0.0), axis=-1, keepdims=True))
        key_norm = aux_ref[0, h, 1:2, :]
        block_end_c = aux_ref[0, h, 0:1, :]
        slack = jnp.max(q_norm * key_norm - m_ref[h], axis=0, keepdims=True)
        keep = (lane < i) & (slack - block_end_c >= -FOX_SKIP_MARGIN)
        n = jnp.maximum(n, jnp.sum(keep.astype(jnp.int32)))

    def body(t, carry):
        j = i - 1 - 2 * t
        softmax(s1_ref, p1_ref, a1_ref, False)
        scores(jnp.maximum(j - 1, 0), s0_ref)
        weighted_values(j, p1_ref, a1_ref)
        softmax(s0_ref, p0_ref, a0_ref, False)
        scores(jnp.maximum(j - 2, 0), s1_ref)
        weighted_values(j - 1, p0_ref, a0_ref)
        return carry

    lax.fori_loop(0, n // 2, body, 0)

    @pl.when(n % 2 == 1)
    def _():
        softmax(s1_ref, p1_ref, a1_ref, False)
        weighted_values(i - n, p1_ref, a1_ref)

    o = [acc_ref[h] / acc_ref[h][:, AUG_BIAS_LANE:AUG_BIAS_LANE + 1] for h in heads]
    lane = lax.broadcasted_iota(jnp.int32, (1, PAIR), 1)
    o_pair = jnp.where(lane < HEAD_DIM, o[0], pltpu.roll(o[1], HEAD_DIM, axis=1))
    o_ref[0] = (o_pair * ms_ref[...]).astype(o_ref.dtype)


def _fox_aux(stats):
    b, nt = stats.shape[:2]
    k_sq = jnp.max(stats[:, :, :N_PAIRS, :], axis=1)
    k_sq = k_sq.reshape(b, N_PAIRS, 2, HEAD_DIM)[..., 0].reshape(b, FOX_HEADS)
    key_norm = jnp.sqrt(k_sq) * KEY_NORM_SLACK
    block_end_c = jnp.transpose(stats[:, :, N_PAIRS, 2 * GDN_HEADS:2 * GDN_HEADS + FOX_HEADS], (0, 2, 1))
    aux = jnp.zeros((b, FOX_HEADS, SUBLANES, LANES), F32)
    aux = aux.at[:, :, 0, :nt].set(block_end_c)
    return aux.at[:, :, 1, :].set(jnp.broadcast_to(key_norm[:, :, None], (b, FOX_HEADS, LANES)))


def _fox(qa, ka, va, aux, mscale, tq):
    b, _, t, _ = qa.shape
    kern = functools.partial(_fox_kernel, tq=tq)
    return pl.pallas_call(
        kern,
        grid=(b, N_PAIRS, t // tq),
        in_specs=[pl.BlockSpec((1, 2, tq, LANES), lambda bi, p, i: (bi, p, i, 0)),
                  pl.BlockSpec((1, 2, t, LANES), lambda bi, p, i: (bi, p, 0, 0)),
                  pl.BlockSpec((1, 2, t, LANES), lambda bi, p, i: (bi, p, 0, 0)),
                  pl.BlockSpec((1, 2, SUBLANES, LANES), lambda bi, p, i: (bi, p, 0, 0)),
                  pl.BlockSpec((1, PAIR), lambda bi, p, i: (0, p))],
        out_specs=pl.BlockSpec((1, tq, PAIR), lambda bi, p, i: (bi, i, p)),
        out_shape=jax.ShapeDtypeStruct((b, t, FOX_WIDTH), MXU_DTYPE),
        scratch_shapes=[pltpu.VMEM((2, tq, tq), F32), pltpu.VMEM((2, tq, tq), F32),
                        pltpu.VMEM((2, tq, tq), jnp.bfloat16), pltpu.VMEM((2, tq, tq), jnp.bfloat16),
                        pltpu.VMEM((2, tq, LANES), F32), pltpu.VMEM((2, tq, LANES), F32),
                        pltpu.VMEM((2, tq, LANES), F32), pltpu.VMEM((2, tq, LANES), F32)],
        compiler_params=_cparams("parallel", "parallel", "arbitrary"),
        name="fox_attention",
    )(qa, ka, va, aux, mscale)


def _out_proj_kernel(oa_ref, ob_ref, oc_ref, x_ref, w_ref, g_ref, b_ref, o_ref):
    mixed = jnp.concatenate([oa_ref[...], ob_ref[...], oc_ref[...]], axis=1)
    mix = jnp.dot(mixed, w_ref[...], preferred_element_type=F32)
    o_ref[...] = _layer_norm_rows(DEEPNORM_ALPHA * x_ref[...] + mix, g_ref[...], b_ref[...])


def _out_proj(oa, ob, oc, xf, w, g, bvec, tm):
    n = xf.shape[0]
    row = lambda i: (i, 0)
    const = lambda i: (0, 0)
    return pl.pallas_call(
        _out_proj_kernel,
        grid=(n // tm,),
        in_specs=[pl.BlockSpec((tm, GDN_WIDTH), row), pl.BlockSpec((tm, CONV_WIDTH), row),
                  pl.BlockSpec((tm, FOX_WIDTH), row), pl.BlockSpec((tm, D_MODEL), row),
                  pl.BlockSpec((D_MODEL, D_MODEL), const),
                  pl.BlockSpec((1, D_MODEL), const), pl.BlockSpec((1, D_MODEL), const)],
        out_specs=pl.BlockSpec((tm, D_MODEL), row),
        out_shape=jax.ShapeDtypeStruct((n, D_MODEL), F32),
        compiler_params=_cparams("parallel"),
        name="out_proj_ln",
    )(oa, ob, oc, xf, w, g, bvec)


FFN_CHUNK = 256


def _swiglu_chunks(xb, w1, w3, w2, width):
    assert width % FFN_CHUNK == 0
    acc = None
    for c in range(width // FFN_CHUNK):
        cols = slice(c * FFN_CHUNK, (c + 1) * FFN_CHUNK)
        h = _silu(jnp.dot(xb, w1(slice(None), cols), preferred_element_type=F32)) * jnp.dot(
            xb, w3(slice(None), cols), preferred_element_type=F32)
        part = jnp.dot(h.astype(xb.dtype), w2(cols, slice(None)), preferred_element_type=F32)
        acc = part if acc is None else acc + part
    return acc


def _ffn_kernel(x_ref, w1_ref, w3_ref, w2_ref, g_ref, b_ref, o_ref):
    x = x_ref[...]
    ff = _swiglu_chunks(x.astype(w1_ref.dtype), lambda r, c: w1_ref[r, c], lambda r, c: w3_ref[r, c],
                        lambda r, c: w2_ref[r, c], w1_ref.shape[1])
    o_ref[...] = _layer_norm_rows(DEEPNORM_ALPHA * x + ff, g_ref[...], b_ref[...])


def _dense_ffn(xf, w1, w3, w2, g, bvec, tm):
    n = xf.shape[0]
    ff = w1.shape[1]
    const = lambda shape: pl.BlockSpec(shape, lambda i: (0, 0), pipeline_mode=pl.Buffered(1))
    return pl.pallas_call(
        _ffn_kernel,
        grid=(n // tm,),
        in_specs=[pl.BlockSpec((tm, D_MODEL), lambda i: (i, 0)),
                  const((D_MODEL, ff)), const((D_MODEL, ff)), const((ff, D_MODEL)),
                  const((1, D_MODEL)), const((1, D_MODEL))],
        out_specs=pl.BlockSpec((tm, D_MODEL), lambda i: (i, 0)),
        out_shape=jax.ShapeDtypeStruct((n, D_MODEL), F32),
        compiler_params=_cparams("parallel"),
        name="dense_ffn_ln",
    )(xf, w1, w3, w2, g, bvec)


def _router_kernel(x_ref, wr_ref, info_ref, info_t_ref, cnt_ref, run_ref):
    i = pl.program_id(0)
    tm = x_ref.shape[0]

    @pl.when(i == 0)
    def _():
        run_ref[...] = jnp.zeros_like(run_ref)

    logits = jnp.dot(x_ref[...].astype(MXU_DTYPE), wr_ref[...], preferred_element_type=F32)
    lane = lax.broadcasted_iota(jnp.int32, (tm, LANES), 1)
    logits = jnp.where(lane < N_EXPERTS, logits, -jnp.inf)
    m1 = jnp.max(logits, axis=-1, keepdims=True)
    e1 = jnp.min(jnp.where(logits == m1, lane, LANES), axis=-1, keepdims=True)
    rest = jnp.where(lane == e1, -jnp.inf, logits)
    m2 = jnp.max(rest, axis=-1, keepdims=True)
    e2 = jnp.min(jnp.where(rest == m2, lane, LANES), axis=-1, keepdims=True)
    z = jnp.exp(m2 - m1)
    g1 = 1.0 / (1.0 + z)
    g2 = z / (1.0 + z)
    onehot = ((lane == e1) | (lane == e2)).astype(F32)
    r = lax.broadcasted_iota(jnp.int32, (tm, tm), 0)
    c = lax.broadcasted_iota(jnp.int32, (tm, tm), 1)
    before = jnp.dot((c < r).astype(jnp.bfloat16), onehot.astype(jnp.bfloat16),
                     preferred_element_type=F32) + run_ref[...]
    rank1 = jnp.sum(jnp.where(lane == e1, before, 0.0), axis=-1, keepdims=True)
    rank2 = jnp.sum(jnp.where(lane == e2, before, 0.0), axis=-1, keepdims=True)
    run_ref[...] = run_ref[...] + jnp.sum(onehot, axis=0, keepdims=True)
    cnt_ref[...] = run_ref[...]
    info = jnp.where(lane == 0, e1.astype(F32),
                     jnp.where(lane == 1, e2.astype(F32),
                               jnp.where(lane == 2, rank1,
                                         jnp.where(lane == 3, rank2,
                                                   jnp.where(lane == 4, g1, jnp.where(lane == 5, g2, 0.0))))))
    info_ref[...] = info
    info_t_ref[...] = info.T[0:SUBLANES, :]


def _router(xf, wr_pad, tm):
    n = xf.shape[0]
    return pl.pallas_call(
        _router_kernel,
        grid=(n // tm,),
        in_specs=[pl.BlockSpec((tm, D_MODEL), lambda i: (i, 0)),
                  pl.BlockSpec((D_MODEL, LANES), lambda i: (0, 0))],
        out_specs=[pl.BlockSpec((tm, LANES), lambda i: (i, 0)),
                   pl.BlockSpec((SUBLANES, tm), lambda i: (0, i)),
                   pl.BlockSpec((1, LANES), lambda i: (0, 0))],
        out_shape=[jax.ShapeDtypeStruct((n, LANES), F32), jax.ShapeDtypeStruct((SUBLANES, n), F32),
                   jax.ShapeDtypeStruct((1, LANES), F32)],
        scratch_shapes=[pltpu.VMEM((1, LANES), F32)],
        compiler_params=_cparams("arbitrary"),
        name="moe_router",
    )(xf, wr_pad)


DMA_LOOP_UNROLL = 8


def _dispatch_kernel(fill_ref, d1_ref, d2_ref, x_ref, xs_ref, zero_ref, sem):
    tm = x_ref.shape[0]

    @pl.when(pl.program_id(0) == 0)
    def _():
        zero_ref[...] = jnp.zeros_like(zero_ref)

        def strip_copy(e, k):
            start = pl.multiple_of(fill_ref[e] - SUBLANES * (k + 1), SUBLANES)
            return pltpu.make_async_copy(zero_ref, xs_ref.at[pl.ds(start, SUBLANES), :], sem)

        for e in range(N_EXPERTS):
            lax.fori_loop(0, fill_ref[N_EXPERTS + e], lambda k, c, e=e: (strip_copy(e, k).start(), c)[1], 0)
        for e in range(N_EXPERTS):
            lax.fori_loop(0, fill_ref[N_EXPERTS + e], lambda k, c, e=e: (strip_copy(e, k).wait(), c)[1], 0)

    def row_copy(r, dst):
        return pltpu.make_async_copy(x_ref.at[pl.ds(r, 1), :], xs_ref.at[pl.ds(dst, 1), :], sem)

    def issue(r, carry):
        row_copy(r, d1_ref[0, 0, r]).start(priority=0)
        row_copy(r, d2_ref[0, 0, r]).start(priority=1)
        return carry

    lax.fori_loop(0, tm, issue, 0, unroll=DMA_LOOP_UNROLL)

    def drain(r, carry):
        row_copy(r, d1_ref[0, 0, r]).wait()
        row_copy(r, d2_ref[0, 0, r]).wait()
        return carry

    lax.fori_loop(0, tm, drain, 0, unroll=DMA_LOOP_UNROLL)


def _dispatch(xf, d1, d2, gap_start, gap_end, rows, tm):
    n = xf.shape[0]
    idx = lambda a: a.reshape(n // tm, 1, tm)
    n_strips = (gap_end - gap_start + SUBLANES - 1) // SUBLANES
    fill = jnp.concatenate([gap_end, n_strips]).astype(jnp.int32)
    smem = lambda: pl.BlockSpec((1, 1, tm), lambda i, fill: (i, 0, 0), memory_space=pltpu.SMEM)
    grid_spec = pltpu.PrefetchScalarGridSpec(
        num_scalar_prefetch=1,
        grid=(n // tm,),
        in_specs=[smem(), smem(), pl.BlockSpec((tm, D_MODEL), lambda i, fill: (i, 0))],
        out_specs=pl.BlockSpec(memory_space=pl.ANY),
        scratch_shapes=[pltpu.VMEM((SUBLANES, D_MODEL), xf.dtype), pltpu.SemaphoreType.DMA(())],
    )
    return pl.pallas_call(
        _dispatch_kernel,
        grid_spec=grid_spec,
        out_shape=jax.ShapeDtypeStruct((rows, D_MODEL), xf.dtype),
        compiler_params=_cparams("arbitrary"),
        name="moe_dispatch",
    )(fill, idx(d1), idx(d2), xf)


def _expert_kernel(te_ref, nu_ref, xs_ref, w1_ref, w3_ref, w2_ref, ys_ref, xb_ref):
    del te_ref
    i = pl.program_id(0)
    f = pl.program_id(1)
    used = i < nu_ref[0]

    def partial_ffn(xb):
        return _swiglu_chunks(xb, lambda r, c: w1_ref[0, r, c], lambda r, c: w3_ref[0, r, c],
                              lambda r, c: w2_ref[0, r, c], w1_ref.shape[2])

    @pl.when(used & (f == 0))
    def _():
        xb = xs_ref[...].astype(xb_ref.dtype)
        xb_ref[...] = xb
        ys_ref[...] = partial_ffn(xb)

    @pl.when(used & (f > 0))
    def _():
        ys_ref[...] += partial_ffn(xb_ref[...])

    @pl.when(jnp.logical_not(used) & (f == 0))
    def _():
        ys_ref[...] = jnp.zeros_like(ys_ref)


def _experts(xs, tile_expert, n_used, w1, w3, w2, tm, tf):
    rows = xs.shape[0]
    ff = w1.shape[2]
    n_tiles = rows // tm

    def x_map(i, f, te, nu):
        return (jnp.maximum(jnp.minimum(i, nu[0] - 1), 0), 0)

    def w13_map(i, f, te, nu):
        return (te[i], 0, jnp.where(i < nu[0], f, ff // tf - 1))

    def w2_map(i, f, te, nu):
        return (te[i], jnp.where(i < nu[0], f, ff // tf - 1), 0)

    grid_spec = pltpu.PrefetchScalarGridSpec(
        num_scalar_prefetch=2,
        grid=(n_tiles, ff // tf),
        in_specs=[pl.BlockSpec((tm, D_MODEL), x_map),
                  pl.BlockSpec((1, D_MODEL, tf), w13_map),
                  pl.BlockSpec((1, D_MODEL, tf), w13_map),
                  pl.BlockSpec((1, tf, D_MODEL), w2_map)],
        out_specs=pl.BlockSpec((tm, D_MODEL), lambda i, f, te, nu: (i, 0)),
        scratch_shapes=[pltpu.VMEM((tm, D_MODEL), MXU_DTYPE)],
    )
    return pl.pallas_call(
        _expert_kernel,
        grid_spec=grid_spec,
        out_shape=jax.ShapeDtypeStruct((rows, D_MODEL), F32),
        compiler_params=_cparams("arbitrary", "arbitrary"),
        name="moe_experts",
    )(tile_expert, n_used, xs, w1, w3, w2)


def _combine_kernel(d1_ref, d2_ref, x_ref, info_ref, ys_ref, g_ref, b_ref, o_ref, y1_ref, y2_ref, sem):
    tm = x_ref.shape[0]

    def row_copy(src, r, buf):
        return pltpu.make_async_copy(ys_ref.at[pl.ds(src, 1), :], buf.at[pl.ds(r, 1), :], sem)

    def issue(r, carry):
        row_copy(d1_ref[0, 0, r], r, y1_ref).start(priority=0)
        row_copy(d2_ref[0, 0, r], r, y2_ref).start(priority=1)
        return carry

    lax.fori_loop(0, tm, issue, 0, unroll=DMA_LOOP_UNROLL)

    def drain(r, carry):
        row_copy(d1_ref[0, 0, r], r, y1_ref).wait()
        row_copy(d2_ref[0, 0, r], r, y2_ref).wait()
        return carry

    lax.fori_loop(0, tm, drain, 0, unroll=DMA_LOOP_UNROLL)
    info = info_ref[...]
    ff = info[:, 4:5] * y1_ref[...] + info[:, 5:6] * y2_ref[...]
    o_ref[...] = _layer_norm_rows(DEEPNORM_ALPHA * x_ref[...] + ff, g_ref[...], b_ref[...])


def _combine(xf, info, ys, d1, d2, g, bvec, tm):
    n = xf.shape[0]
    idx = lambda a: a.reshape(n // tm, 1, tm)
    smem = lambda: pl.BlockSpec((1, 1, tm), lambda i: (i, 0, 0), memory_space=pltpu.SMEM)
    return pl.pallas_call(
        _combine_kernel,
        grid=(n // tm,),
        in_specs=[smem(), smem(),
                  pl.BlockSpec((tm, D_MODEL), lambda i: (i, 0)),
                  pl.BlockSpec((tm, LANES), lambda i: (i, 0)),
                  pl.BlockSpec(memory_space=pl.ANY),
                  pl.BlockSpec((1, D_MODEL), lambda i: (0, 0)),
                  pl.BlockSpec((1, D_MODEL), lambda i: (0, 0))],
        out_specs=pl.BlockSpec((tm, D_MODEL), lambda i: (i, 0)),
        out_shape=jax.ShapeDtypeStruct((n, D_MODEL), F32),
        scratch_shapes=[pltpu.VMEM((tm, D_MODEL), F32), pltpu.VMEM((tm, D_MODEL), F32),
                        pltpu.SemaphoreType.DMA(())],
        compiler_params=_cparams("arbitrary"),
        name="moe_combine_ln",
    )(idx(d1), idx(d2), xf, info, ys, g, bvec)


MOE_TILE = 512


def _moe_ffn(xf, w_router, w1, w3, w2, g, bvec, tm_tok, tf):
    n = xf.shape[0]
    wr_pad = jnp.zeros((D_MODEL, LANES), MXU_DTYPE).at[:, :N_EXPERTS].set(w_router.astype(MXU_DTYPE))
    info, info_t, counts = _router(xf, wr_pad, tm_tok)
    sizes = counts[0, :N_EXPERTS].astype(jnp.int32)
    tiles_per = (sizes + MOE_TILE - 1) // MOE_TILE
    tile_end = jnp.cumsum(tiles_per)
    seg_start = (tile_end - tiles_per) * MOE_TILE
    experts = jnp.arange(N_EXPERTS, dtype=jnp.int32)[:, None]

    def sorted_row(expert_row, rank_row):
        start = jnp.sum(jnp.where(expert_row.astype(jnp.int32)[None, :] == experts, seg_start[:, None], 0), axis=0)
        return start + rank_row.astype(jnp.int32)

    d1 = sorted_row(info_t[0], info_t[2])
    d2 = sorted_row(info_t[1], info_t[3])
    n_tiles = (n * TOP_K) // MOE_TILE + N_EXPERTS
    rows = n_tiles * MOE_TILE
    tile_ids = jnp.arange(n_tiles, dtype=jnp.int32)
    tile_expert = jnp.minimum(jnp.sum((tile_end[None, :] <= tile_ids[:, None]).astype(jnp.int32), axis=1),
                              N_EXPERTS - 1)
    n_used = tile_end[N_EXPERTS - 1:].astype(jnp.int32)
    gap_end = jnp.concatenate([seg_start[1:], jnp.full((1,), rows, jnp.int32)])
    xs = _dispatch(xf, d1, d2, seg_start + sizes, gap_end, rows, tm_tok)
    ys = _experts(xs, tile_expert, n_used, w1, w3, w2, MOE_TILE, tf)
    return _combine(xf, info, ys, d1, d2, g, bvec, tm_tok)


def _pack_in_proj(w_in):
    cuts = np.cumsum([0, GDN_WIDTH, GDN_WIDTH, GDN_WIDTH, GDN_HEADS, GDN_HEADS, GDN_WIDTH,
                      2 * CONV_WIDTH, FOX_WIDTH, FOX_WIDTH, FOX_WIDTH, FOX_HEADS])
    seg = lambda i: w_in[:, cuts[i]:cuts[i + 1]]
    w_main = jnp.concatenate([seg(0), seg(1), seg(2), seg(5), seg(6), seg(7), seg(8), seg(9)], axis=1)
    pad = jnp.zeros((D_MODEL, SMALL_COLS - 2 * GDN_HEADS - FOX_HEADS), w_in.dtype)
    w_small = jnp.concatenate([seg(3), seg(4), seg(10), pad], axis=1).astype(F32)
    ws_hi = w_small.astype(MXU_DTYPE)
    ws_lo = (w_small - ws_hi.astype(F32)).astype(MXU_DTYPE)
    return w_main.astype(MXU_DTYPE), jnp.concatenate([ws_hi, ws_lo], axis=1)


def _lane_row(vals, offset):
    return jnp.zeros((1, LANES), F32).at[0, offset:offset + vals.shape[0]].set(vals.astype(F32))


def _mixer(xf, b, t, w_in, mix_scale, w_out, gdn_conv_w, gdn_a_log, gdn_dt_bias, gdn_norm_w,
           cnv_dw_w, cnv_dw_b, cnv_ln_g, cnv_ln_b, fox_f_bias, ln_g, ln_b, tiles):
    w_main, w_small = _pack_in_proj(w_in)
    ms = mix_scale.reshape(1, D_MODEL).astype(F32)
    row = lambda v: v.reshape(1, -1).astype(F32)
    qkv, gate, o_b, fq, fk, fv, small = _in_proj(
        xf, w_main, w_small, gdn_conv_w, cnv_dw_w, row(cnv_dw_b), row(cnv_ln_g), row(cnv_ln_b),
        ms[:, GDN_WIDTH:GDN_WIDTH + CONV_WIDTH], tiles["tm"], t // tiles["tm"])
    r3 = lambda a: a.reshape(b, t, a.shape[-1])
    assert tiles["tt"] == tiles["tq"]
    g_cum, beta, g_t, stats, qa, ka, va = _gates(
        r3(small), _lane_row(gdn_a_log, 0), _lane_row(gdn_dt_bias, 0), _lane_row(fox_f_bias, 2 * GDN_HEADS),
        r3(fq), r3(fk), r3(fv), tiles["tt"])
    norm_w2 = jnp.tile(gdn_norm_w.astype(F32), 2).reshape(1, PAIR)
    o_a = _gdn(r3(qkv), g_cum, beta, g_t, r3(gate), norm_w2, ms[:, :GDN_WIDTH], tiles["gdn_blk"])
    o_c = _fox(qa, ka, va, _fox_aux(stats), ms[:, GDN_WIDTH + CONV_WIDTH:], tiles["tq"])
    flat = lambda a: a.reshape(b * t, a.shape[-1])
    return _out_proj(flat(o_a), o_b, flat(o_c), xf, w_out.astype(MXU_DTYPE), row(ln_g), row(ln_b),
                     tiles["tm"])


def _tiles(t):
    return dict(tm=min(512, t), tt=min(512, t), gdn_blk=min(512, t), tq=min(512, t),
                tf_moe=1792)


def kernel(x, w_in, mix_scale, w_out, gdn_conv_w, gdn_a_log, gdn_dt_bias, gdn_norm_w, cnv_dw_w, cnv_dw_b,
           cnv_ln_g, cnv_ln_b, fox_f_bias, ln_mix_g, ln_mix_b, ln_ffn_g, ln_ffn_b, ffn_w1, ffn_w3, ffn_w2,
           moe_router, moe_w1, moe_w3, moe_w2):
    b, t, d = x.shape
    tiles = _tiles(t)
    row = lambda v: v.reshape(1, -1).astype(F32)
    xf = x.reshape(b * t, d)
    for l in range(DEPTH):
        xf = _mixer(xf, b, t, w_in[l], mix_scale[l], w_out[l], gdn_conv_w[l], gdn_a_log[l], gdn_dt_bias[l],
                    gdn_norm_w[l], cnv_dw_w[l], cnv_dw_b[l], cnv_ln_g[l], cnv_ln_b[l], fox_f_bias[l],
                    ln_mix_g[l], ln_mix_b[l], tiles)
        if l % 2 == 0:
            e = l // 2
            xf = _dense_ffn(xf, ffn_w1[e].astype(MXU_DTYPE), ffn_w3[e].astype(MXU_DTYPE),
                            ffn_w2[e].astype(MXU_DTYPE), row(ln_ffn_g[l]), row(ln_ffn_b[l]),
                            tiles["tm"])
        else:
            e = l // 2
            xf = _moe_ffn(xf, moe_router[e], moe_w1[e].astype(MXU_DTYPE), moe_w3[e].astype(MXU_DTYPE),
                          moe_w2[e].astype(MXU_DTYPE), row(ln_ffn_g[l]), row(ln_ffn_b[l]),
                          tiles["tm"], tiles["tf_moe"])
    return xf.reshape(b, t, d)
```
